```python
import math
import jax
import jax.numpy as jnp
from jax import lax
import numpy as np

D_MODEL = 1024
BATCH = 4
SEQ = 8192
DEPTH = 1
DEC_BATCH = 16
DEC_SEQ = 16
PAST_LEN = 2048

CHUNK = 64
EPS = 1e-6
NEG_INF = -1e30
DA_HEADS = 8
DA_HEAD_DIM = 64
DA_V_DIM = 2 * DA_HEAD_DIM
DA_WIDTH = DA_HEADS * DA_V_DIM
ROPE_THETA = 10000.0
Q_BLOCK = 128
RW_HEAD = 64
RW_HEADS = D_MODEL // RW_HEAD
RW_WIDTH = RW_HEADS * RW_HEAD
W_LORA = 64
A_LORA = 64
G_LORA = 160
RW_GN_EPS = 64e-5
DA_QK_COLS = DA_HEADS * 2 * DA_HEAD_DIM
RW_COLS = 3 * RW_WIDTH + W_LORA + A_LORA + G_LORA
GATE_COLS = 2 * D_MODEL
IN_COLS = 2 * DA_QK_COLS + DA_WIDTH + RW_COLS + GATE_COLS
IN_SPLITS = (DA_QK_COLS, 2 * DA_QK_COLS, 2 * DA_QK_COLS + DA_WIDTH, 2 * DA_QK_COLS + DA_WIDTH + RW_COLS)
RW_SPLITS = (RW_WIDTH, 2 * RW_WIDTH, 3 * RW_WIDTH, 3 * RW_WIDTH + W_LORA, 3 * RW_WIDTH + W_LORA + A_LORA)
N_GROUPS = 4
EXPERTS_PER_GROUP = 8
TOP_K = 2
D_EXPERT = 256

kernel_name = 'diffattn_rwkv7_hmoe_streaming_step'


def rms_norm(x, g):
    xf = x.astype(jnp.float32)
    y = xf * lax.rsqrt(jnp.mean(xf * xf, axis=-1, keepdims=True) + EPS)
    return (y * g.astype(jnp.float32)).astype(x.dtype)


def rope(x, pos):
    d = x.shape[-1]
    inv = ROPE_THETA ** (-jnp.arange(0, d, 2, dtype=jnp.float32) / d)
    ang = pos.astype(jnp.float32)[:, None] * inv[None, :]
    cos = jnp.cos(ang)[:, None, :]
    sin = jnp.sin(ang)[:, None, :]
    xf = x.astype(jnp.float32)
    x1, x2 = xf[..., : d // 2], xf[..., d // 2:]
    return jnp.concatenate([x1 * cos - x2 * sin, x2 * cos + x1 * sin], axis=-1).astype(x.dtype)


def diff_attend(q, k, v, mask, lam):
    s = jnp.einsum('bqhcd,bkhcd->bhcqk', q, k).astype(jnp.float32) * (DA_HEAD_DIM ** -0.5)
    s = jnp.where(mask, s, NEG_INF)
    p = jax.nn.softmax(s, axis=-1)
    a = p[:, :, 0] - lam * p[:, :, 1]
    return jnp.einsum('bhqk,bkhe->bqhe', a.astype(v.dtype), v)


def diff_attention_prompt(q, k, v, lam):
    B, S = q.shape[:2]
    nblk = S // Q_BLOCK
    kchunk = jnp.arange(S) // CHUNK
    qb = q.reshape(B, nblk, Q_BLOCK, DA_HEADS, 2, DA_HEAD_DIM).swapaxes(0, 1)

    def one(args):
        qi, i = args
        qchunk = (i * Q_BLOCK + jnp.arange(Q_BLOCK)) // CHUNK
        mask = kchunk[None, :] <= qchunk[:, None]
        return diff_attend(qi, k, v, mask, lam)

    o = lax.map(one, (qb, jnp.arange(nblk)))
    return o.swapaxes(0, 1).reshape(B, S, DA_HEADS, DA_V_DIM)


def wkv7_step(S, inp):
    r_t, w_t, k_t, v_t, kk_t, a_t = inp
    sa = jnp.einsum('bhvk,bhk->bhv', S, -kk_t)
    S = S * w_t[:, :, None, :] + sa[..., None] * (kk_t * a_t)[:, :, None, :] + v_t[..., None] * k_t[:, :, None, :]
    y = jnp.einsum('bhvk,bhk->bhv', S, r_t)
    return S, y


def rwkv7_mix(rw, shift0, s0, p):
    B, S, _ = rw.shape
    prev = jnp.concatenate([shift0.astype(rw.dtype), rw[:, :-1]], axis=1)
    xm = rw + (prev - rw) * p['rw_mu']
    r, k, v, wl, al, gl = jnp.split(xm, RW_SPLITS, axis=-1)
    w = -jax.nn.softplus(-(p['rw_w0'] + jnp.tanh(wl) @ p['rw_w2'])) - 0.5
    decay = jnp.exp(-jnp.exp(w.astype(jnp.float32)))
    a = jax.nn.sigmoid(p['rw_a0'] + al @ p['rw_a2'])
    g = jax.nn.sigmoid(gl) @ p['rw_g2']

    def heads(t):
        return t.reshape(B, S, RW_HEADS, RW_HEAD).astype(jnp.float32)

    kk = heads(k * p['rw_k_k'])
    kk = kk / jnp.maximum(jnp.sqrt(jnp.sum(kk * kk, axis=-1, keepdims=True)), 1e-12)
    k = k * (1 + (a - 1) * p['rw_k_a'])
    r_h, w_h, k_h, v_h, a_h = heads(r), heads(decay), heads(k), heads(v), heads(a)
    xs = tuple(jnp.moveaxis(t, 1, 0) for t in (r_h, w_h, k_h, v_h, kk, a_h))
    s_fin, y = lax.scan(wkv7_step, s0.astype(jnp.float32), xs)
    y = jnp.moveaxis(y, 0, 1)
    mu = jnp.mean(y, axis=-1, keepdims=True)
    var = jnp.mean(jnp.square(y - mu), axis=-1, keepdims=True)
    yn = ((y - mu) * lax.rsqrt(var + RW_GN_EPS)).reshape(B, S, RW_WIDTH)
    yn = yn * p['rw_ln_g'].astype(jnp.float32) + p['rw_ln_b'].astype(jnp.float32)
    bonus = jnp.sum(r_h * k_h * p['rw_r_k'].astype(jnp.float32), axis=-1, keepdims=True) * v_h
    out = ((yn + bonus.reshape(B, S, RW_WIDTH)) * g.astype(jnp.float32)).astype(rw.dtype)
    return out @ p['w_ob'], s_fin, rw[:, -1:]


def hier_moe(h, p):
    B, S, _ = h.shape
    pg = jax.nn.softmax((h @ p['router_g'] + p['router_g_b']).astype(jnp.float32), axis=-1)
    g_idx = jnp.argmax(pg, axis=-1)
    p_top = jnp.max(pg, axis=-1)
    le = (h @ p['router_e'] + p['router_e_b']).astype(jnp.float32).reshape(B, S, N_GROUPS, EXPERTS_PER_GROUP)
    le_sel = jnp.take_along_axis(le, g_idx[..., None, None], axis=2)[:, :, 0]
    pe = jax.nn.softmax(le_sel, axis=-1)
    top_v, top_i = lax.top_k(pe, TOP_K)
    top_v = top_v / jnp.sum(top_v, axis=-1, keepdims=True)
    ew = jnp.sum(jax.nn.one_hot(top_i, EXPERTS_PER_GROUP, dtype=jnp.float32) * top_v[..., None], axis=-2)
    combine = (p_top[..., None, None] * jax.nn.one_hot(g_idx, N_GROUPS, dtype=jnp.float32)[..., None]
               * ew[:, :, None, :]).astype(h.dtype)
    out = jnp.zeros_like(h)
    for gi in range(N_GROUPS):
        a = jnp.einsum('bsd,edf->bsef', h, p['exp_w1'][gi])
        u = jnp.einsum('bsd,edf->bsef', h, p['exp_w3'][gi])
        act = jax.nn.silu(a) * u * combine[:, :, gi, :, None]
        out = out + jnp.einsum('bsef,efd->bsd', act, p['exp_w2'][gi])
    return out


def trunk_layer(x, c, pos, past, p, lam_init):
    B, S, _ = x.shape
    mod = (jax.nn.silu(c) @ p['w_ada'] + p['b_ada'])[:, None, :]
    sh1, sc1, gt1, sh2, sc2, gt2 = jnp.split(mod, 6, axis=-1)
    h = rms_norm(x, p['norm1_g']) * (1 + sc1) + sh1
    proj = h @ p['w_in']
    q, k, v, rw, gates = jnp.split(proj, IN_SPLITS, axis=-1)
    q = rope(rms_norm(q.reshape(B, S, 2 * DA_HEADS, DA_HEAD_DIM), p['da_qn_g']), pos)
    q = q.reshape(B, S, DA_HEADS, 2, DA_HEAD_DIM)
    k = rope(rms_norm(k.reshape(B, S, 2 * DA_HEADS, DA_HEAD_DIM), p['da_kn_g']), pos)
    k = k.reshape(B, S, DA_HEADS, 2, DA_HEAD_DIM)
    v = v.reshape(B, S, DA_HEADS, DA_V_DIM)
    lv = p['da_lambda'].astype(jnp.float32)
    lam = jnp.exp(jnp.sum(lv[0] * lv[1])) - jnp.exp(jnp.sum(lv[2] * lv[3])) + lam_init
    if past is None:
        o_a = diff_attention_prompt(q, k, v, lam)
        shift0 = jnp.zeros((B, 1, RW_COLS), rw.dtype)
        s0 = jnp.zeros((B, RW_HEADS, RW_HEAD, RW_HEAD), jnp.float32)
    else:
        ck, cv, s0, shift0 = past
        P = ck.shape[1]
        k_all = jnp.concatenate([ck.reshape(B, P, DA_HEADS, 2, DA_HEAD_DIM).astype(k.dtype), k], axis=1)
        v_all = jnp.concatenate([cv.astype(v.dtype), v], axis=1)
        kchunk = jnp.arange(P + S) // CHUNK
        mask = kchunk[None, :] <= (pos // CHUNK)[:, None]
        o_a = diff_attend(q, k_all, v_all, mask, lam)
    o_a = rms_norm(o_a, p['da_subln_g']) * (1.0 - lam_init)
    o_a = o_a.reshape(B, S, DA_WIDTH) @ p['w_oa']
    o_b, s_new, shift_new = rwkv7_mix(rw, shift0, s0, p)
    ga, gb = jnp.split(gates, 2, axis=-1)
    merged = jax.nn.sigmoid(ga) * o_a + jax.nn.sigmoid(gb) * o_b
    x = x + gt1 * (merged @ p['w_out'])
    h2 = rms_norm(x, p['norm2_g']) * (1 + sc2) + sh2
    x = x + gt2 * hier_moe(h2, p)
    return x, (k.reshape(B, S, DA_HEADS, 2 * DA_HEAD_DIM), v, s_new, shift_new)


def setup_inputs(seed: int = 0) -> dict:
    key = jax.random.key(seed)
    keys = jax.random.split(key, 48)
    counter = iter(range(48))

    def nrm(shape, s):
        return s * jax.random.normal(keys[next(counter)], shape, jnp.float32)

    def uni(shape, lo, hi):
        return jax.random.uniform(keys[next(counter)], shape, jnp.float32, lo, hi)

    L, D = DEPTH, D_MODEL
    return {
        'x_prompt': nrm((BATCH, SEQ, D), 1.0),
        'x_sample': nrm((DEC_BATCH, DEC_SEQ, D), 1.0),
        'cache_k': nrm((L, DEC_BATCH, PAST_LEN, DA_HEADS, 2 * DA_HEAD_DIM), 1.0),
        'cache_v': nrm((L, DEC_BATCH, PAST_LEN, DA_HEADS, DA_V_DIM), 1.0),
        'state_wkv': nrm((L, DEC_BATCH, RW_HEADS, RW_HEAD, RW_HEAD), 0.1),
        'state_shift': nrm((L, DEC_BATCH, 1, RW_COLS), 1.0),
        'c_prompt': nrm((BATCH, D), 1.0),
        'c_sample': nrm((DEC_BATCH, D), 1.0),
        'norm1_g': 1.0 + nrm((L, D), 0.05),
        'norm2_g': 1.0 + nrm((L, D), 0.05),
        'w_ada': nrm((L, D, 6 * D), 0.5 * D ** -0.5),
        'b_ada': nrm((L, 6 * D), 0.1),
        'w_in': nrm((L, D, IN_COLS), D ** -0.5),
        'da_qn_g': 1.0 + nrm((L, DA_HEAD_DIM), 0.05),
        'da_kn_g': 1.0 + nrm((L, DA_HEAD_DIM), 0.05),
        'da_lambda': nrm((L, 4, DA_HEAD_DIM), 0.1),
        'da_subln_g': 1.0 + nrm((L, DA_V_DIM), 0.05),
        'w_oa': nrm((L, DA_WIDTH, D), DA_WIDTH ** -0.5),
        'rw_mu': uni((L, RW_COLS), 0.0, 1.0),
        'rw_w0': uni((L, RW_WIDTH), -6.0, 1.0),
        'rw_w2': nrm((L, W_LORA, RW_WIDTH), 0.1),
        'rw_a0': nrm((L, RW_WIDTH), 0.1),
        'rw_a2': nrm((L, A_LORA, RW_WIDTH), 0.1),
        'rw_g2': nrm((L, G_LORA, RW_WIDTH), G_LORA ** -0.5),
        'rw_k_k': 0.85 + nrm((L, RW_WIDTH), 0.05),
        'rw_k_a': 1.0 + nrm((L, RW_WIDTH), 0.05),
        'rw_r_k': nrm((L, RW_HEADS, RW_HEAD), 0.1),
        'rw_ln_g': 1.0 + nrm((L, RW_WIDTH), 0.05),
        'rw_ln_b': nrm((L, RW_WIDTH), 0.02),
        'w_ob': nrm((L, RW_WIDTH, D), RW_WIDTH ** -0.5),
        'w_out': nrm((L, D, D), D ** -0.5),
        'router_g': nrm((L, D, N_GROUPS), D ** -0.5),
        'router_g_b': nrm((L, N_GROUPS), 0.01),
        'router_e': nrm((L, D, N_GROUPS * EXPERTS_PER_GROUP), D ** -0.5),
        'router_e_b': nrm((L, N_GROUPS * EXPERTS_PER_GROUP), 0.01),
        'exp_w1': nrm((L, N_GROUPS, EXPERTS_PER_GROUP, D, D_EXPERT), D ** -0.5),
        'exp_w3': nrm((L, N_GROUPS, EXPERTS_PER_GROUP, D, D_EXPERT), D ** -0.5),
        'exp_w2': nrm((L, N_GROUPS, EXPERTS_PER_GROUP, D_EXPERT, D), D_EXPERT ** -0.5),
    }


def reference(x_prompt, x_sample, cache_k, cache_v, state_wkv, state_shift, c_prompt, c_sample,
              norm1_g, norm2_g, w_ada, b_ada, w_in, da_qn_g, da_kn_g, da_lambda, da_subln_g, w_oa,
              rw_mu, rw_w0, rw_w2, rw_a0, rw_a2, rw_g2, rw_k_k, rw_k_a, rw_r_k, rw_ln_g, rw_ln_b, w_ob,
              w_out, router_g, router_g_b, router_e, router_e_b, exp_w1, exp_w3, exp_w2):
    pos_p = jnp.arange(x_prompt.shape[1], dtype=jnp.int32)
    pos_s = cache_k.shape[2] + jnp.arange(x_sample.shape[1], dtype=jnp.int32)
    xp, xs = x_prompt, x_sample
    st_p_all, st_s_all = [], []
    for l in range(DEPTH):
        p = dict(norm1_g=norm1_g[l], norm2_g=norm2_g[l], w_ada=w_ada[l], b_ada=b_ada[l], w_in=w_in[l],
                 da_qn_g=da_qn_g[l], da_kn_g=da_kn_g[l], da_lambda=da_lambda[l], da_subln_g=da_subln_g[l],
                 w_oa=w_oa[l], rw_mu=rw_mu[l], rw_w0=rw_w0[l], rw_w2=rw_w2[l], rw_a0=rw_a0[l],
                 rw_a2=rw_a2[l], rw_g2=rw_g2[l], rw_k_k=rw_k_k[l], rw_k_a=rw_k_a[l], rw_r_k=rw_r_k[l],
                 rw_ln_g=rw_ln_g[l], rw_ln_b=rw_ln_b[l], w_ob=w_ob[l], w_out=w_out[l],
                 router_g=router_g[l], router_g_b=router_g_b[l], router_e=router_e[l],
                 router_e_b=router_e_b[l], exp_w1=exp_w1[l], exp_w3=exp_w3[l], exp_w2=exp_w2[l])
        lam_init = 0.8 - 0.6 * math.exp(-0.3 * l)
        xp, st_p = trunk_layer(xp, c_prompt, pos_p, None, p, lam_init)
        xs, st_s = trunk_layer(xs, c_sample, pos_s, (cache_k[l], cache_v[l], state_wkv[l], state_shift[l]), p, lam_init)
        st_p_all.append(st_p)
        st_s_all.append(st_s)
    new_k_prompt = jnp.stack([s[0] for s in st_p_all])
    new_v_prompt = jnp.stack([s[1] for s in st_p_all])
    new_wkv_prompt = jnp.stack([s[2] for s in st_p_all])
    new_shift_prompt = jnp.stack([s[3] for s in st_p_all])
    new_k_sample = jnp.stack([s[0] for s in st_s_all])
    new_v_sample = jnp.stack([s[1] for s in st_s_all])
    new_wkv_sample = jnp.stack([s[2] for s in st_s_all])
    new_shift_sample = jnp.stack([s[3] for s in st_s_all])
    return (xp, xs, new_k_prompt, new_v_prompt, new_wkv_prompt, new_shift_prompt,
            new_k_sample, new_v_sample, new_wkv_sample, new_shift_sample)
```

```python
import functools
import math

import jax
import jax.numpy as jnp
from jax import lax
from jax.experimental import pallas as pl
from jax.experimental.pallas import tpu as pltpu

F32 = jnp.float32
BF16 = jnp.bfloat16

EPS = 1e-6
NEG_INF = -1e30
CHUNK = 64
DA_HEADS = 8
DA_HEAD_DIM = 64
ROPE_THETA = 10000.0
RW_HEAD = 64
RW_GN_EPS = 64e-5
W_LORA, A_LORA, G_LORA = 64, 64, 160
N_GROUPS, EXPERTS_PER_GROUP = 4, 8
N_EXPERTS = N_GROUPS * EXPERTS_PER_GROUP
LAM_INIT = 0.8 - 0.6 * math.exp(-0.3 * 0)

LANES = 128
LORA_PAD = 384
VMEM_LIMIT = 56 * 1024 * 1024


def _cparams(*sem):
    return pltpu.CompilerParams(dimension_semantics=sem, vmem_limit_bytes=VMEM_LIMIT)


def _dot(a, b):
    return jnp.dot(a, b, preferred_element_type=F32)


def _dot_nt(a, b):
    return lax.dot_general(a, b, (((1,), (1,)), ((), ())), preferred_element_type=F32)


def _dot_tn(a, b):
    return lax.dot_general(a, b, (((0,), (0,)), ((), ())), preferred_element_type=F32)


def _split2(a):
    hi = a.astype(BF16)
    lo = (a - hi.astype(F32)).astype(BF16)
    return hi, lo


def _split3(a):
    hi = a.astype(BF16)
    r = a - hi.astype(F32)
    mid = r.astype(BF16)
    lo = (r - mid.astype(F32)).astype(BF16)
    return hi, mid, lo


def _group_mean(sq, gm):
    hi, lo = _split2(sq)
    return _dot(hi, gm) + _dot(lo, gm)


def _adaln_kernel(c_ref, w_ref, b_ref, l_ref, o_ref, lam_ref):
    c = c_ref[...]
    sc = (c * jax.nn.sigmoid(c)).astype(BF16)
    o_ref[...] = _dot(sc, w_ref[...].astype(BF16)) + b_ref[...]
    l = l_ref[...]
    s1 = jnp.sum(l[0:1] * l[1:2], axis=-1, keepdims=True)
    s2 = jnp.sum(l[2:3] * l[3:4], axis=-1, keepdims=True)
    lam = jnp.exp(s1) - jnp.exp(s2) + LAM_INIT
    lam_ref[...] = jnp.broadcast_to(lam, lam_ref.shape)


def _adaln(c_all, w_ada, b_ada, da_lambda):
    bp, d = c_all.shape
    n = w_ada.shape[1]
    tn = 1024
    return pl.pallas_call(
        _adaln_kernel,
        grid=(n // tn,),
        in_specs=[pl.BlockSpec((bp, d), lambda j: (0, 0)),
                  pl.BlockSpec((d, tn), lambda j: (0, j)),
                  pl.BlockSpec((1, tn), lambda j: (0, j)),
                  pl.BlockSpec(da_lambda.shape, lambda j: (0, 0))],
        out_specs=[pl.BlockSpec((bp, tn), lambda j: (0, j)),
                   pl.BlockSpec((8, LANES), lambda j: (0, 0))],
        out_shape=[jax.ShapeDtypeStruct((bp, n), F32), jax.ShapeDtypeStruct((8, LANES), F32)],
        compiler_params=_cparams("arbitrary"),
        name="adaln",
    )(c_all, w_ada, b_ada.reshape(1, n), da_lambda)


def _prenorm_kernel(x_ref, sc_ref, sh_ref, g_ref, o_ref):
    x = x_ref[...]
    ms = jnp.mean(x * x, axis=-1, keepdims=True)
    h = x * lax.rsqrt(ms + EPS) * g_ref[...]
    o_ref[...] = (h * (1.0 + sc_ref[...]) + sh_ref[...]).astype(o_ref.dtype)


def _mod_spec(mod, tm, seq):
    d = mod.shape[-1]
    if mod.shape[1] == 1:
        per = seq // tm
        return pl.BlockSpec((None, 1, d), lambda i, *_: (i // per, 0, 0))
    return pl.BlockSpec((None, tm, d), lambda i, *_: (0, i, 0))


def _prenorm(x2d, sc, sh, g, tm, seq):
    t, d = x2d.shape
    return pl.pallas_call(
        _prenorm_kernel,
        grid=(t // tm,),
        in_specs=[pl.BlockSpec((tm, d), lambda i: (i, 0)), _mod_spec(sc, tm, seq), _mod_spec(sh, tm, seq),
                  pl.BlockSpec((1, d), lambda i: (0, 0))],
        out_specs=pl.BlockSpec((tm, d), lambda i: (i, 0)),
        out_shape=jax.ShapeDtypeStruct((t, d), BF16),
        compiler_params=_cparams("parallel"),
        name="prenorm",
    )(x2d, sc, sh, g)


def _qkv_kernel(h_ref, w_ref, qg_ref, kg_ref, cos_ref, sin_ref, gm_ref,
                q_ref, k_ref, kb_ref, v_ref, vb_ref):
    j = pl.program_id(1)
    acc = _dot(h_ref[...], w_ref[...])
    n_slab = acc.shape[1] // LANES

    def norm_rope(gain_ref, scale, write):
        lane = lax.broadcasted_iota(jnp.int32, (acc.shape[0], LANES), 1)
        first_half = (lane % DA_HEAD_DIM) < (DA_HEAD_DIM // 2)
        for c in range(n_slab):
            a = acc[:, c * LANES:(c + 1) * LANES]
            ms = _group_mean(a * a, gm_ref[...])
            y = a * lax.rsqrt(ms + EPS) * gain_ref[...]
            rot = jnp.where(first_half, pltpu.roll(y, LANES - DA_HEAD_DIM // 2, 1),
                            pltpu.roll(y, DA_HEAD_DIM // 2, 1))
            write(c, (y * cos_ref[...] + rot * sin_ref[...]) * scale)

    @pl.when(j == 0)
    def _():
        def write(c, val):
            q_ref[:, c * LANES:(c + 1) * LANES] = val.astype(q_ref.dtype)
        norm_rope(qg_ref, DA_HEAD_DIM ** -0.5, write)

    @pl.when(j == 1)
    def _():
        def write(c, val):
            k_ref[:, c * LANES:(c + 1) * LANES] = val
            kb_ref[:, c * LANES:(c + 1) * LANES] = val.astype(BF16)
        norm_rope(kg_ref, 1.0, write)

    @pl.when(j == 2)
    def _():
        v_ref[...] = acc
        vb_ref[...] = acc.astype(BF16)


def _qkv(h, w_qkv, qg, kg, cos_t, sin_t, gm, tm):
    t, d = h.shape
    n = 1024
    per = cos_t.shape[0] // tm
    row = lambda i, j: (i, 0)
    return pl.pallas_call(
        _qkv_kernel,
        grid=(t // tm, 3),
        in_specs=[pl.BlockSpec((tm, d), row),
                  pl.BlockSpec((d, n), lambda i, j: (0, j)),
                  pl.BlockSpec((1, LANES), lambda i, j: (0, 0)),
                  pl.BlockSpec((1, LANES), lambda i, j: (0, 0)),
                  pl.BlockSpec((tm, LANES), lambda i, j: (i % per, 0)),
                  pl.BlockSpec((tm, LANES), lambda i, j: (i % per, 0)),
                  pl.BlockSpec((LANES, LANES), lambda i, j: (0, 0))],
        out_specs=[pl.BlockSpec((tm, n), row)] * 5,
        out_shape=[jax.ShapeDtypeStruct((t, n), BF16), jax.ShapeDtypeStruct((t, n), F32),
                   jax.ShapeDtypeStruct((t, n), BF16), jax.ShapeDtypeStruct((t, n), F32),
                   jax.ShapeDtypeStruct((t, n), BF16)],
        compiler_params=_cparams("parallel", "arbitrary"),
        name="qkv_proj",
    )(h, w_qkv, qg, kg, cos_t, sin_t, gm)


def _proj_kernel(h_ref, w_ref, o_ref, *, act):
    acc = _dot(h_ref[...], w_ref[...])
    if act == "sigmoid":
        acc = jax.nn.sigmoid(acc)
    o_ref[...] = acc.astype(o_ref.dtype)


def _proj(h, w, tm, tn, act, out_dtype, name):
    t, d = h.shape
    n = w.shape[1]
    return pl.pallas_call(
        functools.partial(_proj_kernel, act=act),
        grid=(t // tm, n // tn),
        in_specs=[pl.BlockSpec((tm, d), lambda i, j: (i, 0)), pl.BlockSpec((d, tn), lambda i, j: (0, j))],
        out_specs=pl.BlockSpec((tm, tn), lambda i, j: (i, j)),
        out_shape=jax.ShapeDtypeStruct((t, n), out_dtype),
        compiler_params=_cparams("parallel", "arbitrary"),
        name=name,
    )(h, w)


def _subln(o, g):
    ms = jnp.mean(o * o, axis=-1, keepdims=True)
    return o * lax.rsqrt(ms + EPS) * g * (1.0 - LAM_INIT)


def _attn_prompt_kernel(qi_tab, ki_tab, lam_ref, q_ref, k_ref, v_ref, g_ref, o_ref,
                        qs_scr, m_scr, l_scr, acc_scr, *, tq):
    p = pl.program_id(2)
    qi = qi_tab[p]
    ki = ki_tab[p]

    @pl.when(ki == 0)
    def _():
        q = q_ref[...]
        lane = lax.broadcasted_iota(jnp.int32, q.shape, 1)
        zero = jnp.zeros_like(q)
        qs_scr[0:tq, :] = jnp.where(lane < DA_HEAD_DIM, q, zero)
        qs_scr[tq:2 * tq, :] = jnp.where(lane >= DA_HEAD_DIM, q, zero)
        m_scr[...] = jnp.full(m_scr.shape, NEG_INF, F32)
        l_scr[...] = jnp.zeros(l_scr.shape, F32)
        acc_scr[...] = jnp.zeros(acc_scr.shape, F32)

    def step(masked):
        s = _dot_nt(qs_scr[...], k_ref[...])
        if masked:
            row = lax.broadcasted_iota(jnp.int32, s.shape, 0)
            col = lax.broadcasted_iota(jnp.int32, s.shape, 1)
            qchunk = (row // CHUNK) % (tq // CHUNK)
            s = jnp.where((col // CHUNK) <= qchunk, s, NEG_INF)
        m_prev = m_scr[...]
        m_new = jnp.maximum(m_prev, jnp.max(s, axis=-1, keepdims=True))
        alpha = jnp.exp(m_prev - m_new)
        e = jnp.exp(s - m_new)
        l_scr[...] = alpha * l_scr[...] + jnp.sum(e, axis=-1, keepdims=True)
        acc_scr[...] = alpha * acc_scr[...] + _dot(e.astype(BF16), v_ref[...])
        m_scr[...] = m_new

    @pl.when(ki < qi)
    def _():
        step(False)

    @pl.when(ki == qi)
    def _():
        step(True)
        o = acc_scr[...] / l_scr[...]
        o = o[0:tq] - lam_ref[...] * o[tq:2 * tq]
        o_ref[...] = _subln(o, g_ref[...]).astype(o_ref.dtype)


def _attn_prompt(q, k, v, lam_row, subln_g, batch, seq, tq):
    d = q.shape[1]
    nq = seq // tq
    qi_l, ki_l = [], []
    for a in range(nq):
        for b in range(a + 1):
            qi_l.append(a)
            ki_l.append(b)
    qi_tab = jnp.asarray(qi_l, jnp.int32)
    ki_tab = jnp.asarray(ki_l, jnp.int32)
    q3, k3, v3 = (a.reshape(batch, seq, d) for a in (q, k, v))
    qspec = pl.BlockSpec((None, tq, LANES), lambda b, h, p, qt, kt: (b, qt[p], h))
    kspec = pl.BlockSpec((None, tq, LANES), lambda b, h, p, qt, kt: (b, kt[p], h))
    vec = pl.BlockSpec((1, LANES), lambda b, h, p, qt, kt: (0, 0))
    out = pl.pallas_call(
        functools.partial(_attn_prompt_kernel, tq=tq),
        grid_spec=pltpu.PrefetchScalarGridSpec(
            num_scalar_prefetch=2,
            grid=(batch, DA_HEADS, len(qi_l)),
            in_specs=[vec, qspec, kspec, kspec, vec],
            out_specs=qspec,
            scratch_shapes=[pltpu.VMEM((2 * tq, LANES), BF16), pltpu.VMEM((2 * tq, 1), F32),
                            pltpu.VMEM((2 * tq, 1), F32), pltpu.VMEM((2 * tq, LANES), F32)]),
        out_shape=jax.ShapeDtypeStruct((batch, seq, d), BF16),
        compiler_params=_cparams("parallel", "parallel", "arbitrary"),
        name="attn_prompt",
    )(qi_tab, ki_tab, lam_row, q3, k3, v3, subln_g)
    return out.reshape(batch * seq, d)


def _attn_sample_kernel(lam_ref, q_ref, kn_ref, vn_ref, ck_ref, cv_ref, g_ref, o_ref, *, past, sq):
    q = q_ref[...]
    lane = lax.broadcasted_iota(jnp.int32, q.shape, 1)
    zero = jnp.zeros_like(q)
    qs = jnp.concatenate([jnp.where(lane < DA_HEAD_DIM, q, zero),
                          jnp.where(lane >= DA_HEAD_DIM, q, zero)], axis=0)
    s_c = _dot_nt(qs, ck_ref[...].astype(BF16))
    s_n = _dot_nt(qs, kn_ref[...])

    def masked(s, key0):
        row = lax.broadcasted_iota(jnp.int32, s.shape, 0)
        col = lax.broadcasted_iota(jnp.int32, s.shape, 1)
        qchunk = (past + row % sq) // CHUNK
        return jnp.where((key0 + col) // CHUNK <= qchunk, s, NEG_INF)

    s_c = masked(s_c, 0)
    s_n = masked(s_n, past)
    m = jnp.maximum(jnp.max(s_c, axis=-1, keepdims=True), jnp.max(s_n, axis=-1, keepdims=True))
    e_c = jnp.exp(s_c - m)
    e_n = jnp.exp(s_n - m)
    l = jnp.sum(e_c, axis=-1, keepdims=True) + jnp.sum(e_n, axis=-1, keepdims=True)
    o = (_dot(e_c.astype(BF16), cv_ref[...].astype(BF16)) + _dot(e_n.astype(BF16), vn_ref[...])) / l
    o = o[0:sq] - lam_ref[...] * o[sq:2 * sq]
    o_ref[...] = _subln(o, g_ref[...]).astype(o_ref.dtype)


def _attn_sample(q, kn, vn, cache_k, cache_v, lam_row, subln_g, batch, sq):
    d = q.shape[1]
    past = cache_k.shape[1]
    ck = cache_k.reshape(batch, past, d)
    cv = cache_v.reshape(batch, past, d)
    q3, k3, v3 = (a.reshape(batch, sq, d) for a in (q, kn, vn))
    new = pl.BlockSpec((None, sq, LANES), lambda b, h: (b, 0, h))
    old = pl.BlockSpec((None, past, LANES), lambda b, h: (b, 0, h))
    vec = pl.BlockSpec((1, LANES), lambda b, h: (0, 0))
    out = pl.pallas_call(
        functools.partial(_attn_sample_kernel, past=past, sq=sq),
        grid=(batch, DA_HEADS),
        in_specs=[vec, new, new, new, old, old, vec],
        out_specs=new,
        out_shape=jax.ShapeDtypeStruct((batch, sq, d), BF16),
        compiler_params=_cparams("parallel", "parallel"),
        name="attn_sample",
    )(lam_row, q3, k3, v3, ck, cv, subln_g)
    return out.reshape(batch * sq, d)


RW_L = 64
RW_PAIRS = 8


def _mm3(a, b, dot=_dot):
    ah, al = _split2(a)
    bh, bl = _split2(b)
    return dot(ah, bh) + dot(ah, bl) + dot(al, bh)


def _mm1(a, b, dot=_dot):
    return dot(a.astype(BF16), b.astype(BF16))


def _softplus(z):
    return jnp.maximum(z, 0.0) + jnp.log(1.0 + jnp.exp(-jnp.abs(z)))


def _rwkv_kernel(rw_ref, sh0_ref, mu_ref, lw_ref, w0_ref, a0_ref, kk_ref, ka_ref, rk_ref, lng_ref, lnb_ref,
                 h0_ref, gm_ref, tri_ref, ob_ref, hout_ref, h_scr, last_scr, *, n_valid):
    L = RW_L
    c = pl.program_id(1)
    d = ob_ref.shape[-1]

    @pl.when(c == 0)
    def _():
        h_scr[...] = h0_ref[...]
        last_scr[...] = sh0_ref[...]

    rw = rw_ref[...]
    row_w = lax.broadcasted_iota(jnp.int32, rw.shape, 0)
    prev = jnp.where(row_w == 0, last_scr[...], pltpu.roll(rw, 1, 0))
    last_scr[...] = rw[L - 1:L, :]
    xm = rw + (prev - rw) * mu_ref[...]
    r = xm[:, 0:d]
    k = xm[:, d:2 * d]
    v = xm[:, 2 * d:3 * d]
    lo = xm[:, 3 * d:3 * d + LORA_PAD]
    lane_l = lax.broadcasted_iota(jnp.int32, lo.shape, 1)
    z = jnp.where(lane_l < W_LORA, jnp.tanh(lo),
                  jnp.where(lane_l < W_LORA + A_LORA, lo, jax.nn.sigmoid(lo)))
    lora = _dot(z.astype(BF16), lw_ref[...])
    w = -_softplus(-(w0_ref[...] + lora[:, 0:d])) - 0.5
    ld = -jnp.exp(w)
    a = jax.nn.sigmoid(a0_ref[...] + lora[:, d:2 * d])
    g = lora[:, 2 * d:3 * d]
    kkr = k * kk_ref[...]
    kmod = k * (1.0 + (a - 1.0) * ka_ref[...])
    rkk = r * kmod * rk_ref[...]
    gm = gm_ref[...]
    if n_valid < L:
        valid = lax.broadcasted_iota(jnp.int32, ld.shape, 0) < n_valid
        ld = jnp.where(valid, ld, 0.0)
        kkr = jnp.where(valid, kkr, 0.0)
        kmod = jnp.where(valid, kmod, 0.0)

    tri = tri_ref[...]
    l1, l2, l3 = _split3(ld)
    cin = _dot(tri, l1) + _dot(tri, l2) + _dot(tri, l3)
    cex = cin - ld
    c_last = cin[L - 1:L, :]
    e_in = jnp.exp(cin)
    e_ex = jnp.exp(cex)
    e_neg = jnp.exp(-cin)
    e_tail = jnp.exp(c_last - cin)
    g_last = jnp.exp(c_last)

    row = lax.broadcasted_iota(jnp.int32, (L, LANES), 0)
    lane = lax.broadcasted_iota(jnp.int32, (L, LANES), 1)
    m_lo = lane < RW_HEAD
    tri_s = row > (lane % L)
    tri_i = row >= (lane % L)
    eye_pair = jnp.where(row == (lane % L), 1.0, 0.0).astype(F32)
    r2 = lax.broadcasted_iota(jnp.int32, (LANES, LANES), 0)
    c2 = lax.broadcasted_iota(jnp.int32, (LANES, LANES), 1)
    bd_mask = (r2 < RW_HEAD) == (c2 < RW_HEAD)
    diag_mask = r2 == c2

    def bd(x):
        return jnp.concatenate([jnp.where(m_lo, x, 0.0), jnp.where(m_lo, 0.0, x)], axis=0)

    for p in range(RW_PAIRS):
        sl = slice(p * LANES, (p + 1) * LANES)
        r_p, v_p, km_p, a_p = r[:, sl], v[:, sl], kmod[:, sl], a[:, sl]
        kk_p = kkr[:, sl]
        ss = _group_mean(kk_p * kk_p, gm) * RW_HEAD
        kk_p = kk_p / jnp.maximum(jnp.sqrt(ss), 1e-12)
        b_p = kk_p * a_p
        abar = -kk_p * e_ex[:, sl]
        rbar = r_p * e_in[:, sl]
        bbar = b_p * e_neg[:, sl]
        kbar = km_p * e_neg[:, sl]
        btil = b_p * e_tail[:, sl]
        ktil = km_p * e_tail[:, sl]

        lhs = jnp.concatenate([abar, rbar], axis=0)
        rhs = jnp.concatenate([bd(bbar), bd(kbar)], axis=0)
        a_all = _mm3(lhs, rhs, _dot_nt)
        n_ab = jnp.where(tri_s, a_all[0:L, 0:2 * L], 0.0)
        a_ak = jnp.where(tri_s, a_all[0:L, 2 * L:4 * L], 0.0)
        a_rb = jnp.where(tri_i, a_all[L:2 * L, 0:2 * L], 0.0)
        a_rk = jnp.where(tri_i, a_all[L:2 * L, 2 * L:4 * L], 0.0)

        t_inv = eye_pair + n_ab
        pw = n_ab
        for _ in range(int(math.log2(L)) - 1):
            pw = _mm3(pw, bd(pw))
            t_inv = t_inv + _mm3(t_inv, bd(pw))

        av = _mm3(a_ak, bd(v_p))
        pu = _mm3(t_inv, jnp.concatenate([bd(abar), bd(av)], axis=1))
        p_m, u0 = pu[:, 0:LANES], pu[:, LANES:2 * LANES]
        zero = jnp.zeros((2 * L, LANES), F32)
        rhs2 = jnp.concatenate([jnp.concatenate([bd(p_m), bd(u0)], axis=1),
                                jnp.concatenate([zero, bd(v_p)], axis=1)], axis=0)
        qy = _mm3(jnp.concatenate([a_rb, a_rk], axis=1), rhs2)
        q_m = rbar + qy[:, 0:LANES]
        y0 = qy[:, LANES:2 * LANES]

        m_full = _mm3(btil, p_m, _dot_tn)
        h0_full = _mm3(jnp.concatenate([btil, ktil], axis=0), jnp.concatenate([u0, v_p], axis=0), _dot_tn)
        m_bd = jnp.where(bd_mask, m_full, 0.0) + jnp.where(diag_mask, g_last[:, sl], 0.0)
        h0_bd = jnp.where(bd_mask, h0_full, 0.0)

        h_prev = h_scr[p]
        y = _mm3(q_m, h_prev) + y0
        h_scr[p] = _mm3(m_bd, h_prev) + h0_bd

        mu_y = _group_mean(y, gm)
        dy = y - mu_y
        var = _group_mean(dy * dy, gm)
        yn = dy * lax.rsqrt(var + RW_GN_EPS) * lng_ref[:, sl] + lnb_ref[:, sl]
        bonus = _group_mean(rkk[:, sl], gm) * RW_HEAD * v_p
        ob_ref[:, sl] = ((yn + bonus) * g[:, sl]).astype(ob_ref.dtype)

    @pl.when(c == pl.num_programs(1) - 1)
    def _():
        hout_ref[...] = h_scr[...]


def _rwkv(rw3, shift0, h0_bd, prm, gm, n_valid):
    batch, seq, wcols = rw3.shape
    d = prm["w0"].shape[1]
    nc = seq // RW_L
    tri = jnp.tril(jnp.ones((RW_L, RW_L), F32)).astype(BF16)
    const2 = lambda b, c: (0, 0)
    vec = pl.BlockSpec((1, d), const2)
    st = pl.BlockSpec((None, RW_PAIRS, LANES, LANES), lambda b, c: (b, 0, 0, 0))
    return pl.pallas_call(
        functools.partial(_rwkv_kernel, n_valid=n_valid),
        grid=(batch, nc),
        in_specs=[pl.BlockSpec((None, RW_L, wcols), lambda b, c: (b, c, 0)),
                  pl.BlockSpec((None, 1, wcols), lambda b, c: (b, 0, 0)),
                  pl.BlockSpec((1, wcols), const2),
                  pl.BlockSpec(prm["lw"].shape, const2),
                  vec, vec, vec, vec, vec, vec, vec, st,
                  pl.BlockSpec((LANES, LANES), const2),
                  pl.BlockSpec((RW_L, RW_L), const2)],
        out_specs=[pl.BlockSpec((None, RW_L, d), lambda b, c: (b, c, 0)), st],
        out_shape=[jax.ShapeDtypeStruct((batch, seq, d), BF16),
                   jax.ShapeDtypeStruct((batch, RW_PAIRS, LANES, LANES), F32)],
        scratch_shapes=[pltpu.VMEM((RW_PAIRS, LANES, LANES), F32), pltpu.VMEM((1, wcols), F32)],
        compiler_params=_cparams("parallel", "arbitrary"),
        name="rwkv7",
    )(rw3, shift0, prm["mu"], prm["lw"], prm["w0"], prm["a0"], prm["k_k"], prm["k_a"], prm["r_k"],
      prm["ln_g"], prm["ln_b"], h0_bd, gm, tri)


def _state_to_bd(s):
    b = s.shape[0]
    ht = jnp.swapaxes(s, -1, -2).reshape(b, RW_PAIRS, 2, RW_HEAD, RW_HEAD)
    z = jnp.zeros_like(ht[:, :, 0])
    top = jnp.concatenate([ht[:, :, 0], z], axis=-1)
    bot = jnp.concatenate([z, ht[:, :, 1]], axis=-1)
    return jnp.concatenate([top, bot], axis=-2)


def _bd_to_state(hbd):
    b = hbd.shape[0]
    h0 = hbd[:, :, 0:RW_HEAD, 0:RW_HEAD]
    h1 = hbd[:, :, RW_HEAD:, RW_HEAD:]
    ht = jnp.stack([h0, h1], axis=2).reshape(b, 2 * RW_PAIRS, RW_HEAD, RW_HEAD)
    return jnp.swapaxes(ht, -1, -2)


def _route(logits):
    lane = lax.broadcasted_iota(jnp.int32, logits.shape, 1).astype(F32)
    big = float(LANES)
    lg = jnp.where(lane < N_GROUPS, logits, NEG_INF)
    mg = jnp.max(lg, axis=-1, keepdims=True)
    sg = jnp.sum(jnp.exp(lg - mg), axis=-1, keepdims=True)
    p_top = 1.0 / sg
    g_idx = jnp.min(jnp.where(lg == mg, lane, big), axis=-1, keepdims=True)
    e0 = N_GROUPS + EXPERTS_PER_GROUP * g_idx
    emask = jnp.where(lane >= e0, jnp.where(lane < e0 + EXPERTS_PER_GROUP, 1.0, 0.0), 0.0) > 0.5
    le = jnp.where(emask, logits, NEG_INF)
    me = jnp.max(le, axis=-1, keepdims=True)
    ee = jnp.exp(le - me)
    pe = ee / jnp.sum(ee, axis=-1, keepdims=True)
    pe = jnp.where(emask, pe, -1.0)
    v1 = jnp.max(pe, axis=-1, keepdims=True)
    i1 = jnp.min(jnp.where(pe == v1, lane, big), axis=-1, keepdims=True)
    pe2 = jnp.where(lane == i1, -1.0, pe)
    v2 = jnp.max(pe2, axis=-1, keepdims=True)
    i2 = jnp.min(jnp.where(pe2 == v2, lane, big), axis=-1, keepdims=True)
    den = v1 + v2
    ew = jnp.where(lane == i1, v1 / den, 0.0) + jnp.where(lane == i2, v2 / den, 0.0)
    return p_top * ew


def _merge_kernel(oa_ref, ob_ref, gates_ref, x_ref, gt1_ref, sc2_ref, sh2_ref, g2_ref,
                  woa_ref, wob_ref, wout_ref, wr_ref, br_ref, x1_ref, h2_ref, comb_ref):
    d = x_ref.shape[1]
    a_out = _dot(oa_ref[...], woa_ref[...])
    b_out = _dot(ob_ref[...], wob_ref[...])
    merged = gates_ref[:, 0:d].astype(F32) * a_out + gates_ref[:, d:2 * d].astype(F32) * b_out
    x1 = x_ref[...] + gt1_ref[...] * _dot(merged.astype(BF16), wout_ref[...])
    x1_ref[...] = x1
    ms = jnp.mean(x1 * x1, axis=-1, keepdims=True)
    h2 = x1 * lax.rsqrt(ms + EPS) * g2_ref[...]
    h2 = h2 * (1.0 + sc2_ref[...]) + sh2_ref[...]
    h2_ref[...] = h2.astype(h2_ref.dtype)
    logits = _mm3(h2, wr_ref[...]) + br_ref[...]
    comb_ref[...] = _route(logits)


def _merge(oa, ob, gates, x2d, gt1, sc2, sh2, g2, w_oa, w_ob, w_out, wr, br, tm, seq):
    t, d = x2d.shape
    row = lambda i: (i, 0)
    const = lambda i: (0, 0)
    full = lambda w: pl.BlockSpec(w.shape, const)
    return pl.pallas_call(
        _merge_kernel,
        grid=(t // tm,),
        in_specs=[pl.BlockSpec((tm, d), row), pl.BlockSpec((tm, d), row), pl.BlockSpec((tm, 2 * d), row),
                  pl.BlockSpec((tm, d), row), _mod_spec(gt1, tm, seq), _mod_spec(sc2, tm, seq),
                  _mod_spec(sh2, tm, seq), pl.BlockSpec((1, d), const),
                  full(w_oa), full(w_ob), full(w_out), full(wr), full(br)],
        out_specs=[pl.BlockSpec((tm, d), row), pl.BlockSpec((tm, d), row), pl.BlockSpec((tm, LANES), row)],
        out_shape=[jax.ShapeDtypeStruct((t, d), F32), jax.ShapeDtypeStruct((t, d), BF16),
                   jax.ShapeDtypeStruct((t, LANES), F32)],
        compiler_params=_cparams("parallel"),
        name="merge_router",
    )(oa, ob, gates, x2d, gt1, sc2, sh2, g2, w_oa, w_ob, w_out, wr, br)


def _moe_kernel(h2_ref, comb_ref, w13_ref, w2_ref, x1_ref, gt2_ref, y_ref, acc_scr):
    e = pl.program_id(1)

    @pl.when(e == 0)
    def _():
        acc_scr[...] = jnp.zeros(acc_scr.shape, F32)

    au = _dot(h2_ref[...], w13_ref[...])
    f = au.shape[1] // 2
    a = au[:, 0:f]
    u = au[:, f:2 * f]
    comb = comb_ref[...]
    lane = lax.broadcasted_iota(jnp.int32, comb.shape, 1)
    cw = jnp.sum(jnp.where(lane == e + N_GROUPS, comb, 0.0), axis=-1, keepdims=True)
    act = a * jax.nn.sigmoid(a) * u * cw
    acc_scr[...] += _dot(act.astype(BF16), w2_ref[...])

    @pl.when(e == pl.num_programs(1) - 1)
    def _():
        y_ref[...] = x1_ref[...] + gt2_ref[...] * acc_scr[...]


def _moe(h2, comb, w13, w2, x1, gt2, tm, seq):
    t, d = x1.shape
    ne, _, f2 = w13.shape
    row = lambda i, e: (i, 0)
    return pl.pallas_call(
        _moe_kernel,
        grid=(t // tm, ne),
        in_specs=[pl.BlockSpec((tm, d), row), pl.BlockSpec((tm, LANES), row),
                  pl.BlockSpec((None, d, f2), lambda i, e: (e, 0, 0)),
                  pl.BlockSpec((None, f2 // 2, d), lambda i, e: (e, 0, 0)),
                  pl.BlockSpec((tm, d), row), _mod_spec(gt2, tm, seq)],
        out_specs=pl.BlockSpec((tm, d), row),
        out_shape=jax.ShapeDtypeStruct((t, d), F32),
        scratch_shapes=[pltpu.VMEM((tm, d), F32)],
        compiler_params=_cparams("parallel", "arbitrary"),
        name="moe",
    )(h2, comb, w13, w2, x1, gt2)


def _rope_tables(pos0, seq, reps):
    pos = (pos0 + jnp.arange(seq)).astype(F32)
    inv = ROPE_THETA ** (-jnp.arange(0, DA_HEAD_DIM, 2, dtype=F32) / DA_HEAD_DIM)
    ang = pos[:, None] * inv[None, :]
    cos, sin = jnp.cos(ang), jnp.sin(ang)
    cos_t = jnp.tile(jnp.concatenate([cos, cos], axis=-1), (reps, LANES // DA_HEAD_DIM))
    sin_t = jnp.tile(jnp.concatenate([-sin, sin], axis=-1), (reps, LANES // DA_HEAD_DIM))
    return cos_t, sin_t


def _layer(x, mod, pos0, past, wts):
    bx, sx, d = x.shape
    t = bx * sx
    x2d = x.reshape(t, d)
    per_batch = sx >= 512
    tm = 512 if per_batch else t

    def modv(i):
        if per_batch:
            return mod[:, i:i + 1, :]
        return jnp.repeat(mod[:, i, :], sx, axis=0)[None]

    sh1, sc1, gt1, sh2, sc2, gt2 = (modv(i) for i in range(6))
    cos_t, sin_t = _rope_tables(pos0, sx, 1 if per_batch else bx)

    h = _prenorm(x2d, sc1, sh1, wts["norm1_g"], tm, sx)
    q, k32, kb, v32, vb = _qkv(h, wts["w_qkv"], wts["qg"], wts["kg"], cos_t, sin_t, wts["gm"], tm)
    rw = _proj(h, wts["w_rw"], tm, wts["w_rw"].shape[1] // 3, None, F32, "rw_proj")
    gates = _proj(h, wts["w_gate"], tm, d, "sigmoid", BF16, "gate_proj")

    if past is None:
        oa = _attn_prompt(q, kb, vb, wts["lam"], wts["subln_g"], bx, sx, min(512, sx))
        shift0 = jnp.zeros((bx, 1, rw.shape[1]), F32)
        h0_bd = jnp.zeros((bx, RW_PAIRS, LANES, LANES), F32)
    else:
        ck, cv, s0, sh0 = past
        oa = _attn_sample(q, kb, vb, ck, cv, wts["lam"], wts["subln_g"], bx, sx)
        shift0 = jnp.pad(sh0, ((0, 0), (0, 0), (0, rw.shape[1] - sh0.shape[-1])))
        h0_bd = _state_to_bd(s0.astype(F32))

    rw3 = rw.reshape(bx, sx, rw.shape[1])
    if sx % RW_L == 0:
        rw_in, n_valid = rw3, RW_L
    else:
        assert sx < RW_L
        rw_in, n_valid = jnp.pad(rw3, ((0, 0), (0, RW_L - sx), (0, 0))), sx
    ob, h_bd = _rwkv(rw_in, shift0, h0_bd, wts, wts["gm"], n_valid)
    ob = ob[:, 0:sx, :].reshape(t, d)

    x1, h2, comb = _merge(oa, ob, gates, x2d, gt1, sc2, sh2, wts["norm2_g"],
                          wts["w_oa"], wts["w_ob"], wts["w_out"], wts["wr"], wts["br"], tm, sx)
    tm_moe = 1024 if (per_batch and sx % 1024 == 0) else tm
    y = _moe(h2, comb, wts["w13"], wts["w2"], x1, gt2, tm_moe, sx)

    n_cols = 3 * d + W_LORA + A_LORA + G_LORA
    return (y.reshape(bx, sx, d),
            k32.reshape(1, bx, sx, DA_HEADS, 2 * DA_HEAD_DIM),
            v32.reshape(1, bx, sx, DA_HEADS, 2 * DA_HEAD_DIM),
            _bd_to_state(h_bd)[None],
            rw3[:, sx - 1:sx, 0:n_cols][None])


def kernel(x_prompt, x_sample, cache_k, cache_v, state_wkv, state_shift, c_prompt, c_sample, norm1_g, norm2_g, w_ada, b_ada, w_in, da_qn_g, da_kn_g, da_lambda, da_subln_g, w_oa, rw_mu, rw_w0, rw_w2, rw_a0, rw_a2, rw_g2, rw_k_k, rw_k_a, rw_r_k, rw_ln_g, rw_ln_b, w_ob, w_out, router_g, router_g_b, router_e, router_e_b, exp_w1, exp_w3, exp_w2):
    b, s, d = x_prompt.shape
    bs = x_sample.shape[0]
    past_len = cache_k.shape[2]
    assert w_in.shape[0] == 1, "single layer"

    c_all = jnp.concatenate([c_prompt, c_sample], axis=0)
    c_all = jnp.pad(c_all, ((0, (-c_all.shape[0]) % 8), (0, 0)))
    mod, lam_tile = _adaln(c_all, w_ada[0], b_ada[0], da_lambda[0])
    mod_p = mod[0:b].reshape(b, 6, d)
    mod_s = mod[b:b + bs].reshape(bs, 6, d)

    win = w_in[0]
    rw_cols = 3 * d + W_LORA + A_LORA + G_LORA
    rw_pad = 3 * d + LORA_PAD
    w_rw = jnp.pad(win[:, 3 * d:3 * d + rw_cols], ((0, 0), (0, rw_pad - rw_cols))).astype(BF16)
    lw = jnp.zeros((LORA_PAD, 3 * d), F32)
    lw = lw.at[0:W_LORA, 0:d].set(rw_w2[0])
    lw = lw.at[W_LORA:W_LORA + A_LORA, d:2 * d].set(rw_a2[0])
    lw = lw.at[W_LORA + A_LORA:W_LORA + A_LORA + G_LORA, 2 * d:3 * d].set(rw_g2[0])
    f = exp_w1.shape[-1]
    wr = jnp.pad(jnp.concatenate([router_g[0], router_e[0]], axis=1), ((0, 0), (0, LANES - N_GROUPS - N_EXPERTS)))
    br = jnp.pad(jnp.concatenate([router_g_b[0], router_e_b[0]]), (0, LANES - N_GROUPS - N_EXPERTS))[None]
    half = jnp.ones((DA_HEAD_DIM, DA_HEAD_DIM), F32) / DA_HEAD_DIM
    wts = dict(
        norm1_g=norm1_g[0][None], norm2_g=norm2_g[0][None],
        w_qkv=win[:, 0:3 * d].astype(BF16), w_rw=w_rw, w_gate=win[:, 3 * d + rw_cols:].astype(BF16),
        qg=jnp.tile(da_qn_g[0], 2)[None], kg=jnp.tile(da_kn_g[0], 2)[None],
        gm=jnp.kron(jnp.eye(2, dtype=F32), half).astype(BF16),
        lam=lam_tile[0:1], subln_g=da_subln_g[0][None],
        mu=jnp.pad(rw_mu[0], (0, rw_pad - rw_cols))[None], lw=lw.astype(BF16),
        w0=rw_w0[0][None], a0=rw_a0[0][None], k_k=rw_k_k[0][None], k_a=rw_k_a[0][None],
        r_k=rw_r_k[0].reshape(1, d), ln_g=rw_ln_g[0][None], ln_b=rw_ln_b[0][None],
        w_oa=w_oa[0].astype(BF16), w_ob=w_ob[0].astype(BF16), w_out=w_out[0].astype(BF16),
        wr=wr, br=br,
        w13=jnp.concatenate([exp_w1[0], exp_w3[0]], axis=-1).reshape(N_EXPERTS, d, 2 * f).astype(BF16),
        w2=exp_w2[0].reshape(N_EXPERTS, f, d).astype(BF16),
    )

    out_p = _layer(x_prompt, mod_p, 0, None, wts)
    out_s = _layer(x_sample, mod_s, past_len,
                   (cache_k[0], cache_v[0], state_wkv[0], state_shift[0]), wts)
    return (out_p[0], out_s[0], out_p[1], out_p[2], out_p[3], out_p[4],
            out_s[1], out_s[2], out_s[3], out_s[4])
```

```python
import functools
import math

import jax
import jax.numpy as jnp
from jax import lax
from jax.experimental import pallas as pl
from jax.experimental.pallas import tpu as pltpu

F32 = jnp.float32
BF16 = jnp.bfloat16

EPS = 1e-6
NEG_INF = -1e30
CHUNK = 64
DA_HEADS = 8
DA_HEAD_DIM = 64
ROPE_THETA = 10000.0
RW_HEAD = 64
RW_GN_EPS = 64e-5
W_LORA, A_LORA, G_LORA = 64, 64, 160
N_GROUPS, EXPERTS_PER_GROUP = 4, 8
N_EXPERTS = N_GROUPS * EXPERTS_PER_GROUP
LAM_INIT = 0.8 - 0.6 * math.exp(-0.3 * 0)

LANES = 128
LORA_PAD = 384
VMEM_LIMIT = 56 * 1024 * 1024


def _cparams(*sem):
    return pltpu.CompilerParams(dimension_semantics=sem, vmem_limit_bytes=VMEM_LIMIT)


def _dot(a, b):
    return jnp.dot(a, b, preferred_element_type=F32)


def _dot_nt(a, b):
    return lax.dot_general(a, b, (((1,), (1,)), ((), ())), preferred_element_type=F32)


def _dot_tn(a, b):
    return lax.dot_general(a, b, (((0,), (0,)), ((), ())), preferred_element_type=F32)


def _split2(a):
    hi = a.astype(BF16)
    lo = (a - hi.astype(F32)).astype(BF16)
    return hi, lo


def _split3(a):
    hi = a.astype(BF16)
    r = a - hi.astype(F32)
    mid = r.astype(BF16)
    lo = (r - mid.astype(F32)).astype(BF16)
    return hi, mid, lo


def _group_mean(sq, gm):
    hi, lo = _split2(sq)
    return _dot(hi, gm) + _dot(lo, gm)


def _adaln_kernel(c_ref, w_ref, b_ref, l_ref, o_ref, lam_ref):
    c = c_ref[...]
    sc = (c * jax.nn.sigmoid(c)).astype(BF16)
    o_ref[...] = _dot(sc, w_ref[...].astype(BF16)) + b_ref[...]
    l = l_ref[...]
    s1 = jnp.sum(l[0:1] * l[1:2], axis=-1, keepdims=True)
    s2 = jnp.sum(l[2:3] * l[3:4], axis=-1, keepdims=True)
    lam = jnp.exp(s1) - jnp.exp(s2) + LAM_INIT
    lam_ref[...] = jnp.broadcast_to(lam, lam_ref.shape)


def _adaln(c_all, w_ada, b_ada, da_lambda):
    bp, d = c_all.shape
    n = w_ada.shape[1]
    tn = 1024
    return pl.pallas_call(
        _adaln_kernel,
        grid=(n // tn,),
        in_specs=[pl.BlockSpec((bp, d), lambda j: (0, 0)),
                  pl.BlockSpec((d, tn), lambda j: (0, j)),
                  pl.BlockSpec((1, tn), lambda j: (0, j)),
                  pl.BlockSpec(da_lambda.shape, lambda j: (0, 0))],
        out_specs=[pl.BlockSpec((bp, tn), lambda j: (0, j)),
                   pl.BlockSpec((8, LANES), lambda j: (0, 0))],
        out_shape=[jax.ShapeDtypeStruct((bp, n), F32), jax.ShapeDtypeStruct((8, LANES), F32)],
        compiler_params=_cparams("arbitrary"),
        name="adaln",
    )(c_all, w_ada, b_ada.reshape(1, n), da_lambda)


def _prenorm_kernel(x_ref, sc_ref, sh_ref, g_ref, o_ref):
    x = x_ref[...]
    ms = jnp.mean(x * x, axis=-1, keepdims=True)
    h = x * lax.rsqrt(ms + EPS) * g_ref[...]
    o_ref[...] = (h * (1.0 + sc_ref[...]) + sh_ref[...]).astype(o_ref.dtype)


def _mod_spec(mod, tm, seq):
    d = mod.shape[-1]
    if mod.shape[1] == 1:
        per = seq // tm
        return pl.BlockSpec((None, 1, d), lambda i, *_: (i // per, 0, 0))
    return pl.BlockSpec((None, tm, d), lambda i, *_: (0, i, 0))


def _prenorm(x2d, sc, sh, g, tm, seq):
    t, d = x2d.shape
    return pl.pallas_call(
        _prenorm_kernel,
        grid=(t // tm,),
        in_specs=[pl.BlockSpec((tm, d), lambda i: (i, 0)), _mod_spec(sc, tm, seq), _mod_spec(sh, tm, seq),
                  pl.BlockSpec((1, d), lambda i: (0, 0))],
        out_specs=pl.BlockSpec((tm, d), lambda i: (i, 0)),
        out_shape=jax.ShapeDtypeStruct((t, d), BF16),
        compiler_params=_cparams("parallel"),
        name="prenorm",
    )(x2d, sc, sh, g)


def _qkv_kernel(h_ref, w_ref, qg_ref, kg_ref, cos_ref, sin_ref, gm_ref,
                q_ref, k_ref, kb_ref, v_ref, vb_ref):
    j = pl.program_id(1)
    acc = _dot(h_ref[...], w_ref[...])
    n_slab = acc.shape[1] // LANES

    def norm_rope(gain_ref, scale, write):
        lane = lax.broadcasted_iota(jnp.int32, (acc.shape[0], LANES), 1)
        first_half = (lane % DA_HEAD_DIM) < (DA_HEAD_DIM // 2)
        for c in range(n_slab):
            a = acc[:, c * LANES:(c + 1) * LANES]
            ms = _group_mean(a * a, gm_ref[...])
            y = a * lax.rsqrt(ms + EPS) * gain_ref[...]
            rot = jnp.where(first_half, pltpu.roll(y, LANES - DA_HEAD_DIM // 2, 1),
                            pltpu.roll(y, DA_HEAD_DIM // 2, 1))
            write(c, (y * cos_ref[...] + rot * sin_ref[...]) * scale)

    @pl.when(j == 0)
    def _():
        def write(c, val):
            q_ref[:, c * LANES:(c + 1) * LANES] = val.astype(q_ref.dtype)
        norm_rope(qg_ref, DA_HEAD_DIM ** -0.5 * math.log2(math.e), write)

    @pl.when(j == 1)
    def _():
        def write(c, val):
            k_ref[:, c * LANES:(c + 1) * LANES] = val
            kb_ref[:, c * LANES:(c + 1) * LANES] = val.astype(BF16)
        norm_rope(kg_ref, 1.0, write)

    @pl.when(j == 2)
    def _():
        v_ref[...] = acc
        vb_ref[...] = acc.astype(BF16)


def _qkv(h, w_qkv, qg, kg, cos_t, sin_t, gm, tm):
    t, d = h.shape
    n = 1024
    per = cos_t.shape[0] // tm
    row = lambda i, j: (i, 0)
    return pl.pallas_call(
        _qkv_kernel,
        grid=(t // tm, 3),
        in_specs=[pl.BlockSpec((tm, d), row),
                  pl.BlockSpec((d, n), lambda i, j: (0, j)),
                  pl.BlockSpec((1, LANES), lambda i, j: (0, 0)),
                  pl.BlockSpec((1, LANES), lambda i, j: (0, 0)),
                  pl.BlockSpec((tm, LANES), lambda i, j: (i % per, 0)),
                  pl.BlockSpec((tm, LANES), lambda i, j: (i % per, 0)),
                  pl.BlockSpec((LANES, LANES), lambda i, j: (0, 0))],
        out_specs=[pl.BlockSpec((tm, n), row)] * 5,
        out_shape=[jax.ShapeDtypeStruct((t, n), BF16), jax.ShapeDtypeStruct((t, n), F32),
                   jax.ShapeDtypeStruct((t, n), BF16), jax.ShapeDtypeStruct((t, n), F32),
                   jax.ShapeDtypeStruct((t, n), BF16)],
        compiler_params=_cparams("parallel", "arbitrary"),
        name="qkv_proj",
    )(h, w_qkv, qg, kg, cos_t, sin_t, gm)


def _proj_kernel(h_ref, w_ref, o_ref, *, act):
    acc = _dot(h_ref[...], w_ref[...])
    if act == "sigmoid":
        acc = jax.nn.sigmoid(acc)
    o_ref[...] = acc.astype(o_ref.dtype)


def _proj(h, w, tm, tn, act, out_dtype, name):
    t, d = h.shape
    n = w.shape[1]
    return pl.pallas_call(
        functools.partial(_proj_kernel, act=act),
        grid=(t // tm, n // tn),
        in_specs=[pl.BlockSpec((tm, d), lambda i, j: (i, 0)), pl.BlockSpec((d, tn), lambda i, j: (0, j))],
        out_specs=pl.BlockSpec((tm, tn), lambda i, j: (i, j)),
        out_shape=jax.ShapeDtypeStruct((t, n), out_dtype),
        compiler_params=_cparams("parallel", "arbitrary"),
        name=name,
    )(h, w)


def _subln(o, g):
    ms = jnp.mean(o * o, axis=-1, keepdims=True)
    return o * lax.rsqrt(ms + EPS) * g * (1.0 - LAM_INIT)


def _attn_prompt_kernel(qi_tab, ki_tab, lam_ref, q_ref, k_ref, v_ref, g_ref, o_ref,
                        qs_scr, vx_scr, m_scr, acc_scr, *, tq, tk):
    p = pl.program_id(2)
    qi = qi_tab[p]
    ki = ki_tab[p]
    ratio = tk // tq
    n_slab = tk // LANES

    @pl.when(ki == 0)
    def _():
        q = q_ref[...]
        lane = lax.broadcasted_iota(jnp.int32, q.shape, 1)
        zero = jnp.zeros_like(q)
        qs_scr[0:tq, :] = jnp.where(lane < DA_HEAD_DIM, q, zero)
        qs_scr[tq:2 * tq, :] = jnp.where(lane >= DA_HEAD_DIM, q, zero)
        m_scr[...] = jnp.full(m_scr.shape, NEG_INF, F32)
        acc_scr[...] = jnp.zeros(acc_scr.shape, F32)
        vx_scr[:, LANES:2 * LANES] = jnp.ones((tk, LANES), BF16)

    vx_scr[:, 0:LANES] = v_ref[...]

    def step(masked):
        s = _dot_nt(qs_scr[...], k_ref[...])
        slabs = [s[:, c * LANES:(c + 1) * LANES] for c in range(n_slab)]
        if masked:
            row = lax.broadcasted_iota(jnp.int32, (2 * tq, LANES), 0)
            lane = lax.broadcasted_iota(jnp.int32, (2 * tq, LANES), 1)
            qchunk = (row % tq) // CHUNK + (qi % ratio) * (tq // CHUNK)
            slabs = [jnp.where((c * LANES + lane) // CHUNK <= qchunk, sl, NEG_INF)
                     for c, sl in enumerate(slabs)]
        mt = slabs[0]
        for sl in slabs[1:]:
            mt = jnp.maximum(mt, sl)
        m_prev = m_scr[...]
        m_new = jnp.maximum(m_prev, jnp.max(mt, axis=-1, keepdims=True))
        alpha = jnp.exp2(m_prev - m_new)
        e = jnp.concatenate([jnp.exp2(sl - m_new).astype(BF16) for sl in slabs], axis=1)
        pv = _dot(e, vx_scr[...])
        acc_scr[...] = jnp.concatenate([alpha, alpha], axis=1) * acc_scr[...] + pv
        m_scr[...] = m_new

    @pl.when(ki < qi // ratio)
    def _():
        step(False)

    @pl.when(ki == qi // ratio)
    def _():
        step(True)
        o = acc_scr[:, 0:LANES] / acc_scr[:, LANES:2 * LANES]
        o = o[0:tq] - lam_ref[...] * o[tq:2 * tq]
        o_ref[...] = _subln(o, g_ref[...]).astype(o_ref.dtype)


def _attn_prompt(q, k, v, lam_row, subln_g, batch, seq, tq, tk):
    d = q.shape[1]
    assert tk % tq == 0 and seq % tk == 0 and tq % CHUNK == 0
    qi_l, ki_l = [], []
    for a in range(seq // tq):
        for b in range(a * tq // tk + 1):
            qi_l.append(a)
            ki_l.append(b)
    qi_tab = jnp.asarray(qi_l, jnp.int32)
    ki_tab = jnp.asarray(ki_l, jnp.int32)
    q3, k3, v3 = (a.reshape(batch, seq, d) for a in (q, k, v))
    qspec = pl.BlockSpec((None, tq, LANES), lambda b, h, p, qt, kt: (b, qt[p], h))
    kspec = pl.BlockSpec((None, tk, LANES), lambda b, h, p, qt, kt: (b, kt[p], h))
    vec = pl.BlockSpec((1, LANES), lambda b, h, p, qt, kt: (0, 0))
    out = pl.pallas_call(
        functools.partial(_attn_prompt_kernel, tq=tq, tk=tk),
        grid_spec=pltpu.PrefetchScalarGridSpec(
            num_scalar_prefetch=2,
            grid=(batch, DA_HEADS, len(qi_l)),
            in_specs=[vec, qspec, kspec, kspec, vec],
            out_specs=qspec,
            scratch_shapes=[pltpu.VMEM((2 * tq, LANES), BF16), pltpu.VMEM((tk, 2 * LANES), BF16),
                            pltpu.VMEM((2 * tq, LANES), F32), pltpu.VMEM((2 * tq, 2 * LANES), F32)]),
        out_shape=jax.ShapeDtypeStruct((batch, seq, d), BF16),
        compiler_params=_cparams("parallel", "parallel", "arbitrary"),
        name="attn_prompt",
    )(qi_tab, ki_tab, lam_row, q3, k3, v3, subln_g)
    return out.reshape(batch * seq, d)


def _attn_sample_kernel(lam_ref, q_ref, kn_ref, vn_ref, ck_ref, cv_ref, g_ref, o_ref, *, past, sq):
    q = q_ref[...]
    lane = lax.broadcasted_iota(jnp.int32, q.shape, 1)
    zero = jnp.zeros_like(q)
    qs = jnp.concatenate([jnp.where(lane < DA_HEAD_DIM, q, zero),
                          jnp.where(lane >= DA_HEAD_DIM, q, zero)], axis=0)
    s_c = _dot_nt(qs, ck_ref[...].astype(BF16))
    s_n = _dot_nt(qs, kn_ref[...])

    def masked(s, key0):
        row = lax.broadcasted_iota(jnp.int32, s.shape, 0)
        col = lax.broadcasted_iota(jnp.int32, s.shape, 1)
        qchunk = (past + row % sq) // CHUNK
        return jnp.where((key0 + col) // CHUNK <= qchunk, s, NEG_INF)

    s_c = masked(s_c, 0)
    s_n = masked(s_n, past)
    m = jnp.maximum(jnp.max(s_c, axis=-1, keepdims=True), jnp.max(s_n, axis=-1, keepdims=True))
    e_c = jnp.exp2(s_c - m)
    e_n = jnp.exp2(s_n - m)
    l = jnp.sum(e_c, axis=-1, keepdims=True) + jnp.sum(e_n, axis=-1, keepdims=True)
    o = (_dot(e_c.astype(BF16), cv_ref[...].astype(BF16)) + _dot(e_n.astype(BF16), vn_ref[...])) / l
    o = o[0:sq] - lam_ref[...] * o[sq:2 * sq]
    o_ref[...] = _subln(o, g_ref[...]).astype(o_ref.dtype)


def _attn_sample(q, kn, vn, cache_k, cache_v, lam_row, subln_g, batch, sq):
    d = q.shape[1]
    past = cache_k.shape[1]
    ck = cache_k.reshape(batch, past, d)
    cv = cache_v.reshape(batch, past, d)
    q3, k3, v3 = (a.reshape(batch, sq, d) for a in (q, kn, vn))
    new = pl.BlockSpec((None, sq, LANES), lambda b, h: (b, 0, h))
    old = pl.BlockSpec((None, past, LANES), lambda b, h: (b, 0, h))
    vec = pl.BlockSpec((1, LANES), lambda b, h: (0, 0))
    out = pl.pallas_call(
        functools.partial(_attn_sample_kernel, past=past, sq=sq),
        grid=(batch, DA_HEADS),
        in_specs=[vec, new, new, new, old, old, vec],
        out_specs=new,
        out_shape=jax.ShapeDtypeStruct((batch, sq, d), BF16),
        compiler_params=_cparams("parallel", "parallel"),
        name="attn_sample",
    )(lam_row, q3, k3, v3, ck, cv, subln_g)
    return out.reshape(batch * sq, d)


RW_L = 64
RW_PAIRS = 8


def _mm3(a, b, dot=_dot):
    ah, al = _split2(a)
    bh, bl = _split2(b)
    return dot(ah, bh) + dot(ah, bl) + dot(al, bh)


def _terms(x, n):
    return (x.astype(BF16),) if n == 1 else _split2(x)


def _mmt(at, bt, dot=_dot):
    acc = dot(at[0], bt[0])
    if len(bt) > 1:
        acc = acc + dot(at[0], bt[1])
    if len(at) > 1:
        acc = acc + dot(at[1], bt[0])
    return acc


RW_PREC = dict(a_all=(1, 1), neumann=(1, 1), av=(1, 1), pu=(1, 1), qy=(1, 1), mh=(1, 1), state=(2, 2))


def _softplus(z):
    return jnp.maximum(z, 0.0) + jnp.log(1.0 + jnp.exp(-jnp.abs(z)))


def _rwkv_kernel(rw_ref, sh0_ref, mu_ref, lw_ref, w0_ref, a0_ref, kk_ref, ka_ref, rk_ref, lng_ref, lnb_ref,
                 h0_ref, gm_ref, tri_ref, ob_ref, hout_ref, h_scr, last_scr, *, n_valid):
    L = RW_L
    c = pl.program_id(1)
    d = ob_ref.shape[-1]

    @pl.when(c == 0)
    def _():
        h_scr[...] = h0_ref[...]
        last_scr[...] = sh0_ref[...]

    rw = rw_ref[...]
    row_w = lax.broadcasted_iota(jnp.int32, rw.shape, 0)
    prev = jnp.where(row_w == 0, last_scr[...], pltpu.roll(rw, 1, 0))
    last_scr[...] = rw[L - 1:L, :]
    xm = rw + (prev - rw) * mu_ref[...]
    r = xm[:, 0:d]
    k = xm[:, d:2 * d]
    v = xm[:, 2 * d:3 * d]
    lo = xm[:, 3 * d:3 * d + LORA_PAD]
    lane_l = lax.broadcasted_iota(jnp.int32, lo.shape, 1)
    z = jnp.where(lane_l < W_LORA, jnp.tanh(lo),
                  jnp.where(lane_l < W_LORA + A_LORA, lo, jax.nn.sigmoid(lo)))
    lora = _dot(z.astype(BF16), lw_ref[...])
    w = -_softplus(-(w0_ref[...] + lora[:, 0:d])) - 0.5
    ld = -jnp.exp(w)
    a = jax.nn.sigmoid(a0_ref[...] + lora[:, d:2 * d])
    g = lora[:, 2 * d:3 * d]
    kkr = k * kk_ref[...]
    kmod = k * (1.0 + (a - 1.0) * ka_ref[...])
    rkk = r * kmod * rk_ref[...]
    gm = gm_ref[...]
    if n_valid < L:
        valid = lax.broadcasted_iota(jnp.int32, ld.shape, 0) < n_valid
        ld = jnp.where(valid, ld, 0.0)
        kkr = jnp.where(valid, kkr, 0.0)
        kmod = jnp.where(valid, kmod, 0.0)

    tri = tri_ref[...]
    l1, l2, l3 = _split3(ld)
    cin = _dot(tri, l1) + _dot(tri, l2) + _dot(tri, l3)
    cex = cin - ld
    c_last = cin[L - 1:L, :]
    e_in = jnp.exp(cin)
    e_ex = jnp.exp(cex)
    e_neg = jnp.exp(-cin)
    e_tail = jnp.exp(c_last - cin)
    g_last = jnp.exp(c_last)

    row = lax.broadcasted_iota(jnp.int32, (L, LANES), 0)
    lane = lax.broadcasted_iota(jnp.int32, (L, LANES), 1)
    m_lo = lane < RW_HEAD
    tri_s = row > (lane % L)
    tri_i = row >= (lane % L)
    eye_pair = jnp.where(row == (lane % L), 1.0, 0.0).astype(F32)
    r2 = lax.broadcasted_iota(jnp.int32, (LANES, LANES), 0)
    c2 = lax.broadcasted_iota(jnp.int32, (LANES, LANES), 1)
    bd_mask = (r2 < RW_HEAD) == (c2 < RW_HEAD)
    diag_mask = r2 == c2

    def bd(x):
        z0 = jnp.zeros_like(x)
        return jnp.concatenate([jnp.where(m_lo, x, z0), jnp.where(m_lo, z0, x)], axis=0)

    def bdt(t):
        return tuple(bd(x) for x in t)

    def cat(ts, axis):
        return tuple(jnp.concatenate(xs, axis=axis) for xs in zip(*ts))

    prec = RW_PREC
    pairs = range(RW_PAIRS)
    sls = [slice(p * LANES, (p + 1) * LANES) for p in pairs]

    v_p = [v[:, sl] for sl in sls]
    kk_p = []
    for sl in sls:
        kk_raw = kkr[:, sl]
        ss = _group_mean(kk_raw * kk_raw, gm) * RW_HEAD
        kk_p.append(kk_raw / jnp.maximum(jnp.sqrt(ss), 1e-12))
    b_p = [kk_p[p] * a[:, sls[p]] for p in pairs]
    abar = [-kk_p[p] * e_ex[:, sls[p]] for p in pairs]
    rbar = [r[:, sls[p]] * e_in[:, sls[p]] for p in pairs]
    bbar = [b_p[p] * e_neg[:, sls[p]] for p in pairs]
    kbar = [kmod[:, sls[p]] * e_neg[:, sls[p]] for p in pairs]
    btil = [b_p[p] * e_tail[:, sls[p]] for p in pairs]
    ktil = [kmod[:, sls[p]] * e_tail[:, sls[p]] for p in pairs]

    na, nb = prec["a_all"]
    abar_t = [_terms(abar[p], max(na, prec["pu"][1])) for p in pairs]
    v_t = [_terms(v_p[p], max(prec["av"][1], prec["qy"][1], prec["mh"][1])) for p in pairs]
    a_all = []
    for p in pairs:
        lhs = cat([abar_t[p][:na], _terms(rbar[p], na)], 0)
        rhs = cat([bdt(_terms(bbar[p], nb)), bdt(_terms(kbar[p], nb))], 0)
        a_all.append(_mmt(lhs, rhs, _dot_nt))
    n_ab = [jnp.where(tri_s, a_all[p][0:L, 0:2 * L], 0.0) for p in pairs]
    a_ak = [jnp.where(tri_s, a_all[p][0:L, 2 * L:4 * L], 0.0) for p in pairs]
    a_rb = [jnp.where(tri_i, a_all[p][L:2 * L, 0:2 * L], 0.0) for p in pairs]
    a_rk = [jnp.where(tri_i, a_all[p][L:2 * L, 2 * L:4 * L], 0.0) for p in pairs]

    na, nb = prec["neumann"]
    t_inv = [eye_pair + n_ab[p] for p in pairs]
    pw = n_ab
    for _ in range(int(math.log2(L)) - 1):
        pw_t = [_terms(pw[p], max(na, nb)) for p in pairs]
        pw = [_mmt(pw_t[p][:na], bdt(pw_t[p][:nb])) for p in pairs]
        pw_b = [bdt(_terms(pw[p], nb)) for p in pairs]
        t_inv = [t_inv[p] + _mmt(_terms(t_inv[p], na), pw_b[p]) for p in pairs]

    na, nb = prec["av"]
    av = [_mmt(_terms(a_ak[p], na), bdt(v_t[p][:nb])) for p in pairs]
    na, nb = prec["pu"]
    pu = [_mmt(_terms(t_inv[p], na), cat([bdt(abar_t[p][:nb]), bdt(_terms(av[p], nb))], 1)) for p in pairs]
    p_m = [x[:, 0:LANES] for x in pu]
    u0 = [x[:, LANES:2 * LANES] for x in pu]
    na, nb = prec["qy"]
    nmh = prec["mh"][1]
    p_t = [_terms(p_m[p], max(nb, nmh)) for p in pairs]
    u_t = [_terms(u0[p], max(nb, nmh)) for p in pairs]
    q_m, y0 = [], []
    for p in pairs:
        zero = tuple(jnp.zeros((2 * L, LANES), BF16) for _ in range(nb))
        rhs2 = cat([cat([bdt(p_t[p][:nb]), bdt(u_t[p][:nb])], 1),
                    cat([zero, bdt(v_t[p][:nb])], 1)], 0)
        qy = _mmt(cat([_terms(a_rb[p], na), _terms(a_rk[p], na)], 1), rhs2)
        q_m.append(rbar[p] + qy[:, 0:LANES])
        y0.append(qy[:, LANES:2 * LANES])

    na, nb = prec["mh"]
    btil_t = [_terms(btil[p], na) for p in pairs]
    m_full = [_mmt(btil_t[p], p_t[p][:nb], _dot_tn) for p in pairs]
    h0_full = [_mmt(cat([btil_t[p], _terms(ktil[p], na)], 0), cat([u_t[p][:nb], v_t[p][:nb]], 0), _dot_tn)
               for p in pairs]
    m_bd = [jnp.where(bd_mask, m_full[p], 0.0) + jnp.where(diag_mask, g_last[:, sls[p]], 0.0) for p in pairs]
    h0_bd = [jnp.where(bd_mask, h0_full[p], 0.0) for p in pairs]

    na, nb = prec["state"]
    h_t = [_terms(h_scr[p], nb) for p in pairs]
    y = [_mmt(_terms(q_m[p], na), h_t[p]) + y0[p] for p in pairs]
    for p in pairs:
        h_scr[p] = _mmt(_terms(m_bd[p], na), h_t[p]) + h0_bd[p]

    for p in pairs:
        sl = sls[p]
        mu_y = _group_mean(y[p], gm)
        dy = y[p] - mu_y
        var = _group_mean(dy * dy, gm)
        yn = dy * lax.rsqrt(var + RW_GN_EPS) * lng_ref[:, sl] + lnb_ref[:, sl]
        bonus = _group_mean(rkk[:, sl], gm) * RW_HEAD * v_p[p]
        ob_ref[:, sl] = ((yn + bonus) * g[:, sl]).astype(ob_ref.dtype)

    @pl.when(c == pl.num_programs(1) - 1)
    def _():
        hout_ref[...] = h_scr[...]


def _rwkv(rw3, shift0, h0_bd, prm, gm, n_valid):
    batch, seq, wcols = rw3.shape
    d = prm["w0"].shape[1]
    nc = seq // RW_L
    tri = jnp.tril(jnp.ones((RW_L, RW_L), F32)).astype(BF16)
    const2 = lambda b, c: (0, 0)
    vec = pl.BlockSpec((1, d), const2)
    st = pl.BlockSpec((None, RW_PAIRS, LANES, LANES), lambda b, c: (b, 0, 0, 0))
    return pl.pallas_call(
        functools.partial(_rwkv_kernel, n_valid=n_valid),
        grid=(batch, nc),
        in_specs=[pl.BlockSpec((None, RW_L, wcols), lambda b, c: (b, c, 0)),
                  pl.BlockSpec((None, 1, wcols), lambda b, c: (b, 0, 0)),
                  pl.BlockSpec((1, wcols), const2),
                  pl.BlockSpec(prm["lw"].shape, const2),
                  vec, vec, vec, vec, vec, vec, vec, st,
                  pl.BlockSpec((LANES, LANES), const2),
                  pl.BlockSpec((RW_L, RW_L), const2)],
        out_specs=[pl.BlockSpec((None, RW_L, d), lambda b, c: (b, c, 0)), st],
        out_shape=[jax.ShapeDtypeStruct((batch, seq, d), BF16),
                   jax.ShapeDtypeStruct((batch, RW_PAIRS, LANES, LANES), F32)],
        scratch_shapes=[pltpu.VMEM((RW_PAIRS, LANES, LANES), F32), pltpu.VMEM((1, wcols), F32)],
        compiler_params=_cparams("parallel", "arbitrary"),
        name="rwkv7",
    )(rw3, shift0, prm["mu"], prm["lw"], prm["w0"], prm["a0"], prm["k_k"], prm["k_a"], prm["r_k"],
      prm["ln_g"], prm["ln_b"], h0_bd, gm, tri)


def _state_to_bd(s):
    b = s.shape[0]
    ht = jnp.swapaxes(s, -1, -2).reshape(b, RW_PAIRS, 2, RW_HEAD, RW_HEAD)
    z = jnp.zeros_like(ht[:, :, 0])
    top = jnp.concatenate([ht[:, :, 0], z], axis=-1)
    bot = jnp.concatenate([z, ht[:, :, 1]], axis=-1)
    return jnp.concatenate([top, bot], axis=-2)


def _bd_to_state(hbd):
    b = hbd.shape[0]
    h0 = hbd[:, :, 0:RW_HEAD, 0:RW_HEAD]
    h1 = hbd[:, :, RW_HEAD:, RW_HEAD:]
    ht = jnp.stack([h0, h1], axis=2).reshape(b, 2 * RW_PAIRS, RW_HEAD, RW_HEAD)
    return jnp.swapaxes(ht, -1, -2)


def _route(logits):
    lane = lax.broadcasted_iota(jnp.int32, logits.shape, 1).astype(F32)
    big = float(LANES)
    lg = jnp.where(lane < N_GROUPS, logits, NEG_INF)
    mg = jnp.max(lg, axis=-1, keepdims=True)
    sg = jnp.sum(jnp.exp(lg - mg), axis=-1, keepdims=True)
    p_top = 1.0 / sg
    g_idx = jnp.min(jnp.where(lg == mg, lane, big), axis=-1, keepdims=True)
    e0 = N_GROUPS + EXPERTS_PER_GROUP * g_idx
    emask = jnp.where(lane >= e0, jnp.where(lane < e0 + EXPERTS_PER_GROUP, 1.0, 0.0), 0.0) > 0.5
    le = jnp.where(emask, logits, NEG_INF)
    me = jnp.max(le, axis=-1, keepdims=True)
    ee = jnp.exp(le - me)
    pe = ee / jnp.sum(ee, axis=-1, keepdims=True)
    pe = jnp.where(emask, pe, -1.0)
    v1 = jnp.max(pe, axis=-1, keepdims=True)
    i1 = jnp.min(jnp.where(pe == v1, lane, big), axis=-1, keepdims=True)
    pe2 = jnp.where(lane == i1, -1.0, pe)
    v2 = jnp.max(pe2, axis=-1, keepdims=True)
    i2 = jnp.min(jnp.where(pe2 == v2, lane, big), axis=-1, keepdims=True)
    den = v1 + v2
    ew = jnp.where(lane == i1, v1 / den, 0.0) + jnp.where(lane == i2, v2 / den, 0.0)
    return p_top * ew


def _merge_kernel(oa_ref, ob_ref, gates_ref, x_ref, gt1_ref, sc2_ref, sh2_ref, g2_ref,
                  woa_ref, wob_ref, wout_ref, wr_ref, br_ref, x1_ref, h2_ref, comb_ref):
    d = x_ref.shape[1]
    a_out = _dot(oa_ref[...], woa_ref[...])
    b_out = _dot(ob_ref[...], wob_ref[...])
    merged = gates_ref[:, 0:d].astype(F32) * a_out + gates_ref[:, d:2 * d].astype(F32) * b_out
    x1 = x_ref[...] + gt1_ref[...] * _dot(merged.astype(BF16), wout_ref[...])
    x1_ref[...] = x1
    ms = jnp.mean(x1 * x1, axis=-1, keepdims=True)
    h2 = x1 * lax.rsqrt(ms + EPS) * g2_ref[...]
    h2 = h2 * (1.0 + sc2_ref[...]) + sh2_ref[...]
    h2_ref[...] = h2.astype(h2_ref.dtype)
    logits = _mm3(h2, wr_ref[...]) + br_ref[...]
    comb_ref[...] = _route(logits)


def _merge(oa, ob, gates, x2d, gt1, sc2, sh2, g2, w_oa, w_ob, w_out, wr, br, tm, seq):
    t, d = x2d.shape
    row = lambda i: (i, 0)
    const = lambda i: (0, 0)
    full = lambda w: pl.BlockSpec(w.shape, const)
    return pl.pallas_call(
        _merge_kernel,
        grid=(t // tm,),
        in_specs=[pl.BlockSpec((tm, d), row), pl.BlockSpec((tm, d), row), pl.BlockSpec((tm, 2 * d), row),
                  pl.BlockSpec((tm, d), row), _mod_spec(gt1, tm, seq), _mod_spec(sc2, tm, seq),
                  _mod_spec(sh2, tm, seq), pl.BlockSpec((1, d), const),
                  full(w_oa), full(w_ob), full(w_out), full(wr), full(br)],
        out_specs=[pl.BlockSpec((tm, d), row), pl.BlockSpec((tm, d), row), pl.BlockSpec((tm, LANES), row)],
        out_shape=[jax.ShapeDtypeStruct((t, d), F32), jax.ShapeDtypeStruct((t, d), BF16),
                   jax.ShapeDtypeStruct((t, LANES), F32)],
        compiler_params=_cparams("parallel"),
        name="merge_router",
    )(oa, ob, gates, x2d, gt1, sc2, sh2, g2, w_oa, w_ob, w_out, wr, br)


def _moe_kernel(h2_ref, comb_ref, w13_ref, w2_ref, x1_ref, gt2_ref, y_ref, acc_scr):
    e = pl.program_id(1)

    @pl.when(e == 0)
    def _():
        acc_scr[...] = jnp.zeros(acc_scr.shape, F32)

    au = _dot(h2_ref[...], w13_ref[...])
    f = au.shape[1] // 2
    a = au[:, 0:f]
    u = au[:, f:2 * f]
    comb = comb_ref[...]
    lane = lax.broadcasted_iota(jnp.int32, comb.shape, 1)
    cw = jnp.sum(jnp.where(lane == e + N_GROUPS, comb, 0.0), axis=-1, keepdims=True)
    act = a * jax.nn.sigmoid(a) * u * cw
    acc_scr[...] += _dot(act.astype(BF16), w2_ref[...])

    @pl.when(e == pl.num_programs(1) - 1)
    def _():
        y_ref[...] = x1_ref[...] + gt2_ref[...] * acc_scr[...]


def _moe(h2, comb, w13, w2, x1, gt2, tm, seq):
    t, d = x1.shape
    ne, _, f2 = w13.shape
    row = lambda i, e: (i, 0)
    return pl.pallas_call(
        _moe_kernel,
        grid=(t // tm, ne),
        in_specs=[pl.BlockSpec((tm, d), row), pl.BlockSpec((tm, LANES), row),
                  pl.BlockSpec((None, d, f2), lambda i, e: (e, 0, 0)),
                  pl.BlockSpec((None, f2 // 2, d), lambda i, e: (e, 0, 0)),
                  pl.BlockSpec((tm, d), row), _mod_spec(gt2, tm, seq)],
        out_specs=pl.BlockSpec((tm, d), row),
        out_shape=jax.ShapeDtypeStruct((t, d), F32),
        scratch_shapes=[pltpu.VMEM((tm, d), F32)],
        compiler_params=_cparams("parallel", "arbitrary"),
        name="moe",
    )(h2, comb, w13, w2, x1, gt2)


def _rope_tables(pos0, seq, reps):
    pos = (pos0 + jnp.arange(seq)).astype(F32)
    inv = ROPE_THETA ** (-jnp.arange(0, DA_HEAD_DIM, 2, dtype=F32) / DA_HEAD_DIM)
    ang = pos[:, None] * inv[None, :]
    cos, sin = jnp.cos(ang), jnp.sin(ang)
    cos_t = jnp.tile(jnp.concatenate([cos, cos], axis=-1), (reps, LANES // DA_HEAD_DIM))
    sin_t = jnp.tile(jnp.concatenate([-sin, sin], axis=-1), (reps, LANES // DA_HEAD_DIM))
    return cos_t, sin_t


def _layer(x, mod, pos0, past, wts):
    bx, sx, d = x.shape
    t = bx * sx
    x2d = x.reshape(t, d)
    per_batch = sx >= 512
    tm = 512 if per_batch else t

    def modv(i):
        if per_batch:
            return mod[:, i:i + 1, :]
        return jnp.repeat(mod[:, i, :], sx, axis=0)[None]

    sh1, sc1, gt1, sh2, sc2, gt2 = (modv(i) for i in range(6))
    cos_t, sin_t = _rope_tables(pos0, sx, 1 if per_batch else bx)

    h = _prenorm(x2d, sc1, sh1, wts["norm1_g"], tm, sx)
    q, k32, kb, v32, vb = _qkv(h, wts["w_qkv"], wts["qg"], wts["kg"], cos_t, sin_t, wts["gm"], tm)
    rw = _proj(h, wts["w_rw"], tm, wts["w_rw"].shape[1] // 3, None, F32, "rw_proj")
    gates = _proj(h, wts["w_gate"], tm, d, "sigmoid", BF16, "gate_proj")

    if past is None:
        oa = _attn_prompt(q, kb, vb, wts["lam"], wts["subln_g"], bx, sx, min(512, sx), min(1024, sx))
        shift0 = jnp.zeros((bx, 1, rw.shape[1]), F32)
        h0_bd = jnp.zeros((bx, RW_PAIRS, LANES, LANES), F32)
    else:
        ck, cv, s0, sh0 = past
        oa = _attn_sample(q, kb, vb, ck, cv, wts["lam"], wts["subln_g"], bx, sx)
        shift0 = jnp.pad(sh0, ((0, 0), (0, 0), (0, rw.shape[1] - sh0.shape[-1])))
        h0_bd = _state_to_bd(s0.astype(F32))

    rw3 = rw.reshape(bx, sx, rw.shape[1])
    if sx % RW_L == 0:
        rw_in, n_valid = rw3, RW_L
    else:
        assert sx < RW_L
        rw_in, n_valid = jnp.pad(rw3, ((0, 0), (0, RW_L - sx), (0, 0))), sx
    ob, h_bd = _rwkv(rw_in, shift0, h0_bd, wts, wts["gm"], n_valid)
    ob = ob[:, 0:sx, :].reshape(t, d)

    x1, h2, comb = _merge(oa, ob, gates, x2d, gt1, sc2, sh2, wts["norm2_g"],
                          wts["w_oa"], wts["w_ob"], wts["w_out"], wts["wr"], wts["br"], tm, sx)
    tm_moe = 1024 if (per_batch and sx % 1024 == 0) else tm
    y = _moe(h2, comb, wts["w13"], wts["w2"], x1, gt2, tm_moe, sx)

    n_cols = 3 * d + W_LORA + A_LORA + G_LORA
    return (y.reshape(bx, sx, d),
            k32.reshape(1, bx, sx, DA_HEADS, 2 * DA_HEAD_DIM),
            v32.reshape(1, bx, sx, DA_HEADS, 2 * DA_HEAD_DIM),
            _bd_to_state(h_bd)[None],
            rw3[:, sx - 1:sx, 0:n_cols][None])


def kernel(x_prompt, x_sample, cache_k, cache_v, state_wkv, state_shift, c_prompt, c_sample, norm1_g, norm2_g, w_ada, b_ada, w_in, da_qn_g, da_kn_g, da_lambda, da_subln_g, w_oa, rw_mu, rw_w0, rw_w2, rw_a0, rw_a2, rw_g2, rw_k_k, rw_k_a, rw_r_k, rw_ln_g, rw_ln_b, w_ob, w_out, router_g, router_g_b, router_e, router_e_b, exp_w1, exp_w3, exp_w2):
    b, s, d = x_prompt.shape
    bs = x_sample.shape[0]
    past_len = cache_k.shape[2]
    assert w_in.shape[0] == 1, "single layer"

    c_all = jnp.concatenate([c_prompt, c_sample], axis=0)
    c_all = jnp.pad(c_all, ((0, (-c_all.shape[0]) % 8), (0, 0)))
    mod, lam_tile = _adaln(c_all, w_ada[0], b_ada[0], da_lambda[0])
    mod_p = mod[0:b].reshape(b, 6, d)
    mod_s = mod[b:b + bs].reshape(bs, 6, d)

    win = w_in[0]
    rw_cols = 3 * d + W_LORA + A_LORA + G_LORA
    rw_pad = 3 * d + LORA_PAD
    w_rw = jnp.pad(win[:, 3 * d:3 * d + rw_cols], ((0, 0), (0, rw_pad - rw_cols))).astype(BF16)
    lw = jnp.zeros((LORA_PAD, 3 * d), F32)
    lw = lw.at[0:W_LORA, 0:d].set(rw_w2[0])
    lw = lw.at[W_LORA:W_LORA + A_LORA, d:2 * d].set(rw_a2[0])
    lw = lw.at[W_LORA + A_LORA:W_LORA + A_LORA + G_LORA, 2 * d:3 * d].set(rw_g2[0])
    f = exp_w1.shape[-1]
    wr = jnp.pad(jnp.concatenate([router_g[0], router_e[0]], axis=1), ((0, 0), (0, LANES - N_GROUPS - N_EXPERTS)))
    br = jnp.pad(jnp.concatenate([router_g_b[0], router_e_b[0]]), (0, LANES - N_GROUPS - N_EXPERTS))[None]
    half = jnp.ones((DA_HEAD_DIM, DA_HEAD_DIM), F32) / DA_HEAD_DIM
    wts = dict(
        norm1_g=norm1_g[0][None], norm2_g=norm2_g[0][None],
        w_qkv=win[:, 0:3 * d].astype(BF16), w_rw=w_rw, w_gate=win[:, 3 * d + rw_cols:].astype(BF16),
        qg=jnp.tile(da_qn_g[0], 2)[None], kg=jnp.tile(da_kn_g[0], 2)[None],
        gm=jnp.kron(jnp.eye(2, dtype=F32), half).astype(BF16),
        lam=lam_tile[0:1], subln_g=da_subln_g[0][None],
        mu=jnp.pad(rw_mu[0], (0, rw_pad - rw_cols))[None], lw=lw.astype(BF16),
        w0=rw_w0[0][None], a0=rw_a0[0][None], k_k=rw_k_k[0][None], k_a=rw_k_a[0][None],
        r_k=rw_r_k[0].reshape(1, d), ln_g=rw_ln_g[0][None], ln_b=rw_ln_b[0][None],
        w_oa=w_oa[0].astype(BF16), w_ob=w_ob[0].astype(BF16), w_out=w_out[0].astype(BF16),
        wr=wr, br=br,
        w13=jnp.concatenate([exp_w1[0], exp_w3[0]], axis=-1).reshape(N_EXPERTS, d, 2 * f).astype(BF16),
        w2=exp_w2[0].reshape(N_EXPERTS, f, d).astype(BF16),
    )

    out_p = _layer(x_prompt, mod_p, 0, None, wts)
    out_s = _layer(x_sample, mod_s, past_len,
                   (cache_k[0], cache_v[0], state_wkv[0], state_shift[0]), wts)
    return (out_p[0], out_s[0], out_p[1], out_p[2], out_p[3], out_p[4],
            out_s[1], out_s[2], out_s[3], out_s[4])
```

```python
import functools
import math

import jax
import jax.numpy as jnp
from jax import lax
from jax.experimental import pallas as pl
from jax.experimental.pallas import tpu as pltpu

F32 = jnp.float32
BF16 = jnp.bfloat16

EPS = 1e-6
NEG_INF = -1e30
CHUNK = 64
DA_HEADS = 8
DA_HEAD_DIM = 64
ROPE_THETA = 10000.0
RW_HEAD = 64
RW_GN_EPS = 64e-5
W_LORA, A_LORA, G_LORA = 64, 64, 160
N_GROUPS, EXPERTS_PER_GROUP = 4, 8
N_EXPERTS = N_GROUPS * EXPERTS_PER_GROUP
LAM_INIT = 0.8 - 0.6 * math.exp(-0.3 * 0)

LANES = 128
LORA_PAD = 384
VMEM_LIMIT = 56 * 1024 * 1024


def _cparams(*sem):
    return pltpu.CompilerParams(dimension_semantics=sem, vmem_limit_bytes=VMEM_LIMIT)


def _dot(a, b):
    return jnp.dot(a, b, preferred_element_type=F32)


def _dot_nt(a, b):
    return lax.dot_general(a, b, (((1,), (1,)), ((), ())), preferred_element_type=F32)


def _dot_tn(a, b):
    return lax.dot_general(a, b, (((0,), (0,)), ((), ())), preferred_element_type=F32)


def _split2(a):
    hi = a.astype(BF16)
    lo = (a - hi.astype(F32)).astype(BF16)
    return hi, lo


def _split3(a):
    hi = a.astype(BF16)
    r = a - hi.astype(F32)
    mid = r.astype(BF16)
    lo = (r - mid.astype(F32)).astype(BF16)
    return hi, mid, lo


def _group_mean(sq, gm):
    hi, lo = _split2(sq)
    return _dot(hi, gm) + _dot(lo, gm)


def _adaln_kernel(c_ref, w_ref, b_ref, l_ref, o_ref, lam_ref):
    c = c_ref[...]
    sc = (c * jax.nn.sigmoid(c)).astype(BF16)
    o_ref[...] = _dot(sc, w_ref[...].astype(BF16)) + b_ref[...]
    l = l_ref[...]
    s1 = jnp.sum(l[0:1] * l[1:2], axis=-1, keepdims=True)
    s2 = jnp.sum(l[2:3] * l[3:4], axis=-1, keepdims=True)
    lam = jnp.exp(s1) - jnp.exp(s2) + LAM_INIT
    lam_ref[...] = jnp.broadcast_to(lam, lam_ref.shape)


def _adaln(c_all, w_ada, b_ada, da_lambda):
    bp, d = c_all.shape
    n = w_ada.shape[1]
    tn = 1024
    return pl.pallas_call(
        _adaln_kernel,
        grid=(n // tn,),
        in_specs=[pl.BlockSpec((bp, d), lambda j: (0, 0)),
                  pl.BlockSpec((d, tn), lambda j: (0, j)),
                  pl.BlockSpec((1, tn), lambda j: (0, j)),
                  pl.BlockSpec(da_lambda.shape, lambda j: (0, 0))],
        out_specs=[pl.BlockSpec((bp, tn), lambda j: (0, j)),
                   pl.BlockSpec((8, LANES), lambda j: (0, 0))],
        out_shape=[jax.ShapeDtypeStruct((bp, n), F32), jax.ShapeDtypeStruct((8, LANES), F32)],
        compiler_params=_cparams("arbitrary"),
        name="adaln",
    )(c_all, w_ada, b_ada.reshape(1, n), da_lambda)


def _prenorm_kernel(x_ref, sc_ref, sh_ref, g_ref, o_ref):
    x = x_ref[...]
    ms = jnp.mean(x * x, axis=-1, keepdims=True)
    h = x * lax.rsqrt(ms + EPS) * g_ref[...]
    o_ref[...] = (h * (1.0 + sc_ref[...]) + sh_ref[...]).astype(o_ref.dtype)


def _mod_spec(mod, tm, seq):
    d = mod.shape[-1]
    if mod.shape[1] == 1:
        per = seq // tm
        return pl.BlockSpec((None, 1, d), lambda i, *_: (i // per, 0, 0))
    return pl.BlockSpec((None, tm, d), lambda i, *_: (0, i, 0))


def _prenorm(x2d, sc, sh, g, tm, seq):
    t, d = x2d.shape
    return pl.pallas_call(
        _prenorm_kernel,
        grid=(t // tm,),
        in_specs=[pl.BlockSpec((tm, d), lambda i: (i, 0)), _mod_spec(sc, tm, seq), _mod_spec(sh, tm, seq),
                  pl.BlockSpec((1, d), lambda i: (0, 0))],
        out_specs=pl.BlockSpec((tm, d), lambda i: (i, 0)),
        out_shape=jax.ShapeDtypeStruct((t, d), BF16),
        compiler_params=_cparams("parallel"),
        name="prenorm",
    )(x2d, sc, sh, g)


def _qkv_kernel(h_ref, w_ref, qg_ref, kg_ref, cos_ref, sin_ref, gm_ref,
                q_ref, k_ref, kb_ref, v_ref, vb_ref):
    j = pl.program_id(1)
    acc = _dot(h_ref[...], w_ref[...])
    n_slab = acc.shape[1] // LANES

    def norm_rope(gain_ref, scale, write):
        lane = lax.broadcasted_iota(jnp.int32, (acc.shape[0], LANES), 1)
        first_half = (lane % DA_HEAD_DIM) < (DA_HEAD_DIM // 2)
        for c in range(n_slab):
            a = acc[:, c * LANES:(c + 1) * LANES]
            ms = _group_mean(a * a, gm_ref[...])
            y = a * lax.rsqrt(ms + EPS) * gain_ref[...]
            rot = jnp.where(first_half, pltpu.roll(y, LANES - DA_HEAD_DIM // 2, 1),
                            pltpu.roll(y, DA_HEAD_DIM // 2, 1))
            write(c, (y * cos_ref[...] + rot * sin_ref[...]) * scale)

    @pl.when(j == 0)
    def _():
        def write(c, val):
            q_ref[:, c * LANES:(c + 1) * LANES] = val.astype(q_ref.dtype)
        norm_rope(qg_ref, DA_HEAD_DIM ** -0.5 * math.log2(math.e), write)

    @pl.when(j == 1)
    def _():
        def write(c, val):
            k_ref[:, c * LANES:(c + 1) * LANES] = val
            kb_ref[:, c * LANES:(c + 1) * LANES] = val.astype(BF16)
        norm_rope(kg_ref, 1.0, write)

    @pl.when(j == 2)
    def _():
        v_ref[...] = acc
        vb_ref[...] = acc.astype(BF16)


def _qkv(h, w_qkv, qg, kg, cos_t, sin_t, gm, tm):
    t, d = h.shape
    n = 1024
    per = cos_t.shape[0] // tm
    row = lambda i, j: (i, 0)
    return pl.pallas_call(
        _qkv_kernel,
        grid=(t // tm, 3),
        in_specs=[pl.BlockSpec((tm, d), row),
                  pl.BlockSpec((d, n), lambda i, j: (0, j)),
                  pl.BlockSpec((1, LANES), lambda i, j: (0, 0)),
                  pl.BlockSpec((1, LANES), lambda i, j: (0, 0)),
                  pl.BlockSpec((tm, LANES), lambda i, j: (i % per, 0)),
                  pl.BlockSpec((tm, LANES), lambda i, j: (i % per, 0)),
                  pl.BlockSpec((LANES, LANES), lambda i, j: (0, 0))],
        out_specs=[pl.BlockSpec((tm, n), row)] * 5,
        out_shape=[jax.ShapeDtypeStruct((t, n), BF16), jax.ShapeDtypeStruct((t, n), F32),
                   jax.ShapeDtypeStruct((t, n), BF16), jax.ShapeDtypeStruct((t, n), F32),
                   jax.ShapeDtypeStruct((t, n), BF16)],
        compiler_params=_cparams("parallel", "arbitrary"),
        name="qkv_proj",
    )(h, w_qkv, qg, kg, cos_t, sin_t, gm)


def _proj_kernel(h_ref, w_ref, o_ref, *, act):
    acc = _dot(h_ref[...], w_ref[...])
    if act == "sigmoid":
        acc = jax.nn.sigmoid(acc)
    o_ref[...] = acc.astype(o_ref.dtype)


def _proj(h, w, tm, tn, act, out_dtype, name):
    t, d = h.shape
    n = w.shape[1]
    return pl.pallas_call(
        functools.partial(_proj_kernel, act=act),
        grid=(t // tm, n // tn),
        in_specs=[pl.BlockSpec((tm, d), lambda i, j: (i, 0)), pl.BlockSpec((d, tn), lambda i, j: (0, j))],
        out_specs=pl.BlockSpec((tm, tn), lambda i, j: (i, j)),
        out_shape=jax.ShapeDtypeStruct((t, n), out_dtype),
        compiler_params=_cparams("parallel", "arbitrary"),
        name=name,
    )(h, w)


def _subln(o, g):
    ms = jnp.mean(o * o, axis=-1, keepdims=True)
    return o * lax.rsqrt(ms + EPS) * g * (1.0 - LAM_INIT)


ATT_ROWS = 128


def _attn_prompt_kernel(qi_tab, ki_tab, lam_ref, q_ref, k_ref, v_ref, g_ref, o_ref,
                        qs_scr, vx_scr, m_scr, acc_scr, *, tq, tk):
    p = pl.program_id(2)
    qi = qi_tab[p]
    ki = ki_tab[p]
    ratio = tk // tq
    n_slab = tk // LANES

    @pl.when(ki == 0)
    def _():
        q = q_ref[...]
        lane = lax.broadcasted_iota(jnp.int32, q.shape, 1)
        zero = jnp.zeros_like(q)
        qs_scr[0:tq, :] = jnp.where(lane < DA_HEAD_DIM, q, zero)
        qs_scr[tq:2 * tq, :] = jnp.where(lane >= DA_HEAD_DIM, q, zero)
        m_scr[...] = jnp.full(m_scr.shape, NEG_INF, F32)
        acc_scr[...] = jnp.zeros(acc_scr.shape, F32)
        vx_scr[:, LANES:2 * LANES] = jnp.ones((tk, LANES), BF16)

    vx_scr[:, 0:LANES] = v_ref[...]

    def step(masked):
        for rb in range(2 * tq // ATT_ROWS):
            rows = slice(rb * ATT_ROWS, (rb + 1) * ATT_ROWS)
            s = _dot_nt(qs_scr[rows, :], k_ref[...])
            slabs = [s[:, c * LANES:(c + 1) * LANES] for c in range(n_slab)]
            if masked:
                row = lax.broadcasted_iota(jnp.int32, (ATT_ROWS, LANES), 0) + rb * ATT_ROWS
                lane = lax.broadcasted_iota(jnp.int32, (ATT_ROWS, LANES), 1)
                qchunk = (row % tq) // CHUNK + (qi % ratio) * (tq // CHUNK)
                slabs = [jnp.where((c * LANES + lane) // CHUNK <= qchunk, sl, NEG_INF)
                         for c, sl in enumerate(slabs)]
            mt = slabs[0]
            for sl in slabs[1:]:
                mt = jnp.maximum(mt, sl)
            m_prev = m_scr[rows, :]
            m_new = jnp.maximum(m_prev, jnp.max(mt, axis=-1, keepdims=True))
            alpha = jnp.exp2(m_prev - m_new)
            e = jnp.concatenate([jnp.exp2(sl - m_new).astype(BF16) for sl in slabs], axis=1)
            pv = _dot(e, vx_scr[...])
            acc_scr[rows, :] = jnp.concatenate([alpha, alpha], axis=1) * acc_scr[rows, :] + pv
            m_scr[rows, :] = m_new

    @pl.when(ki < qi // ratio)
    def _():
        step(False)

    @pl.when(ki == qi // ratio)
    def _():
        step(True)
        o = acc_scr[:, 0:LANES] / acc_scr[:, LANES:2 * LANES]
        o = o[0:tq] - lam_ref[...] * o[tq:2 * tq]
        o_ref[...] = _subln(o, g_ref[...]).astype(o_ref.dtype)


def _attn_prompt(q, k, v, lam_row, subln_g, batch, seq, tq, tk):
    d = q.shape[1]
    assert tk % tq == 0 and seq % tk == 0 and tq % CHUNK == 0
    qi_l, ki_l = [], []
    for a in range(seq // tq):
        for b in range(a * tq // tk + 1):
            qi_l.append(a)
            ki_l.append(b)
    qi_tab = jnp.asarray(qi_l, jnp.int32)
    ki_tab = jnp.asarray(ki_l, jnp.int32)
    q3, k3, v3 = (a.reshape(batch, seq, d) for a in (q, k, v))
    qspec = pl.BlockSpec((None, tq, LANES), lambda b, h, p, qt, kt: (b, qt[p], h))
    kspec = pl.BlockSpec((None, tk, LANES), lambda b, h, p, qt, kt: (b, kt[p], h))
    vec = pl.BlockSpec((1, LANES), lambda b, h, p, qt, kt: (0, 0))
    out = pl.pallas_call(
        functools.partial(_attn_prompt_kernel, tq=tq, tk=tk),
        grid_spec=pltpu.PrefetchScalarGridSpec(
            num_scalar_prefetch=2,
            grid=(batch, DA_HEADS, len(qi_l)),
            in_specs=[vec, qspec, kspec, kspec, vec],
            out_specs=qspec,
            scratch_shapes=[pltpu.VMEM((2 * tq, LANES), BF16), pltpu.VMEM((tk, 2 * LANES), BF16),
                            pltpu.VMEM((2 * tq, LANES), F32), pltpu.VMEM((2 * tq, 2 * LANES), F32)]),
        out_shape=jax.ShapeDtypeStruct((batch, seq, d), BF16),
        compiler_params=_cparams("parallel", "parallel", "arbitrary"),
        name="attn_prompt",
    )(qi_tab, ki_tab, lam_row, q3, k3, v3, subln_g)
    return out.reshape(batch * seq, d)


def _attn_sample_kernel(lam_ref, q_ref, kn_ref, vn_ref, ck_ref, cv_ref, g_ref, o_ref, *, past, sq):
    for h in range(DA_HEADS):
        hs = slice(h * LANES, (h + 1) * LANES)
        q = q_ref[:, hs]
        lane = lax.broadcasted_iota(jnp.int32, q.shape, 1)
        zero = jnp.zeros_like(q)
        qs = jnp.concatenate([jnp.where(lane < DA_HEAD_DIM, q, zero),
                              jnp.where(lane >= DA_HEAD_DIM, q, zero)], axis=0)
        kc = ck_ref[pl.ds(h, past, stride=DA_HEADS), :].astype(BF16)
        vc = cv_ref[pl.ds(h, past, stride=DA_HEADS), :].astype(BF16)
        s_c = _dot_nt(qs, kc)
        s_n = _dot_nt(qs, kn_ref[:, hs])

        def masked(s, key0):
            row = lax.broadcasted_iota(jnp.int32, s.shape, 0)
            col = lax.broadcasted_iota(jnp.int32, s.shape, 1)
            qchunk = (past + row % sq) // CHUNK
            return jnp.where((key0 + col) // CHUNK <= qchunk, s, NEG_INF)

        s_c = masked(s_c, 0)
        s_n = masked(s_n, past)
        m = jnp.maximum(jnp.max(s_c, axis=-1, keepdims=True), jnp.max(s_n, axis=-1, keepdims=True))
        e_c = jnp.exp2(s_c - m)
        e_n = jnp.exp2(s_n - m)
        l = jnp.sum(e_c, axis=-1, keepdims=True) + jnp.sum(e_n, axis=-1, keepdims=True)
        o = (_dot(e_c.astype(BF16), vc) + _dot(e_n.astype(BF16), vn_ref[:, hs])) / l
        o = o[0:sq] - lam_ref[...] * o[sq:2 * sq]
        o_ref[:, hs] = _subln(o, g_ref[...]).astype(o_ref.dtype)


def _attn_sample(q, kn, vn, cache_k, cache_v, lam_row, subln_g, batch, sq):
    d = q.shape[1]
    past = cache_k.shape[-3]
    ck = cache_k.reshape(batch, past * DA_HEADS, LANES)
    cv = cache_v.reshape(batch, past * DA_HEADS, LANES)
    q3, k3, v3 = (a.reshape(batch, sq, d) for a in (q, kn, vn))
    new = pl.BlockSpec((None, sq, d), lambda b: (b, 0, 0))
    old = pl.BlockSpec((None, past * DA_HEADS, LANES), lambda b: (b, 0, 0))
    vec = pl.BlockSpec((1, LANES), lambda b: (0, 0))
    out = pl.pallas_call(
        functools.partial(_attn_sample_kernel, past=past, sq=sq),
        grid=(batch,),
        in_specs=[vec, new, new, new, old, old, vec],
        out_specs=new,
        out_shape=jax.ShapeDtypeStruct((batch, sq, d), BF16),
        compiler_params=_cparams("parallel"),
        name="attn_sample",
    )(lam_row, q3, k3, v3, ck, cv, subln_g)
    return out.reshape(batch * sq, d)


RW_L = 64
RW_PAIRS = 8


def _mm3(a, b, dot=_dot):
    ah, al = _split2(a)
    bh, bl = _split2(b)
    return dot(ah, bh) + dot(ah, bl) + dot(al, bh)


def _terms(x, n):
    return (x.astype(BF16),) if n == 1 else _split2(x)


def _mmt(at, bt, dot=_dot):
    acc = dot(at[0], bt[0])
    if len(bt) > 1:
        acc = acc + dot(at[0], bt[1])
    if len(at) > 1:
        acc = acc + dot(at[1], bt[0])
    return acc


RW_PREC = dict(a_all=(1, 1), neumann=(1, 1), av=(1, 1), pu=(1, 1), qy=(1, 1), mh=(1, 1), state=(2, 2))


def _softplus(z):
    return jnp.maximum(z, 0.0) + jnp.log(1.0 + jnp.exp(-jnp.abs(z)))


def _rwkv_kernel(rw_ref, sh0_ref, mu_ref, lw_ref, w0_ref, a0_ref, kk_ref, ka_ref, rk_ref, lng_ref, lnb_ref,
                 h0_ref, gm_ref, tri_ref, ob_ref, hout_ref, h_scr, last_scr, *, n_valid):
    L = RW_L
    c = pl.program_id(1)
    d = ob_ref.shape[-1]

    @pl.when(c == 0)
    def _():
        h_scr[...] = h0_ref[...]
        last_scr[...] = sh0_ref[...]

    rw = rw_ref[...]
    row_w = lax.broadcasted_iota(jnp.int32, rw.shape, 0)
    prev = jnp.where(row_w == 0, last_scr[...], pltpu.roll(rw, 1, 0))
    last_scr[...] = rw[L - 1:L, :]
    xm = rw + (prev - rw) * mu_ref[...]
    r = xm[:, 0:d]
    k = xm[:, d:2 * d]
    v = xm[:, 2 * d:3 * d]
    lo = xm[:, 3 * d:3 * d + LORA_PAD]
    lane_l = lax.broadcasted_iota(jnp.int32, lo.shape, 1)
    z = jnp.where(lane_l < W_LORA, jnp.tanh(lo),
                  jnp.where(lane_l < W_LORA + A_LORA, lo, jax.nn.sigmoid(lo)))
    lora = _dot(z.astype(BF16), lw_ref[...])
    w = -_softplus(-(w0_ref[...] + lora[:, 0:d])) - 0.5
    ld = -jnp.exp(w)
    a = jax.nn.sigmoid(a0_ref[...] + lora[:, d:2 * d])
    g = lora[:, 2 * d:3 * d]
    kkr = k * kk_ref[...]
    kmod = k * (1.0 + (a - 1.0) * ka_ref[...])
    rkk = r * kmod * rk_ref[...]
    gm = gm_ref[...]
    if n_valid < L:
        valid = lax.broadcasted_iota(jnp.int32, ld.shape, 0) < n_valid
        ld = jnp.where(valid, ld, 0.0)
        kkr = jnp.where(valid, kkr, 0.0)
        kmod = jnp.where(valid, kmod, 0.0)

    tri = tri_ref[...]
    l1, l2, l3 = _split3(ld)
    cin = _dot(tri, l1) + _dot(tri, l2) + _dot(tri, l3)
    cex = cin - ld
    c_last = cin[L - 1:L, :]
    e_in = jnp.exp(cin)
    e_ex = jnp.exp(cex)
    e_neg = jnp.exp(-cin)
    e_tail = jnp.exp(c_last - cin)
    g_last = jnp.exp(c_last)

    row = lax.broadcasted_iota(jnp.int32, (L, LANES), 0)
    lane = lax.broadcasted_iota(jnp.int32, (L, LANES), 1)
    m_lo = lane < RW_HEAD
    tri_s = row > (lane % L)
    tri_i = row >= (lane % L)
    eye_pair = jnp.where(row == (lane % L), 1.0, 0.0).astype(F32)
    r2 = lax.broadcasted_iota(jnp.int32, (LANES, LANES), 0)
    c2 = lax.broadcasted_iota(jnp.int32, (LANES, LANES), 1)
    bd_mask = (r2 < RW_HEAD) == (c2 < RW_HEAD)
    diag_mask = r2 == c2

    def bd(x):
        z0 = jnp.zeros_like(x)
        return jnp.concatenate([jnp.where(m_lo, x, z0), jnp.where(m_lo, z0, x)], axis=0)

    def bdt(t):
        return tuple(bd(x) for x in t)

    def cat(ts, axis):
        return tuple(jnp.concatenate(xs, axis=axis) for xs in zip(*ts))

    prec = RW_PREC
    pairs = range(RW_PAIRS)
    sls = [slice(p * LANES, (p + 1) * LANES) for p in pairs]

    v_p = [v[:, sl] for sl in sls]
    kk_p = []
    for sl in sls:
        kk_raw = kkr[:, sl]
        ss = _group_mean(kk_raw * kk_raw, gm) * RW_HEAD
        kk_p.append(kk_raw / jnp.maximum(jnp.sqrt(ss), 1e-12))
    b_p = [kk_p[p] * a[:, sls[p]] for p in pairs]
    abar = [-kk_p[p] * e_ex[:, sls[p]] for p in pairs]
    rbar = [r[:, sls[p]] * e_in[:, sls[p]] for p in pairs]
    bbar = [b_p[p] * e_neg[:, sls[p]] for p in pairs]
    kbar = [kmod[:, sls[p]] * e_neg[:, sls[p]] for p in pairs]
    btil = [b_p[p] * e_tail[:, sls[p]] for p in pairs]
    ktil = [kmod[:, sls[p]] * e_tail[:, sls[p]] for p in pairs]

    na, nb = prec["a_all"]
    abar_t = [_terms(abar[p], max(na, prec["pu"][1])) for p in pairs]
    v_t = [_terms(v_p[p], max(prec["av"][1], prec["qy"][1], prec["mh"][1])) for p in pairs]
    a_all = []
    for p in pairs:
        lhs = cat([abar_t[p][:na], _terms(rbar[p], na)], 0)
        rhs = cat([bdt(_terms(bbar[p], nb)), bdt(_terms(kbar[p], nb))], 0)
        a_all.append(_mmt(lhs, rhs, _dot_nt))
    n_ab = [jnp.where(tri_s, a_all[p][0:L, 0:2 * L], 0.0) for p in pairs]
    a_ak = [jnp.where(tri_s, a_all[p][0:L, 2 * L:4 * L], 0.0) for p in pairs]
    a_rb = [jnp.where(tri_i, a_all[p][L:2 * L, 0:2 * L], 0.0) for p in pairs]
    a_rk = [jnp.where(tri_i, a_all[p][L:2 * L, 2 * L:4 * L], 0.0) for p in pairs]

    na, nb = prec["neumann"]
    t_inv = [eye_pair + n_ab[p] for p in pairs]
    pw = n_ab
    for _ in range(int(math.log2(L)) - 1):
        pw_t = [_terms(pw[p], max(na, nb)) for p in pairs]
        pw = [_mmt(pw_t[p][:na], bdt(pw_t[p][:nb])) for p in pairs]
        pw_b = [bdt(_terms(pw[p], nb)) for p in pairs]
        t_inv = [t_inv[p] + _mmt(_terms(t_inv[p], na), pw_b[p]) for p in pairs]

    na, nb = prec["av"]
    av = [_mmt(_terms(a_ak[p], na), bdt(v_t[p][:nb])) for p in pairs]
    na, nb = prec["pu"]
    pu = [_mmt(_terms(t_inv[p], na), cat([bdt(abar_t[p][:nb]), bdt(_terms(av[p], nb))], 1)) for p in pairs]
    p_m = [x[:, 0:LANES] for x in pu]
    u0 = [x[:, LANES:2 * LANES] for x in pu]
    na, nb = prec["qy"]
    nmh = prec["mh"][1]
    p_t = [_terms(p_m[p], max(nb, nmh)) for p in pairs]
    u_t = [_terms(u0[p], max(nb, nmh)) for p in pairs]
    q_m, y0 = [], []
    for p in pairs:
        zero = tuple(jnp.zeros((2 * L, LANES), BF16) for _ in range(nb))
        rhs2 = cat([cat([bdt(p_t[p][:nb]), bdt(u_t[p][:nb])], 1),
                    cat([zero, bdt(v_t[p][:nb])], 1)], 0)
        qy = _mmt(cat([_terms(a_rb[p], na), _terms(a_rk[p], na)], 1), rhs2)
        q_m.append(rbar[p] + qy[:, 0:LANES])
        y0.append(qy[:, LANES:2 * LANES])

    na, nb = prec["mh"]
    btil_t = [_terms(btil[p], na) for p in pairs]
    m_full = [_mmt(btil_t[p], p_t[p][:nb], _dot_tn) for p in pairs]
    h0_full = [_mmt(cat([btil_t[p], _terms(ktil[p], na)], 0), cat([u_t[p][:nb], v_t[p][:nb]], 0), _dot_tn)
               for p in pairs]
    m_bd = [jnp.where(bd_mask, m_full[p], 0.0) + jnp.where(diag_mask, g_last[:, sls[p]], 0.0) for p in pairs]
    h0_bd = [jnp.where(bd_mask, h0_full[p], 0.0) for p in pairs]

    na, nb = prec["state"]
    h_t = [_terms(h_scr[p], nb) for p in pairs]
    y = [_mmt(_terms(q_m[p], na), h_t[p]) + y0[p] for p in pairs]
    for p in pairs:
        h_scr[p] = _mmt(_terms(m_bd[p], na), h_t[p]) + h0_bd[p]

    for p in pairs:
        sl = sls[p]
        mu_y = _group_mean(y[p], gm)
        dy = y[p] - mu_y
        var = _group_mean(dy * dy, gm)
        yn = dy * lax.rsqrt(var + RW_GN_EPS) * lng_ref[:, sl] + lnb_ref[:, sl]
        bonus = _group_mean(rkk[:, sl], gm) * RW_HEAD * v_p[p]
        ob_ref[:, sl] = ((yn + bonus) * g[:, sl]).astype(ob_ref.dtype)

    @pl.when(c == pl.num_programs(1) - 1)
    def _():
        hout_ref[...] = h_scr[...]


def _rwkv(rw3, shift0, h0_bd, prm, gm, n_valid):
    batch, seq, wcols = rw3.shape
    d = prm["w0"].shape[1]
    nc = seq // RW_L
    tri = jnp.tril(jnp.ones((RW_L, RW_L), F32)).astype(BF16)
    const2 = lambda b, c: (0, 0)
    vec = pl.BlockSpec((1, d), const2)
    st = pl.BlockSpec((None, RW_PAIRS, LANES, LANES), lambda b, c: (b, 0, 0, 0))
    return pl.pallas_call(
        functools.partial(_rwkv_kernel, n_valid=n_valid),
        grid=(batch, nc),
        in_specs=[pl.BlockSpec((None, RW_L, wcols), lambda b, c: (b, c, 0)),
                  pl.BlockSpec((None, 1, wcols), lambda b, c: (b, 0, 0)),
                  pl.BlockSpec((1, wcols), const2),
                  pl.BlockSpec(prm["lw"].shape, const2),
                  vec, vec, vec, vec, vec, vec, vec, st,
                  pl.BlockSpec((LANES, LANES), const2),
                  pl.BlockSpec((RW_L, RW_L), const2)],
        out_specs=[pl.BlockSpec((None, RW_L, d), lambda b, c: (b, c, 0)), st],
        out_shape=[jax.ShapeDtypeStruct((batch, seq, d), BF16),
                   jax.ShapeDtypeStruct((batch, RW_PAIRS, LANES, LANES), F32)],
        scratch_shapes=[pltpu.VMEM((RW_PAIRS, LANES, LANES), F32), pltpu.VMEM((1, wcols), F32)],
        compiler_params=_cparams("parallel", "arbitrary"),
        name="rwkv7",
    )(rw3, shift0, prm["mu"], prm["lw"], prm["w0"], prm["a0"], prm["k_k"], prm["k_a"], prm["r_k"],
      prm["ln_g"], prm["ln_b"], h0_bd, gm, tri)


def _state_to_bd(s):
    b = s.shape[0]
    ht = jnp.swapaxes(s, -1, -2).reshape(b, RW_PAIRS, 2, RW_HEAD, RW_HEAD)
    z = jnp.zeros_like(ht[:, :, 0])
    top = jnp.concatenate([ht[:, :, 0], z], axis=-1)
    bot = jnp.concatenate([z, ht[:, :, 1]], axis=-1)
    return jnp.concatenate([top, bot], axis=-2)


def _bd_to_state(hbd):
    b = hbd.shape[0]
    h0 = hbd[:, :, 0:RW_HEAD, 0:RW_HEAD]
    h1 = hbd[:, :, RW_HEAD:, RW_HEAD:]
    ht = jnp.stack([h0, h1], axis=2).reshape(b, 2 * RW_PAIRS, RW_HEAD, RW_HEAD)
    return jnp.swapaxes(ht, -1, -2)


def _route(logits):
    lane = lax.broadcasted_iota(jnp.int32, logits.shape, 1).astype(F32)
    big = float(LANES)
    lg = jnp.where(lane < N_GROUPS, logits, NEG_INF)
    mg = jnp.max(lg, axis=-1, keepdims=True)
    sg = jnp.sum(jnp.exp(lg - mg), axis=-1, keepdims=True)
    p_top = 1.0 / sg
    g_idx = jnp.min(jnp.where(lg == mg, lane, big), axis=-1, keepdims=True)
    e0 = N_GROUPS + EXPERTS_PER_GROUP * g_idx
    emask = jnp.where(lane >= e0, jnp.where(lane < e0 + EXPERTS_PER_GROUP, 1.0, 0.0), 0.0) > 0.5
    le = jnp.where(emask, logits, NEG_INF)
    me = jnp.max(le, axis=-1, keepdims=True)
    ee = jnp.exp(le - me)
    pe = ee / jnp.sum(ee, axis=-1, keepdims=True)
    pe = jnp.where(emask, pe, -1.0)
    v1 = jnp.max(pe, axis=-1, keepdims=True)
    i1 = jnp.min(jnp.where(pe == v1, lane, big), axis=-1, keepdims=True)
    pe2 = jnp.where(lane == i1, -1.0, pe)
    v2 = jnp.max(pe2, axis=-1, keepdims=True)
    i2 = jnp.min(jnp.where(pe2 == v2, lane, big), axis=-1, keepdims=True)
    den = v1 + v2
    ew = jnp.where(lane == i1, v1 / den, 0.0) + jnp.where(lane == i2, v2 / den, 0.0)
    return p_top * ew


def _merge_kernel(oa_ref, ob_ref, gates_ref, x_ref, gt1_ref, sc2_ref, sh2_ref, g2_ref,
                  woa_ref, wob_ref, wout_ref, wr_ref, br_ref, x1_ref, h2_ref, comb_ref):
    d = x_ref.shape[1]
    a_out = _dot(oa_ref[...], woa_ref[...])
    b_out = _dot(ob_ref[...], wob_ref[...])
    merged = gates_ref[:, 0:d].astype(F32) * a_out + gates_ref[:, d:2 * d].astype(F32) * b_out
    x1 = x_ref[...] + gt1_ref[...] * _dot(merged.astype(BF16), wout_ref[...])
    x1_ref[...] = x1
    ms = jnp.mean(x1 * x1, axis=-1, keepdims=True)
    h2 = x1 * lax.rsqrt(ms + EPS) * g2_ref[...]
    h2 = h2 * (1.0 + sc2_ref[...]) + sh2_ref[...]
    h2_ref[...] = h2.astype(h2_ref.dtype)
    logits = _mm3(h2, wr_ref[...]) + br_ref[...]
    comb_ref[...] = _route(logits)


def _merge(oa, ob, gates, x2d, gt1, sc2, sh2, g2, w_oa, w_ob, w_out, wr, br, tm, seq):
    t, d = x2d.shape
    row = lambda i: (i, 0)
    const = lambda i: (0, 0)
    full = lambda w: pl.BlockSpec(w.shape, const)
    return pl.pallas_call(
        _merge_kernel,
        grid=(t // tm,),
        in_specs=[pl.BlockSpec((tm, d), row), pl.BlockSpec((tm, d), row), pl.BlockSpec((tm, 2 * d), row),
                  pl.BlockSpec((tm, d), row), _mod_spec(gt1, tm, seq), _mod_spec(sc2, tm, seq),
                  _mod_spec(sh2, tm, seq), pl.BlockSpec((1, d), const),
                  full(w_oa), full(w_ob), full(w_out), full(wr), full(br)],
        out_specs=[pl.BlockSpec((tm, d), row), pl.BlockSpec((tm, d), row), pl.BlockSpec((tm, LANES), row)],
        out_shape=[jax.ShapeDtypeStruct((t, d), F32), jax.ShapeDtypeStruct((t, d), BF16),
                   jax.ShapeDtypeStruct((t, LANES), F32)],
        compiler_params=_cparams("parallel"),
        name="merge_router",
    )(oa, ob, gates, x2d, gt1, sc2, sh2, g2, w_oa, w_ob, w_out, wr, br)


def _moe_kernel(h2_ref, comb_ref, w13_ref, w2_ref, x1_ref, gt2_ref, y_ref, acc_scr):
    e = pl.program_id(1)

    @pl.when(e == 0)
    def _():
        acc_scr[...] = jnp.zeros(acc_scr.shape, F32)

    au = _dot(h2_ref[...], w13_ref[...])
    f = au.shape[1] // 2
    a = au[:, 0:f]
    u = au[:, f:2 * f]
    comb = comb_ref[...]
    lane = lax.broadcasted_iota(jnp.int32, comb.shape, 1)
    cw = jnp.sum(jnp.where(lane == e + N_GROUPS, comb, 0.0), axis=-1, keepdims=True)
    act = a * jax.nn.sigmoid(a) * u * cw
    acc_scr[...] += _dot(act.astype(BF16), w2_ref[...])

    @pl.when(e == pl.num_programs(1) - 1)
    def _():
        y_ref[...] = x1_ref[...] + gt2_ref[...] * acc_scr[...]


def _moe(h2, comb, w13, w2, x1, gt2, tm, seq):
    t, d = x1.shape
    ne, _, f2 = w13.shape
    row = lambda i, e: (i, 0)
    return pl.pallas_call(
        _moe_kernel,
        grid=(t // tm, ne),
        in_specs=[pl.BlockSpec((tm, d), row), pl.BlockSpec((tm, LANES), row),
                  pl.BlockSpec((None, d, f2), lambda i, e: (e, 0, 0)),
                  pl.BlockSpec((None, f2 // 2, d), lambda i, e: (e, 0, 0)),
                  pl.BlockSpec((tm, d), row), _mod_spec(gt2, tm, seq)],
        out_specs=pl.BlockSpec((tm, d), row),
        out_shape=jax.ShapeDtypeStruct((t, d), F32),
        scratch_shapes=[pltpu.VMEM((tm, d), F32)],
        compiler_params=_cparams("parallel", "arbitrary"),
        name="moe",
    )(h2, comb, w13, w2, x1, gt2)


def _rope_tables(pos0, seq, reps):
    pos = (pos0 + jnp.arange(seq)).astype(F32)
    inv = ROPE_THETA ** (-jnp.arange(0, DA_HEAD_DIM, 2, dtype=F32) / DA_HEAD_DIM)
    ang = pos[:, None] * inv[None, :]
    cos, sin = jnp.cos(ang), jnp.sin(ang)
    cos_t = jnp.tile(jnp.concatenate([cos, cos], axis=-1), (reps, LANES // DA_HEAD_DIM))
    sin_t = jnp.tile(jnp.concatenate([-sin, sin], axis=-1), (reps, LANES // DA_HEAD_DIM))
    return cos_t, sin_t


def _layer(x, mod, pos0, past, wts):
    bx, sx, d = x.shape
    t = bx * sx
    x2d = x.reshape(t, d)
    per_batch = sx >= 512
    tm = 512 if per_batch else t

    def modv(i):
        if per_batch:
            return mod[:, i:i + 1, :]
        return jnp.repeat(mod[:, i, :], sx, axis=0)[None]

    sh1, sc1, gt1, sh2, sc2, gt2 = (modv(i) for i in range(6))
    cos_t, sin_t = _rope_tables(pos0, sx, 1 if per_batch else bx)

    tm_big = 1024 if (per_batch and sx % 1024 == 0) else tm
    h = _prenorm(x2d, sc1, sh1, wts["norm1_g"], tm, sx)
    q, k32, kb, v32, vb = _qkv(h, wts["w_qkv"], wts["qg"], wts["kg"], cos_t, sin_t, wts["gm"], tm_big)
    rw = _proj(h, wts["w_rw"], tm_big, wts["w_rw"].shape[1] // 3, None, F32, "rw_proj")
    gates = _proj(h, wts["w_gate"], tm_big, d, "sigmoid", BF16, "gate_proj")

    if past is None:
        oa = _attn_prompt(q, kb, vb, wts["lam"], wts["subln_g"], bx, sx, min(1024, sx), min(1024, sx))
        shift0 = jnp.zeros((bx, 1, rw.shape[1]), F32)
        h0_bd = jnp.zeros((bx, RW_PAIRS, LANES, LANES), F32)
    else:
        ck, cv, s0, sh0 = past
        oa = _attn_sample(q, kb, vb, ck, cv, wts["lam"], wts["subln_g"], bx, sx)
        shift0 = jnp.pad(sh0, ((0, 0), (0, 0), (0, rw.shape[1] - sh0.shape[-1])))
        h0_bd = _state_to_bd(s0.astype(F32))

    rw3 = rw.reshape(bx, sx, rw.shape[1])
    if sx % RW_L == 0:
        rw_in, n_valid = rw3, RW_L
    else:
        assert sx < RW_L
        rw_in, n_valid = jnp.pad(rw3, ((0, 0), (0, RW_L - sx), (0, 0))), sx
    ob, h_bd = _rwkv(rw_in, shift0, h0_bd, wts, wts["gm"], n_valid)
    ob = ob[:, 0:sx, :].reshape(t, d)

    x1, h2, comb = _merge(oa, ob, gates, x2d, gt1, sc2, sh2, wts["norm2_g"],
                          wts["w_oa"], wts["w_ob"], wts["w_out"], wts["wr"], wts["br"], tm, sx)
    y = _moe(h2, comb, wts["w13"], wts["w2"], x1, gt2, tm_big, sx)

    n_cols = 3 * d + W_LORA + A_LORA + G_LORA
    return (y.reshape(bx, sx, d),
            k32.reshape(1, bx, sx, DA_HEADS, 2 * DA_HEAD_DIM),
            v32.reshape(1, bx, sx, DA_HEADS, 2 * DA_HEAD_DIM),
            _bd_to_state(h_bd)[None],
            rw3[:, sx - 1:sx, 0:n_cols][None])


def kernel(x_prompt, x_sample, cache_k, cache_v, state_wkv, state_shift, c_prompt, c_sample, norm1_g, norm2_g, w_ada, b_ada, w_in, da_qn_g, da_kn_g, da_lambda, da_subln_g, w_oa, rw_mu, rw_w0, rw_w2, rw_a0, rw_a2, rw_g2, rw_k_k, rw_k_a, rw_r_k, rw_ln_g, rw_ln_b, w_ob, w_out, router_g, router_g_b, router_e, router_e_b, exp_w1, exp_w3, exp_w2):
    b, s, d = x_prompt.shape
    bs = x_sample.shape[0]
    past_len = cache_k.shape[2]
    assert w_in.shape[0] == 1, "single layer"

    c_all = jnp.concatenate([c_prompt, c_sample], axis=0)
    c_all = jnp.pad(c_all, ((0, (-c_all.shape[0]) % 8), (0, 0)))
    mod, lam_tile = _adaln(c_all, w_ada[0], b_ada[0], da_lambda[0])
    mod_p = mod[0:b].reshape(b, 6, d)
    mod_s = mod[b:b + bs].reshape(bs, 6, d)

    win = w_in[0]
    rw_cols = 3 * d + W_LORA + A_LORA + G_LORA
    rw_pad = 3 * d + LORA_PAD
    w_rw = jnp.pad(win[:, 3 * d:3 * d + rw_cols], ((0, 0), (0, rw_pad - rw_cols))).astype(BF16)
    lw = jnp.zeros((LORA_PAD, 3 * d), F32)
    lw = lw.at[0:W_LORA, 0:d].set(rw_w2[0])
    lw = lw.at[W_LORA:W_LORA + A_LORA, d:2 * d].set(rw_a2[0])
    lw = lw.at[W_LORA + A_LORA:W_LORA + A_LORA + G_LORA, 2 * d:3 * d].set(rw_g2[0])
    f = exp_w1.shape[-1]
    wr = jnp.pad(jnp.concatenate([router_g[0], router_e[0]], axis=1), ((0, 0), (0, LANES - N_GROUPS - N_EXPERTS)))
    br = jnp.pad(jnp.concatenate([router_g_b[0], router_e_b[0]]), (0, LANES - N_GROUPS - N_EXPERTS))[None]
    half = jnp.ones((DA_HEAD_DIM, DA_HEAD_DIM), F32) / DA_HEAD_DIM
    wts = dict(
        norm1_g=norm1_g[0][None], norm2_g=norm2_g[0][None],
        w_qkv=win[:, 0:3 * d].astype(BF16), w_rw=w_rw, w_gate=win[:, 3 * d + rw_cols:].astype(BF16),
        qg=jnp.tile(da_qn_g[0], 2)[None], kg=jnp.tile(da_kn_g[0], 2)[None],
        gm=jnp.kron(jnp.eye(2, dtype=F32), half).astype(BF16),
        lam=lam_tile[0:1], subln_g=da_subln_g[0][None],
        mu=jnp.pad(rw_mu[0], (0, rw_pad - rw_cols))[None], lw=lw.astype(BF16),
        w0=rw_w0[0][None], a0=rw_a0[0][None], k_k=rw_k_k[0][None], k_a=rw_k_a[0][None],
        r_k=rw_r_k[0].reshape(1, d), ln_g=rw_ln_g[0][None], ln_b=rw_ln_b[0][None],
        w_oa=w_oa[0].astype(BF16), w_ob=w_ob[0].astype(BF16), w_out=w_out[0].astype(BF16),
        wr=wr, br=br,
        w13=jnp.concatenate([exp_w1[0], exp_w3[0]], axis=-1).reshape(N_EXPERTS, d, 2 * f).astype(BF16),
        w2=exp_w2[0].reshape(N_EXPERTS, f, d).astype(BF16),
    )

    out_p = _layer(x_prompt, mod_p, 0, None, wts)
    out_s = _layer(x_sample, mod_s, past_len,
                   (cache_k, cache_v, state_wkv[0], state_shift[0]), wts)
    return (out_p[0], out_s[0], out_p[1], out_p[2], out_p[3], out_p[4],
            out_s[1], out_s[2], out_s[3], out_s[4])
```

```python
import functools
import math

import jax
import jax.numpy as jnp
from jax import lax
from jax.experimental import pallas as pl
from jax.experimental.pallas import tpu as pltpu

F32 = jnp.float32
BF16 = jnp.bfloat16

EPS = 1e-6
NEG_INF = -1e30
CHUNK = 64
DA_HEADS = 8
DA_HEAD_DIM = 64
ROPE_THETA = 10000.0
RW_HEAD = 64
RW_GN_EPS = 64e-5
W_LORA, A_LORA, G_LORA = 64, 64, 160
N_GROUPS, EXPERTS_PER_GROUP = 4, 8
N_EXPERTS = N_GROUPS * EXPERTS_PER_GROUP
LAM_INIT = 0.8 - 0.6 * math.exp(-0.3 * 0)

LANES = 128
LORA_PAD = 384
VMEM_LIMIT = 56 * 1024 * 1024


def _cparams(*sem):
    return pltpu.CompilerParams(dimension_semantics=sem, vmem_limit_bytes=VMEM_LIMIT)


def _dot(a, b):
    return jnp.dot(a, b, preferred_element_type=F32)


def _dot_nt(a, b):
    return lax.dot_general(a, b, (((1,), (1,)), ((), ())), preferred_element_type=F32)


def _dot_tn(a, b):
    return lax.dot_general(a, b, (((0,), (0,)), ((), ())), preferred_element_type=F32)


def _split2(a):
    hi = a.astype(BF16)
    lo = (a - hi.astype(F32)).astype(BF16)
    return hi, lo


def _split3(a):
    hi = a.astype(BF16)
    r = a - hi.astype(F32)
    mid = r.astype(BF16)
    lo = (r - mid.astype(F32)).astype(BF16)
    return hi, mid, lo


def _group_mean(sq, gm):
    hi, lo = _split2(sq)
    return _dot(hi, gm) + _dot(lo, gm)


def _adaln_kernel(c_ref, w_ref, b_ref, l_ref, o_ref, lam_ref):
    c = c_ref[...]
    sc = (c * jax.nn.sigmoid(c)).astype(BF16)
    o_ref[...] = _dot(sc, w_ref[...].astype(BF16)) + b_ref[...]
    l = l_ref[...]
    s1 = jnp.sum(l[0:1] * l[1:2], axis=-1, keepdims=True)
    s2 = jnp.sum(l[2:3] * l[3:4], axis=-1, keepdims=True)
    lam = jnp.exp(s1) - jnp.exp(s2) + LAM_INIT
    lam_ref[...] = jnp.broadcast_to(lam, lam_ref.shape)


def _adaln(c_all, w_ada, b_ada, da_lambda):
    bp, d = c_all.shape
    n = w_ada.shape[1]
    tn = 1024
    return pl.pallas_call(
        _adaln_kernel,
        grid=(n // tn,),
        in_specs=[pl.BlockSpec((bp, d), lambda j: (0, 0)),
                  pl.BlockSpec((d, tn), lambda j: (0, j)),
                  pl.BlockSpec((1, tn), lambda j: (0, j)),
                  pl.BlockSpec(da_lambda.shape, lambda j: (0, 0))],
        out_specs=[pl.BlockSpec((bp, tn), lambda j: (0, j)),
                   pl.BlockSpec((8, LANES), lambda j: (0, 0))],
        out_shape=[jax.ShapeDtypeStruct((bp, n), F32), jax.ShapeDtypeStruct((8, LANES), F32)],
        compiler_params=_cparams("arbitrary"),
        name="adaln",
    )(c_all, w_ada, b_ada.reshape(1, n), da_lambda)


def _prenorm_kernel(x_ref, sc_ref, sh_ref, g_ref, o_ref):
    x = x_ref[...]
    ms = jnp.mean(x * x, axis=-1, keepdims=True)
    h = x * lax.rsqrt(ms + EPS) * g_ref[...]
    o_ref[...] = (h * (1.0 + sc_ref[...]) + sh_ref[...]).astype(o_ref.dtype)


def _mod_spec(mod, tm, seq):
    d = mod.shape[-1]
    if mod.shape[1] == 1:
        per = seq // tm
        return pl.BlockSpec((None, 1, d), lambda i, *_: (i // per, 0, 0))
    return pl.BlockSpec((None, tm, d), lambda i, *_: (0, i, 0))


def _prenorm(x2d, sc, sh, g, tm, seq):
    t, d = x2d.shape
    return pl.pallas_call(
        _prenorm_kernel,
        grid=(t // tm,),
        in_specs=[pl.BlockSpec((tm, d), lambda i: (i, 0)), _mod_spec(sc, tm, seq), _mod_spec(sh, tm, seq),
                  pl.BlockSpec((1, d), lambda i: (0, 0))],
        out_specs=pl.BlockSpec((tm, d), lambda i: (i, 0)),
        out_shape=jax.ShapeDtypeStruct((t, d), BF16),
        compiler_params=_cparams("parallel"),
        name="prenorm",
    )(x2d, sc, sh, g)


def _qkv_kernel(h_ref, w_ref, qg_ref, kg_ref, cos_ref, sin_ref, gm_ref,
                q_ref, k_ref, kb_ref, v_ref, vb_ref):
    j = pl.program_id(1)
    acc = _dot(h_ref[...], w_ref[...])
    n_slab = acc.shape[1] // LANES

    def norm_rope(gain_ref, scale, write):
        lane = lax.broadcasted_iota(jnp.int32, (acc.shape[0], LANES), 1)
        first_half = (lane % DA_HEAD_DIM) < (DA_HEAD_DIM // 2)
        for c in range(n_slab):
            a = acc[:, c * LANES:(c + 1) * LANES]
            ms = _group_mean(a * a, gm_ref[...])
            y = a * lax.rsqrt(ms + EPS) * gain_ref[...]
            rot = jnp.where(first_half, pltpu.roll(y, LANES - DA_HEAD_DIM // 2, 1),
                            pltpu.roll(y, DA_HEAD_DIM // 2, 1))
            write(c, (y * cos_ref[...] + rot * sin_ref[...]) * scale)

    @pl.when(j == 0)
    def _():
        def write(c, val):
            q_ref[:, c * LANES:(c + 1) * LANES] = val.astype(q_ref.dtype)
        norm_rope(qg_ref, DA_HEAD_DIM ** -0.5 * math.log2(math.e), write)

    @pl.when(j == 1)
    def _():
        def write(c, val):
            k_ref[:, c * LANES:(c + 1) * LANES] = val
            kb_ref[:, c * LANES:(c + 1) * LANES] = val.astype(BF16)
        norm_rope(kg_ref, 1.0, write)

    @pl.when(j == 2)
    def _():
        v_ref[...] = acc
        vb_ref[...] = acc.astype(BF16)


def _qkv(h, w_qkv, qg, kg, cos_t, sin_t, gm, tm):
    t, d = h.shape
    n = 1024
    per = cos_t.shape[0] // tm
    row = lambda i, j: (i, 0)
    return pl.pallas_call(
        _qkv_kernel,
        grid=(t // tm, 3),
        in_specs=[pl.BlockSpec((tm, d), row),
                  pl.BlockSpec((d, n), lambda i, j: (0, j)),
                  pl.BlockSpec((1, LANES), lambda i, j: (0, 0)),
                  pl.BlockSpec((1, LANES), lambda i, j: (0, 0)),
                  pl.BlockSpec((tm, LANES), lambda i, j: (i % per, 0)),
                  pl.BlockSpec((tm, LANES), lambda i, j: (i % per, 0)),
                  pl.BlockSpec((LANES, LANES), lambda i, j: (0, 0))],
        out_specs=[pl.BlockSpec((tm, n), row)] * 5,
        out_shape=[jax.ShapeDtypeStruct((t, n), BF16), jax.ShapeDtypeStruct((t, n), F32),
                   jax.ShapeDtypeStruct((t, n), BF16), jax.ShapeDtypeStruct((t, n), F32),
                   jax.ShapeDtypeStruct((t, n), BF16)],
        compiler_params=_cparams("parallel", "arbitrary"),
        name="qkv_proj",
    )(h, w_qkv, qg, kg, cos_t, sin_t, gm)


def _proj_kernel(h_ref, w_ref, o_ref, *, act):
    acc = _dot(h_ref[...], w_ref[...])
    if act == "sigmoid":
        acc = jax.nn.sigmoid(acc)
    o_ref[...] = acc.astype(o_ref.dtype)


def _proj(h, w, tm, tn, act, out_dtype, name):
    t, d = h.shape
    n = w.shape[1]
    return pl.pallas_call(
        functools.partial(_proj_kernel, act=act),
        grid=(t // tm, n // tn),
        in_specs=[pl.BlockSpec((tm, d), lambda i, j: (i, 0)), pl.BlockSpec((d, tn), lambda i, j: (0, j))],
        out_specs=pl.BlockSpec((tm, tn), lambda i, j: (i, j)),
        out_shape=jax.ShapeDtypeStruct((t, n), out_dtype),
        compiler_params=_cparams("parallel", "arbitrary"),
        name=name,
    )(h, w)


def _subln(o, g):
    ms = jnp.mean(o * o, axis=-1, keepdims=True)
    return o * lax.rsqrt(ms + EPS) * g * (1.0 - LAM_INIT)


ATT_ROWS = 128


def _attn_prompt_kernel(qi_tab, ki_tab, lam_ref, q_ref, k_ref, v_ref, g_ref, o_ref,
                        qs_scr, vx_scr, m_scr, acc_scr, *, tq, tk):
    p = pl.program_id(2)
    qi = qi_tab[p]
    ki = ki_tab[p]
    ratio = tk // tq
    n_slab = tk // LANES

    @pl.when(ki == 0)
    def _():
        q = q_ref[...]
        lane = lax.broadcasted_iota(jnp.int32, q.shape, 1)
        zero = jnp.zeros_like(q)
        qs_scr[0:tq, :] = jnp.where(lane < DA_HEAD_DIM, q, zero)
        qs_scr[tq:2 * tq, :] = jnp.where(lane >= DA_HEAD_DIM, q, zero)
        m_scr[...] = jnp.full(m_scr.shape, NEG_INF, F32)
        acc_scr[...] = jnp.zeros(acc_scr.shape, F32)
        vx_scr[:, LANES:2 * LANES] = jnp.ones((tk, LANES), BF16)

    vx_scr[:, 0:LANES] = v_ref[...]

    def step(masked):
        for rb in range(2 * tq // ATT_ROWS):
            rows = slice(rb * ATT_ROWS, (rb + 1) * ATT_ROWS)
            s = _dot_nt(qs_scr[rows, :], k_ref[...])
            slabs = [s[:, c * LANES:(c + 1) * LANES] for c in range(n_slab)]
            if masked:
                row = lax.broadcasted_iota(jnp.int32, (ATT_ROWS, LANES), 0) + rb * ATT_ROWS
                lane = lax.broadcasted_iota(jnp.int32, (ATT_ROWS, LANES), 1)
                qchunk = (row % tq) // CHUNK + (qi % ratio) * (tq // CHUNK)
                slabs = [jnp.where((c * LANES + lane) // CHUNK <= qchunk, sl, NEG_INF)
                         for c, sl in enumerate(slabs)]
            mt = slabs[0]
            for sl in slabs[1:]:
                mt = jnp.maximum(mt, sl)
            m_prev = m_scr[rows, :]
            m_new = jnp.maximum(m_prev, jnp.max(mt, axis=-1, keepdims=True))
            alpha = jnp.exp2(m_prev - m_new)
            e = jnp.concatenate([jnp.exp2(sl - m_new).astype(BF16) for sl in slabs], axis=1)
            pv = _dot(e, vx_scr[...])
            acc_scr[rows, :] = jnp.concatenate([alpha, alpha], axis=1) * acc_scr[rows, :] + pv
            m_scr[rows, :] = m_new

    @pl.when(ki < qi // ratio)
    def _():
        step(False)

    @pl.when(ki == qi // ratio)
    def _():
        step(True)
        o = acc_scr[:, 0:LANES] / acc_scr[:, LANES:2 * LANES]
        o = o[0:tq] - lam_ref[...] * o[tq:2 * tq]
        o_ref[...] = _subln(o, g_ref[...]).astype(o_ref.dtype)


def _attn_prompt(q, k, v, lam_row, subln_g, batch, seq, tq, tk):
    d = q.shape[1]
    assert tk % tq == 0 and seq % tk == 0 and tq % CHUNK == 0
    qi_l, ki_l = [], []
    for a in range(seq // tq):
        for b in range(a * tq // tk + 1):
            qi_l.append(a)
            ki_l.append(b)
    qi_tab = jnp.asarray(qi_l, jnp.int32)
    ki_tab = jnp.asarray(ki_l, jnp.int32)
    q3, k3, v3 = (a.reshape(batch, seq, d) for a in (q, k, v))
    qspec = pl.BlockSpec((None, tq, LANES), lambda b, h, p, qt, kt: (b, qt[p], h))
    kspec = pl.BlockSpec((None, tk, LANES), lambda b, h, p, qt, kt: (b, kt[p], h))
    vec = pl.BlockSpec((1, LANES), lambda b, h, p, qt, kt: (0, 0))
    out = pl.pallas_call(
        functools.partial(_attn_prompt_kernel, tq=tq, tk=tk),
        grid_spec=pltpu.PrefetchScalarGridSpec(
            num_scalar_prefetch=2,
            grid=(batch, DA_HEADS, len(qi_l)),
            in_specs=[vec, qspec, kspec, kspec, vec],
            out_specs=qspec,
            scratch_shapes=[pltpu.VMEM((2 * tq, LANES), BF16), pltpu.VMEM((tk, 2 * LANES), BF16),
                            pltpu.VMEM((2 * tq, LANES), F32), pltpu.VMEM((2 * tq, 2 * LANES), F32)]),
        out_shape=jax.ShapeDtypeStruct((batch, seq, d), BF16),
        compiler_params=_cparams("parallel", "parallel", "arbitrary"),
        name="attn_prompt",
    )(qi_tab, ki_tab, lam_row, q3, k3, v3, subln_g)
    return out.reshape(batch * seq, d)


def _attn_sample_kernel(lam_ref, q_ref, kn_ref, vn_ref, ck_ref, cv_ref, g_ref, o_ref, *, past, sq):
    for h in range(DA_HEADS):
        hs = slice(h * LANES, (h + 1) * LANES)
        q = q_ref[:, hs]
        lane = lax.broadcasted_iota(jnp.int32, q.shape, 1)
        zero = jnp.zeros_like(q)
        qs = jnp.concatenate([jnp.where(lane < DA_HEAD_DIM, q, zero),
                              jnp.where(lane >= DA_HEAD_DIM, q, zero)], axis=0)
        kc = ck_ref[pl.ds(h, past, stride=DA_HEADS), :].astype(BF16)
        vc = cv_ref[pl.ds(h, past, stride=DA_HEADS), :].astype(BF16)
        s_c = _dot_nt(qs, kc)
        s_n = _dot_nt(qs, kn_ref[:, hs])

        def masked(s, key0):
            row = lax.broadcasted_iota(jnp.int32, s.shape, 0)
            col = lax.broadcasted_iota(jnp.int32, s.shape, 1)
            qchunk = (past + row % sq) // CHUNK
            return jnp.where((key0 + col) // CHUNK <= qchunk, s, NEG_INF)

        s_c = masked(s_c, 0)
        s_n = masked(s_n, past)
        m = jnp.maximum(jnp.max(s_c, axis=-1, keepdims=True), jnp.max(s_n, axis=-1, keepdims=True))
        e_c = jnp.exp2(s_c - m)
        e_n = jnp.exp2(s_n - m)
        l = jnp.sum(e_c, axis=-1, keepdims=True) + jnp.sum(e_n, axis=-1, keepdims=True)
        o = (_dot(e_c.astype(BF16), vc) + _dot(e_n.astype(BF16), vn_ref[:, hs])) / l
        o = o[0:sq] - lam_ref[...] * o[sq:2 * sq]
        o_ref[:, hs] = _subln(o, g_ref[...]).astype(o_ref.dtype)


def _attn_sample(q, kn, vn, cache_k, cache_v, lam_row, subln_g, batch, sq):
    d = q.shape[1]
    past = cache_k.shape[-3]
    ck = cache_k.reshape(batch, past * DA_HEADS, LANES)
    cv = cache_v.reshape(batch, past * DA_HEADS, LANES)
    q3, k3, v3 = (a.reshape(batch, sq, d) for a in (q, kn, vn))
    new = pl.BlockSpec((None, sq, d), lambda b: (b, 0, 0))
    old = pl.BlockSpec((None, past * DA_HEADS, LANES), lambda b: (b, 0, 0))
    vec = pl.BlockSpec((1, LANES), lambda b: (0, 0))
    out = pl.pallas_call(
        functools.partial(_attn_sample_kernel, past=past, sq=sq),
        grid=(batch,),
        in_specs=[vec, new, new, new, old, old, vec],
        out_specs=new,
        out_shape=jax.ShapeDtypeStruct((batch, sq, d), BF16),
        compiler_params=_cparams("parallel"),
        name="attn_sample",
    )(lam_row, q3, k3, v3, ck, cv, subln_g)
    return out.reshape(batch * sq, d)


RW_L = 64
RW_PAIRS = 8
RW_CPS = 2


def _mm3(a, b, dot=_dot):
    ah, al = _split2(a)
    bh, bl = _split2(b)
    return dot(ah, bh) + dot(ah, bl) + dot(al, bh)


def _terms(x, n):
    return (x.astype(BF16),) if n == 1 else _split2(x)


def _mmt(at, bt, dot=_dot):
    acc = dot(at[0], bt[0])
    if len(bt) > 1:
        acc = acc + dot(at[0], bt[1])
    if len(at) > 1:
        acc = acc + dot(at[1], bt[0])
    return acc


RW_PREC = dict(a_all=(1, 1), neumann=(1, 1), av=(1, 1), pu=(1, 1), qy=(1, 1), mh=(1, 1), state=(1, 1))


def _softplus(z):
    return jnp.maximum(z, 0.0) + jnp.log(1.0 + jnp.exp(-jnp.abs(z)))


def _rwkv_kernel(rw_ref, sh0_ref, mu_ref, lw_ref, w0_ref, a0_ref, kk_ref, ka_ref, rk_ref, lng_ref, lnb_ref,
                 h0_ref, gm_ref, tri_ref, ob_ref, hout_ref, h_scr, last_scr, *, n_valid, cps):
    L = RW_L
    rows = cps * L
    c = pl.program_id(1)
    d = ob_ref.shape[-1]

    @pl.when(c == 0)
    def _():
        h_scr[...] = h0_ref[...]
        last_scr[...] = sh0_ref[...]

    rw = rw_ref[...]
    row_w = lax.broadcasted_iota(jnp.int32, rw.shape, 0)
    prev = jnp.where(row_w == 0, last_scr[...], pltpu.roll(rw, 1, 0))
    last_scr[...] = rw[rows - 1:rows, :]
    xm = rw + (prev - rw) * mu_ref[...]
    r = xm[:, 0:d]
    k = xm[:, d:2 * d]
    v = xm[:, 2 * d:3 * d]
    lo = xm[:, 3 * d:3 * d + LORA_PAD]
    lane_l = lax.broadcasted_iota(jnp.int32, lo.shape, 1)
    z = jnp.where(lane_l < W_LORA, jnp.tanh(lo),
                  jnp.where(lane_l < W_LORA + A_LORA, lo, jax.nn.sigmoid(lo)))
    lora = _dot(z.astype(BF16), lw_ref[...])
    w = -_softplus(-(w0_ref[...] + lora[:, 0:d])) - 0.5
    ld = -jnp.exp(w)
    a = jax.nn.sigmoid(a0_ref[...] + lora[:, d:2 * d])
    g = lora[:, 2 * d:3 * d]
    kkr = k * kk_ref[...]
    kmod = k * (1.0 + (a - 1.0) * ka_ref[...])
    rkk = r * kmod * rk_ref[...]
    gm = gm_ref[...]
    if n_valid < rows:
        valid = lax.broadcasted_iota(jnp.int32, ld.shape, 0) < n_valid
        ld = jnp.where(valid, ld, 0.0)
        kkr = jnp.where(valid, kkr, 0.0)
        kmod = jnp.where(valid, kmod, 0.0)

    tri = tri_ref[...]
    l1, l2, l3 = _split3(ld)
    cin = _dot(tri, l1) + _dot(tri, l2) + _dot(tri, l3)
    cex = cin - ld
    c_last = [cin[(ch + 1) * L - 1:(ch + 1) * L, :] for ch in range(cps)]
    e_in = jnp.exp(cin)
    e_ex = jnp.exp(cex)
    e_neg = jnp.exp(-cin)
    e_tail = jnp.exp(jnp.concatenate([jnp.broadcast_to(cl, (L, d)) for cl in c_last], axis=0) - cin)
    g_last = [jnp.exp(cl) for cl in c_last]

    row = lax.broadcasted_iota(jnp.int32, (L, LANES), 0)
    lane = lax.broadcasted_iota(jnp.int32, (L, LANES), 1)
    m_lo = lane < RW_HEAD
    tri_s = row > (lane % L)
    tri_i = row >= (lane % L)
    eye_pair = jnp.where(row == (lane % L), 1.0, 0.0).astype(F32)
    r2 = lax.broadcasted_iota(jnp.int32, (LANES, LANES), 0)
    c2 = lax.broadcasted_iota(jnp.int32, (LANES, LANES), 1)
    bd_mask = (r2 < RW_HEAD) == (c2 < RW_HEAD)
    diag_mask = r2 == c2

    def bd(x):
        z0 = jnp.zeros_like(x)
        return jnp.concatenate([jnp.where(m_lo, x, z0), jnp.where(m_lo, z0, x)], axis=0)

    def bdt(t):
        return tuple(bd(x) for x in t)

    def cat(ts, axis):
        return tuple(jnp.concatenate(xs, axis=axis) for xs in zip(*ts))

    prec = RW_PREC
    items = [(slice(ch * L, (ch + 1) * L), slice(p * LANES, (p + 1) * LANES))
             for ch in range(cps) for p in range(RW_PAIRS)]
    pairs = range(len(items))

    v_p = [v[rs, sl] for rs, sl in items]
    kk_p = []
    for rs, sl in items:
        kk_raw = kkr[rs, sl]
        ss = _group_mean(kk_raw * kk_raw, gm) * RW_HEAD
        kk_p.append(kk_raw / jnp.maximum(jnp.sqrt(ss), 1e-12))
    b_p = [kk_p[i] * a[rs, sl] for i, (rs, sl) in enumerate(items)]
    abar = [-kk_p[i] * e_ex[rs, sl] for i, (rs, sl) in enumerate(items)]
    rbar = [r[rs, sl] * e_in[rs, sl] for rs, sl in items]
    bbar = [b_p[i] * e_neg[rs, sl] for i, (rs, sl) in enumerate(items)]
    kbar = [kmod[rs, sl] * e_neg[rs, sl] for rs, sl in items]
    btil = [b_p[i] * e_tail[rs, sl] for i, (rs, sl) in enumerate(items)]
    ktil = [kmod[rs, sl] * e_tail[rs, sl] for rs, sl in items]

    na, nb = prec["a_all"]
    abar_t = [_terms(abar[p], max(na, prec["pu"][1])) for p in pairs]
    v_t = [_terms(v_p[p], max(prec["av"][1], prec["qy"][1], prec["mh"][1])) for p in pairs]
    a_all = []
    for p in pairs:
        lhs = cat([abar_t[p][:na], _terms(rbar[p], na)], 0)
        rhs = cat([bdt(_terms(bbar[p], nb)), bdt(_terms(kbar[p], nb))], 0)
        a_all.append(_mmt(lhs, rhs, _dot_nt))
    n_ab = [jnp.where(tri_s, a_all[p][0:L, 0:2 * L], 0.0) for p in pairs]
    a_ak = [jnp.where(tri_s, a_all[p][0:L, 2 * L:4 * L], 0.0) for p in pairs]
    a_rb = [jnp.where(tri_i, a_all[p][L:2 * L, 0:2 * L], 0.0) for p in pairs]
    a_rk = [jnp.where(tri_i, a_all[p][L:2 * L, 2 * L:4 * L], 0.0) for p in pairs]

    na, nb = prec["neumann"]
    t_inv = [eye_pair + n_ab[p] for p in pairs]
    pw = n_ab
    for _ in range(int(math.log2(L)) - 1):
        pw_t = [_terms(pw[p], max(na, nb)) for p in pairs]
        pw = [_mmt(pw_t[p][:na], bdt(pw_t[p][:nb])) for p in pairs]
        pw_b = [bdt(_terms(pw[p], nb)) for p in pairs]
        t_inv = [t_inv[p] + _mmt(_terms(t_inv[p], na), pw_b[p]) for p in pairs]

    na, nb = prec["av"]
    av = [_mmt(_terms(a_ak[p], na), bdt(v_t[p][:nb])) for p in pairs]
    na, nb = prec["pu"]
    pu = [_mmt(_terms(t_inv[p], na), cat([bdt(abar_t[p][:nb]), bdt(_terms(av[p], nb))], 1)) for p in pairs]
    p_m = [x[:, 0:LANES] for x in pu]
    u0 = [x[:, LANES:2 * LANES] for x in pu]
    na, nb = prec["qy"]
    nmh = prec["mh"][1]
    p_t = [_terms(p_m[p], max(nb, nmh)) for p in pairs]
    u_t = [_terms(u0[p], max(nb, nmh)) for p in pairs]
    q_m, y0 = [], []
    for p in pairs:
        zero = tuple(jnp.zeros((2 * L, LANES), BF16) for _ in range(nb))
        rhs2 = cat([cat([bdt(p_t[p][:nb]), bdt(u_t[p][:nb])], 1),
                    cat([zero, bdt(v_t[p][:nb])], 1)], 0)
        qy = _mmt(cat([_terms(a_rb[p], na), _terms(a_rk[p], na)], 1), rhs2)
        q_m.append(rbar[p] + qy[:, 0:LANES])
        y0.append(qy[:, LANES:2 * LANES])

    na, nb = prec["mh"]
    btil_t = [_terms(btil[p], na) for p in pairs]
    m_full = [_mmt(btil_t[p], p_t[p][:nb], _dot_tn) for p in pairs]
    h0_full = [_mmt(cat([btil_t[p], _terms(ktil[p], na)], 0), cat([u_t[p][:nb], v_t[p][:nb]], 0), _dot_tn)
               for p in pairs]
    m_bd = [jnp.where(bd_mask, m_full[i], 0.0) + jnp.where(diag_mask, g_last[i // RW_PAIRS][:, sl], 0.0)
            for i, (rs, sl) in enumerate(items)]
    h0_bd = [jnp.where(bd_mask, h0_full[p], 0.0) for p in pairs]

    na, nb = prec["state"]
    h_cur = [h_scr[p] for p in range(RW_PAIRS)]
    y = []
    for ch in range(cps):
        h_t = [_terms(h, nb) for h in h_cur]
        for p in range(RW_PAIRS):
            i = ch * RW_PAIRS + p
            y.append(_mmt(_terms(q_m[i], na), h_t[p]) + y0[i])
        h_cur = [_mmt(_terms(m_bd[ch * RW_PAIRS + p], na), h_t[p]) + h0_bd[ch * RW_PAIRS + p]
                 for p in range(RW_PAIRS)]
    for p in range(RW_PAIRS):
        h_scr[p] = h_cur[p]

    for i, (rs, sl) in enumerate(items):
        mu_y = _group_mean(y[i], gm)
        dy = y[i] - mu_y
        var = _group_mean(dy * dy, gm)
        yn = dy * lax.rsqrt(var + RW_GN_EPS) * lng_ref[:, sl] + lnb_ref[:, sl]
        bonus = _group_mean(rkk[rs, sl], gm) * RW_HEAD * v_p[i]
        ob_ref[rs, sl] = ((yn + bonus) * g[rs, sl]).astype(ob_ref.dtype)

    @pl.when(c == pl.num_programs(1) - 1)
    def _():
        hout_ref[...] = h_scr[...]


def _rwkv(rw3, shift0, h0_bd, prm, gm, n_valid, cps):
    batch, seq, wcols = rw3.shape
    d = prm["w0"].shape[1]
    rows = cps * RW_L
    nc = seq // rows
    tri = jnp.kron(jnp.eye(cps, dtype=F32), jnp.tril(jnp.ones((RW_L, RW_L), F32))).astype(BF16)
    const2 = lambda b, c: (0, 0)
    vec = pl.BlockSpec((1, d), const2)
    st = pl.BlockSpec((None, RW_PAIRS, LANES, LANES), lambda b, c: (b, 0, 0, 0))
    return pl.pallas_call(
        functools.partial(_rwkv_kernel, n_valid=n_valid, cps=cps),
        grid=(batch, nc),
        in_specs=[pl.BlockSpec((None, rows, wcols), lambda b, c: (b, c, 0)),
                  pl.BlockSpec((None, 1, wcols), lambda b, c: (b, 0, 0)),
                  pl.BlockSpec((1, wcols), const2),
                  pl.BlockSpec(prm["lw"].shape, const2),
                  vec, vec, vec, vec, vec, vec, vec, st,
                  pl.BlockSpec((LANES, LANES), const2),
                  pl.BlockSpec((rows, rows), const2)],
        out_specs=[pl.BlockSpec((None, rows, d), lambda b, c: (b, c, 0)), st],
        out_shape=[jax.ShapeDtypeStruct((batch, seq, d), BF16),
                   jax.ShapeDtypeStruct((batch, RW_PAIRS, LANES, LANES), F32)],
        scratch_shapes=[pltpu.VMEM((RW_PAIRS, LANES, LANES), F32), pltpu.VMEM((1, wcols), F32)],
        compiler_params=_cparams("parallel", "arbitrary"),
        name="rwkv7",
    )(rw3, shift0, prm["mu"], prm["lw"], prm["w0"], prm["a0"], prm["k_k"], prm["k_a"], prm["r_k"],
      prm["ln_g"], prm["ln_b"], h0_bd, gm, tri)


def _state_to_bd(s):
    b = s.shape[0]
    ht = jnp.swapaxes(s, -1, -2).reshape(b, RW_PAIRS, 2, RW_HEAD, RW_HEAD)
    z = jnp.zeros_like(ht[:, :, 0])
    top = jnp.concatenate([ht[:, :, 0], z], axis=-1)
    bot = jnp.concatenate([z, ht[:, :, 1]], axis=-1)
    return jnp.concatenate([top, bot], axis=-2)


def _bd_to_state(hbd):
    b = hbd.shape[0]
    h0 = hbd[:, :, 0:RW_HEAD, 0:RW_HEAD]
    h1 = hbd[:, :, RW_HEAD:, RW_HEAD:]
    ht = jnp.stack([h0, h1], axis=2).reshape(b, 2 * RW_PAIRS, RW_HEAD, RW_HEAD)
    return jnp.swapaxes(ht, -1, -2)


def _route(logits):
    lane = lax.broadcasted_iota(jnp.int32, logits.shape, 1).astype(F32)
    big = float(LANES)
    lg = jnp.where(lane < N_GROUPS, logits, NEG_INF)
    mg = jnp.max(lg, axis=-1, keepdims=True)
    sg = jnp.sum(jnp.exp(lg - mg), axis=-1, keepdims=True)
    p_top = 1.0 / sg
    g_idx = jnp.min(jnp.where(lg == mg, lane, big), axis=-1, keepdims=True)
    e0 = N_GROUPS + EXPERTS_PER_GROUP * g_idx
    emask = jnp.where(lane >= e0, jnp.where(lane < e0 + EXPERTS_PER_GROUP, 1.0, 0.0), 0.0) > 0.5
    le = jnp.where(emask, logits, NEG_INF)
    me = jnp.max(le, axis=-1, keepdims=True)
    ee = jnp.exp(le - me)
    pe = ee / jnp.sum(ee, axis=-1, keepdims=True)
    pe = jnp.where(emask, pe, -1.0)
    v1 = jnp.max(pe, axis=-1, keepdims=True)
    i1 = jnp.min(jnp.where(pe == v1, lane, big), axis=-1, keepdims=True)
    pe2 = jnp.where(lane == i1, -1.0, pe)
    v2 = jnp.max(pe2, axis=-1, keepdims=True)
    i2 = jnp.min(jnp.where(pe2 == v2, lane, big), axis=-1, keepdims=True)
    den = v1 + v2
    ew = jnp.where(lane == i1, v1 / den, 0.0) + jnp.where(lane == i2, v2 / den, 0.0)
    return p_top * ew


def _merge_kernel(oa_ref, ob_ref, gates_ref, x_ref, gt1_ref, sc2_ref, sh2_ref, g2_ref,
                  woa_ref, wob_ref, wout_ref, wr_ref, br_ref, x1_ref, h2_ref, comb_ref):
    d = x_ref.shape[1]
    a_out = _dot(oa_ref[...], woa_ref[...])
    b_out = _dot(ob_ref[...], wob_ref[...])
    merged = gates_ref[:, 0:d].astype(F32) * a_out + gates_ref[:, d:2 * d].astype(F32) * b_out
    x1 = x_ref[...] + gt1_ref[...] * _dot(merged.astype(BF16), wout_ref[...])
    x1_ref[...] = x1
    ms = jnp.mean(x1 * x1, axis=-1, keepdims=True)
    h2 = x1 * lax.rsqrt(ms + EPS) * g2_ref[...]
    h2 = h2 * (1.0 + sc2_ref[...]) + sh2_ref[...]
    h2_ref[...] = h2.astype(h2_ref.dtype)
    logits = _mm3(h2, wr_ref[...]) + br_ref[...]
    comb_ref[...] = _route(logits)


def _merge(oa, ob, gates, x2d, gt1, sc2, sh2, g2, w_oa, w_ob, w_out, wr, br, tm, seq):
    t, d = x2d.shape
    row = lambda i: (i, 0)
    const = lambda i: (0, 0)
    full = lambda w: pl.BlockSpec(w.shape, const)
    return pl.pallas_call(
        _merge_kernel,
        grid=(t // tm,),
        in_specs=[pl.BlockSpec((tm, d), row), pl.BlockSpec((tm, d), row), pl.BlockSpec((tm, 2 * d), row),
                  pl.BlockSpec((tm, d), row), _mod_spec(gt1, tm, seq), _mod_spec(sc2, tm, seq),
                  _mod_spec(sh2, tm, seq), pl.BlockSpec((1, d), const),
                  full(w_oa), full(w_ob), full(w_out), full(wr), full(br)],
        out_specs=[pl.BlockSpec((tm, d), row), pl.BlockSpec((tm, d), row), pl.BlockSpec((tm, LANES), row)],
        out_shape=[jax.ShapeDtypeStruct((t, d), F32), jax.ShapeDtypeStruct((t, d), BF16),
                   jax.ShapeDtypeStruct((t, LANES), F32)],
        compiler_params=_cparams("parallel"),
        name="merge_router",
    )(oa, ob, gates, x2d, gt1, sc2, sh2, g2, w_oa, w_ob, w_out, wr, br)


def _moe_kernel(h2_ref, comb_ref, w13_ref, w2_ref, x1_ref, gt2_ref, y_ref, acc_scr):
    e = pl.program_id(1)

    @pl.when(e == 0)
    def _():
        acc_scr[...] = jnp.zeros(acc_scr.shape, F32)

    au = _dot(h2_ref[...], w13_ref[...])
    f = au.shape[1] // 2
    a = au[:, 0:f]
    u = au[:, f:2 * f]
    comb = comb_ref[...]
    lane = lax.broadcasted_iota(jnp.int32, comb.shape, 1)
    cw = jnp.sum(jnp.where(lane == e + N_GROUPS, comb, 0.0), axis=-1, keepdims=True)
    act = a * jax.nn.sigmoid(a) * u * cw
    acc_scr[...] += _dot(act.astype(BF16), w2_ref[...])

    @pl.when(e == pl.num_programs(1) - 1)
    def _():
        y_ref[...] = x1_ref[...] + gt2_ref[...] * acc_scr[...]


def _moe(h2, comb, w13, w2, x1, gt2, tm, seq):
    t, d = x1.shape
    ne, _, f2 = w13.shape
    row = lambda i, e: (i, 0)
    return pl.pallas_call(
        _moe_kernel,
        grid=(t // tm, ne),
        in_specs=[pl.BlockSpec((tm, d), row), pl.BlockSpec((tm, LANES), row),
                  pl.BlockSpec((None, d, f2), lambda i, e: (e, 0, 0)),
                  pl.BlockSpec((None, f2 // 2, d), lambda i, e: (e, 0, 0)),
                  pl.BlockSpec((tm, d), row), _mod_spec(gt2, tm, seq)],
        out_specs=pl.BlockSpec((tm, d), row),
        out_shape=jax.ShapeDtypeStruct((t, d), F32),
        scratch_shapes=[pltpu.VMEM((tm, d), F32)],
        compiler_params=_cparams("parallel", "arbitrary"),
        name="moe",
    )(h2, comb, w13, w2, x1, gt2)


def _rope_tables(pos0, seq, reps):
    pos = (pos0 + jnp.arange(seq)).astype(F32)
    inv = ROPE_THETA ** (-jnp.arange(0, DA_HEAD_DIM, 2, dtype=F32) / DA_HEAD_DIM)
    ang = pos[:, None] * inv[None, :]
    cos, sin = jnp.cos(ang), jnp.sin(ang)
    cos_t = jnp.tile(jnp.concatenate([cos, cos], axis=-1), (reps, LANES // DA_HEAD_DIM))
    sin_t = jnp.tile(jnp.concatenate([-sin, sin], axis=-1), (reps, LANES // DA_HEAD_DIM))
    return cos_t, sin_t


def _layer(x, mod, pos0, past, wts):
    bx, sx, d = x.shape
    t = bx * sx
    x2d = x.reshape(t, d)
    per_batch = sx >= 512
    tm = 512 if per_batch else t

    def modv(i):
        if per_batch:
            return mod[:, i:i + 1, :]
        return jnp.repeat(mod[:, i, :], sx, axis=0)[None]

    sh1, sc1, gt1, sh2, sc2, gt2 = (modv(i) for i in range(6))
    cos_t, sin_t = _rope_tables(pos0, sx, 1 if per_batch else bx)

    tm_big = 1024 if (per_batch and sx % 1024 == 0) else tm
    h = _prenorm(x2d, sc1, sh1, wts["norm1_g"], tm, sx)
    q, k32, kb, v32, vb = _qkv(h, wts["w_qkv"], wts["qg"], wts["kg"], cos_t, sin_t, wts["gm"], tm_big)
    rw = _proj(h, wts["w_rw"], tm_big, wts["w_rw"].shape[1] // 3, None, F32, "rw_proj")
    gates = _proj(h, wts["w_gate"], tm_big, d, "sigmoid", BF16, "gate_proj")

    if past is None:
        oa = _attn_prompt(q, kb, vb, wts["lam"], wts["subln_g"], bx, sx, min(1024, sx), min(1024, sx))
        shift0 = jnp.zeros((bx, 1, rw.shape[1]), F32)
        h0_bd = jnp.zeros((bx, RW_PAIRS, LANES, LANES), F32)
    else:
        ck, cv, s0, sh0 = past
        oa = _attn_sample(q, kb, vb, ck, cv, wts["lam"], wts["subln_g"], bx, sx)
        shift0 = jnp.pad(sh0, ((0, 0), (0, 0), (0, rw.shape[1] - sh0.shape[-1])))
        h0_bd = _state_to_bd(s0.astype(F32))

    rw3 = rw.reshape(bx, sx, rw.shape[1])
    if sx % RW_L == 0:
        cps = RW_CPS if sx % (RW_CPS * RW_L) == 0 else 1
        rw_in, n_valid = rw3, cps * RW_L
    else:
        assert sx < RW_L
        cps = 1
        rw_in, n_valid = jnp.pad(rw3, ((0, 0), (0, RW_L - sx), (0, 0))), sx
    ob, h_bd = _rwkv(rw_in, shift0, h0_bd, wts, wts["gm"], n_valid, cps)
    ob = ob[:, 0:sx, :].reshape(t, d)

    x1, h2, comb = _merge(oa, ob, gates, x2d, gt1, sc2, sh2, wts["norm2_g"],
                          wts["w_oa"], wts["w_ob"], wts["w_out"], wts["wr"], wts["br"], tm, sx)
    y = _moe(h2, comb, wts["w13"], wts["w2"], x1, gt2, tm_big, sx)

    n_cols = 3 * d + W_LORA + A_LORA + G_LORA
    return (y.reshape(bx, sx, d),
            k32.reshape(1, bx, sx, DA_HEADS, 2 * DA_HEAD_DIM),
            v32.reshape(1, bx, sx, DA_HEADS, 2 * DA_HEAD_DIM),
            _bd_to_state(h_bd)[None],
            rw3[:, sx - 1:sx, 0:n_cols][None])


def kernel(x_prompt, x_sample, cache_k, cache_v, state_wkv, state_shift, c_prompt, c_sample, norm1_g, norm2_g, w_ada, b_ada, w_in, da_qn_g, da_kn_g, da_lambda, da_subln_g, w_oa, rw_mu, rw_w0, rw_w2, rw_a0, rw_a2, rw_g2, rw_k_k, rw_k_a, rw_r_k, rw_ln_g, rw_ln_b, w_ob, w_out, router_g, router_g_b, router_e, router_e_b, exp_w1, exp_w3, exp_w2):
    b, s, d = x_prompt.shape
    bs = x_sample.shape[0]
    past_len = cache_k.shape[2]
    assert w_in.shape[0] == 1, "single layer"

    c_all = jnp.concatenate([c_prompt, c_sample], axis=0)
    c_all = jnp.pad(c_all, ((0, (-c_all.shape[0]) % 8), (0, 0)))
    mod, lam_tile = _adaln(c_all, w_ada[0], b_ada[0], da_lambda[0])
    mod_p = mod[0:b].reshape(b, 6, d)
    mod_s = mod[b:b + bs].reshape(bs, 6, d)

    win = w_in[0]
    rw_cols = 3 * d + W_LORA + A_LORA + G_LORA
    rw_pad = 3 * d + LORA_PAD
    w_rw = jnp.pad(win[:, 3 * d:3 * d + rw_cols], ((0, 0), (0, rw_pad - rw_cols))).astype(BF16)
    lw = jnp.zeros((LORA_PAD, 3 * d), F32)
    lw = lw.at[0:W_LORA, 0:d].set(rw_w2[0])
    lw = lw.at[W_LORA:W_LORA + A_LORA, d:2 * d].set(rw_a2[0])
    lw = lw.at[W_LORA + A_LORA:W_LORA + A_LORA + G_LORA, 2 * d:3 * d].set(rw_g2[0])
    f = exp_w1.shape[-1]
    wr = jnp.pad(jnp.concatenate([router_g[0], router_e[0]], axis=1), ((0, 0), (0, LANES - N_GROUPS - N_EXPERTS)))
    br = jnp.pad(jnp.concatenate([router_g_b[0], router_e_b[0]]), (0, LANES - N_GROUPS - N_EXPERTS))[None]
    half = jnp.ones((DA_HEAD_DIM, DA_HEAD_DIM), F32) / DA_HEAD_DIM
    wts = dict(
        norm1_g=norm1_g[0][None], norm2_g=norm2_g[0][None],
        w_qkv=win[:, 0:3 * d].astype(BF16), w_rw=w_rw, w_gate=win[:, 3 * d + rw_cols:].astype(BF16),
        qg=jnp.tile(da_qn_g[0], 2)[None], kg=jnp.tile(da_kn_g[0], 2)[None],
        gm=jnp.kron(jnp.eye(2, dtype=F32), half).astype(BF16),
        lam=lam_tile[0:1], subln_g=da_subln_g[0][None],
        mu=jnp.pad(rw_mu[0], (0, rw_pad - rw_cols))[None], lw=lw.astype(BF16),
        w0=rw_w0[0][None], a0=rw_a0[0][None], k_k=rw_k_k[0][None], k_a=rw_k_a[0][None],
        r_k=rw_r_k[0].reshape(1, d), ln_g=rw_ln_g[0][None], ln_b=rw_ln_b[0][None],
        w_oa=w_oa[0].astype(BF16), w_ob=w_ob[0].astype(BF16), w_out=w_out[0].astype(BF16),
        wr=wr, br=br,
        w13=jnp.concatenate([exp_w1[0], exp_w3[0]], axis=-1).reshape(N_EXPERTS, d, 2 * f).astype(BF16),
        w2=exp_w2[0].reshape(N_EXPERTS, f, d).astype(BF16),
    )

    out_p = _layer(x_prompt, mod_p, 0, None, wts)
    out_s = _layer(x_sample, mod_s, past_len,
                   (cache_k, cache_v, state_wkv[0], state_shift[0]), wts)
    return (out_p[0], out_s[0], out_p[1], out_p[2], out_p[3], out_p[4],
            out_s[1], out_s[2], out_s[3], out_s[4])
```

```python
import functools
import math

import jax
import jax.numpy as jnp
from jax import lax
from jax.experimental import pallas as pl
from jax.experimental.pallas import tpu as pltpu

F32 = jnp.float32
BF16 = jnp.bfloat16

EPS = 1e-6
NEG_INF = -1e30
CHUNK = 64
DA_HEADS = 8
DA_HEAD_DIM = 64
ROPE_THETA = 10000.0
RW_HEAD = 64
RW_GN_EPS = 64e-5
W_LORA, A_LORA, G_LORA = 64, 64, 160
N_GROUPS, EXPERTS_PER_GROUP = 4, 8
N_EXPERTS = N_GROUPS * EXPERTS_PER_GROUP
LAM_INIT = 0.8 - 0.6 * math.exp(-0.3 * 0)

LANES = 128
LORA_PAD = 384
VMEM_LIMIT = 56 * 1024 * 1024


def _cparams(*sem):
    return pltpu.CompilerParams(dimension_semantics=sem, vmem_limit_bytes=VMEM_LIMIT)


def _dot(a, b):
    return jnp.dot(a, b, preferred_element_type=F32)


def _dot_nt(a, b):
    return lax.dot_general(a, b, (((1,), (1,)), ((), ())), preferred_element_type=F32)


def _dot_tn(a, b):
    return lax.dot_general(a, b, (((0,), (0,)), ((), ())), preferred_element_type=F32)


def _split2(a):
    hi = a.astype(BF16)
    lo = (a - hi.astype(F32)).astype(BF16)
    return hi, lo


def _split3(a):
    hi = a.astype(BF16)
    r = a - hi.astype(F32)
    mid = r.astype(BF16)
    lo = (r - mid.astype(F32)).astype(BF16)
    return hi, mid, lo


def _group_mean(sq, gm):
    hi, lo = _split2(sq)
    return _dot(hi, gm) + _dot(lo, gm)


def _adaln_kernel(c_ref, w_ref, b_ref, l_ref, o_ref, lam_ref):
    c = c_ref[...]
    sc = (c * jax.nn.sigmoid(c)).astype(BF16)
    o_ref[...] = _dot(sc, w_ref[...].astype(BF16)) + b_ref[...]
    l = l_ref[...]
    s1 = jnp.sum(l[0:1] * l[1:2], axis=-1, keepdims=True)
    s2 = jnp.sum(l[2:3] * l[3:4], axis=-1, keepdims=True)
    lam = jnp.exp(s1) - jnp.exp(s2) + LAM_INIT
    lam_ref[...] = jnp.broadcast_to(lam, lam_ref.shape)


def _adaln(c_all, w_ada, b_ada, da_lambda):
    bp, d = c_all.shape
    n = w_ada.shape[1]
    tn = 1024
    return pl.pallas_call(
        _adaln_kernel,
        grid=(n // tn,),
        in_specs=[pl.BlockSpec((bp, d), lambda j: (0, 0)),
                  pl.BlockSpec((d, tn), lambda j: (0, j)),
                  pl.BlockSpec((1, tn), lambda j: (0, j)),
                  pl.BlockSpec(da_lambda.shape, lambda j: (0, 0))],
        out_specs=[pl.BlockSpec((bp, tn), lambda j: (0, j)),
                   pl.BlockSpec((8, LANES), lambda j: (0, 0))],
        out_shape=[jax.ShapeDtypeStruct((bp, n), F32), jax.ShapeDtypeStruct((8, LANES), F32)],
        compiler_params=_cparams("arbitrary"),
        name="adaln",
    )(c_all, w_ada, b_ada.reshape(1, n), da_lambda)


def _prenorm_kernel(x_ref, sc_ref, sh_ref, g_ref, o_ref):
    x = x_ref[...]
    ms = jnp.mean(x * x, axis=-1, keepdims=True)
    h = x * lax.rsqrt(ms + EPS) * g_ref[...]
    o_ref[...] = (h * (1.0 + sc_ref[...]) + sh_ref[...]).astype(o_ref.dtype)


def _mod_spec(mod, tm, seq):
    d = mod.shape[-1]
    if mod.shape[1] == 1:
        per = seq // tm
        return pl.BlockSpec((None, 1, d), lambda i, *_: (i // per, 0, 0))
    return pl.BlockSpec((None, tm, d), lambda i, *_: (0, i, 0))


def _prenorm(x2d, sc, sh, g, tm, seq):
    t, d = x2d.shape
    return pl.pallas_call(
        _prenorm_kernel,
        grid=(t // tm,),
        in_specs=[pl.BlockSpec((tm, d), lambda i: (i, 0)), _mod_spec(sc, tm, seq), _mod_spec(sh, tm, seq),
                  pl.BlockSpec((1, d), lambda i: (0, 0))],
        out_specs=pl.BlockSpec((tm, d), lambda i: (i, 0)),
        out_shape=jax.ShapeDtypeStruct((t, d), BF16),
        compiler_params=_cparams("parallel"),
        name="prenorm",
    )(x2d, sc, sh, g)


def _qkv_kernel(h_ref, w_ref, qg_ref, kg_ref, cos_ref, sin_ref, gm_ref,
                q_ref, k_ref, kb_ref, v_ref, vb_ref):
    j = pl.program_id(1)
    acc = _dot(h_ref[...], w_ref[...])
    n_slab = acc.shape[1] // LANES

    def norm_rope(gain_ref, scale, write):
        lane = lax.broadcasted_iota(jnp.int32, (acc.shape[0], LANES), 1)
        first_half = (lane % DA_HEAD_DIM) < (DA_HEAD_DIM // 2)
        for c in range(n_slab):
            a = acc[:, c * LANES:(c + 1) * LANES]
            ms = _group_mean(a * a, gm_ref[...])
            y = a * lax.rsqrt(ms + EPS) * gain_ref[...]
            rot = jnp.where(first_half, pltpu.roll(y, LANES - DA_HEAD_DIM // 2, 1),
                            pltpu.roll(y, DA_HEAD_DIM // 2, 1))
            write(c, (y * cos_ref[...] + rot * sin_ref[...]) * scale)

    @pl.when(j == 0)
    def _():
        def write(c, val):
            q_ref[:, c * LANES:(c + 1) * LANES] = val.astype(q_ref.dtype)
        norm_rope(qg_ref, DA_HEAD_DIM ** -0.5 * math.log2(math.e), write)

    @pl.when(j == 1)
    def _():
        def write(c, val):
            k_ref[:, c * LANES:(c + 1) * LANES] = val
            kb_ref[:, c * LANES:(c + 1) * LANES] = val.astype(BF16)
        norm_rope(kg_ref, 1.0, write)

    @pl.when(j == 2)
    def _():
        v_ref[...] = acc
        vb_ref[...] = acc.astype(BF16)


def _qkv(h, w_qkv, qg, kg, cos_t, sin_t, gm, tm):
    t, d = h.shape
    n = 1024
    per = cos_t.shape[0] // tm
    row = lambda i, j: (i, 0)
    return pl.pallas_call(
        _qkv_kernel,
        grid=(t // tm, 3),
        in_specs=[pl.BlockSpec((tm, d), row),
                  pl.BlockSpec((d, n), lambda i, j: (0, j)),
                  pl.BlockSpec((1, LANES), lambda i, j: (0, 0)),
                  pl.BlockSpec((1, LANES), lambda i, j: (0, 0)),
                  pl.BlockSpec((tm, LANES), lambda i, j: (i % per, 0)),
                  pl.BlockSpec((tm, LANES), lambda i, j: (i % per, 0)),
                  pl.BlockSpec((LANES, LANES), lambda i, j: (0, 0))],
        out_specs=[pl.BlockSpec((tm, n), row)] * 5,
        out_shape=[jax.ShapeDtypeStruct((t, n), BF16), jax.ShapeDtypeStruct((t, n), F32),
                   jax.ShapeDtypeStruct((t, n), BF16), jax.ShapeDtypeStruct((t, n), F32),
                   jax.ShapeDtypeStruct((t, n), BF16)],
        compiler_params=_cparams("parallel", "arbitrary"),
        name="qkv_proj",
    )(h, w_qkv, qg, kg, cos_t, sin_t, gm)


def _proj_kernel(h_ref, w_ref, o_ref, *, act):
    acc = _dot(h_ref[...], w_ref[...])
    if act == "sigmoid":
        acc = jax.nn.sigmoid(acc)
    o_ref[...] = acc.astype(o_ref.dtype)


def _proj(h, w, tm, tn, act, out_dtype, name):
    t, d = h.shape
    n = w.shape[1]
    return pl.pallas_call(
        functools.partial(_proj_kernel, act=act),
        grid=(t // tm, n // tn),
        in_specs=[pl.BlockSpec((tm, d), lambda i, j: (i, 0)), pl.BlockSpec((d, tn), lambda i, j: (0, j))],
        out_specs=pl.BlockSpec((tm, tn), lambda i, j: (i, j)),
        out_shape=jax.ShapeDtypeStruct((t, n), out_dtype),
        compiler_params=_cparams("parallel", "arbitrary"),
        name=name,
    )(h, w)


def _subln(o, g):
    ms = jnp.mean(o * o, axis=-1, keepdims=True)
    return o * lax.rsqrt(ms + EPS) * g * (1.0 - LAM_INIT)


ATT_ROWS = 128


def _attn_prompt_kernel(qi_tab, ki_tab, lam_ref, q_ref, k_ref, v_ref, g_ref, o_ref,
                        qs_scr, vx_scr, m_scr, acc_scr, *, tq, tk):
    p = pl.program_id(2)
    qi = qi_tab[p]
    ki = ki_tab[p]
    ratio = tk // tq
    n_slab = tk // LANES

    @pl.when(ki == 0)
    def _():
        q = q_ref[...]
        lane = lax.broadcasted_iota(jnp.int32, q.shape, 1)
        zero = jnp.zeros_like(q)
        qs_scr[0:tq, :] = jnp.where(lane < DA_HEAD_DIM, q, zero)
        qs_scr[tq:2 * tq, :] = jnp.where(lane >= DA_HEAD_DIM, q, zero)
        m_scr[...] = jnp.full(m_scr.shape, NEG_INF, F32)
        acc_scr[...] = jnp.zeros(acc_scr.shape, F32)
        vx_scr[:, LANES:2 * LANES] = jnp.ones((tk, LANES), BF16)

    vx_scr[:, 0:LANES] = v_ref[...]

    def step(masked):
        for rb in range(2 * tq // ATT_ROWS):
            rows = slice(rb * ATT_ROWS, (rb + 1) * ATT_ROWS)
            s = _dot_nt(qs_scr[rows, :], k_ref[...])
            slabs = [s[:, c * LANES:(c + 1) * LANES] for c in range(n_slab)]
            if masked:
                row = lax.broadcasted_iota(jnp.int32, (ATT_ROWS, LANES), 0) + rb * ATT_ROWS
                lane = lax.broadcasted_iota(jnp.int32, (ATT_ROWS, LANES), 1)
                qchunk = (row % tq) // CHUNK + (qi % ratio) * (tq // CHUNK)
                slabs = [jnp.where((c * LANES + lane) // CHUNK <= qchunk, sl, NEG_INF)
                         for c, sl in enumerate(slabs)]
            mt = slabs[0]
            for sl in slabs[1:]:
                mt = jnp.maximum(mt, sl)
            m_prev = m_scr[rows, :]
            m_new = jnp.maximum(m_prev, jnp.max(mt, axis=-1, keepdims=True))
            alpha = jnp.exp2(m_prev - m_new)
            e = jnp.concatenate([jnp.exp2(sl - m_new).astype(BF16) for sl in slabs], axis=1)
            pv = _dot(e, vx_scr[...])
            acc_scr[rows, :] = jnp.concatenate([alpha, alpha], axis=1) * acc_scr[rows, :] + pv
            m_scr[rows, :] = m_new

    @pl.when(ki < qi // ratio)
    def _():
        step(False)

    @pl.when(ki == qi // ratio)
    def _():
        step(True)
        o = acc_scr[:, 0:LANES] / acc_scr[:, LANES:2 * LANES]
        o = o[0:tq] - lam_ref[...] * o[tq:2 * tq]
        o_ref[...] = _subln(o, g_ref[...]).astype(o_ref.dtype)


def _attn_prompt(q, k, v, lam_row, subln_g, batch, seq, tq, tk):
    d = q.shape[1]
    assert tk % tq == 0 and seq % tk == 0 and tq % CHUNK == 0
    qi_l, ki_l = [], []
    for a in range(seq // tq):
        for b in range(a * tq // tk + 1):
            qi_l.append(a)
            ki_l.append(b)
    qi_tab = jnp.asarray(qi_l, jnp.int32)
    ki_tab = jnp.asarray(ki_l, jnp.int32)
    q3, k3, v3 = (a.reshape(batch, seq, d) for a in (q, k, v))
    qspec = pl.BlockSpec((None, tq, LANES), lambda b, h, p, qt, kt: (b, qt[p], h))
    kspec = pl.BlockSpec((None, tk, LANES), lambda b, h, p, qt, kt: (b, kt[p], h))
    vec = pl.BlockSpec((1, LANES), lambda b, h, p, qt, kt: (0, 0))
    out = pl.pallas_call(
        functools.partial(_attn_prompt_kernel, tq=tq, tk=tk),
        grid_spec=pltpu.PrefetchScalarGridSpec(
            num_scalar_prefetch=2,
            grid=(batch, DA_HEADS, len(qi_l)),
            in_specs=[vec, qspec, kspec, kspec, vec],
            out_specs=qspec,
            scratch_shapes=[pltpu.VMEM((2 * tq, LANES), BF16), pltpu.VMEM((tk, 2 * LANES), BF16),
                            pltpu.VMEM((2 * tq, LANES), F32), pltpu.VMEM((2 * tq, 2 * LANES), F32)]),
        out_shape=jax.ShapeDtypeStruct((batch, seq, d), BF16),
        compiler_params=_cparams("parallel", "parallel", "arbitrary"),
        name="attn_prompt",
    )(qi_tab, ki_tab, lam_row, q3, k3, v3, subln_g)
    return out.reshape(batch * seq, d)


def _attn_sample_kernel(lam_ref, q_ref, kn_ref, vn_ref, ck_ref, cv_ref, g_ref, o_ref, *, past, sq):
    for h in range(DA_HEADS):
        hs = slice(h * LANES, (h + 1) * LANES)
        q = q_ref[:, hs]
        lane = lax.broadcasted_iota(jnp.int32, q.shape, 1)
        zero = jnp.zeros_like(q)
        qs = jnp.concatenate([jnp.where(lane < DA_HEAD_DIM, q, zero),
                              jnp.where(lane >= DA_HEAD_DIM, q, zero)], axis=0)
        kc = ck_ref[pl.ds(h, past, stride=DA_HEADS), :].astype(BF16)
        vc = cv_ref[pl.ds(h, past, stride=DA_HEADS), :].astype(BF16)
        s_c = _dot_nt(qs, kc)
        s_n = _dot_nt(qs, kn_ref[:, hs])

        def masked(s, key0):
            row = lax.broadcasted_iota(jnp.int32, s.shape, 0)
            col = lax.broadcasted_iota(jnp.int32, s.shape, 1)
            qchunk = (past + row % sq) // CHUNK
            return jnp.where((key0 + col) // CHUNK <= qchunk, s, NEG_INF)

        s_c = masked(s_c, 0)
        s_n = masked(s_n, past)
        m = jnp.maximum(jnp.max(s_c, axis=-1, keepdims=True), jnp.max(s_n, axis=-1, keepdims=True))
        e_c = jnp.exp2(s_c - m)
        e_n = jnp.exp2(s_n - m)
        l = jnp.sum(e_c, axis=-1, keepdims=True) + jnp.sum(e_n, axis=-1, keepdims=True)
        o = (_dot(e_c.astype(BF16), vc) + _dot(e_n.astype(BF16), vn_ref[:, hs])) / l
        o = o[0:sq] - lam_ref[...] * o[sq:2 * sq]
        o_ref[:, hs] = _subln(o, g_ref[...]).astype(o_ref.dtype)


def _attn_sample(q, kn, vn, cache_k, cache_v, lam_row, subln_g, batch, sq):
    d = q.shape[1]
    past = cache_k.shape[-3]
    ck = cache_k.reshape(batch, past * DA_HEADS, LANES)
    cv = cache_v.reshape(batch, past * DA_HEADS, LANES)
    q3, k3, v3 = (a.reshape(batch, sq, d) for a in (q, kn, vn))
    new = pl.BlockSpec((None, sq, d), lambda b: (b, 0, 0))
    old = pl.BlockSpec((None, past * DA_HEADS, LANES), lambda b: (b, 0, 0))
    vec = pl.BlockSpec((1, LANES), lambda b: (0, 0))
    out = pl.pallas_call(
        functools.partial(_attn_sample_kernel, past=past, sq=sq),
        grid=(batch,),
        in_specs=[vec, new, new, new, old, old, vec],
        out_specs=new,
        out_shape=jax.ShapeDtypeStruct((batch, sq, d), BF16),
        compiler_params=_cparams("parallel"),
        name="attn_sample",
    )(lam_row, q3, k3, v3, ck, cv, subln_g)
    return out.reshape(batch * sq, d)


RW_L = 64
RW_PAIRS = 8
RW_CPS = 2


def _mm3(a, b, dot=_dot):
    ah, al = _split2(a)
    bh, bl = _split2(b)
    return dot(ah, bh) + dot(ah, bl) + dot(al, bh)


def _terms(x, n):
    return (x.astype(BF16),) if n == 1 else _split2(x)


def _mmt(at, bt, dot=_dot):
    acc = dot(at[0], bt[0])
    if len(bt) > 1:
        acc = acc + dot(at[0], bt[1])
    if len(at) > 1:
        acc = acc + dot(at[1], bt[0])
    return acc


RW_PREC = dict(a_all=(1, 1), neumann=(1, 1), av=(1, 1), pu=(1, 1), qy=(1, 1), mh=(1, 1), state=(1, 1))


def _softplus(z):
    return jnp.maximum(z, 0.0) + jnp.log(1.0 + jnp.exp(-jnp.abs(z)))


def _rwkv_kernel(rw_ref, sh0_ref, mu_ref, lw_ref, w0_ref, a0_ref, kk_ref, ka_ref, rk_ref, lng_ref, lnb_ref,
                 h0_ref, gm_ref, tri_ref, ob_ref, hout_ref, h_scr, last_scr, *, n_valid, cps):
    L = RW_L
    rows = cps * L
    c = pl.program_id(1)
    d = ob_ref.shape[-1]

    @pl.when(c == 0)
    def _():
        h_scr[...] = h0_ref[...]
        last_scr[...] = sh0_ref[...]

    rw = rw_ref[...]
    row_w = lax.broadcasted_iota(jnp.int32, rw.shape, 0)
    prev = jnp.where(row_w == 0, last_scr[...], pltpu.roll(rw, 1, 0))
    last_scr[...] = rw[rows - 1:rows, :]
    xm = rw + (prev - rw) * mu_ref[...]
    r = xm[:, 0:d]
    k = xm[:, d:2 * d]
    v = xm[:, 2 * d:3 * d]
    lo = xm[:, 3 * d:3 * d + LORA_PAD]
    lane_l = lax.broadcasted_iota(jnp.int32, lo.shape, 1)
    z = jnp.where(lane_l < W_LORA, jnp.tanh(lo),
                  jnp.where(lane_l < W_LORA + A_LORA, lo, jax.nn.sigmoid(lo)))
    lora = _dot(z.astype(BF16), lw_ref[...])
    w = -_softplus(-(w0_ref[...] + lora[:, 0:d])) - 0.5
    ld = -jnp.exp(w)
    a = jax.nn.sigmoid(a0_ref[...] + lora[:, d:2 * d])
    g = lora[:, 2 * d:3 * d]
    kkr = k * kk_ref[...]
    kmod = k * (1.0 + (a - 1.0) * ka_ref[...])
    rkk = r * kmod * rk_ref[...]
    gm = gm_ref[...]
    if n_valid < rows:
        valid = lax.broadcasted_iota(jnp.int32, ld.shape, 0) < n_valid
        ld = jnp.where(valid, ld, 0.0)
        kkr = jnp.where(valid, kkr, 0.0)
        kmod = jnp.where(valid, kmod, 0.0)

    tri = tri_ref[...]
    l1, l2, l3 = _split3(ld)
    cin = _dot(tri, l1) + _dot(tri, l2) + _dot(tri, l3)
    cex = cin - ld
    c_last = [cin[(ch + 1) * L - 1:(ch + 1) * L, :] for ch in range(cps)]
    e_in = jnp.exp(cin)
    e_ex = jnp.exp(cex)
    e_neg = jnp.exp(-cin)
    e_tail = jnp.exp(jnp.concatenate([jnp.broadcast_to(cl, (L, d)) for cl in c_last], axis=0) - cin)
    g_last = [jnp.exp(cl) for cl in c_last]

    row = lax.broadcasted_iota(jnp.int32, (L, LANES), 0)
    lane = lax.broadcasted_iota(jnp.int32, (L, LANES), 1)
    m_lo = lane < RW_HEAD
    tri_s = row > (lane % L)
    tri_i = row >= (lane % L)
    eye_pair = jnp.where(row == (lane % L), 1.0, 0.0).astype(F32)
    r2 = lax.broadcasted_iota(jnp.int32, (LANES, LANES), 0)
    c2 = lax.broadcasted_iota(jnp.int32, (LANES, LANES), 1)
    bd_mask = (r2 < RW_HEAD) == (c2 < RW_HEAD)
    diag_mask = r2 == c2

    def bd(x):
        z0 = jnp.zeros_like(x)
        return jnp.concatenate([jnp.where(m_lo, x, z0), jnp.where(m_lo, z0, x)], axis=0)

    def bdt(t):
        return tuple(bd(x) for x in t)

    def cat(ts, axis):
        return tuple(jnp.concatenate(xs, axis=axis) for xs in zip(*ts))

    prec = RW_PREC
    items = [(slice(ch * L, (ch + 1) * L), slice(p * LANES, (p + 1) * LANES))
             for ch in range(cps) for p in range(RW_PAIRS)]
    pairs = range(len(items))

    v_p = [v[rs, sl] for rs, sl in items]
    kk_p = []
    for rs, sl in items:
        kk_raw = kkr[rs, sl]
        ss = _group_mean(kk_raw * kk_raw, gm) * RW_HEAD
        kk_p.append(kk_raw / jnp.maximum(jnp.sqrt(ss), 1e-12))
    b_p = [kk_p[i] * a[rs, sl] for i, (rs, sl) in enumerate(items)]
    abar = [-kk_p[i] * e_ex[rs, sl] for i, (rs, sl) in enumerate(items)]
    rbar = [r[rs, sl] * e_in[rs, sl] for rs, sl in items]
    bbar = [b_p[i] * e_neg[rs, sl] for i, (rs, sl) in enumerate(items)]
    kbar = [kmod[rs, sl] * e_neg[rs, sl] for rs, sl in items]
    btil = [b_p[i] * e_tail[rs, sl] for i, (rs, sl) in enumerate(items)]
    ktil = [kmod[rs, sl] * e_tail[rs, sl] for rs, sl in items]

    na, nb = prec["a_all"]
    abar_t = [_terms(abar[p], max(na, prec["pu"][1])) for p in pairs]
    v_t = [_terms(v_p[p], max(prec["av"][1], prec["qy"][1], prec["mh"][1])) for p in pairs]
    a_all = []
    for p in pairs:
        lhs = cat([abar_t[p][:na], _terms(rbar[p], na)], 0)
        rhs = cat([bdt(_terms(bbar[p], nb)), bdt(_terms(kbar[p], nb))], 0)
        a_all.append(_mmt(lhs, rhs, _dot_nt))
    n_ab = [jnp.where(tri_s, a_all[p][0:L, 0:2 * L], 0.0) for p in pairs]
    a_ak = [jnp.where(tri_s, a_all[p][0:L, 2 * L:4 * L], 0.0) for p in pairs]
    a_rb = [jnp.where(tri_i, a_all[p][L:2 * L, 0:2 * L], 0.0) for p in pairs]
    a_rk = [jnp.where(tri_i, a_all[p][L:2 * L, 2 * L:4 * L], 0.0) for p in pairs]

    na, nb = prec["neumann"]
    t_inv = [eye_pair + n_ab[p] for p in pairs]
    pw = n_ab
    for _ in range(int(math.log2(L)) - 1):
        pw_t = [_terms(pw[p], max(na, nb)) for p in pairs]
        pw = [_mmt(pw_t[p][:na], bdt(pw_t[p][:nb])) for p in pairs]
        pw_b = [bdt(_terms(pw[p], nb)) for p in pairs]
        t_inv = [t_inv[p] + _mmt(_terms(t_inv[p], na), pw_b[p]) for p in pairs]

    na, nb = prec["av"]
    av = [_mmt(_terms(a_ak[p], na), bdt(v_t[p][:nb])) for p in pairs]
    na, nb = prec["pu"]
    pu = [_mmt(_terms(t_inv[p], na), cat([bdt(abar_t[p][:nb]), bdt(_terms(av[p], nb))], 1)) for p in pairs]
    p_m = [x[:, 0:LANES] for x in pu]
    u0 = [x[:, LANES:2 * LANES] for x in pu]
    na, nb = prec["qy"]
    nmh = prec["mh"][1]
    p_t = [_terms(p_m[p], max(nb, nmh)) for p in pairs]
    u_t = [_terms(u0[p], max(nb, nmh)) for p in pairs]
    q_m, y0 = [], []
    for p in pairs:
        zero = tuple(jnp.zeros((2 * L, LANES), BF16) for _ in range(nb))
        rhs2 = cat([cat([bdt(p_t[p][:nb]), bdt(u_t[p][:nb])], 1),
                    cat([zero, bdt(v_t[p][:nb])], 1)], 0)
        qy = _mmt(cat([_terms(a_rb[p], na), _terms(a_rk[p], na)], 1), rhs2)
        q_m.append(rbar[p] + qy[:, 0:LANES])
        y0.append(qy[:, LANES:2 * LANES])

    na, nb = prec["mh"]
    btil_t = [_terms(btil[p], na) for p in pairs]
    m_full = [_mmt(btil_t[p], p_t[p][:nb], _dot_tn) for p in pairs]
    h0_full = [_mmt(cat([btil_t[p], _terms(ktil[p], na)], 0), cat([u_t[p][:nb], v_t[p][:nb]], 0), _dot_tn)
               for p in pairs]
    m_bd = [jnp.where(bd_mask, m_full[i], 0.0) + jnp.where(diag_mask, g_last[i // RW_PAIRS][:, sl], 0.0)
            for i, (rs, sl) in enumerate(items)]
    h0_bd = [jnp.where(bd_mask, h0_full[p], 0.0) for p in pairs]

    na, nb = prec["state"]
    h_cur = [h_scr[p] for p in range(RW_PAIRS)]
    y = []
    for ch in range(cps):
        h_t = [_terms(h, nb) for h in h_cur]
        for p in range(RW_PAIRS):
            i = ch * RW_PAIRS + p
            y.append(_mmt(_terms(q_m[i], na), h_t[p]) + y0[i])
        h_cur = [_mmt(_terms(m_bd[ch * RW_PAIRS + p], na), h_t[p]) + h0_bd[ch * RW_PAIRS + p]
                 for p in range(RW_PAIRS)]
    for p in range(RW_PAIRS):
        h_scr[p] = h_cur[p]

    for i, (rs, sl) in enumerate(items):
        mu_y = _group_mean(y[i], gm)
        dy = y[i] - mu_y
        var = _group_mean(dy * dy, gm)
        yn = dy * lax.rsqrt(var + RW_GN_EPS) * lng_ref[:, sl] + lnb_ref[:, sl]
        bonus = _group_mean(rkk[rs, sl], gm) * RW_HEAD * v_p[i]
        ob_ref[rs, sl] = ((yn + bonus) * g[rs, sl]).astype(ob_ref.dtype)

    @pl.when(c == pl.num_programs(1) - 1)
    def _():
        hout_ref[...] = h_scr[...]


def _rwkv(rw3, shift0, h0_bd, prm, gm, n_valid, cps):
    batch, seq, wcols = rw3.shape
    d = prm["w0"].shape[1]
    rows = cps * RW_L
    nc = seq // rows
    tri = jnp.kron(jnp.eye(cps, dtype=F32), jnp.tril(jnp.ones((RW_L, RW_L), F32))).astype(BF16)
    const2 = lambda b, c: (0, 0)
    vec = pl.BlockSpec((1, d), const2)
    st = pl.BlockSpec((None, RW_PAIRS, LANES, LANES), lambda b, c: (b, 0, 0, 0))
    return pl.pallas_call(
        functools.partial(_rwkv_kernel, n_valid=n_valid, cps=cps),
        grid=(batch, nc),
        in_specs=[pl.BlockSpec((None, rows, wcols), lambda b, c: (b, c, 0)),
                  pl.BlockSpec((None, 1, wcols), lambda b, c: (b, 0, 0)),
                  pl.BlockSpec((1, wcols), const2),
                  pl.BlockSpec(prm["lw"].shape, const2),
                  vec, vec, vec, vec, vec, vec, vec, st,
                  pl.BlockSpec((LANES, LANES), const2),
                  pl.BlockSpec((rows, rows), const2)],
        out_specs=[pl.BlockSpec((None, rows, d), lambda b, c: (b, c, 0)), st],
        out_shape=[jax.ShapeDtypeStruct((batch, seq, d), BF16),
                   jax.ShapeDtypeStruct((batch, RW_PAIRS, LANES, LANES), F32)],
        scratch_shapes=[pltpu.VMEM((RW_PAIRS, LANES, LANES), F32), pltpu.VMEM((1, wcols), F32)],
        compiler_params=_cparams("parallel", "arbitrary"),
        name="rwkv7",
    )(rw3, shift0, prm["mu"], prm["lw"], prm["w0"], prm["a0"], prm["k_k"], prm["k_a"], prm["r_k"],
      prm["ln_g"], prm["ln_b"], h0_bd, gm, tri)


def _state_to_bd(s):
    b = s.shape[0]
    ht = jnp.swapaxes(s, -1, -2).reshape(b, RW_PAIRS, 2, RW_HEAD, RW_HEAD)
    z = jnp.zeros_like(ht[:, :, 0])
    top = jnp.concatenate([ht[:, :, 0], z], axis=-1)
    bot = jnp.concatenate([z, ht[:, :, 1]], axis=-1)
    return jnp.concatenate([top, bot], axis=-2)


def _bd_to_state(hbd):
    b = hbd.shape[0]
    h0 = hbd[:, :, 0:RW_HEAD, 0:RW_HEAD]
    h1 = hbd[:, :, RW_HEAD:, RW_HEAD:]
    ht = jnp.stack([h0, h1], axis=2).reshape(b, 2 * RW_PAIRS, RW_HEAD, RW_HEAD)
    return jnp.swapaxes(ht, -1, -2)


def _route(logits):
    lane = lax.broadcasted_iota(jnp.int32, logits.shape, 1).astype(F32)
    big = float(LANES)
    lg = jnp.where(lane < N_GROUPS, logits, NEG_INF)
    mg = jnp.max(lg, axis=-1, keepdims=True)
    sg = jnp.sum(jnp.exp(lg - mg), axis=-1, keepdims=True)
    p_top = 1.0 / sg
    g_idx = jnp.min(jnp.where(lg == mg, lane, big), axis=-1, keepdims=True)
    e0 = N_GROUPS + EXPERTS_PER_GROUP * g_idx
    emask = jnp.where(lane >= e0, jnp.where(lane < e0 + EXPERTS_PER_GROUP, 1.0, 0.0), 0.0) > 0.5
    le = jnp.where(emask, logits, NEG_INF)
    me = jnp.max(le, axis=-1, keepdims=True)
    ee = jnp.exp(le - me)
    pe = ee / jnp.sum(ee, axis=-1, keepdims=True)
    pe = jnp.where(emask, pe, -1.0)
    v1 = jnp.max(pe, axis=-1, keepdims=True)
    i1 = jnp.min(jnp.where(pe == v1, lane, big), axis=-1, keepdims=True)
    pe2 = jnp.where(lane == i1, -1.0, pe)
    v2 = jnp.max(pe2, axis=-1, keepdims=True)
    i2 = jnp.min(jnp.where(pe2 == v2, lane, big), axis=-1, keepdims=True)
    den = v1 + v2
    ew = jnp.where(lane == i1, v1 / den, 0.0) + jnp.where(lane == i2, v2 / den, 0.0)
    return p_top * ew + jnp.where(lane == 0.0, g_idx, 0.0)


def _merge_kernel(oa_ref, ob_ref, gates_ref, x_ref, gt1_ref, sc2_ref, sh2_ref, g2_ref,
                  woa_ref, wob_ref, wout_ref, wr_ref, br_ref, x1_ref, hx_ref):
    d = x_ref.shape[1]
    a_out = _dot(oa_ref[...], woa_ref[...])
    b_out = _dot(ob_ref[...], wob_ref[...])
    merged = gates_ref[:, 0:d].astype(F32) * a_out + gates_ref[:, d:2 * d].astype(F32) * b_out
    x1 = x_ref[...] + gt1_ref[...] * _dot(merged.astype(BF16), wout_ref[...])
    x1_ref[...] = x1
    ms = jnp.mean(x1 * x1, axis=-1, keepdims=True)
    h2 = x1 * lax.rsqrt(ms + EPS) * g2_ref[...]
    h2 = h2 * (1.0 + sc2_ref[...]) + sh2_ref[...]
    logits = _mm3(h2, wr_ref[...]) + br_ref[...]
    hx_ref[:, 0:d] = h2
    hx_ref[:, d:d + LANES] = _route(logits)


def _merge(oa, ob, gates, x2d, gt1, sc2, sh2, g2, w_oa, w_ob, w_out, wr, br, tm, seq):
    t, d = x2d.shape
    row = lambda i: (i, 0)
    const = lambda i: (0, 0)
    full = lambda w: pl.BlockSpec(w.shape, const)
    return pl.pallas_call(
        _merge_kernel,
        grid=(t // tm,),
        in_specs=[pl.BlockSpec((tm, d), row), pl.BlockSpec((tm, d), row), pl.BlockSpec((tm, 2 * d), row),
                  pl.BlockSpec((tm, d), row), _mod_spec(gt1, tm, seq), _mod_spec(sc2, tm, seq),
                  _mod_spec(sh2, tm, seq), pl.BlockSpec((1, d), const),
                  full(w_oa), full(w_ob), full(w_out), full(wr), full(br)],
        out_specs=[pl.BlockSpec((tm, d), row), pl.BlockSpec((tm, d + LANES), row)],
        out_shape=[jax.ShapeDtypeStruct((t, d), F32), jax.ShapeDtypeStruct((t, d + LANES), F32)],
        compiler_params=_cparams("parallel"),
        name="merge_router",
    )(oa, ob, gates, x2d, gt1, sc2, sh2, g2, w_oa, w_ob, w_out, wr, br)


def _expert_ffn(xb, comb, lane_e, w13, w2):
    au = _dot(xb, w13)
    f = au.shape[1] // 2
    a = au[:, 0:f]
    u = au[:, f:2 * f]
    lane = lax.broadcasted_iota(jnp.int32, comb.shape, 1)
    cw = jnp.sum(jnp.where(lane == lane_e, comb, 0.0), axis=-1, keepdims=True)
    act = a * jax.nn.sigmoid(a) * u * cw
    return _dot(act.astype(BF16), w2)


def _moe_dense_kernel(hx_ref, w13_ref, w2_ref, x1_ref, gt2_ref, y_ref, xb_scr, acc_scr):
    e = pl.program_id(1)
    d = y_ref.shape[1]

    @pl.when(e == 0)
    def _():
        xb_scr[...] = hx_ref[:, 0:d].astype(BF16)
        acc_scr[...] = jnp.zeros(acc_scr.shape, F32)

    acc_scr[...] += _expert_ffn(xb_scr[...], hx_ref[:, d:d + LANES], e + N_GROUPS, w13_ref[...], w2_ref[...])

    @pl.when(e == pl.num_programs(1) - 1)
    def _():
        y_ref[...] = x1_ref[...] + gt2_ref[...] * acc_scr[...]


def _moe_dense(hx, w13, w2, x1, gt2, tm, seq):
    t, d = x1.shape
    ne, _, f2 = w13.shape
    row = lambda i, e: (i, 0)
    return pl.pallas_call(
        _moe_dense_kernel,
        grid=(t // tm, ne),
        in_specs=[pl.BlockSpec((tm, d + LANES), row),
                  pl.BlockSpec((None, d, f2), lambda i, e: (e, 0, 0)),
                  pl.BlockSpec((None, f2 // 2, d), lambda i, e: (e, 0, 0)),
                  pl.BlockSpec((tm, d), row), _mod_spec(gt2, tm, seq)],
        out_specs=pl.BlockSpec((tm, d), row),
        out_shape=jax.ShapeDtypeStruct((t, d), F32),
        scratch_shapes=[pltpu.VMEM((tm, d), BF16), pltpu.VMEM((tm, d), F32)],
        compiler_params=_cparams("parallel", "arbitrary"),
        name="moe_dense",
    )(hx, w13, w2, x1, gt2)


MOE_TM = 512
MOE_TS = 1024


def _plan_kernel(rt_ref, tri_ref, rank_ref, cnt_ref, carry_scr):
    @pl.when(pl.program_id(0) == 0)
    def _():
        carry_scr[...] = jnp.zeros(carry_scr.shape, F32)

    rt = rt_ref[...]
    lane = lax.broadcasted_iota(jnp.int32, rt.shape, 1).astype(F32)
    onehot = jnp.where(lane == rt[:, 0:1], 1.0, 0.0)
    before = _dot(tri_ref[...], onehot.astype(BF16)) + carry_scr[...]
    rank_ref[...] = jnp.sum(onehot * before, axis=-1, keepdims=True).astype(jnp.int32)
    carry_scr[...] += jnp.sum(onehot, axis=0, keepdims=True)
    cnt_ref[...] = carry_scr[...]


def _moe_plan(hx, d):
    t = hx.shape[0]
    tp = 512
    tri = jnp.tril(jnp.ones((tp, tp), F32), -1).astype(BF16)
    return pl.pallas_call(
        _plan_kernel,
        grid=(t // tp,),
        in_specs=[pl.BlockSpec((tp, LANES), lambda i: (i, d // LANES)),
                  pl.BlockSpec((tp, tp), lambda i: (0, 0))],
        out_specs=[pl.BlockSpec((tp, 1), lambda i: (i, 0)), pl.BlockSpec((1, LANES), lambda i: (0, 0))],
        out_shape=[jax.ShapeDtypeStruct((t, 1), jnp.int32), jax.ShapeDtypeStruct((1, LANES), F32)],
        scratch_shapes=[pltpu.VMEM((1, LANES), F32)],
        compiler_params=_cparams("arbitrary"),
        name="moe_plan",
    )(hx, tri)


def _row_copy(src_hbm, src_row, dst, dst_row, sem):
    return pltpu.make_async_copy(src_hbm.at[pl.ds(src_row, 1), :], dst.at[pl.ds(dst_row, 1), :], sem)


def _dispatch_kernel(pos_ref, hx_hbm, xs_in_hbm, xs_hbm, sem):
    del xs_in_hbm
    base = pl.program_id(0) * MOE_TS

    def issue(t, carry):
        _row_copy(hx_hbm, base + t, xs_hbm, pos_ref[t], sem).start()
        return carry

    def drain(t, carry):
        _row_copy(hx_hbm, base + t, xs_hbm, pos_ref[t], sem).wait()
        return carry

    lax.fori_loop(0, MOE_TS, issue, 0)
    lax.fori_loop(0, MOE_TS, drain, 0)


def _moe_dispatch(pos, hx, n_rows):
    t, w = hx.shape
    xs0 = jnp.zeros((n_rows, w), F32)
    return pl.pallas_call(
        _dispatch_kernel,
        grid=(t // MOE_TS,),
        in_specs=[pl.BlockSpec((MOE_TS,), lambda i: (i,), memory_space=pltpu.SMEM),
                  pl.BlockSpec(memory_space=pl.ANY), pl.BlockSpec(memory_space=pl.ANY)],
        out_specs=pl.BlockSpec(memory_space=pl.ANY),
        out_shape=jax.ShapeDtypeStruct((n_rows, w), F32),
        scratch_shapes=[pltpu.SemaphoreType.DMA(())],
        input_output_aliases={2: 0},
        compiler_params=_cparams("arbitrary"),
        name="moe_dispatch",
    )(pos, hx, xs0)


def _moe_bucket_kernel(tg_ref, tv_ref, xs_ref, w13_ref, w2_ref, ys_ref, xb_scr):
    i = pl.program_id(0)
    e = pl.program_id(1)
    d = ys_ref.shape[1]

    @pl.when(e == 0)
    def _():
        xb_scr[...] = xs_ref[:, 0:d].astype(BF16)
        ys_ref[...] = jnp.zeros(ys_ref.shape, F32)

    @pl.when(tv_ref[i] > 0)
    def _():
        lane_e = N_GROUPS + tg_ref[i] * EXPERTS_PER_GROUP + e
        ys_ref[...] += _expert_ffn(xb_scr[...], xs_ref[:, d:d + LANES], lane_e, w13_ref[...], w2_ref[...])


def _moe_buckets(tile_group, tile_valid, xs, w13, w2, d):
    n_rows, w = xs.shape
    _, _, f2 = w13.shape
    row = lambda i, e, tg, tv: (i, 0)
    wsel = lambda i, e, tg, tv: (tg[i] * EXPERTS_PER_GROUP + e, 0, 0)
    return pl.pallas_call(
        _moe_bucket_kernel,
        grid_spec=pltpu.PrefetchScalarGridSpec(
            num_scalar_prefetch=2,
            grid=(n_rows // MOE_TM, EXPERTS_PER_GROUP),
            in_specs=[pl.BlockSpec((MOE_TM, w), row),
                      pl.BlockSpec((None, d, f2), wsel), pl.BlockSpec((None, f2 // 2, d), wsel)],
            out_specs=pl.BlockSpec((MOE_TM, d), row),
            scratch_shapes=[pltpu.VMEM((MOE_TM, d), BF16)]),
        out_shape=jax.ShapeDtypeStruct((n_rows, d), F32),
        compiler_params=_cparams("parallel", "arbitrary"),
        name="moe_buckets",
    )(tile_group, tile_valid, xs, w13, w2)


def _collect_kernel(pos_ref, ys_hbm, x1_ref, gt2_ref, y_ref, buf, sem):
    tu = buf.shape[0]

    def issue(t, carry):
        _row_copy(ys_hbm, pos_ref[t], buf, t, sem).start()
        return carry

    def drain(t, carry):
        _row_copy(ys_hbm, pos_ref[t], buf, t, sem).wait()
        return carry

    lax.fori_loop(0, tu, issue, 0)
    lax.fori_loop(0, tu, drain, 0)
    y_ref[...] = x1_ref[...] + gt2_ref[...] * buf[...]


def _moe_collect(pos, ys, x1, gt2, seq):
    t, d = x1.shape
    tu = MOE_TS
    return pl.pallas_call(
        _collect_kernel,
        grid=(t // tu,),
        in_specs=[pl.BlockSpec((tu,), lambda i: (i,), memory_space=pltpu.SMEM),
                  pl.BlockSpec(memory_space=pl.ANY),
                  pl.BlockSpec((tu, d), lambda i: (i, 0)), _mod_spec(gt2, tu, seq)],
        out_specs=pl.BlockSpec((tu, d), lambda i: (i, 0)),
        out_shape=jax.ShapeDtypeStruct((t, d), F32),
        scratch_shapes=[pltpu.VMEM((tu, d), F32), pltpu.SemaphoreType.DMA(())],
        compiler_params=_cparams("arbitrary"),
        name="moe_collect",
    )(pos, ys, x1, gt2)


def _moe_sorted(hx, w13, w2, x1, gt2, seq):
    t, d = x1.shape
    rank, cnt = _moe_plan(hx, d)
    group = hx[:, d].astype(jnp.int32)
    counts = cnt[0, 0:N_GROUPS].astype(jnp.int32)
    padded = (counts + MOE_TM - 1) // MOE_TM * MOE_TM
    ends = jnp.cumsum(padded)
    starts = ends - padded
    pos = jnp.take(starts, group) + rank[:, 0]
    n_tiles = t // MOE_TM + N_GROUPS
    tile_start = jnp.arange(n_tiles, dtype=jnp.int32) * MOE_TM
    tile_group = jnp.minimum(jnp.sum((tile_start[:, None] >= ends[None, :]).astype(jnp.int32), axis=1),
                             N_GROUPS - 1)
    tile_valid = (tile_start < ends[-1]).astype(jnp.int32)
    xs = _moe_dispatch(pos, hx, n_tiles * MOE_TM)
    ys = _moe_buckets(tile_group, tile_valid, xs, w13, w2, d)
    return _moe_collect(pos, ys, x1, gt2, seq)


def _rope_tables(pos0, seq, reps):
    pos = (pos0 + jnp.arange(seq)).astype(F32)
    inv = ROPE_THETA ** (-jnp.arange(0, DA_HEAD_DIM, 2, dtype=F32) / DA_HEAD_DIM)
    ang = pos[:, None] * inv[None, :]
    cos, sin = jnp.cos(ang), jnp.sin(ang)
    cos_t = jnp.tile(jnp.concatenate([cos, cos], axis=-1), (reps, LANES // DA_HEAD_DIM))
    sin_t = jnp.tile(jnp.concatenate([-sin, sin], axis=-1), (reps, LANES // DA_HEAD_DIM))
    return cos_t, sin_t


def _layer(x, mod, pos0, past, wts):
    bx, sx, d = x.shape
    t = bx * sx
    x2d = x.reshape(t, d)
    per_batch = sx >= 512
    tm = 512 if per_batch else t

    def modv(i):
        if per_batch:
            return mod[:, i:i + 1, :]
        return jnp.repeat(mod[:, i, :], sx, axis=0)[None]

    sh1, sc1, gt1, sh2, sc2, gt2 = (modv(i) for i in range(6))
    cos_t, sin_t = _rope_tables(pos0, sx, 1 if per_batch else bx)

    tm_big = 1024 if (per_batch and sx % 1024 == 0) else tm
    h = _prenorm(x2d, sc1, sh1, wts["norm1_g"], tm, sx)
    q, k32, kb, v32, vb = _qkv(h, wts["w_qkv"], wts["qg"], wts["kg"], cos_t, sin_t, wts["gm"], tm_big)
    rw = _proj(h, wts["w_rw"], tm_big, wts["w_rw"].shape[1] // 3, None, F32, "rw_proj")
    gates = _proj(h, wts["w_gate"], tm_big, d, "sigmoid", BF16, "gate_proj")

    if past is None:
        oa = _attn_prompt(q, kb, vb, wts["lam"], wts["subln_g"], bx, sx, min(1024, sx), min(1024, sx))
        shift0 = jnp.zeros((bx, 1, rw.shape[1]), F32)
        h0_bd = jnp.zeros((bx, RW_PAIRS, LANES, LANES), F32)
    else:
        ck, cv, s0, sh0 = past
        oa = _attn_sample(q, kb, vb, ck, cv, wts["lam"], wts["subln_g"], bx, sx)
        shift0 = jnp.pad(sh0, ((0, 0), (0, 0), (0, rw.shape[1] - sh0.shape[-1])))
        h0_bd = _state_to_bd(s0.astype(F32))

    rw3 = rw.reshape(bx, sx, rw.shape[1])
    if sx % RW_L == 0:
        cps = RW_CPS if sx % (RW_CPS * RW_L) == 0 else 1
        rw_in, n_valid = rw3, cps * RW_L
    else:
        assert sx < RW_L
        cps = 1
        rw_in, n_valid = jnp.pad(rw3, ((0, 0), (0, RW_L - sx), (0, 0))), sx
    ob, h_bd = _rwkv(rw_in, shift0, h0_bd, wts, wts["gm"], n_valid, cps)
    ob = ob[:, 0:sx, :].reshape(t, d)

    x1, hx = _merge(oa, ob, gates, x2d, gt1, sc2, sh2, wts["norm2_g"],
                    wts["w_oa"], wts["w_ob"], wts["w_out"], wts["wr"], wts["br"], tm, sx)
    if per_batch and sx % MOE_TS == 0:
        y = _moe_sorted(hx, wts["w13"], wts["w2"], x1, gt2, sx)
    else:
        y = _moe_dense(hx, wts["w13"], wts["w2"], x1, gt2, tm, sx)

    n_cols = 3 * d + W_LORA + A_LORA + G_LORA
    return (y.reshape(bx, sx, d),
            k32.reshape(1, bx, sx, DA_HEADS, 2 * DA_HEAD_DIM),
            v32.reshape(1, bx, sx, DA_HEADS, 2 * DA_HEAD_DIM),
            _bd_to_state(h_bd)[None],
            rw3[:, sx - 1:sx, 0:n_cols][None])


def kernel(x_prompt, x_sample, cache_k, cache_v, state_wkv, state_shift, c_prompt, c_sample, norm1_g, norm2_g, w_ada, b_ada, w_in, da_qn_g, da_kn_g, da_lambda, da_subln_g, w_oa, rw_mu, rw_w0, rw_w2, rw_a0, rw_a2, rw_g2, rw_k_k, rw_k_a, rw_r_k, rw_ln_g, rw_ln_b, w_ob, w_out, router_g, router_g_b, router_e, router_e_b, exp_w1, exp_w3, exp_w2):
    b, s, d = x_prompt.shape
    bs = x_sample.shape[0]
    past_len = cache_k.shape[2]
    assert w_in.shape[0] == 1, "single layer"

    c_all = jnp.concatenate([c_prompt, c_sample], axis=0)
    c_all = jnp.pad(c_all, ((0, (-c_all.shape[0]) % 8), (0, 0)))
    mod, lam_tile = _adaln(c_all, w_ada[0], b_ada[0], da_lambda[0])
    mod_p = mod[0:b].reshape(b, 6, d)
    mod_s = mod[b:b + bs].reshape(bs, 6, d)

    win = w_in[0]
    rw_cols = 3 * d + W_LORA + A_LORA + G_LORA
    rw_pad = 3 * d + LORA_PAD
    w_rw = jnp.pad(win[:, 3 * d:3 * d + rw_cols], ((0, 0), (0, rw_pad - rw_cols))).astype(BF16)
    lw = jnp.zeros((LORA_PAD, 3 * d), F32)
    lw = lw.at[0:W_LORA, 0:d].set(rw_w2[0])
    lw = lw.at[W_LORA:W_LORA + A_LORA, d:2 * d].set(rw_a2[0])
    lw = lw.at[W_LORA + A_LORA:W_LORA + A_LORA + G_LORA, 2 * d:3 * d].set(rw_g2[0])
    f = exp_w1.shape[-1]
    wr = jnp.pad(jnp.concatenate([router_g[0], router_e[0]], axis=1), ((0, 0), (0, LANES - N_GROUPS - N_EXPERTS)))
    br = jnp.pad(jnp.concatenate([router_g_b[0], router_e_b[0]]), (0, LANES - N_GROUPS - N_EXPERTS))[None]
    half = jnp.ones((DA_HEAD_DIM, DA_HEAD_DIM), F32) / DA_HEAD_DIM
    wts = dict(
        norm1_g=norm1_g[0][None], norm2_g=norm2_g[0][None],
        w_qkv=win[:, 0:3 * d].astype(BF16), w_rw=w_rw, w_gate=win[:, 3 * d + rw_cols:].astype(BF16),
        qg=jnp.tile(da_qn_g[0], 2)[None], kg=jnp.tile(da_kn_g[0], 2)[None],
        gm=jnp.kron(jnp.eye(2, dtype=F32), half).astype(BF16),
        lam=lam_tile[0:1], subln_g=da_subln_g[0][None],
        mu=jnp.pad(rw_mu[0], (0, rw_pad - rw_cols))[None], lw=lw.astype(BF16),
        w0=rw_w0[0][None], a0=rw_a0[0][None], k_k=rw_k_k[0][None], k_a=rw_k_a[0][None],
        r_k=rw_r_k[0].reshape(1, d), ln_g=rw_ln_g[0][None], ln_b=rw_ln_b[0][None],
        w_oa=w_oa[0].astype(BF16), w_ob=w_ob[0].astype(BF16), w_out=w_out[0].astype(BF16),
        wr=wr, br=br,
        w13=jnp.concatenate([exp_w1[0], exp_w3[0]], axis=-1).reshape(N_EXPERTS, d, 2 * f).astype(BF16),
        w2=exp_w2[0].reshape(N_EXPERTS, f, d).astype(BF16),
    )

    out_p = _layer(x_prompt, mod_p, 0, None, wts)
    out_s = _layer(x_sample, mod_s, past_len,
                   (cache_k, cache_v, state_wkv[0], state_shift[0]), wts)
    return (out_p[0], out_s[0], out_p[1], out_p[2], out_p[3], out_p[4],
            out_s[1], out_s[2], out_s[3], out_s[4])
```

```python
import functools
import math

import jax
import jax.numpy as jnp
from jax import lax
from jax.experimental import pallas as pl
from jax.experimental.pallas import tpu as pltpu

F32 = jnp.float32
BF16 = jnp.bfloat16

EPS = 1e-6
NEG_INF = -1e30
CHUNK = 64
DA_HEADS = 8
DA_HEAD_DIM = 64
ROPE_THETA = 10000.0
RW_HEAD = 64
RW_GN_EPS = 64e-5
W_LORA, A_LORA, G_LORA = 64, 64, 160
N_GROUPS, EXPERTS_PER_GROUP = 4, 8
N_EXPERTS = N_GROUPS * EXPERTS_PER_GROUP
LAM_INIT = 0.8 - 0.6 * math.exp(-0.3 * 0)

LANES = 128
LORA_PAD = 384
VMEM_LIMIT = 56 * 1024 * 1024


def _cparams(*sem):
    return pltpu.CompilerParams(dimension_semantics=sem, vmem_limit_bytes=VMEM_LIMIT)


def _dot(a, b):
    return jnp.dot(a, b, preferred_element_type=F32)


def _dot_nt(a, b):
    return lax.dot_general(a, b, (((1,), (1,)), ((), ())), preferred_element_type=F32)


def _dot_tn(a, b):
    return lax.dot_general(a, b, (((0,), (0,)), ((), ())), preferred_element_type=F32)


def _split2(a):
    hi = a.astype(BF16)
    lo = (a - hi.astype(F32)).astype(BF16)
    return hi, lo


def _split3(a):
    hi = a.astype(BF16)
    r = a - hi.astype(F32)
    mid = r.astype(BF16)
    lo = (r - mid.astype(F32)).astype(BF16)
    return hi, mid, lo


def _group_mean(sq, gm):
    hi, lo = _split2(sq)
    return _dot(hi, gm) + _dot(lo, gm)


def _adaln_kernel(c_ref, w_ref, b_ref, l_ref, o_ref, lam_ref):
    c = c_ref[...]
    sc = (c * jax.nn.sigmoid(c)).astype(BF16)
    o_ref[...] = _dot(sc, w_ref[...].astype(BF16)) + b_ref[...]
    l = l_ref[...]
    s1 = jnp.sum(l[0:1] * l[1:2], axis=-1, keepdims=True)
    s2 = jnp.sum(l[2:3] * l[3:4], axis=-1, keepdims=True)
    lam = jnp.exp(s1) - jnp.exp(s2) + LAM_INIT
    lam_ref[...] = jnp.broadcast_to(lam, lam_ref.shape)


def _adaln(c_all, w_ada, b_ada, da_lambda):
    bp, d = c_all.shape
    n = w_ada.shape[1]
    tn = 1024
    return pl.pallas_call(
        _adaln_kernel,
        grid=(n // tn,),
        in_specs=[pl.BlockSpec((bp, d), lambda j: (0, 0)),
                  pl.BlockSpec((d, tn), lambda j: (0, j)),
                  pl.BlockSpec((1, tn), lambda j: (0, j)),
                  pl.BlockSpec(da_lambda.shape, lambda j: (0, 0))],
        out_specs=[pl.BlockSpec((bp, tn), lambda j: (0, j)),
                   pl.BlockSpec((8, LANES), lambda j: (0, 0))],
        out_shape=[jax.ShapeDtypeStruct((bp, n), F32), jax.ShapeDtypeStruct((8, LANES), F32)],
        compiler_params=_cparams("arbitrary"),
        name="adaln",
    )(c_all, w_ada, b_ada.reshape(1, n), da_lambda)


def _prenorm_kernel(x_ref, sc_ref, sh_ref, g_ref, o_ref):
    x = x_ref[...]
    ms = jnp.mean(x * x, axis=-1, keepdims=True)
    h = x * lax.rsqrt(ms + EPS) * g_ref[...]
    o_ref[...] = (h * (1.0 + sc_ref[...]) + sh_ref[...]).astype(o_ref.dtype)


def _mod_spec(mod, tm, seq):
    d = mod.shape[-1]
    if mod.shape[1] == 1:
        per = seq // tm
        return pl.BlockSpec((None, 1, d), lambda i, *_: (i // per, 0, 0))
    return pl.BlockSpec((None, tm, d), lambda i, *_: (0, i, 0))


def _prenorm(x2d, sc, sh, g, tm, seq):
    t, d = x2d.shape
    return pl.pallas_call(
        _prenorm_kernel,
        grid=(t // tm,),
        in_specs=[pl.BlockSpec((tm, d), lambda i: (i, 0)), _mod_spec(sc, tm, seq), _mod_spec(sh, tm, seq),
                  pl.BlockSpec((1, d), lambda i: (0, 0))],
        out_specs=pl.BlockSpec((tm, d), lambda i: (i, 0)),
        out_shape=jax.ShapeDtypeStruct((t, d), BF16),
        compiler_params=_cparams("parallel"),
        name="prenorm",
    )(x2d, sc, sh, g)


def _qkv_kernel(h_ref, w_ref, qg_ref, kg_ref, cos_ref, sin_ref, gm_ref,
                q_ref, k_ref, kb_ref, v_ref, vb_ref):
    j = pl.program_id(1)
    acc = _dot(h_ref[...], w_ref[...])
    n_slab = acc.shape[1] // LANES

    def norm_rope(gain_ref, scale, write):
        lane = lax.broadcasted_iota(jnp.int32, (acc.shape[0], LANES), 1)
        first_half = (lane % DA_HEAD_DIM) < (DA_HEAD_DIM // 2)
        for c in range(n_slab):
            a = acc[:, c * LANES:(c + 1) * LANES]
            ms = _group_mean(a * a, gm_ref[...])
            y = a * lax.rsqrt(ms + EPS) * gain_ref[...]
            rot = jnp.where(first_half, pltpu.roll(y, LANES - DA_HEAD_DIM // 2, 1),
                            pltpu.roll(y, DA_HEAD_DIM // 2, 1))
            write(c, (y * cos_ref[...] + rot * sin_ref[...]) * scale)

    @pl.when(j == 0)
    def _():
        def write(c, val):
            q_ref[:, c * LANES:(c + 1) * LANES] = val.astype(q_ref.dtype)
        norm_rope(qg_ref, DA_HEAD_DIM ** -0.5 * math.log2(math.e), write)

    @pl.when(j == 1)
    def _():
        def write(c, val):
            k_ref[:, c * LANES:(c + 1) * LANES] = val
            kb_ref[:, c * LANES:(c + 1) * LANES] = val.astype(BF16)
        norm_rope(kg_ref, 1.0, write)

    @pl.when(j == 2)
    def _():
        v_ref[...] = acc
        vb_ref[...] = acc.astype(BF16)


def _qkv(h, w_qkv, qg, kg, cos_t, sin_t, gm, tm):
    t, d = h.shape
    n = 1024
    per = cos_t.shape[0] // tm
    row = lambda i, j: (i, 0)
    return pl.pallas_call(
        _qkv_kernel,
        grid=(t // tm, 3),
        in_specs=[pl.BlockSpec((tm, d), row),
                  pl.BlockSpec((d, n), lambda i, j: (0, j)),
                  pl.BlockSpec((1, LANES), lambda i, j: (0, 0)),
                  pl.BlockSpec((1, LANES), lambda i, j: (0, 0)),
                  pl.BlockSpec((tm, LANES), lambda i, j: (i % per, 0)),
                  pl.BlockSpec((tm, LANES), lambda i, j: (i % per, 0)),
                  pl.BlockSpec((LANES, LANES), lambda i, j: (0, 0))],
        out_specs=[pl.BlockSpec((tm, n), row)] * 5,
        out_shape=[jax.ShapeDtypeStruct((t, n), BF16), jax.ShapeDtypeStruct((t, n), F32),
                   jax.ShapeDtypeStruct((t, n), BF16), jax.ShapeDtypeStruct((t, n), F32),
                   jax.ShapeDtypeStruct((t, n), BF16)],
        compiler_params=_cparams("parallel", "arbitrary"),
        name="qkv_proj",
    )(h, w_qkv, qg, kg, cos_t, sin_t, gm)


def _proj_kernel(h_ref, w_ref, o_ref, *, act):
    acc = _dot(h_ref[...], w_ref[...])
    if act == "sigmoid":
        acc = jax.nn.sigmoid(acc)
    o_ref[...] = acc.astype(o_ref.dtype)


def _proj(h, w, tm, tn, act, out_dtype, name):
    t, d = h.shape
    n = w.shape[1]
    return pl.pallas_call(
        functools.partial(_proj_kernel, act=act),
        grid=(t // tm, n // tn),
        in_specs=[pl.BlockSpec((tm, d), lambda i, j: (i, 0)), pl.BlockSpec((d, tn), lambda i, j: (0, j))],
        out_specs=pl.BlockSpec((tm, tn), lambda i, j: (i, j)),
        out_shape=jax.ShapeDtypeStruct((t, n), out_dtype),
        compiler_params=_cparams("parallel", "arbitrary"),
        name=name,
    )(h, w)


def _subln(o, g):
    ms = jnp.mean(o * o, axis=-1, keepdims=True)
    return o * lax.rsqrt(ms + EPS) * g * (1.0 - LAM_INIT)


ATT_ROWS = 128


def _attn_prompt_kernel(qi_tab, ki_tab, lam_ref, q_ref, k_ref, v_ref, g_ref, o_ref,
                        qs_scr, vx_scr, m_scr, acc_scr, *, tq, tk):
    p = pl.program_id(2)
    qi = qi_tab[p]
    ki = ki_tab[p]
    ratio = tk // tq
    n_slab = tk // LANES

    @pl.when(ki == 0)
    def _():
        q = q_ref[...]
        lane = lax.broadcasted_iota(jnp.int32, q.shape, 1)
        zero = jnp.zeros_like(q)
        qs_scr[0:tq, :] = jnp.where(lane < DA_HEAD_DIM, q, zero)
        qs_scr[tq:2 * tq, :] = jnp.where(lane >= DA_HEAD_DIM, q, zero)
        m_scr[...] = jnp.full(m_scr.shape, NEG_INF, F32)
        acc_scr[...] = jnp.zeros(acc_scr.shape, F32)
        vx_scr[:, LANES:2 * LANES] = jnp.ones((tk, LANES), BF16)

    vx_scr[:, 0:LANES] = v_ref[...]

    def step(masked):
        for rb in range(2 * tq // ATT_ROWS):
            rows = slice(rb * ATT_ROWS, (rb + 1) * ATT_ROWS)
            s = _dot_nt(qs_scr[rows, :], k_ref[...])
            slabs = [s[:, c * LANES:(c + 1) * LANES] for c in range(n_slab)]
            if masked:
                row = lax.broadcasted_iota(jnp.int32, (ATT_ROWS, LANES), 0) + rb * ATT_ROWS
                lane = lax.broadcasted_iota(jnp.int32, (ATT_ROWS, LANES), 1)
                qchunk = (row % tq) // CHUNK + (qi % ratio) * (tq // CHUNK)
                slabs = [jnp.where((c * LANES + lane) // CHUNK <= qchunk, sl, NEG_INF)
                         for c, sl in enumerate(slabs)]
            mt = slabs[0]
            for sl in slabs[1:]:
                mt = jnp.maximum(mt, sl)
            m_prev = m_scr[rows, :]
            m_new = jnp.maximum(m_prev, jnp.max(mt, axis=-1, keepdims=True))
            alpha = jnp.exp2(m_prev - m_new)
            e = jnp.concatenate([jnp.exp2(sl - m_new).astype(BF16) for sl in slabs], axis=1)
            pv = _dot(e, vx_scr[...])
            acc_scr[rows, :] = jnp.concatenate([alpha, alpha], axis=1) * acc_scr[rows, :] + pv
            m_scr[rows, :] = m_new

    @pl.when(ki < qi // ratio)
    def _():
        step(False)

    @pl.when(ki == qi // ratio)
    def _():
        step(True)
        o = acc_scr[:, 0:LANES] / acc_scr[:, LANES:2 * LANES]
        o = o[0:tq] - lam_ref[...] * o[tq:2 * tq]
        o_ref[...] = _subln(o, g_ref[...]).astype(o_ref.dtype)


def _attn_prompt(q, k, v, lam_row, subln_g, batch, seq, tq, tk):
    d = q.shape[1]
    assert tk % tq == 0 and seq % tk == 0 and tq % CHUNK == 0
    qi_l, ki_l = [], []
    for a in range(seq // tq):
        for b in range(a * tq // tk + 1):
            qi_l.append(a)
            ki_l.append(b)
    qi_tab = jnp.asarray(qi_l, jnp.int32)
    ki_tab = jnp.asarray(ki_l, jnp.int32)
    q3, k3, v3 = (a.reshape(batch, seq, d) for a in (q, k, v))
    qspec = pl.BlockSpec((None, tq, LANES), lambda b, h, p, qt, kt: (b, qt[p], h))
    kspec = pl.BlockSpec((None, tk, LANES), lambda b, h, p, qt, kt: (b, kt[p], h))
    vec = pl.BlockSpec((1, LANES), lambda b, h, p, qt, kt: (0, 0))
    out = pl.pallas_call(
        functools.partial(_attn_prompt_kernel, tq=tq, tk=tk),
        grid_spec=pltpu.PrefetchScalarGridSpec(
            num_scalar_prefetch=2,
            grid=(batch, DA_HEADS, len(qi_l)),
            in_specs=[vec, qspec, kspec, kspec, vec],
            out_specs=qspec,
            scratch_shapes=[pltpu.VMEM((2 * tq, LANES), BF16), pltpu.VMEM((tk, 2 * LANES), BF16),
                            pltpu.VMEM((2 * tq, LANES), F32), pltpu.VMEM((2 * tq, 2 * LANES), F32)]),
        out_shape=jax.ShapeDtypeStruct((batch, seq, d), BF16),
        compiler_params=_cparams("parallel", "parallel", "arbitrary"),
        name="attn_prompt",
    )(qi_tab, ki_tab, lam_row, q3, k3, v3, subln_g)
    return out.reshape(batch * seq, d)


def _attn_sample_kernel(lam_ref, q_ref, kn_ref, vn_ref, ck_ref, cv_ref, g_ref, o_ref, *, past, sq):
    for h in range(DA_HEADS):
        hs = slice(h * LANES, (h + 1) * LANES)
        q = q_ref[:, hs]
        lane = lax.broadcasted_iota(jnp.int32, q.shape, 1)
        zero = jnp.zeros_like(q)
        qs = jnp.concatenate([jnp.where(lane < DA_HEAD_DIM, q, zero),
                              jnp.where(lane >= DA_HEAD_DIM, q, zero)], axis=0)
        kc = ck_ref[pl.ds(h, past, stride=DA_HEADS), :].astype(BF16)
        vc = cv_ref[pl.ds(h, past, stride=DA_HEADS), :].astype(BF16)
        s_c = _dot_nt(qs, kc)
        s_n = _dot_nt(qs, kn_ref[:, hs])

        def masked(s, key0):
            row = lax.broadcasted_iota(jnp.int32, s.shape, 0)
            col = lax.broadcasted_iota(jnp.int32, s.shape, 1)
            qchunk = (past + row % sq) // CHUNK
            return jnp.where((key0 + col) // CHUNK <= qchunk, s, NEG_INF)

        s_c = masked(s_c, 0)
        s_n = masked(s_n, past)
        m = jnp.maximum(jnp.max(s_c, axis=-1, keepdims=True), jnp.max(s_n, axis=-1, keepdims=True))
        e_c = jnp.exp2(s_c - m)
        e_n = jnp.exp2(s_n - m)
        l = jnp.sum(e_c, axis=-1, keepdims=True) + jnp.sum(e_n, axis=-1, keepdims=True)
        o = (_dot(e_c.astype(BF16), vc) + _dot(e_n.astype(BF16), vn_ref[:, hs])) / l
        o = o[0:sq] - lam_ref[...] * o[sq:2 * sq]
        o_ref[:, hs] = _subln(o, g_ref[...]).astype(o_ref.dtype)


def _attn_sample(q, kn, vn, cache_k, cache_v, lam_row, subln_g, batch, sq):
    d = q.shape[1]
    past = cache_k.shape[-3]
    ck = cache_k.reshape(batch, past * DA_HEADS, LANES)
    cv = cache_v.reshape(batch, past * DA_HEADS, LANES)
    q3, k3, v3 = (a.reshape(batch, sq, d) for a in (q, kn, vn))
    new = pl.BlockSpec((None, sq, d), lambda b: (b, 0, 0))
    old = pl.BlockSpec((None, past * DA_HEADS, LANES), lambda b: (b, 0, 0))
    vec = pl.BlockSpec((1, LANES), lambda b: (0, 0))
    out = pl.pallas_call(
        functools.partial(_attn_sample_kernel, past=past, sq=sq),
        grid=(batch,),
        in_specs=[vec, new, new, new, old, old, vec],
        out_specs=new,
        out_shape=jax.ShapeDtypeStruct((batch, sq, d), BF16),
        compiler_params=_cparams("parallel"),
        name="attn_sample",
    )(lam_row, q3, k3, v3, ck, cv, subln_g)
    return out.reshape(batch * sq, d)


RW_L = 64
RW_PAIRS = 8
RW_CPS = 2


def _mm3(a, b, dot=_dot):
    ah, al = _split2(a)
    bh, bl = _split2(b)
    return dot(ah, bh) + dot(ah, bl) + dot(al, bh)


def _terms(x, n):
    return (x.astype(BF16),) if n == 1 else _split2(x)


def _mmt(at, bt, dot=_dot):
    acc = dot(at[0], bt[0])
    if len(bt) > 1:
        acc = acc + dot(at[0], bt[1])
    if len(at) > 1:
        acc = acc + dot(at[1], bt[0])
    return acc


RW_PREC = dict(a_all=(1, 1), neumann=(1, 1), av=(1, 1), pu=(1, 1), qy=(1, 1), mh=(1, 1), state=(1, 1))


def _softplus(z):
    return jnp.maximum(z, 0.0) + jnp.log(1.0 + jnp.exp(-jnp.abs(z)))


def _rwkv_kernel(rw_ref, sh0_ref, mu_ref, lw_ref, w0_ref, a0_ref, kk_ref, ka_ref, rk_ref, lng_ref, lnb_ref,
                 h0_ref, gm_ref, tri_ref, ob_ref, hout_ref, h_scr, last_scr, *, n_valid, cps):
    L = RW_L
    rows = cps * L
    c = pl.program_id(1)
    d = ob_ref.shape[-1]

    @pl.when(c == 0)
    def _():
        h_scr[...] = h0_ref[...]
        last_scr[...] = sh0_ref[...]

    rw = rw_ref[...]
    row_w = lax.broadcasted_iota(jnp.int32, rw.shape, 0)
    prev = jnp.where(row_w == 0, last_scr[...], pltpu.roll(rw, 1, 0))
    last_scr[...] = rw[rows - 1:rows, :]
    xm = rw + (prev - rw) * mu_ref[...]
    r = xm[:, 0:d]
    k = xm[:, d:2 * d]
    v = xm[:, 2 * d:3 * d]
    lo = xm[:, 3 * d:3 * d + LORA_PAD]
    lane_l = lax.broadcasted_iota(jnp.int32, lo.shape, 1)
    z = jnp.where(lane_l < W_LORA, jnp.tanh(lo),
                  jnp.where(lane_l < W_LORA + A_LORA, lo, jax.nn.sigmoid(lo)))
    lora = _dot(z.astype(BF16), lw_ref[...])
    w = -_softplus(-(w0_ref[...] + lora[:, 0:d])) - 0.5
    ld = -jnp.exp(w)
    a = jax.nn.sigmoid(a0_ref[...] + lora[:, d:2 * d])
    g = lora[:, 2 * d:3 * d]
    kkr = k * kk_ref[...]
    kmod = k * (1.0 + (a - 1.0) * ka_ref[...])
    rkk = r * kmod * rk_ref[...]
    gm = gm_ref[...]
    if n_valid < rows:
        valid = lax.broadcasted_iota(jnp.int32, ld.shape, 0) < n_valid
        ld = jnp.where(valid, ld, 0.0)
        kkr = jnp.where(valid, kkr, 0.0)
        kmod = jnp.where(valid, kmod, 0.0)

    tri = tri_ref[...]
    l1, l2, l3 = _split3(ld)
    cin = _dot(tri, l1) + _dot(tri, l2) + _dot(tri, l3)
    cex = cin - ld
    c_last = [cin[(ch + 1) * L - 1:(ch + 1) * L, :] for ch in range(cps)]
    e_in = jnp.exp(cin)
    e_ex = jnp.exp(cex)
    e_neg = jnp.exp(-cin)
    e_tail = jnp.exp(jnp.concatenate([jnp.broadcast_to(cl, (L, d)) for cl in c_last], axis=0) - cin)
    g_last = [jnp.exp(cl) for cl in c_last]

    row = lax.broadcasted_iota(jnp.int32, (L, LANES), 0)
    lane = lax.broadcasted_iota(jnp.int32, (L, LANES), 1)
    m_lo = lane < RW_HEAD
    tri_s = row > (lane % L)
    tri_i = row >= (lane % L)
    eye_pair = jnp.where(row == (lane % L), 1.0, 0.0).astype(F32)
    r2 = lax.broadcasted_iota(jnp.int32, (LANES, LANES), 0)
    c2 = lax.broadcasted_iota(jnp.int32, (LANES, LANES), 1)
    bd_mask = (r2 < RW_HEAD) == (c2 < RW_HEAD)
    diag_mask = r2 == c2

    def bd(x):
        z0 = jnp.zeros_like(x)
        return jnp.concatenate([jnp.where(m_lo, x, z0), jnp.where(m_lo, z0, x)], axis=0)

    def bdt(t):
        return tuple(bd(x) for x in t)

    def cat(ts, axis):
        return tuple(jnp.concatenate(xs, axis=axis) for xs in zip(*ts))

    prec = RW_PREC
    items = [(slice(ch * L, (ch + 1) * L), slice(p * LANES, (p + 1) * LANES))
             for ch in range(cps) for p in range(RW_PAIRS)]
    pairs = range(len(items))

    v_p = [v[rs, sl] for rs, sl in items]
    kk_p = []
    for rs, sl in items:
        kk_raw = kkr[rs, sl]
        ss = _group_mean(kk_raw * kk_raw, gm) * RW_HEAD
        kk_p.append(kk_raw / jnp.maximum(jnp.sqrt(ss), 1e-12))
    b_p = [kk_p[i] * a[rs, sl] for i, (rs, sl) in enumerate(items)]
    abar = [-kk_p[i] * e_ex[rs, sl] for i, (rs, sl) in enumerate(items)]
    rbar = [r[rs, sl] * e_in[rs, sl] for rs, sl in items]
    bbar = [b_p[i] * e_neg[rs, sl] for i, (rs, sl) in enumerate(items)]
    kbar = [kmod[rs, sl] * e_neg[rs, sl] for rs, sl in items]
    btil = [b_p[i] * e_tail[rs, sl] for i, (rs, sl) in enumerate(items)]
    ktil = [kmod[rs, sl] * e_tail[rs, sl] for rs, sl in items]

    na, nb = prec["a_all"]
    abar_t = [_terms(abar[p], max(na, prec["pu"][1])) for p in pairs]
    v_t = [_terms(v_p[p], max(prec["av"][1], prec["qy"][1], prec["mh"][1])) for p in pairs]
    a_all = []
    for p in pairs:
        lhs = cat([abar_t[p][:na], _terms(rbar[p], na)], 0)
        rhs = cat([bdt(_terms(bbar[p], nb)), bdt(_terms(kbar[p], nb))], 0)
        a_all.append(_mmt(lhs, rhs, _dot_nt))
    n_ab = [jnp.where(tri_s, a_all[p][0:L, 0:2 * L], 0.0) for p in pairs]
    a_ak = [jnp.where(tri_s, a_all[p][0:L, 2 * L:4 * L], 0.0) for p in pairs]
    a_rb = [jnp.where(tri_i, a_all[p][L:2 * L, 0:2 * L], 0.0) for p in pairs]
    a_rk = [jnp.where(tri_i, a_all[p][L:2 * L, 2 * L:4 * L], 0.0) for p in pairs]

    na, nb = prec["neumann"]
    t_inv = [eye_pair + n_ab[p] for p in pairs]
    pw = n_ab
    for _ in range(int(math.log2(L)) - 1):
        pw_t = [_terms(pw[p], max(na, nb)) for p in pairs]
        pw = [_mmt(pw_t[p][:na], bdt(pw_t[p][:nb])) for p in pairs]
        pw_b = [bdt(_terms(pw[p], nb)) for p in pairs]
        t_inv = [t_inv[p] + _mmt(_terms(t_inv[p], na), pw_b[p]) for p in pairs]

    na, nb = prec["av"]
    av = [_mmt(_terms(a_ak[p], na), bdt(v_t[p][:nb])) for p in pairs]
    na, nb = prec["pu"]
    pu = [_mmt(_terms(t_inv[p], na), cat([bdt(abar_t[p][:nb]), bdt(_terms(av[p], nb))], 1)) for p in pairs]
    p_m = [x[:, 0:LANES] for x in pu]
    u0 = [x[:, LANES:2 * LANES] for x in pu]
    na, nb = prec["qy"]
    nmh = prec["mh"][1]
    p_t = [_terms(p_m[p], max(nb, nmh)) for p in pairs]
    u_t = [_terms(u0[p], max(nb, nmh)) for p in pairs]
    q_m, y0 = [], []
    for p in pairs:
        zero = tuple(jnp.zeros((2 * L, LANES), BF16) for _ in range(nb))
        rhs2 = cat([cat([bdt(p_t[p][:nb]), bdt(u_t[p][:nb])], 1),
                    cat([zero, bdt(v_t[p][:nb])], 1)], 0)
        qy = _mmt(cat([_terms(a_rb[p], na), _terms(a_rk[p], na)], 1), rhs2)
        q_m.append(rbar[p] + qy[:, 0:LANES])
        y0.append(qy[:, LANES:2 * LANES])

    na, nb = prec["mh"]
    btil_t = [_terms(btil[p], na) for p in pairs]
    m_full = [_mmt(btil_t[p], p_t[p][:nb], _dot_tn) for p in pairs]
    h0_full = [_mmt(cat([btil_t[p], _terms(ktil[p], na)], 0), cat([u_t[p][:nb], v_t[p][:nb]], 0), _dot_tn)
               for p in pairs]
    m_bd = [jnp.where(bd_mask, m_full[i], 0.0) + jnp.where(diag_mask, g_last[i // RW_PAIRS][:, sl], 0.0)
            for i, (rs, sl) in enumerate(items)]
    h0_bd = [jnp.where(bd_mask, h0_full[p], 0.0) for p in pairs]

    na, nb = prec["state"]
    h_cur = [h_scr[p] for p in range(RW_PAIRS)]
    y = []
    for ch in range(cps):
        h_t = [_terms(h, nb) for h in h_cur]
        for p in range(RW_PAIRS):
            i = ch * RW_PAIRS + p
            y.append(_mmt(_terms(q_m[i], na), h_t[p]) + y0[i])
        h_cur = [_mmt(_terms(m_bd[ch * RW_PAIRS + p], na), h_t[p]) + h0_bd[ch * RW_PAIRS + p]
                 for p in range(RW_PAIRS)]
    for p in range(RW_PAIRS):
        h_scr[p] = h_cur[p]

    for i, (rs, sl) in enumerate(items):
        mu_y = _group_mean(y[i], gm)
        dy = y[i] - mu_y
        var = _group_mean(dy * dy, gm)
        yn = dy * lax.rsqrt(var + RW_GN_EPS) * lng_ref[:, sl] + lnb_ref[:, sl]
        bonus = _group_mean(rkk[rs, sl], gm) * RW_HEAD * v_p[i]
        ob_ref[rs, sl] = ((yn + bonus) * g[rs, sl]).astype(ob_ref.dtype)

    @pl.when(c == pl.num_programs(1) - 1)
    def _():
        hout_ref[...] = h_scr[...]


def _rwkv(rw3, shift0, h0_bd, prm, gm, n_valid, cps):
    batch, seq, wcols = rw3.shape
    d = prm["w0"].shape[1]
    rows = cps * RW_L
    nc = seq // rows
    tri = jnp.kron(jnp.eye(cps, dtype=F32), jnp.tril(jnp.ones((RW_L, RW_L), F32))).astype(BF16)
    const2 = lambda b, c: (0, 0)
    vec = pl.BlockSpec((1, d), const2)
    st = pl.BlockSpec((None, RW_PAIRS, LANES, LANES), lambda b, c: (b, 0, 0, 0))
    return pl.pallas_call(
        functools.partial(_rwkv_kernel, n_valid=n_valid, cps=cps),
        grid=(batch, nc),
        in_specs=[pl.BlockSpec((None, rows, wcols), lambda b, c: (b, c, 0)),
                  pl.BlockSpec((None, 1, wcols), lambda b, c: (b, 0, 0)),
                  pl.BlockSpec((1, wcols), const2),
                  pl.BlockSpec(prm["lw"].shape, const2),
                  vec, vec, vec, vec, vec, vec, vec, st,
                  pl.BlockSpec((LANES, LANES), const2),
                  pl.BlockSpec((rows, rows), const2)],
        out_specs=[pl.BlockSpec((None, rows, d), lambda b, c: (b, c, 0)), st],
        out_shape=[jax.ShapeDtypeStruct((batch, seq, d), BF16),
                   jax.ShapeDtypeStruct((batch, RW_PAIRS, LANES, LANES), F32)],
        scratch_shapes=[pltpu.VMEM((RW_PAIRS, LANES, LANES), F32), pltpu.VMEM((1, wcols), F32)],
        compiler_params=_cparams("parallel", "arbitrary"),
        name="rwkv7",
    )(rw3, shift0, prm["mu"], prm["lw"], prm["w0"], prm["a0"], prm["k_k"], prm["k_a"], prm["r_k"],
      prm["ln_g"], prm["ln_b"], h0_bd, gm, tri)


def _state_to_bd(s):
    b = s.shape[0]
    ht = jnp.swapaxes(s, -1, -2).reshape(b, RW_PAIRS, 2, RW_HEAD, RW_HEAD)
    z = jnp.zeros_like(ht[:, :, 0])
    top = jnp.concatenate([ht[:, :, 0], z], axis=-1)
    bot = jnp.concatenate([z, ht[:, :, 1]], axis=-1)
    return jnp.concatenate([top, bot], axis=-2)


def _bd_to_state(hbd):
    b = hbd.shape[0]
    h0 = hbd[:, :, 0:RW_HEAD, 0:RW_HEAD]
    h1 = hbd[:, :, RW_HEAD:, RW_HEAD:]
    ht = jnp.stack([h0, h1], axis=2).reshape(b, 2 * RW_PAIRS, RW_HEAD, RW_HEAD)
    return jnp.swapaxes(ht, -1, -2)


def _route(logits):
    lane = lax.broadcasted_iota(jnp.int32, logits.shape, 1).astype(F32)
    big = float(LANES)
    lg = jnp.where(lane < N_GROUPS, logits, NEG_INF)
    mg = jnp.max(lg, axis=-1, keepdims=True)
    sg = jnp.sum(jnp.exp(lg - mg), axis=-1, keepdims=True)
    p_top = 1.0 / sg
    g_idx = jnp.min(jnp.where(lg == mg, lane, big), axis=-1, keepdims=True)
    e0 = N_GROUPS + EXPERTS_PER_GROUP * g_idx
    emask = jnp.where(lane >= e0, jnp.where(lane < e0 + EXPERTS_PER_GROUP, 1.0, 0.0), 0.0) > 0.5
    le = jnp.where(emask, logits, NEG_INF)
    me = jnp.max(le, axis=-1, keepdims=True)
    ee = jnp.exp(le - me)
    pe = ee / jnp.sum(ee, axis=-1, keepdims=True)
    pe = jnp.where(emask, pe, -1.0)
    v1 = jnp.max(pe, axis=-1, keepdims=True)
    i1 = jnp.min(jnp.where(pe == v1, lane, big), axis=-1, keepdims=True)
    pe2 = jnp.where(lane == i1, -1.0, pe)
    v2 = jnp.max(pe2, axis=-1, keepdims=True)
    i2 = jnp.min(jnp.where(pe2 == v2, lane, big), axis=-1, keepdims=True)
    den = v1 + v2
    ew = jnp.where(lane == i1, v1 / den, 0.0) + jnp.where(lane == i2, v2 / den, 0.0)
    return p_top * ew + jnp.where(lane == 0.0, g_idx, 0.0)


def _merge_kernel(oa_ref, ob_ref, gates_ref, x_ref, gt1_ref, sc2_ref, sh2_ref, g2_ref,
                  woa_ref, wob_ref, wout_ref, wr_ref, br_ref, x1_ref, hx_ref):
    d = x_ref.shape[1]
    a_out = _dot(oa_ref[...], woa_ref[...])
    b_out = _dot(ob_ref[...], wob_ref[...])
    merged = gates_ref[:, 0:d].astype(F32) * a_out + gates_ref[:, d:2 * d].astype(F32) * b_out
    x1 = x_ref[...] + gt1_ref[...] * _dot(merged.astype(BF16), wout_ref[...])
    x1_ref[...] = x1
    ms = jnp.mean(x1 * x1, axis=-1, keepdims=True)
    h2 = x1 * lax.rsqrt(ms + EPS) * g2_ref[...]
    h2 = h2 * (1.0 + sc2_ref[...]) + sh2_ref[...]
    logits = _mm3(h2, wr_ref[...]) + br_ref[...]
    hx_ref[:, 0:d] = h2
    hx_ref[:, d:d + LANES] = _route(logits)


def _merge(oa, ob, gates, x2d, gt1, sc2, sh2, g2, w_oa, w_ob, w_out, wr, br, tm, seq):
    t, d = x2d.shape
    row = lambda i: (i, 0)
    const = lambda i: (0, 0)
    full = lambda w: pl.BlockSpec(w.shape, const)
    return pl.pallas_call(
        _merge_kernel,
        grid=(t // tm,),
        in_specs=[pl.BlockSpec((tm, d), row), pl.BlockSpec((tm, d), row), pl.BlockSpec((tm, 2 * d), row),
                  pl.BlockSpec((tm, d), row), _mod_spec(gt1, tm, seq), _mod_spec(sc2, tm, seq),
                  _mod_spec(sh2, tm, seq), pl.BlockSpec((1, d), const),
                  full(w_oa), full(w_ob), full(w_out), full(wr), full(br)],
        out_specs=[pl.BlockSpec((tm, d), row), pl.BlockSpec((tm, d + LANES), row)],
        out_shape=[jax.ShapeDtypeStruct((t, d), F32), jax.ShapeDtypeStruct((t, d + LANES), F32)],
        compiler_params=_cparams("parallel"),
        name="merge_router",
    )(oa, ob, gates, x2d, gt1, sc2, sh2, g2, w_oa, w_ob, w_out, wr, br)


def _expert_ffn(xb, comb, lane_e, w13, w2):
    au = _dot(xb, w13)
    f = au.shape[1] // 2
    a = au[:, 0:f]
    u = au[:, f:2 * f]
    lane = lax.broadcasted_iota(jnp.int32, comb.shape, 1)
    cw = jnp.sum(jnp.where(lane == lane_e, comb, 0.0), axis=-1, keepdims=True)
    act = a * jax.nn.sigmoid(a) * u * cw
    return _dot(act.astype(BF16), w2)


def _moe_dense_kernel(hx_ref, w13_ref, w2_ref, x1_ref, gt2_ref, y_ref, xb_scr, acc_scr):
    e = pl.program_id(1)
    d = y_ref.shape[1]

    @pl.when(e == 0)
    def _():
        xb_scr[...] = hx_ref[:, 0:d].astype(BF16)
        acc_scr[...] = jnp.zeros(acc_scr.shape, F32)

    acc_scr[...] += _expert_ffn(xb_scr[...], hx_ref[:, d:d + LANES], e + N_GROUPS, w13_ref[...], w2_ref[...])

    @pl.when(e == pl.num_programs(1) - 1)
    def _():
        y_ref[...] = x1_ref[...] + gt2_ref[...] * acc_scr[...]


def _moe_dense(hx, w13, w2, x1, gt2, tm, seq):
    t, d = x1.shape
    ne, _, f2 = w13.shape
    row = lambda i, e: (i, 0)
    return pl.pallas_call(
        _moe_dense_kernel,
        grid=(t // tm, ne),
        in_specs=[pl.BlockSpec((tm, d + LANES), row),
                  pl.BlockSpec((None, d, f2), lambda i, e: (e, 0, 0)),
                  pl.BlockSpec((None, f2 // 2, d), lambda i, e: (e, 0, 0)),
                  pl.BlockSpec((tm, d), row), _mod_spec(gt2, tm, seq)],
        out_specs=pl.BlockSpec((tm, d), row),
        out_shape=jax.ShapeDtypeStruct((t, d), F32),
        scratch_shapes=[pltpu.VMEM((tm, d), BF16), pltpu.VMEM((tm, d), F32)],
        compiler_params=_cparams("parallel", "arbitrary"),
        name="moe_dense",
    )(hx, w13, w2, x1, gt2)


MOE_TM = 1024
MOE_TS = 1024


def _plan_kernel(rt_ref, tri_ref, rank_ref, cnt_ref, carry_scr):
    @pl.when(pl.program_id(0) == 0)
    def _():
        carry_scr[...] = jnp.zeros(carry_scr.shape, F32)

    rt = rt_ref[...]
    lane = lax.broadcasted_iota(jnp.int32, rt.shape, 1).astype(F32)
    onehot = jnp.where(lane == rt[:, 0:1], 1.0, 0.0)
    before = _dot(tri_ref[...], onehot.astype(BF16)) + carry_scr[...]
    rank_ref[...] = jnp.sum(onehot * before, axis=-1, keepdims=True).astype(jnp.int32)
    carry_scr[...] += jnp.sum(onehot, axis=0, keepdims=True)
    cnt_ref[...] = carry_scr[...]


def _moe_plan(hx, d):
    t = hx.shape[0]
    tp = 512
    tri = jnp.tril(jnp.ones((tp, tp), F32), -1).astype(BF16)
    return pl.pallas_call(
        _plan_kernel,
        grid=(t // tp,),
        in_specs=[pl.BlockSpec((tp, LANES), lambda i: (i, d // LANES)),
                  pl.BlockSpec((tp, tp), lambda i: (0, 0))],
        out_specs=[pl.BlockSpec((tp, 1), lambda i: (i, 0)), pl.BlockSpec((1, LANES), lambda i: (0, 0))],
        out_shape=[jax.ShapeDtypeStruct((t, 1), jnp.int32), jax.ShapeDtypeStruct((1, LANES), F32)],
        scratch_shapes=[pltpu.VMEM((1, LANES), F32)],
        compiler_params=_cparams("arbitrary"),
        name="moe_plan",
    )(hx, tri)


def _row_copy(src_hbm, src_row, dst, dst_row, sem):
    return pltpu.make_async_copy(src_hbm.at[pl.ds(src_row, 1), :], dst.at[pl.ds(dst_row, 1), :], sem)


def _dispatch_kernel(pos_ref, hx_ref, xs_in_hbm, xs_hbm, sem):
    del xs_in_hbm

    def issue(t, carry):
        _row_copy(hx_ref, t, xs_hbm, pos_ref[t], sem).start()
        return carry

    def drain(t, carry):
        _row_copy(hx_ref, t, xs_hbm, pos_ref[t], sem).wait()
        return carry

    lax.fori_loop(0, MOE_TS, issue, 0, unroll=8)
    lax.fori_loop(0, MOE_TS, drain, 0, unroll=8)


def _moe_dispatch(pos, hx, n_rows):
    t, w = hx.shape
    xs0 = jnp.zeros((n_rows, w), F32)
    return pl.pallas_call(
        _dispatch_kernel,
        grid=(t // MOE_TS,),
        in_specs=[pl.BlockSpec((MOE_TS,), lambda i: (i,), memory_space=pltpu.SMEM),
                  pl.BlockSpec((MOE_TS, w), lambda i: (i, 0)), pl.BlockSpec(memory_space=pl.ANY)],
        out_specs=pl.BlockSpec(memory_space=pl.ANY),
        out_shape=jax.ShapeDtypeStruct((n_rows, w), F32),
        scratch_shapes=[pltpu.SemaphoreType.DMA(())],
        input_output_aliases={2: 0},
        compiler_params=_cparams("arbitrary"),
        name="moe_dispatch",
    )(pos, hx, xs0)


def _moe_bucket_kernel(tg_ref, tv_ref, xs_ref, w13_ref, w2_ref, ys_ref, xb_scr):
    i = pl.program_id(0)
    e = pl.program_id(1)
    d = ys_ref.shape[1]

    @pl.when(e == 0)
    def _():
        xb_scr[...] = xs_ref[:, 0:d].astype(BF16)
        ys_ref[...] = jnp.zeros(ys_ref.shape, F32)

    @pl.when(tv_ref[i] > 0)
    def _():
        lane_e = N_GROUPS + tg_ref[i] * EXPERTS_PER_GROUP + e
        ys_ref[...] += _expert_ffn(xb_scr[...], xs_ref[:, d:d + LANES], lane_e, w13_ref[...], w2_ref[...])


def _moe_buckets(tile_group, tile_valid, xs, w13, w2, d):
    n_rows, w = xs.shape
    _, _, f2 = w13.shape
    row = lambda i, e, tg, tv: (i, 0)
    wsel = lambda i, e, tg, tv: (tg[i] * EXPERTS_PER_GROUP + e, 0, 0)
    return pl.pallas_call(
        _moe_bucket_kernel,
        grid_spec=pltpu.PrefetchScalarGridSpec(
            num_scalar_prefetch=2,
            grid=(n_rows // MOE_TM, EXPERTS_PER_GROUP),
            in_specs=[pl.BlockSpec((MOE_TM, w), row),
                      pl.BlockSpec((None, d, f2), wsel), pl.BlockSpec((None, f2 // 2, d), wsel)],
            out_specs=pl.BlockSpec((MOE_TM, d), row),
            scratch_shapes=[pltpu.VMEM((MOE_TM, d), BF16)]),
        out_shape=jax.ShapeDtypeStruct((n_rows, d), F32),
        compiler_params=_cparams("parallel", "arbitrary"),
        name="moe_buckets",
    )(tile_group, tile_valid, xs, w13, w2)


def _collect_kernel(pos_ref, ys_hbm, x1_ref, gt2_ref, y_ref, buf, sem):
    tu = buf.shape[0]

    def issue(t, carry):
        _row_copy(ys_hbm, pos_ref[t], buf, t, sem).start()
        return carry

    def drain(t, carry):
        _row_copy(ys_hbm, pos_ref[t], buf, t, sem).wait()
        return carry

    lax.fori_loop(0, tu, issue, 0, unroll=8)
    lax.fori_loop(0, tu, drain, 0, unroll=8)
    y_ref[...] = x1_ref[...] + gt2_ref[...] * buf[...]


def _moe_collect(pos, ys, x1, gt2, seq):
    t, d = x1.shape
    tu = MOE_TS
    return pl.pallas_call(
        _collect_kernel,
        grid=(t // tu,),
        in_specs=[pl.BlockSpec((tu,), lambda i: (i,), memory_space=pltpu.SMEM),
                  pl.BlockSpec(memory_space=pl.ANY),
                  pl.BlockSpec((tu, d), lambda i: (i, 0)), _mod_spec(gt2, tu, seq)],
        out_specs=pl.BlockSpec((tu, d), lambda i: (i, 0)),
        out_shape=jax.ShapeDtypeStruct((t, d), F32),
        scratch_shapes=[pltpu.VMEM((tu, d), F32), pltpu.SemaphoreType.DMA(())],
        compiler_params=_cparams("arbitrary"),
        name="moe_collect",
    )(pos, ys, x1, gt2)


def _moe_sorted(hx, w13, w2, x1, gt2, seq):
    t, d = x1.shape
    rank, cnt = _moe_plan(hx, d)
    group = hx[:, d].astype(jnp.int32)
    counts = cnt[0, 0:N_GROUPS].astype(jnp.int32)
    padded = (counts + MOE_TM - 1) // MOE_TM * MOE_TM
    ends = jnp.cumsum(padded)
    starts = ends - padded
    pos = jnp.take(starts, group) + rank[:, 0]
    n_tiles = t // MOE_TM + N_GROUPS
    tile_start = jnp.arange(n_tiles, dtype=jnp.int32) * MOE_TM
    tile_group = jnp.minimum(jnp.sum((tile_start[:, None] >= ends[None, :]).astype(jnp.int32), axis=1),
                             N_GROUPS - 1)
    tile_valid = (tile_start < ends[-1]).astype(jnp.int32)
    xs = _moe_dispatch(pos, hx, n_tiles * MOE_TM)
    ys = _moe_buckets(tile_group, tile_valid, xs, w13, w2, d)
    return _moe_collect(pos, ys, x1, gt2, seq)


def _rope_tables(pos0, seq, reps):
    pos = (pos0 + jnp.arange(seq)).astype(F32)
    inv = ROPE_THETA ** (-jnp.arange(0, DA_HEAD_DIM, 2, dtype=F32) / DA_HEAD_DIM)
    ang = pos[:, None] * inv[None, :]
    cos, sin = jnp.cos(ang), jnp.sin(ang)
    cos_t = jnp.tile(jnp.concatenate([cos, cos], axis=-1), (reps, LANES // DA_HEAD_DIM))
    sin_t = jnp.tile(jnp.concatenate([-sin, sin], axis=-1), (reps, LANES // DA_HEAD_DIM))
    return cos_t, sin_t


def _layer(x, mod, pos0, past, wts):
    bx, sx, d = x.shape
    t = bx * sx
    x2d = x.reshape(t, d)
    per_batch = sx >= 512
    tm = 512 if per_batch else t

    def modv(i):
        if per_batch:
            return mod[:, i:i + 1, :]
        return jnp.repeat(mod[:, i, :], sx, axis=0)[None]

    sh1, sc1, gt1, sh2, sc2, gt2 = (modv(i) for i in range(6))
    cos_t, sin_t = _rope_tables(pos0, sx, 1 if per_batch else bx)

    tm_big = 1024 if (per_batch and sx % 1024 == 0) else tm
    h = _prenorm(x2d, sc1, sh1, wts["norm1_g"], tm, sx)
    q, k32, kb, v32, vb = _qkv(h, wts["w_qkv"], wts["qg"], wts["kg"], cos_t, sin_t, wts["gm"], tm_big)
    rw = _proj(h, wts["w_rw"], tm_big, wts["w_rw"].shape[1] // 3, None, F32, "rw_proj")
    gates = _proj(h, wts["w_gate"], tm_big, d, "sigmoid", BF16, "gate_proj")

    if past is None:
        oa = _attn_prompt(q, kb, vb, wts["lam"], wts["subln_g"], bx, sx, min(1024, sx), min(1024, sx))
        shift0 = jnp.zeros((bx, 1, rw.shape[1]), F32)
        h0_bd = jnp.zeros((bx, RW_PAIRS, LANES, LANES), F32)
    else:
        ck, cv, s0, sh0 = past
        oa = _attn_sample(q, kb, vb, ck, cv, wts["lam"], wts["subln_g"], bx, sx)
        shift0 = jnp.pad(sh0, ((0, 0), (0, 0), (0, rw.shape[1] - sh0.shape[-1])))
        h0_bd = _state_to_bd(s0.astype(F32))

    rw3 = rw.reshape(bx, sx, rw.shape[1])
    if sx % RW_L == 0:
        cps = RW_CPS if sx % (RW_CPS * RW_L) == 0 else 1
        rw_in, n_valid = rw3, cps * RW_L
    else:
        assert sx < RW_L
        cps = 1
        rw_in, n_valid = jnp.pad(rw3, ((0, 0), (0, RW_L - sx), (0, 0))), sx
    ob, h_bd = _rwkv(rw_in, shift0, h0_bd, wts, wts["gm"], n_valid, cps)
    ob = ob[:, 0:sx, :].reshape(t, d)

    x1, hx = _merge(oa, ob, gates, x2d, gt1, sc2, sh2, wts["norm2_g"],
                    wts["w_oa"], wts["w_ob"], wts["w_out"], wts["wr"], wts["br"], tm, sx)
    if per_batch and sx % MOE_TS == 0:
        y = _moe_sorted(hx, wts["w13"], wts["w2"], x1, gt2, sx)
    else:
        y = _moe_dense(hx, wts["w13"], wts["w2"], x1, gt2, tm, sx)

    n_cols = 3 * d + W_LORA + A_LORA + G_LORA
    return (y.reshape(bx, sx, d),
            k32.reshape(1, bx, sx, DA_HEADS, 2 * DA_HEAD_DIM),
            v32.reshape(1, bx, sx, DA_HEADS, 2 * DA_HEAD_DIM),
            _bd_to_state(h_bd)[None],
            rw3[:, sx - 1:sx, 0:n_cols][None])


def kernel(x_prompt, x_sample, cache_k, cache_v, state_wkv, state_shift, c_prompt, c_sample, norm1_g, norm2_g, w_ada, b_ada, w_in, da_qn_g, da_kn_g, da_lambda, da_subln_g, w_oa, rw_mu, rw_w0, rw_w2, rw_a0, rw_a2, rw_g2, rw_k_k, rw_k_a, rw_r_k, rw_ln_g, rw_ln_b, w_ob, w_out, router_g, router_g_b, router_e, router_e_b, exp_w1, exp_w3, exp_w2):
    b, s, d = x_prompt.shape
    bs = x_sample.shape[0]
    past_len = cache_k.shape[2]
    assert w_in.shape[0] == 1, "single layer"

    c_all = jnp.concatenate([c_prompt, c_sample], axis=0)
    c_all = jnp.pad(c_all, ((0, (-c_all.shape[0]) % 8), (0, 0)))
    mod, lam_tile = _adaln(c_all, w_ada[0], b_ada[0], da_lambda[0])
    mod_p = mod[0:b].reshape(b, 6, d)
    mod_s = mod[b:b + bs].reshape(bs, 6, d)

    win = w_in[0]
    rw_cols = 3 * d + W_LORA + A_LORA + G_LORA
    rw_pad = 3 * d + LORA_PAD
    w_rw = jnp.pad(win[:, 3 * d:3 * d + rw_cols], ((0, 0), (0, rw_pad - rw_cols))).astype(BF16)
    lw = jnp.zeros((LORA_PAD, 3 * d), F32)
    lw = lw.at[0:W_LORA, 0:d].set(rw_w2[0])
    lw = lw.at[W_LORA:W_LORA + A_LORA, d:2 * d].set(rw_a2[0])
    lw = lw.at[W_LORA + A_LORA:W_LORA + A_LORA + G_LORA, 2 * d:3 * d].set(rw_g2[0])
    f = exp_w1.shape[-1]
    wr = jnp.pad(jnp.concatenate([router_g[0], router_e[0]], axis=1), ((0, 0), (0, LANES - N_GROUPS - N_EXPERTS)))
    br = jnp.pad(jnp.concatenate([router_g_b[0], router_e_b[0]]), (0, LANES - N_GROUPS - N_EXPERTS))[None]
    half = jnp.ones((DA_HEAD_DIM, DA_HEAD_DIM), F32) / DA_HEAD_DIM
    wts = dict(
        norm1_g=norm1_g[0][None], norm2_g=norm2_g[0][None],
        w_qkv=win[:, 0:3 * d].astype(BF16), w_rw=w_rw, w_gate=win[:, 3 * d + rw_cols:].astype(BF16),
        qg=jnp.tile(da_qn_g[0], 2)[None], kg=jnp.tile(da_kn_g[0], 2)[None],
        gm=jnp.kron(jnp.eye(2, dtype=F32), half).astype(BF16),
        lam=lam_tile[0:1], subln_g=da_subln_g[0][None],
        mu=jnp.pad(rw_mu[0], (0, rw_pad - rw_cols))[None], lw=lw.astype(BF16),
        w0=rw_w0[0][None], a0=rw_a0[0][None], k_k=rw_k_k[0][None], k_a=rw_k_a[0][None],
        r_k=rw_r_k[0].reshape(1, d), ln_g=rw_ln_g[0][None], ln_b=rw_ln_b[0][None],
        w_oa=w_oa[0].astype(BF16), w_ob=w_ob[0].astype(BF16), w_out=w_out[0].astype(BF16),
        wr=wr, br=br,
        w13=jnp.concatenate([exp_w1[0], exp_w3[0]], axis=-1).reshape(N_EXPERTS, d, 2 * f).astype(BF16),
        w2=exp_w2[0].reshape(N_EXPERTS, f, d).astype(BF16),
    )

    out_p = _layer(x_prompt, mod_p, 0, None, wts)
    out_s = _layer(x_sample, mod_s, past_len,
                   (cache_k, cache_v, state_wkv[0], state_shift[0]), wts)
    return (out_p[0], out_s[0], out_p[1], out_p[2], out_p[3], out_p[4],
            out_s[1], out_s[2], out_s[3], out_s[4])
```

```python
import functools
import math

import jax
import jax.numpy as jnp
from jax import lax
from jax.experimental import pallas as pl
from jax.experimental.pallas import tpu as pltpu

F32 = jnp.float32
BF16 = jnp.bfloat16

EPS = 1e-6
NEG_INF = -1e30
CHUNK = 64
DA_HEADS = 8
DA_HEAD_DIM = 64
ROPE_THETA = 10000.0
RW_HEAD = 64
RW_GN_EPS = 64e-5
W_LORA, A_LORA, G_LORA = 64, 64, 160
N_GROUPS, EXPERTS_PER_GROUP = 4, 8
N_EXPERTS = N_GROUPS * EXPERTS_PER_GROUP
LAM_INIT = 0.8 - 0.6 * math.exp(-0.3 * 0)

LANES = 128
LORA_PAD = 384
VMEM_LIMIT = 56 * 1024 * 1024


def _cparams(*sem):
    return pltpu.CompilerParams(dimension_semantics=sem, vmem_limit_bytes=VMEM_LIMIT)


def _dot(a, b):
    return jnp.dot(a, b, preferred_element_type=F32)


def _dot_nt(a, b):
    return lax.dot_general(a, b, (((1,), (1,)), ((), ())), preferred_element_type=F32)


def _dot_tn(a, b):
    return lax.dot_general(a, b, (((0,), (0,)), ((), ())), preferred_element_type=F32)


def _split2(a):
    hi = a.astype(BF16)
    lo = (a - hi.astype(F32)).astype(BF16)
    return hi, lo


def _split3(a):
    hi = a.astype(BF16)
    r = a - hi.astype(F32)
    mid = r.astype(BF16)
    lo = (r - mid.astype(F32)).astype(BF16)
    return hi, mid, lo


def _group_mean(sq, gm):
    hi, lo = _split2(sq)
    return _dot(hi, gm) + _dot(lo, gm)


def _adaln_kernel(c_ref, w_ref, b_ref, l_ref, o_ref, lam_ref):
    c = c_ref[...]
    sc = (c * jax.nn.sigmoid(c)).astype(BF16)
    o_ref[...] = _dot(sc, w_ref[...].astype(BF16)) + b_ref[...]
    l = l_ref[...]
    s1 = jnp.sum(l[0:1] * l[1:2], axis=-1, keepdims=True)
    s2 = jnp.sum(l[2:3] * l[3:4], axis=-1, keepdims=True)
    lam = jnp.exp(s1) - jnp.exp(s2) + LAM_INIT
    lam_ref[...] = jnp.broadcast_to(lam, lam_ref.shape)


def _adaln(c_all, w_ada, b_ada, da_lambda):
    bp, d = c_all.shape
    n = w_ada.shape[1]
    tn = 1024
    return pl.pallas_call(
        _adaln_kernel,
        grid=(n // tn,),
        in_specs=[pl.BlockSpec((bp, d), lambda j: (0, 0)),
                  pl.BlockSpec((d, tn), lambda j: (0, j)),
                  pl.BlockSpec((1, tn), lambda j: (0, j)),
                  pl.BlockSpec(da_lambda.shape, lambda j: (0, 0))],
        out_specs=[pl.BlockSpec((bp, tn), lambda j: (0, j)),
                   pl.BlockSpec((8, LANES), lambda j: (0, 0))],
        out_shape=[jax.ShapeDtypeStruct((bp, n), F32), jax.ShapeDtypeStruct((8, LANES), F32)],
        compiler_params=_cparams("arbitrary"),
        name="adaln",
    )(c_all, w_ada, b_ada.reshape(1, n), da_lambda)


def _prenorm_kernel(x_ref, sc_ref, sh_ref, g_ref, o_ref):
    x = x_ref[...]
    ms = jnp.mean(x * x, axis=-1, keepdims=True)
    h = x * lax.rsqrt(ms + EPS) * g_ref[...]
    o_ref[...] = (h * (1.0 + sc_ref[...]) + sh_ref[...]).astype(o_ref.dtype)


def _mod_spec(mod, tm, seq):
    d = mod.shape[-1]
    if mod.shape[1] == 1:
        per = seq // tm
        return pl.BlockSpec((None, 1, d), lambda i, *_: (i // per, 0, 0))
    return pl.BlockSpec((None, tm, d), lambda i, *_: (0, i, 0))


def _prenorm(x2d, sc, sh, g, tm, seq):
    t, d = x2d.shape
    return pl.pallas_call(
        _prenorm_kernel,
        grid=(t // tm,),
        in_specs=[pl.BlockSpec((tm, d), lambda i: (i, 0)), _mod_spec(sc, tm, seq), _mod_spec(sh, tm, seq),
                  pl.BlockSpec((1, d), lambda i: (0, 0))],
        out_specs=pl.BlockSpec((tm, d), lambda i: (i, 0)),
        out_shape=jax.ShapeDtypeStruct((t, d), BF16),
        compiler_params=_cparams("parallel"),
        name="prenorm",
    )(x2d, sc, sh, g)


def _qkv_kernel(h_ref, w_ref, qg_ref, kg_ref, cos_ref, sin_ref, gm_ref,
                q_ref, k_ref, kb_ref, v_ref, vb_ref):
    j = pl.program_id(1)
    acc = _dot(h_ref[...], w_ref[...])
    n_slab = acc.shape[1] // LANES

    def norm_rope(gain_ref, scale, write):
        lane = lax.broadcasted_iota(jnp.int32, (acc.shape[0], LANES), 1)
        first_half = (lane % DA_HEAD_DIM) < (DA_HEAD_DIM // 2)
        for c in range(n_slab):
            a = acc[:, c * LANES:(c + 1) * LANES]
            ms = _group_mean(a * a, gm_ref[...])
            y = a * lax.rsqrt(ms + EPS) * gain_ref[...]
            rot = jnp.where(first_half, pltpu.roll(y, LANES - DA_HEAD_DIM // 2, 1),
                            pltpu.roll(y, DA_HEAD_DIM // 2, 1))
            write(c, (y * cos_ref[...] + rot * sin_ref[...]) * scale)

    @pl.when(j == 0)
    def _():
        def write(c, val):
            q_ref[:, c * LANES:(c + 1) * LANES] = val.astype(q_ref.dtype)
        norm_rope(qg_ref, DA_HEAD_DIM ** -0.5 * math.log2(math.e), write)

    @pl.when(j == 1)
    def _():
        def write(c, val):
            k_ref[:, c * LANES:(c + 1) * LANES] = val
            kb_ref[:, c * LANES:(c + 1) * LANES] = val.astype(BF16)
        norm_rope(kg_ref, 1.0, write)

    @pl.when(j == 2)
    def _():
        v_ref[...] = acc
        vb_ref[...] = acc.astype(BF16)


def _qkv(h, w_qkv, qg, kg, cos_t, sin_t, gm, tm):
    t, d = h.shape
    n = 1024
    per = cos_t.shape[0] // tm
    row = lambda i, j: (i, 0)
    return pl.pallas_call(
        _qkv_kernel,
        grid=(t // tm, 3),
        in_specs=[pl.BlockSpec((tm, d), row),
                  pl.BlockSpec((d, n), lambda i, j: (0, j)),
                  pl.BlockSpec((1, LANES), lambda i, j: (0, 0)),
                  pl.BlockSpec((1, LANES), lambda i, j: (0, 0)),
                  pl.BlockSpec((tm, LANES), lambda i, j: (i % per, 0)),
                  pl.BlockSpec((tm, LANES), lambda i, j: (i % per, 0)),
                  pl.BlockSpec((LANES, LANES), lambda i, j: (0, 0))],
        out_specs=[pl.BlockSpec((tm, n), row)] * 5,
        out_shape=[jax.ShapeDtypeStruct((t, n), BF16), jax.ShapeDtypeStruct((t, n), F32),
                   jax.ShapeDtypeStruct((t, n), BF16), jax.ShapeDtypeStruct((t, n), F32),
                   jax.ShapeDtypeStruct((t, n), BF16)],
        compiler_params=_cparams("parallel", "arbitrary"),
        name="qkv_proj",
    )(h, w_qkv, qg, kg, cos_t, sin_t, gm)


def _proj_kernel(h_ref, w_ref, o_ref, *, act):
    acc = _dot(h_ref[...], w_ref[...])
    if act == "sigmoid":
        acc = jax.nn.sigmoid(acc)
    o_ref[...] = acc.astype(o_ref.dtype)


def _proj(h, w, tm, tn, act, out_dtype, name):
    t, d = h.shape
    n = w.shape[1]
    return pl.pallas_call(
        functools.partial(_proj_kernel, act=act),
        grid=(t // tm, n // tn),
        in_specs=[pl.BlockSpec((tm, d), lambda i, j: (i, 0)), pl.BlockSpec((d, tn), lambda i, j: (0, j))],
        out_specs=pl.BlockSpec((tm, tn), lambda i, j: (i, j)),
        out_shape=jax.ShapeDtypeStruct((t, n), out_dtype),
        compiler_params=_cparams("parallel", "arbitrary"),
        name=name,
    )(h, w)


def _subln(o, g):
    ms = jnp.mean(o * o, axis=-1, keepdims=True)
    return o * lax.rsqrt(ms + EPS) * g * (1.0 - LAM_INIT)


ATT_ROWS = 128
ATT_HEADS = 2


def _attn_prompt_kernel(qi_tab, ki_tab, lam_ref, q_ref, k_ref, v_ref, g_ref, o_ref,
                        qs_scr, vx_scr, m_scr, acc_scr, *, tq, tk):
    p = pl.program_id(2)
    qi = qi_tab[p]
    ki = ki_tab[p]
    ratio = tk // tq
    n_slab = tk // LANES

    heads = [slice(hh * LANES, (hh + 1) * LANES) for hh in range(ATT_HEADS)]

    @pl.when(ki == 0)
    def _():
        for hh, hs in enumerate(heads):
            q = q_ref[:, hs]
            lane = lax.broadcasted_iota(jnp.int32, q.shape, 1)
            zero = jnp.zeros_like(q)
            qs_scr[hh, 0:tq, :] = jnp.where(lane < DA_HEAD_DIM, q, zero)
            qs_scr[hh, tq:2 * tq, :] = jnp.where(lane >= DA_HEAD_DIM, q, zero)
            vx_scr[hh, :, LANES:2 * LANES] = jnp.ones((tk, LANES), BF16)
        m_scr[...] = jnp.full(m_scr.shape, NEG_INF, F32)
        acc_scr[...] = jnp.zeros(acc_scr.shape, F32)

    for hh, hs in enumerate(heads):
        vx_scr[hh, :, 0:LANES] = v_ref[:, hs]

    def step(masked):
        for rb in range(2 * tq // ATT_ROWS):
            rows = slice(rb * ATT_ROWS, (rb + 1) * ATT_ROWS)
            for hh, hs in enumerate(heads):
                s = _dot_nt(qs_scr[hh, rows, :], k_ref[:, hs])
                slabs = [s[:, c * LANES:(c + 1) * LANES] for c in range(n_slab)]
                if masked:
                    row = lax.broadcasted_iota(jnp.int32, (ATT_ROWS, LANES), 0) + rb * ATT_ROWS
                    lane = lax.broadcasted_iota(jnp.int32, (ATT_ROWS, LANES), 1)
                    qchunk = (row % tq) // CHUNK + (qi % ratio) * (tq // CHUNK)
                    slabs = [jnp.where((c * LANES + lane) // CHUNK <= qchunk, sl, NEG_INF)
                             for c, sl in enumerate(slabs)]
                mt = slabs[0]
                for sl in slabs[1:]:
                    mt = jnp.maximum(mt, sl)
                m_prev = m_scr[hh, rows, :]
                m_new = jnp.maximum(m_prev, jnp.max(mt, axis=-1, keepdims=True))
                alpha = jnp.exp2(m_prev - m_new)
                e = jnp.concatenate([jnp.exp2(sl - m_new).astype(BF16) for sl in slabs], axis=1)
                pv = _dot(e, vx_scr[hh])
                acc_scr[hh, rows, :] = jnp.concatenate([alpha, alpha], axis=1) * acc_scr[hh, rows, :] + pv
                m_scr[hh, rows, :] = m_new

    @pl.when(ki < qi // ratio)
    def _():
        step(False)

    @pl.when(ki == qi // ratio)
    def _():
        step(True)
        for hh, hs in enumerate(heads):
            o = acc_scr[hh, :, 0:LANES] / acc_scr[hh, :, LANES:2 * LANES]
            o = o[0:tq] - lam_ref[...] * o[tq:2 * tq]
            o_ref[:, hs] = _subln(o, g_ref[...]).astype(o_ref.dtype)


def _attn_prompt(q, k, v, lam_row, subln_g, batch, seq, tq, tk):
    d = q.shape[1]
    assert tk % tq == 0 and seq % tk == 0 and tq % CHUNK == 0
    qi_l, ki_l = [], []
    for a in range(seq // tq):
        for b in range(a * tq // tk + 1):
            qi_l.append(a)
            ki_l.append(b)
    qi_tab = jnp.asarray(qi_l, jnp.int32)
    ki_tab = jnp.asarray(ki_l, jnp.int32)
    q3, k3, v3 = (a.reshape(batch, seq, d) for a in (q, k, v))
    hw = ATT_HEADS * LANES
    qspec = pl.BlockSpec((None, tq, hw), lambda b, h, p, qt, kt: (b, qt[p], h))
    kspec = pl.BlockSpec((None, tk, hw), lambda b, h, p, qt, kt: (b, kt[p], h))
    vec = pl.BlockSpec((1, LANES), lambda b, h, p, qt, kt: (0, 0))
    out = pl.pallas_call(
        functools.partial(_attn_prompt_kernel, tq=tq, tk=tk),
        grid_spec=pltpu.PrefetchScalarGridSpec(
            num_scalar_prefetch=2,
            grid=(batch, DA_HEADS // ATT_HEADS, len(qi_l)),
            in_specs=[vec, qspec, kspec, kspec, vec],
            out_specs=qspec,
            scratch_shapes=[pltpu.VMEM((ATT_HEADS, 2 * tq, LANES), BF16),
                            pltpu.VMEM((ATT_HEADS, tk, 2 * LANES), BF16),
                            pltpu.VMEM((ATT_HEADS, 2 * tq, LANES), F32),
                            pltpu.VMEM((ATT_HEADS, 2 * tq, 2 * LANES), F32)]),
        out_shape=jax.ShapeDtypeStruct((batch, seq, d), BF16),
        compiler_params=_cparams("parallel", "parallel", "arbitrary"),
        name="attn_prompt",
    )(qi_tab, ki_tab, lam_row, q3, k3, v3, subln_g)
    return out.reshape(batch * seq, d)


def _attn_sample_kernel(lam_ref, q_ref, kn_ref, vn_ref, ck_ref, cv_ref, g_ref, o_ref, *, past, sq):
    for h in range(DA_HEADS):
        hs = slice(h * LANES, (h + 1) * LANES)
        q = q_ref[:, hs]
        lane = lax.broadcasted_iota(jnp.int32, q.shape, 1)
        zero = jnp.zeros_like(q)
        qs = jnp.concatenate([jnp.where(lane < DA_HEAD_DIM, q, zero),
                              jnp.where(lane >= DA_HEAD_DIM, q, zero)], axis=0)
        kc = ck_ref[pl.ds(h, past, stride=DA_HEADS), :].astype(BF16)
        vc = cv_ref[pl.ds(h, past, stride=DA_HEADS), :].astype(BF16)
        s_c = _dot_nt(qs, kc)
        s_n = _dot_nt(qs, kn_ref[:, hs])

        def masked(s, key0):
            row = lax.broadcasted_iota(jnp.int32, s.shape, 0)
            col = lax.broadcasted_iota(jnp.int32, s.shape, 1)
            qchunk = (past + row % sq) // CHUNK
            return jnp.where((key0 + col) // CHUNK <= qchunk, s, NEG_INF)

        s_c = masked(s_c, 0)
        s_n = masked(s_n, past)
        m = jnp.maximum(jnp.max(s_c, axis=-1, keepdims=True), jnp.max(s_n, axis=-1, keepdims=True))
        e_c = jnp.exp2(s_c - m)
        e_n = jnp.exp2(s_n - m)
        l = jnp.sum(e_c, axis=-1, keepdims=True) + jnp.sum(e_n, axis=-1, keepdims=True)
        o = (_dot(e_c.astype(BF16), vc) + _dot(e_n.astype(BF16), vn_ref[:, hs])) / l
        o = o[0:sq] - lam_ref[...] * o[sq:2 * sq]
        o_ref[:, hs] = _subln(o, g_ref[...]).astype(o_ref.dtype)


def _attn_sample(q, kn, vn, cache_k, cache_v, lam_row, subln_g, batch, sq):
    d = q.shape[1]
    past = cache_k.shape[-3]
    ck = cache_k.reshape(batch, past * DA_HEADS, LANES)
    cv = cache_v.reshape(batch, past * DA_HEADS, LANES)
    q3, k3, v3 = (a.reshape(batch, sq, d) for a in (q, kn, vn))
    new = pl.BlockSpec((None, sq, d), lambda b: (b, 0, 0))
    old = pl.BlockSpec((None, past * DA_HEADS, LANES), lambda b: (b, 0, 0))
    vec = pl.BlockSpec((1, LANES), lambda b: (0, 0))
    out = pl.pallas_call(
        functools.partial(_attn_sample_kernel, past=past, sq=sq),
        grid=(batch,),
        in_specs=[vec, new, new, new, old, old, vec],
        out_specs=new,
        out_shape=jax.ShapeDtypeStruct((batch, sq, d), BF16),
        compiler_params=_cparams("parallel"),
        name="attn_sample",
    )(lam_row, q3, k3, v3, ck, cv, subln_g)
    return out.reshape(batch * sq, d)


RW_L = 64
RW_PAIRS = 8
RW_CPS = 2


def _mm3(a, b, dot=_dot):
    ah, al = _split2(a)
    bh, bl = _split2(b)
    return dot(ah, bh) + dot(ah, bl) + dot(al, bh)


def _terms(x, n):
    return (x.astype(BF16),) if n == 1 else _split2(x)


def _mmt(at, bt, dot=_dot):
    acc = dot(at[0], bt[0])
    if len(bt) > 1:
        acc = acc + dot(at[0], bt[1])
    if len(at) > 1:
        acc = acc + dot(at[1], bt[0])
    return acc


RW_PREC = dict(a_all=(1, 1), neumann=(1, 1), av=(1, 1), pu=(1, 1), qy=(1, 1), mh=(1, 1), state=(1, 1))


def _softplus(z):
    return jnp.maximum(z, 0.0) + jnp.log(1.0 + jnp.exp(-jnp.abs(z)))


def _rwkv_kernel(rw_ref, sh0_ref, mu_ref, lw_ref, w0_ref, a0_ref, kk_ref, ka_ref, rk_ref, lng_ref, lnb_ref,
                 h0_ref, gm_ref, tri_ref, ob_ref, hout_ref, h_scr, last_scr, *, n_valid, cps):
    L = RW_L
    rows = cps * L
    c = pl.program_id(1)
    d = ob_ref.shape[-1]

    @pl.when(c == 0)
    def _():
        h_scr[...] = h0_ref[...]
        last_scr[...] = sh0_ref[...]

    rw = rw_ref[...]
    row_w = lax.broadcasted_iota(jnp.int32, rw.shape, 0)
    prev = jnp.where(row_w == 0, last_scr[...], pltpu.roll(rw, 1, 0))
    last_scr[...] = rw[rows - 1:rows, :]
    xm = rw + (prev - rw) * mu_ref[...]
    r = xm[:, 0:d]
    k = xm[:, d:2 * d]
    v = xm[:, 2 * d:3 * d]
    lo = xm[:, 3 * d:3 * d + LORA_PAD]
    lane_l = lax.broadcasted_iota(jnp.int32, lo.shape, 1)
    z = jnp.where(lane_l < W_LORA, jnp.tanh(lo),
                  jnp.where(lane_l < W_LORA + A_LORA, lo, jax.nn.sigmoid(lo)))
    lora = _dot(z.astype(BF16), lw_ref[...])
    w = -_softplus(-(w0_ref[...] + lora[:, 0:d])) - 0.5
    ld = -jnp.exp(w)
    a = jax.nn.sigmoid(a0_ref[...] + lora[:, d:2 * d])
    g = lora[:, 2 * d:3 * d]
    kkr = k * kk_ref[...]
    kmod = k * (1.0 + (a - 1.0) * ka_ref[...])
    rkk = r * kmod * rk_ref[...]
    gm = gm_ref[...]
    if n_valid < rows:
        valid = lax.broadcasted_iota(jnp.int32, ld.shape, 0) < n_valid
        ld = jnp.where(valid, ld, 0.0)
        kkr = jnp.where(valid, kkr, 0.0)
        kmod = jnp.where(valid, kmod, 0.0)

    tri = tri_ref[...]
    l1, l2, l3 = _split3(ld)
    cin = _dot(tri, l1) + _dot(tri, l2) + _dot(tri, l3)
    cex = cin - ld
    c_last = [cin[(ch + 1) * L - 1:(ch + 1) * L, :] for ch in range(cps)]
    e_in = jnp.exp(cin)
    e_ex = jnp.exp(cex)
    e_neg = jnp.exp(-cin)
    e_tail = jnp.exp(jnp.concatenate([jnp.broadcast_to(cl, (L, d)) for cl in c_last], axis=0) - cin)
    g_last = [jnp.exp(cl) for cl in c_last]

    row = lax.broadcasted_iota(jnp.int32, (L, LANES), 0)
    lane = lax.broadcasted_iota(jnp.int32, (L, LANES), 1)
    m_lo = lane < RW_HEAD
    tri_s = row > (lane % L)
    tri_i = row >= (lane % L)
    eye_pair = jnp.where(row == (lane % L), 1.0, 0.0).astype(F32)
    r2 = lax.broadcasted_iota(jnp.int32, (LANES, LANES), 0)
    c2 = lax.broadcasted_iota(jnp.int32, (LANES, LANES), 1)
    bd_mask = (r2 < RW_HEAD) == (c2 < RW_HEAD)
    diag_mask = r2 == c2

    def bd(x):
        z0 = jnp.zeros_like(x)
        return jnp.concatenate([jnp.where(m_lo, x, z0), jnp.where(m_lo, z0, x)], axis=0)

    def bdt(t):
        return tuple(bd(x) for x in t)

    def cat(ts, axis):
        return tuple(jnp.concatenate(xs, axis=axis) for xs in zip(*ts))

    prec = RW_PREC
    items = [(slice(ch * L, (ch + 1) * L), slice(p * LANES, (p + 1) * LANES))
             for ch in range(cps) for p in range(RW_PAIRS)]
    pairs = range(len(items))

    v_p = [v[rs, sl] for rs, sl in items]
    kk_p = []
    for rs, sl in items:
        kk_raw = kkr[rs, sl]
        ss = _group_mean(kk_raw * kk_raw, gm) * RW_HEAD
        kk_p.append(kk_raw / jnp.maximum(jnp.sqrt(ss), 1e-12))
    b_p = [kk_p[i] * a[rs, sl] for i, (rs, sl) in enumerate(items)]
    abar = [-kk_p[i] * e_ex[rs, sl] for i, (rs, sl) in enumerate(items)]
    rbar = [r[rs, sl] * e_in[rs, sl] for rs, sl in items]
    bbar = [b_p[i] * e_neg[rs, sl] for i, (rs, sl) in enumerate(items)]
    kbar = [kmod[rs, sl] * e_neg[rs, sl] for rs, sl in items]
    btil = [b_p[i] * e_tail[rs, sl] for i, (rs, sl) in enumerate(items)]
    ktil = [kmod[rs, sl] * e_tail[rs, sl] for rs, sl in items]

    na, nb = prec["a_all"]
    abar_t = [_terms(abar[p], max(na, prec["pu"][1])) for p in pairs]
    v_t = [_terms(v_p[p], max(prec["av"][1], prec["qy"][1], prec["mh"][1])) for p in pairs]
    a_all = []
    for p in pairs:
        lhs = cat([abar_t[p][:na], _terms(rbar[p], na)], 0)
        rhs = cat([bdt(_terms(bbar[p], nb)), bdt(_terms(kbar[p], nb))], 0)
        a_all.append(_mmt(lhs, rhs, _dot_nt))
    n_ab = [jnp.where(tri_s, a_all[p][0:L, 0:2 * L], 0.0) for p in pairs]
    a_ak = [jnp.where(tri_s, a_all[p][0:L, 2 * L:4 * L], 0.0) for p in pairs]
    a_rb = [jnp.where(tri_i, a_all[p][L:2 * L, 0:2 * L], 0.0) for p in pairs]
    a_rk = [jnp.where(tri_i, a_all[p][L:2 * L, 2 * L:4 * L], 0.0) for p in pairs]

    na, nb = prec["neumann"]
    t_inv = [eye_pair + n_ab[p] for p in pairs]
    pw_t = [_terms(n_ab[p], max(na, nb)) for p in pairs]
    pw = [_mmt(pw_t[p][:na], bdt(pw_t[p][:nb])) for p in pairs]
    for _ in range(int(math.log2(L)) - 2):
        pw_t = [_terms(pw[p], max(na, nb)) for p in pairs]
        st = [_mmt(cat([_terms(t_inv[p], na), pw_t[p][:na]], 0), bdt(pw_t[p][:nb])) for p in pairs]
        t_inv = [t_inv[p] + st[p][0:L] for p in pairs]
        pw = [st[p][L:2 * L] for p in pairs]
    t_inv = [t_inv[p] + _mmt(_terms(t_inv[p], na), bdt(_terms(pw[p], nb))) for p in pairs]

    na, nb = prec["av"]
    vv = [_mmt(cat([_terms(a_ak[p], na), _terms(a_rk[p], na)], 0), bdt(v_t[p][:nb])) for p in pairs]
    av = [x[0:L] for x in vv]
    arkv = [x[L:2 * L] for x in vv]
    na, nb = prec["pu"]
    pu = [_mmt(_terms(t_inv[p], na), cat([bdt(abar_t[p][:nb]), bdt(_terms(av[p], nb))], 1)) for p in pairs]
    p_m = [x[:, 0:LANES] for x in pu]
    u0 = [x[:, LANES:2 * LANES] for x in pu]
    na, nb = prec["qy"]
    nmh = prec["mh"][1]
    p_t = [_terms(p_m[p], max(nb, nmh)) for p in pairs]
    u_t = [_terms(u0[p], max(nb, nmh)) for p in pairs]
    qy = [_mmt(_terms(a_rb[p], na), cat([bdt(p_t[p][:nb]), bdt(u_t[p][:nb])], 1)) for p in pairs]
    q_m = [rbar[p] + qy[p][:, 0:LANES] for p in pairs]
    y0 = [qy[p][:, LANES:2 * LANES] + arkv[p] for p in pairs]

    na, nb = prec["mh"]
    btil_t = [_terms(btil[p], na) for p in pairs]
    m_full = [_mmt(btil_t[p], p_t[p][:nb], _dot_tn) for p in pairs]
    h0_full = [_mmt(cat([btil_t[p], _terms(ktil[p], na)], 0), cat([u_t[p][:nb], v_t[p][:nb]], 0), _dot_tn)
               for p in pairs]
    m_bd = [jnp.where(bd_mask, m_full[i], 0.0) + jnp.where(diag_mask, g_last[i // RW_PAIRS][:, sl], 0.0)
            for i, (rs, sl) in enumerate(items)]
    h0_bd = [jnp.where(bd_mask, h0_full[p], 0.0) for p in pairs]

    na, nb = prec["state"]
    h_cur = [h_scr[p] for p in range(RW_PAIRS)]
    y = []
    for ch in range(cps):
        idx = [ch * RW_PAIRS + p for p in range(RW_PAIRS)]
        st = [_mmt(cat([_terms(q_m[i], na), _terms(m_bd[i], na)], 0), _terms(h_cur[p], nb))
              for p, i in enumerate(idx)]
        y.extend(st[p][0:L] + y0[i] for p, i in enumerate(idx))
        h_cur = [st[p][L:L + LANES] + h0_bd[i] for p, i in enumerate(idx)]
    for p in range(RW_PAIRS):
        h_scr[p] = h_cur[p]

    for i, (rs, sl) in enumerate(items):
        mu_y = _group_mean(y[i], gm)
        dy = y[i] - mu_y
        var = _group_mean(dy * dy, gm)
        yn = dy * lax.rsqrt(var + RW_GN_EPS) * lng_ref[:, sl] + lnb_ref[:, sl]
        bonus = _group_mean(rkk[rs, sl], gm) * RW_HEAD * v_p[i]
        ob_ref[rs, sl] = ((yn + bonus) * g[rs, sl]).astype(ob_ref.dtype)

    @pl.when(c == pl.num_programs(1) - 1)
    def _():
        hout_ref[...] = h_scr[...]


def _rwkv(rw3, shift0, h0_bd, prm, gm, n_valid, cps):
    batch, seq, wcols = rw3.shape
    d = prm["w0"].shape[1]
    rows = cps * RW_L
    nc = seq // rows
    tri = jnp.kron(jnp.eye(cps, dtype=F32), jnp.tril(jnp.ones((RW_L, RW_L), F32))).astype(BF16)
    const2 = lambda b, c: (0, 0)
    vec = pl.BlockSpec((1, d), const2)
    st = pl.BlockSpec((None, RW_PAIRS, LANES, LANES), lambda b, c: (b, 0, 0, 0))
    return pl.pallas_call(
        functools.partial(_rwkv_kernel, n_valid=n_valid, cps=cps),
        grid=(batch, nc),
        in_specs=[pl.BlockSpec((None, rows, wcols), lambda b, c: (b, c, 0)),
                  pl.BlockSpec((None, 1, wcols), lambda b, c: (b, 0, 0)),
                  pl.BlockSpec((1, wcols), const2),
                  pl.BlockSpec(prm["lw"].shape, const2),
                  vec, vec, vec, vec, vec, vec, vec, st,
                  pl.BlockSpec((LANES, LANES), const2),
                  pl.BlockSpec((rows, rows), const2)],
        out_specs=[pl.BlockSpec((None, rows, d), lambda b, c: (b, c, 0)), st],
        out_shape=[jax.ShapeDtypeStruct((batch, seq, d), BF16),
                   jax.ShapeDtypeStruct((batch, RW_PAIRS, LANES, LANES), F32)],
        scratch_shapes=[pltpu.VMEM((RW_PAIRS, LANES, LANES), F32), pltpu.VMEM((1, wcols), F32)],
        compiler_params=_cparams("parallel", "arbitrary"),
        name="rwkv7",
    )(rw3, shift0, prm["mu"], prm["lw"], prm["w0"], prm["a0"], prm["k_k"], prm["k_a"], prm["r_k"],
      prm["ln_g"], prm["ln_b"], h0_bd, gm, tri)


def _state_to_bd(s):
    b = s.shape[0]
    ht = jnp.swapaxes(s, -1, -2).reshape(b, RW_PAIRS, 2, RW_HEAD, RW_HEAD)
    z = jnp.zeros_like(ht[:, :, 0])
    top = jnp.concatenate([ht[:, :, 0], z], axis=-1)
    bot = jnp.concatenate([z, ht[:, :, 1]], axis=-1)
    return jnp.concatenate([top, bot], axis=-2)


def _bd_to_state(hbd):
    b = hbd.shape[0]
    h0 = hbd[:, :, 0:RW_HEAD, 0:RW_HEAD]
    h1 = hbd[:, :, RW_HEAD:, RW_HEAD:]
    ht = jnp.stack([h0, h1], axis=2).reshape(b, 2 * RW_PAIRS, RW_HEAD, RW_HEAD)
    return jnp.swapaxes(ht, -1, -2)


def _route(logits):
    lane = lax.broadcasted_iota(jnp.int32, logits.shape, 1).astype(F32)
    big = float(LANES)
    lg = jnp.where(lane < N_GROUPS, logits, NEG_INF)
    mg = jnp.max(lg, axis=-1, keepdims=True)
    sg = jnp.sum(jnp.exp(lg - mg), axis=-1, keepdims=True)
    p_top = 1.0 / sg
    g_idx = jnp.min(jnp.where(lg == mg, lane, big), axis=-1, keepdims=True)
    e0 = N_GROUPS + EXPERTS_PER_GROUP * g_idx
    emask = jnp.where(lane >= e0, jnp.where(lane < e0 + EXPERTS_PER_GROUP, 1.0, 0.0), 0.0) > 0.5
    le = jnp.where(emask, logits, NEG_INF)
    me = jnp.max(le, axis=-1, keepdims=True)
    ee = jnp.exp(le - me)
    pe = ee / jnp.sum(ee, axis=-1, keepdims=True)
    pe = jnp.where(emask, pe, -1.0)
    v1 = jnp.max(pe, axis=-1, keepdims=True)
    i1 = jnp.min(jnp.where(pe == v1, lane, big), axis=-1, keepdims=True)
    pe2 = jnp.where(lane == i1, -1.0, pe)
    v2 = jnp.max(pe2, axis=-1, keepdims=True)
    i2 = jnp.min(jnp.where(pe2 == v2, lane, big), axis=-1, keepdims=True)
    den = v1 + v2
    ew = jnp.where(lane == i1, v1 / den, 0.0) + jnp.where(lane == i2, v2 / den, 0.0)
    return p_top * ew + jnp.where(lane == 0.0, g_idx, 0.0)


def _merge_kernel(oa_ref, ob_ref, gates_ref, x_ref, gt1_ref, sc2_ref, sh2_ref, g2_ref,
                  woa_ref, wob_ref, wout_ref, wr_ref, br_ref, x1_ref, hx_ref):
    d = x_ref.shape[1]
    a_out = _dot(oa_ref[...], woa_ref[...])
    b_out = _dot(ob_ref[...], wob_ref[...])
    merged = gates_ref[:, 0:d].astype(F32) * a_out + gates_ref[:, d:2 * d].astype(F32) * b_out
    x1 = x_ref[...] + gt1_ref[...] * _dot(merged.astype(BF16), wout_ref[...])
    x1_ref[...] = x1
    ms = jnp.mean(x1 * x1, axis=-1, keepdims=True)
    h2 = x1 * lax.rsqrt(ms + EPS) * g2_ref[...]
    h2 = h2 * (1.0 + sc2_ref[...]) + sh2_ref[...]
    logits = _mm3(h2, wr_ref[...]) + br_ref[...]
    hx_ref[:, 0:d] = h2
    hx_ref[:, d:d + LANES] = _route(logits)


def _merge(oa, ob, gates, x2d, gt1, sc2, sh2, g2, w_oa, w_ob, w_out, wr, br, tm, seq):
    t, d = x2d.shape
    row = lambda i: (i, 0)
    const = lambda i: (0, 0)
    full = lambda w: pl.BlockSpec(w.shape, const)
    return pl.pallas_call(
        _merge_kernel,
        grid=(t // tm,),
        in_specs=[pl.BlockSpec((tm, d), row), pl.BlockSpec((tm, d), row), pl.BlockSpec((tm, 2 * d), row),
                  pl.BlockSpec((tm, d), row), _mod_spec(gt1, tm, seq), _mod_spec(sc2, tm, seq),
                  _mod_spec(sh2, tm, seq), pl.BlockSpec((1, d), const),
                  full(w_oa), full(w_ob), full(w_out), full(wr), full(br)],
        out_specs=[pl.BlockSpec((tm, d), row), pl.BlockSpec((tm, d + LANES), row)],
        out_shape=[jax.ShapeDtypeStruct((t, d), F32), jax.ShapeDtypeStruct((t, d + LANES), F32)],
        compiler_params=_cparams("parallel"),
        name="merge_router",
    )(oa, ob, gates, x2d, gt1, sc2, sh2, g2, w_oa, w_ob, w_out, wr, br)


def _expert_ffn(xb, comb, lane_e, w13, w2):
    au = _dot(xb, w13)
    f = au.shape[1] // 2
    a = au[:, 0:f]
    u = au[:, f:2 * f]
    lane = lax.broadcasted_iota(jnp.int32, comb.shape, 1)
    cw = jnp.sum(jnp.where(lane == lane_e, comb, 0.0), axis=-1, keepdims=True)
    act = a * jax.nn.sigmoid(a) * u * cw
    return _dot(act.astype(BF16), w2)


def _moe_dense_kernel(hx_ref, w13_ref, w2_ref, x1_ref, gt2_ref, y_ref, xb_scr, acc_scr):
    e = pl.program_id(1)
    d = y_ref.shape[1]

    @pl.when(e == 0)
    def _():
        xb_scr[...] = hx_ref[:, 0:d].astype(BF16)
        acc_scr[...] = jnp.zeros(acc_scr.shape, F32)

    acc_scr[...] += _expert_ffn(xb_scr[...], hx_ref[:, d:d + LANES], e + N_GROUPS, w13_ref[...], w2_ref[...])

    @pl.when(e == pl.num_programs(1) - 1)
    def _():
        y_ref[...] = x1_ref[...] + gt2_ref[...] * acc_scr[...]


def _moe_dense(hx, w13, w2, x1, gt2, tm, seq):
    t, d = x1.shape
    ne, _, f2 = w13.shape
    row = lambda i, e: (i, 0)
    return pl.pallas_call(
        _moe_dense_kernel,
        grid=(t // tm, ne),
        in_specs=[pl.BlockSpec((tm, d + LANES), row),
                  pl.BlockSpec((None, d, f2), lambda i, e: (e, 0, 0)),
                  pl.BlockSpec((None, f2 // 2, d), lambda i, e: (e, 0, 0)),
                  pl.BlockSpec((tm, d), row), _mod_spec(gt2, tm, seq)],
        out_specs=pl.BlockSpec((tm, d), row),
        out_shape=jax.ShapeDtypeStruct((t, d), F32),
        scratch_shapes=[pltpu.VMEM((tm, d), BF16), pltpu.VMEM((tm, d), F32)],
        compiler_params=_cparams("parallel", "arbitrary"),
        name="moe_dense",
    )(hx, w13, w2, x1, gt2)


MOE_TM = 1024
MOE_TS = 1024


def _plan_kernel(rt_ref, tri_ref, rank_ref, cnt_ref, carry_scr):
    @pl.when(pl.program_id(0) == 0)
    def _():
        carry_scr[...] = jnp.zeros(carry_scr.shape, F32)

    rt = rt_ref[...]
    lane = lax.broadcasted_iota(jnp.int32, rt.shape, 1).astype(F32)
    onehot = jnp.where(lane == rt[:, 0:1], 1.0, 0.0)
    before = _dot(tri_ref[...], onehot.astype(BF16)) + carry_scr[...]
    rank_ref[...] = jnp.sum(onehot * before, axis=-1, keepdims=True).astype(jnp.int32)
    carry_scr[...] += jnp.sum(onehot, axis=0, keepdims=True)
    cnt_ref[...] = carry_scr[...]


def _moe_plan(hx, d):
    t = hx.shape[0]
    tp = 512
    tri = jnp.tril(jnp.ones((tp, tp), F32), -1).astype(BF16)
    return pl.pallas_call(
        _plan_kernel,
        grid=(t // tp,),
        in_specs=[pl.BlockSpec((tp, LANES), lambda i: (i, d // LANES)),
                  pl.BlockSpec((tp, tp), lambda i: (0, 0))],
        out_specs=[pl.BlockSpec((tp, 1), lambda i: (i, 0)), pl.BlockSpec((1, LANES), lambda i: (0, 0))],
        out_shape=[jax.ShapeDtypeStruct((t, 1), jnp.int32), jax.ShapeDtypeStruct((1, LANES), F32)],
        scratch_shapes=[pltpu.VMEM((1, LANES), F32)],
        compiler_params=_cparams("arbitrary"),
        name="moe_plan",
    )(hx, tri)


def _row_copy(src_hbm, src_row, dst, dst_row, sem):
    return pltpu.make_async_copy(src_hbm.at[pl.ds(src_row, 1), :], dst.at[pl.ds(dst_row, 1), :], sem)


def _dispatch_kernel(pos_ref, hx_ref, xs_in_hbm, xs_hbm, sem):
    del xs_in_hbm

    def issue(t, carry):
        _row_copy(hx_ref, t, xs_hbm, pos_ref[t], sem).start()
        return carry

    def drain(t, carry):
        _row_copy(hx_ref, t, xs_hbm, pos_ref[t], sem).wait()
        return carry

    lax.fori_loop(0, MOE_TS, issue, 0, unroll=8)
    lax.fori_loop(0, MOE_TS, drain, 0, unroll=8)


def _moe_dispatch(pos, hx, n_rows):
    t, w = hx.shape
    xs0 = jnp.zeros((n_rows, w), F32)
    return pl.pallas_call(
        _dispatch_kernel,
        grid=(t // MOE_TS,),
        in_specs=[pl.BlockSpec((MOE_TS,), lambda i: (i,), memory_space=pltpu.SMEM),
                  pl.BlockSpec((MOE_TS, w), lambda i: (i, 0)), pl.BlockSpec(memory_space=pl.ANY)],
        out_specs=pl.BlockSpec(memory_space=pl.ANY),
        out_shape=jax.ShapeDtypeStruct((n_rows, w), F32),
        scratch_shapes=[pltpu.SemaphoreType.DMA(())],
        input_output_aliases={2: 0},
        compiler_params=_cparams("arbitrary"),
        name="moe_dispatch",
    )(pos, hx, xs0)


def _moe_bucket_kernel(tg_ref, tv_ref, xs_ref, w13_ref, w2_ref, ys_ref, xb_scr):
    i = pl.program_id(0)
    e = pl.program_id(1)
    d = ys_ref.shape[1]

    @pl.when(e == 0)
    def _():
        xb_scr[...] = xs_ref[:, 0:d].astype(BF16)
        ys_ref[...] = jnp.zeros(ys_ref.shape, F32)

    @pl.when(tv_ref[i] > 0)
    def _():
        lane_e = N_GROUPS + tg_ref[i] * EXPERTS_PER_GROUP + e
        ys_ref[...] += _expert_ffn(xb_scr[...], xs_ref[:, d:d + LANES], lane_e, w13_ref[...], w2_ref[...])


def _moe_buckets(tile_group, tile_valid, xs, w13, w2, d):
    n_rows, w = xs.shape
    _, _, f2 = w13.shape
    row = lambda i, e, tg, tv: (i, 0)
    wsel = lambda i, e, tg, tv: (tg[i] * EXPERTS_PER_GROUP + e, 0, 0)
    return pl.pallas_call(
        _moe_bucket_kernel,
        grid_spec=pltpu.PrefetchScalarGridSpec(
            num_scalar_prefetch=2,
            grid=(n_rows // MOE_TM, EXPERTS_PER_GROUP),
            in_specs=[pl.BlockSpec((MOE_TM, w), row),
                      pl.BlockSpec((None, d, f2), wsel), pl.BlockSpec((None, f2 // 2, d), wsel)],
            out_specs=pl.BlockSpec((MOE_TM, d), row),
            scratch_shapes=[pltpu.VMEM((MOE_TM, d), BF16)]),
        out_shape=jax.ShapeDtypeStruct((n_rows, d), F32),
        compiler_params=_cparams("parallel", "arbitrary"),
        name="moe_buckets",
    )(tile_group, tile_valid, xs, w13, w2)


def _collect_kernel(pos_ref, ys_hbm, x1_ref, gt2_ref, y_ref, buf, sem):
    tu = buf.shape[0]

    def issue(t, carry):
        _row_copy(ys_hbm, pos_ref[t], buf, t, sem).start()
        return carry

    def drain(t, carry):
        _row_copy(ys_hbm, pos_ref[t], buf, t, sem).wait()
        return carry

    lax.fori_loop(0, tu, issue, 0, unroll=8)
    lax.fori_loop(0, tu, drain, 0, unroll=8)
    y_ref[...] = x1_ref[...] + gt2_ref[...] * buf[...]


def _moe_collect(pos, ys, x1, gt2, seq):
    t, d = x1.shape
    tu = MOE_TS
    return pl.pallas_call(
        _collect_kernel,
        grid=(t // tu,),
        in_specs=[pl.BlockSpec((tu,), lambda i: (i,), memory_space=pltpu.SMEM),
                  pl.BlockSpec(memory_space=pl.ANY),
                  pl.BlockSpec((tu, d), lambda i: (i, 0)), _mod_spec(gt2, tu, seq)],
        out_specs=pl.BlockSpec((tu, d), lambda i: (i, 0)),
        out_shape=jax.ShapeDtypeStruct((t, d), F32),
        scratch_shapes=[pltpu.VMEM((tu, d), F32), pltpu.SemaphoreType.DMA(())],
        compiler_params=_cparams("arbitrary"),
        name="moe_collect",
    )(pos, ys, x1, gt2)


def _moe_sorted(hx, w13, w2, x1, gt2, seq):
    t, d = x1.shape
    rank, cnt = _moe_plan(hx, d)
    group = hx[:, d].astype(jnp.int32)
    counts = cnt[0, 0:N_GROUPS].astype(jnp.int32)
    padded = (counts + MOE_TM - 1) // MOE_TM * MOE_TM
    ends = jnp.cumsum(padded)
    starts = ends - padded
    pos = jnp.take(starts, group) + rank[:, 0]
    n_tiles = t // MOE_TM + N_GROUPS
    tile_start = jnp.arange(n_tiles, dtype=jnp.int32) * MOE_TM
    tile_group = jnp.minimum(jnp.sum((tile_start[:, None] >= ends[None, :]).astype(jnp.int32), axis=1),
                             N_GROUPS - 1)
    tile_valid = (tile_start < ends[-1]).astype(jnp.int32)
    xs = _moe_dispatch(pos, hx, n_tiles * MOE_TM)
    ys = _moe_buckets(tile_group, tile_valid, xs, w13, w2, d)
    return _moe_collect(pos, ys, x1, gt2, seq)


def _rope_tables(pos0, seq, reps):
    pos = (pos0 + jnp.arange(seq)).astype(F32)
    inv = ROPE_THETA ** (-jnp.arange(0, DA_HEAD_DIM, 2, dtype=F32) / DA_HEAD_DIM)
    ang = pos[:, None] * inv[None, :]
    cos, sin = jnp.cos(ang), jnp.sin(ang)
    cos_t = jnp.tile(jnp.concatenate([cos, cos], axis=-1), (reps, LANES // DA_HEAD_DIM))
    sin_t = jnp.tile(jnp.concatenate([-sin, sin], axis=-1), (reps, LANES // DA_HEAD_DIM))
    return cos_t, sin_t


def _layer(x, mod, pos0, past, wts):
    bx, sx, d = x.shape
    t = bx * sx
    x2d = x.reshape(t, d)
    per_batch = sx >= 512
    tm = 512 if per_batch else t

    def modv(i):
        if per_batch:
            return mod[:, i:i + 1, :]
        return jnp.repeat(mod[:, i, :], sx, axis=0)[None]

    sh1, sc1, gt1, sh2, sc2, gt2 = (modv(i) for i in range(6))
    cos_t, sin_t = _rope_tables(pos0, sx, 1 if per_batch else bx)

    tm_big = 1024 if (per_batch and sx % 1024 == 0) else tm
    h = _prenorm(x2d, sc1, sh1, wts["norm1_g"], tm, sx)
    q, k32, kb, v32, vb = _qkv(h, wts["w_qkv"], wts["qg"], wts["kg"], cos_t, sin_t, wts["gm"], tm_big)
    rw = _proj(h, wts["w_rw"], tm_big, wts["w_rw"].shape[1] // 3, None, F32, "rw_proj")
    gates = _proj(h, wts["w_gate"], tm_big, d, "sigmoid", BF16, "gate_proj")

    if past is None:
        oa = _attn_prompt(q, kb, vb, wts["lam"], wts["subln_g"], bx, sx, min(1024, sx), min(1024, sx))
        shift0 = jnp.zeros((bx, 1, rw.shape[1]), F32)
        h0_bd = jnp.zeros((bx, RW_PAIRS, LANES, LANES), F32)
    else:
        ck, cv, s0, sh0 = past
        oa = _attn_sample(q, kb, vb, ck, cv, wts["lam"], wts["subln_g"], bx, sx)
        shift0 = jnp.pad(sh0, ((0, 0), (0, 0), (0, rw.shape[1] - sh0.shape[-1])))
        h0_bd = _state_to_bd(s0.astype(F32))

    rw3 = rw.reshape(bx, sx, rw.shape[1])
    if sx % RW_L == 0:
        cps = RW_CPS if sx % (RW_CPS * RW_L) == 0 else 1
        rw_in, n_valid = rw3, cps * RW_L
    else:
        assert sx < RW_L
        cps = 1
        rw_in, n_valid = jnp.pad(rw3, ((0, 0), (0, RW_L - sx), (0, 0))), sx
    ob, h_bd = _rwkv(rw_in, shift0, h0_bd, wts, wts["gm"], n_valid, cps)
    ob = ob[:, 0:sx, :].reshape(t, d)

    x1, hx = _merge(oa, ob, gates, x2d, gt1, sc2, sh2, wts["norm2_g"],
                    wts["w_oa"], wts["w_ob"], wts["w_out"], wts["wr"], wts["br"], tm, sx)
    if per_batch and sx % MOE_TS == 0:
        y = _moe_sorted(hx, wts["w13"], wts["w2"], x1, gt2, sx)
    else:
        y = _moe_dense(hx, wts["w13"], wts["w2"], x1, gt2, tm, sx)

    n_cols = 3 * d + W_LORA + A_LORA + G_LORA
    return (y.reshape(bx, sx, d),
            k32.reshape(1, bx, sx, DA_HEADS, 2 * DA_HEAD_DIM),
            v32.reshape(1, bx, sx, DA_HEADS, 2 * DA_HEAD_DIM),
            _bd_to_state(h_bd)[None],
            rw3[:, sx - 1:sx, 0:n_cols][None])


def kernel(x_prompt, x_sample, cache_k, cache_v, state_wkv, state_shift, c_prompt, c_sample, norm1_g, norm2_g, w_ada, b_ada, w_in, da_qn_g, da_kn_g, da_lambda, da_subln_g, w_oa, rw_mu, rw_w0, rw_w2, rw_a0, rw_a2, rw_g2, rw_k_k, rw_k_a, rw_r_k, rw_ln_g, rw_ln_b, w_ob, w_out, router_g, router_g_b, router_e, router_e_b, exp_w1, exp_w3, exp_w2):
    b, s, d = x_prompt.shape
    bs = x_sample.shape[0]
    past_len = cache_k.shape[2]
    assert w_in.shape[0] == 1, "single layer"

    c_all = jnp.concatenate([c_prompt, c_sample], axis=0)
    c_all = jnp.pad(c_all, ((0, (-c_all.shape[0]) % 8), (0, 0)))
    mod, lam_tile = _adaln(c_all, w_ada[0], b_ada[0], da_lambda[0])
    mod_p = mod[0:b].reshape(b, 6, d)
    mod_s = mod[b:b + bs].reshape(bs, 6, d)

    win = w_in[0]
    rw_cols = 3 * d + W_LORA + A_LORA + G_LORA
    rw_pad = 3 * d + LORA_PAD
    w_rw = jnp.pad(win[:, 3 * d:3 * d + rw_cols], ((0, 0), (0, rw_pad - rw_cols))).astype(BF16)
    lw = jnp.zeros((LORA_PAD, 3 * d), F32)
    lw = lw.at[0:W_LORA, 0:d].set(rw_w2[0])
    lw = lw.at[W_LORA:W_LORA + A_LORA, d:2 * d].set(rw_a2[0])
    lw = lw.at[W_LORA + A_LORA:W_LORA + A_LORA + G_LORA, 2 * d:3 * d].set(rw_g2[0])
    f = exp_w1.shape[-1]
    wr = jnp.pad(jnp.concatenate([router_g[0], router_e[0]], axis=1), ((0, 0), (0, LANES - N_GROUPS - N_EXPERTS)))
    br = jnp.pad(jnp.concatenate([router_g_b[0], router_e_b[0]]), (0, LANES - N_GROUPS - N_EXPERTS))[None]
    half = jnp.ones((DA_HEAD_DIM, DA_HEAD_DIM), F32) / DA_HEAD_DIM
    wts = dict(
        norm1_g=norm1_g[0][None], norm2_g=norm2_g[0][None],
        w_qkv=win[:, 0:3 * d].astype(BF16), w_rw=w_rw, w_gate=win[:, 3 * d + rw_cols:].astype(BF16),
        qg=jnp.tile(da_qn_g[0], 2)[None], kg=jnp.tile(da_kn_g[0], 2)[None],
        gm=jnp.kron(jnp.eye(2, dtype=F32), half).astype(BF16),
        lam=lam_tile[0:1], subln_g=da_subln_g[0][None],
        mu=jnp.pad(rw_mu[0], (0, rw_pad - rw_cols))[None], lw=lw.astype(BF16),
        w0=rw_w0[0][None], a0=rw_a0[0][None], k_k=rw_k_k[0][None], k_a=rw_k_a[0][None],
        r_k=rw_r_k[0].reshape(1, d), ln_g=rw_ln_g[0][None], ln_b=rw_ln_b[0][None],
        w_oa=w_oa[0].astype(BF16), w_ob=w_ob[0].astype(BF16), w_out=w_out[0].astype(BF16),
        wr=wr, br=br,
        w13=jnp.concatenate([exp_w1[0], exp_w3[0]], axis=-1).reshape(N_EXPERTS, d, 2 * f).astype(BF16),
        w2=exp_w2[0].reshape(N_EXPERTS, f, d).astype(BF16),
    )

    out_p = _layer(x_prompt, mod_p, 0, None, wts)
    out_s = _layer(x_sample, mod_s, past_len,
                   (cache_k, cache_v, state_wkv[0], state_shift[0]), wts)
    return (out_p[0], out_s[0], out_p[1], out_p[2], out_p[3], out_p[4],
            out_s[1], out_s[2], out_s[3], out_s[4])
```

```python
import functools
import math

import jax
import jax.numpy as jnp
from jax import lax
from jax.experimental import pallas as pl
from jax.experimental.pallas import tpu as pltpu

F32 = jnp.float32
BF16 = jnp.bfloat16

EPS = 1e-6
NEG_INF = -1e30
CHUNK = 64
DA_HEADS = 8
DA_HEAD_DIM = 64
ROPE_THETA = 10000.0
RW_HEAD = 64
RW_GN_EPS = 64e-5
W_LORA, A_LORA, G_LORA = 64, 64, 160
N_GROUPS, EXPERTS_PER_GROUP = 4, 8
N_EXPERTS = N_GROUPS * EXPERTS_PER_GROUP
LAM_INIT = 0.8 - 0.6 * math.exp(-0.3 * 0)

LANES = 128
LORA_PAD = 384
VMEM_LIMIT = 56 * 1024 * 1024


def _cparams(*sem):
    return pltpu.CompilerParams(dimension_semantics=sem, vmem_limit_bytes=VMEM_LIMIT)


def _dot(a, b):
    return jnp.dot(a, b, preferred_element_type=F32)


def _dot_nt(a, b):
    return lax.dot_general(a, b, (((1,), (1,)), ((), ())), preferred_element_type=F32)


def _dot_tn(a, b):
    return lax.dot_general(a, b, (((0,), (0,)), ((), ())), preferred_element_type=F32)


def _split2(a):
    hi = a.astype(BF16)
    lo = (a - hi.astype(F32)).astype(BF16)
    return hi, lo


def _split3(a):
    hi = a.astype(BF16)
    r = a - hi.astype(F32)
    mid = r.astype(BF16)
    lo = (r - mid.astype(F32)).astype(BF16)
    return hi, mid, lo


GMW = 256


def _group_mean(sq, gm):
    hi, lo = _split2(sq)
    return _dot(hi, gm) + _dot(lo, gm)


def _adaln_kernel(c_ref, w_ref, b_ref, l_ref, o_ref, lam_ref):
    c = c_ref[...]
    sc = (c * jax.nn.sigmoid(c)).astype(BF16)
    o_ref[...] = _dot(sc, w_ref[...].astype(BF16)) + b_ref[...]
    l = l_ref[...]
    s1 = jnp.sum(l[0:1] * l[1:2], axis=-1, keepdims=True)
    s2 = jnp.sum(l[2:3] * l[3:4], axis=-1, keepdims=True)
    lam = jnp.exp(s1) - jnp.exp(s2) + LAM_INIT
    lam_ref[...] = jnp.broadcast_to(lam, lam_ref.shape)


def _adaln(c_all, w_ada, b_ada, da_lambda):
    bp, d = c_all.shape
    n = w_ada.shape[1]
    tn = 1024
    return pl.pallas_call(
        _adaln_kernel,
        grid=(n // tn,),
        in_specs=[pl.BlockSpec((bp, d), lambda j: (0, 0)),
                  pl.BlockSpec((d, tn), lambda j: (0, j)),
                  pl.BlockSpec((1, tn), lambda j: (0, j)),
                  pl.BlockSpec(da_lambda.shape, lambda j: (0, 0))],
        out_specs=[pl.BlockSpec((bp, tn), lambda j: (0, j)),
                   pl.BlockSpec((8, LANES), lambda j: (0, 0))],
        out_shape=[jax.ShapeDtypeStruct((bp, n), F32), jax.ShapeDtypeStruct((8, LANES), F32)],
        compiler_params=_cparams("arbitrary"),
        name="adaln",
    )(c_all, w_ada, b_ada.reshape(1, n), da_lambda)


def _prenorm_kernel(x_ref, sc_ref, sh_ref, g_ref, o_ref):
    x = x_ref[...]
    ms = jnp.mean(x * x, axis=-1, keepdims=True)
    h = x * lax.rsqrt(ms + EPS) * g_ref[...]
    o_ref[...] = (h * (1.0 + sc_ref[...]) + sh_ref[...]).astype(o_ref.dtype)


def _mod_spec(mod, tm, seq):
    d = mod.shape[-1]
    if mod.shape[1] == 1:
        per = seq // tm
        return pl.BlockSpec((None, 1, d), lambda i, *_: (i // per, 0, 0))
    return pl.BlockSpec((None, tm, d), lambda i, *_: (0, i, 0))


def _prenorm(x2d, sc, sh, g, tm, seq):
    t, d = x2d.shape
    return pl.pallas_call(
        _prenorm_kernel,
        grid=(t // tm,),
        in_specs=[pl.BlockSpec((tm, d), lambda i: (i, 0)), _mod_spec(sc, tm, seq), _mod_spec(sh, tm, seq),
                  pl.BlockSpec((1, d), lambda i: (0, 0))],
        out_specs=pl.BlockSpec((tm, d), lambda i: (i, 0)),
        out_shape=jax.ShapeDtypeStruct((t, d), BF16),
        compiler_params=_cparams("parallel"),
        name="prenorm",
    )(x2d, sc, sh, g)


def _qkv_kernel(h_ref, w_ref, qg_ref, kg_ref, cos_ref, sin_ref, gm_ref,
                q_ref, k_ref, kb_ref, v_ref, vb_ref):
    j = pl.program_id(1)
    acc = _dot(h_ref[...], w_ref[...])
    n_slab = acc.shape[1] // GMW

    def norm_rope(gain_ref, scale, write):
        lane = lax.broadcasted_iota(jnp.int32, (acc.shape[0], GMW), 1)
        first_half = (lane % DA_HEAD_DIM) < (DA_HEAD_DIM // 2)
        for c in range(n_slab):
            a = acc[:, c * GMW:(c + 1) * GMW]
            ms = _group_mean(a * a, gm_ref[...])
            y = a * lax.rsqrt(ms + EPS) * gain_ref[...]
            rot = jnp.where(first_half, pltpu.roll(y, GMW - DA_HEAD_DIM // 2, 1),
                            pltpu.roll(y, DA_HEAD_DIM // 2, 1))
            write(c, (y * cos_ref[...] + rot * sin_ref[...]) * scale)

    @pl.when(j == 0)
    def _():
        def write(c, val):
            q_ref[:, c * GMW:(c + 1) * GMW] = val.astype(q_ref.dtype)
        norm_rope(qg_ref, DA_HEAD_DIM ** -0.5 * math.log2(math.e), write)

    @pl.when(j == 1)
    def _():
        def write(c, val):
            k_ref[:, c * GMW:(c + 1) * GMW] = val
            kb_ref[:, c * GMW:(c + 1) * GMW] = val.astype(BF16)
        norm_rope(kg_ref, 1.0, write)

    @pl.when(j == 2)
    def _():
        v_ref[...] = acc
        vb_ref[...] = acc.astype(BF16)


def _qkv(h, w_qkv, qg, kg, cos_t, sin_t, gm, tm):
    t, d = h.shape
    n = 1024
    per = cos_t.shape[0] // tm
    row = lambda i, j: (i, 0)
    return pl.pallas_call(
        _qkv_kernel,
        grid=(t // tm, 3),
        in_specs=[pl.BlockSpec((tm, d), row),
                  pl.BlockSpec((d, n), lambda i, j: (0, j)),
                  pl.BlockSpec((1, GMW), lambda i, j: (0, 0)),
                  pl.BlockSpec((1, GMW), lambda i, j: (0, 0)),
                  pl.BlockSpec((tm, GMW), lambda i, j: (i % per, 0)),
                  pl.BlockSpec((tm, GMW), lambda i, j: (i % per, 0)),
                  pl.BlockSpec((GMW, GMW), lambda i, j: (0, 0))],
        out_specs=[pl.BlockSpec((tm, n), row)] * 5,
        out_shape=[jax.ShapeDtypeStruct((t, n), BF16), jax.ShapeDtypeStruct((t, n), F32),
                   jax.ShapeDtypeStruct((t, n), BF16), jax.ShapeDtypeStruct((t, n), F32),
                   jax.ShapeDtypeStruct((t, n), BF16)],
        compiler_params=_cparams("parallel", "arbitrary"),
        name="qkv_proj",
    )(h, w_qkv, qg, kg, cos_t, sin_t, gm)


def _proj_kernel(h_ref, w_ref, o_ref, *, act):
    acc = _dot(h_ref[...], w_ref[...])
    if act == "sigmoid":
        acc = 0.5 * jnp.tanh(0.5 * acc) + 0.5
    o_ref[...] = acc.astype(o_ref.dtype)


def _proj(h, w, tm, tn, act, out_dtype, name):
    t, d = h.shape
    n = w.shape[1]
    return pl.pallas_call(
        functools.partial(_proj_kernel, act=act),
        grid=(t // tm, n // tn),
        in_specs=[pl.BlockSpec((tm, d), lambda i, j: (i, 0)), pl.BlockSpec((d, tn), lambda i, j: (0, j))],
        out_specs=pl.BlockSpec((tm, tn), lambda i, j: (i, j)),
        out_shape=jax.ShapeDtypeStruct((t, n), out_dtype),
        compiler_params=_cparams("parallel", "arbitrary"),
        name=name,
    )(h, w)


def _subln(o, g):
    ms = jnp.mean(o * o, axis=-1, keepdims=True)
    return o * lax.rsqrt(ms + EPS) * g * (1.0 - LAM_INIT)


ATT_ROWS = 128
ATT_HEADS = 2


def _attn_prompt_kernel(qi_tab, ki_tab, lam_ref, q_ref, k_ref, v_ref, g_ref, o_ref,
                        qs_scr, vx_scr, m_scr, acc_scr, *, tq, tk):
    p = pl.program_id(2)
    qi = qi_tab[p]
    ki = ki_tab[p]
    ratio = tk // tq
    n_slab = tk // LANES

    heads = [slice(hh * LANES, (hh + 1) * LANES) for hh in range(ATT_HEADS)]

    @pl.when(ki == 0)
    def _():
        for hh, hs in enumerate(heads):
            q = q_ref[:, hs]
            lane = lax.broadcasted_iota(jnp.int32, q.shape, 1)
            zero = jnp.zeros_like(q)
            qs_scr[hh, 0:tq, :] = jnp.where(lane < DA_HEAD_DIM, q, zero)
            qs_scr[hh, tq:2 * tq, :] = jnp.where(lane >= DA_HEAD_DIM, q, zero)
            vx_scr[hh, :, LANES:2 * LANES] = jnp.ones((tk, LANES), BF16)
        m_scr[...] = jnp.full(m_scr.shape, NEG_INF, F32)
        acc_scr[...] = jnp.zeros(acc_scr.shape, F32)

    for hh, hs in enumerate(heads):
        vx_scr[hh, :, 0:LANES] = v_ref[:, hs]

    def step(masked):
        for rb in range(2 * tq // ATT_ROWS):
            rows = slice(rb * ATT_ROWS, (rb + 1) * ATT_ROWS)
            for hh, hs in enumerate(heads):
                s = _dot_nt(qs_scr[hh, rows, :], k_ref[:, hs])
                slabs = [s[:, c * LANES:(c + 1) * LANES] for c in range(n_slab)]
                if masked:
                    row = lax.broadcasted_iota(jnp.int32, (ATT_ROWS, LANES), 0) + rb * ATT_ROWS
                    lane = lax.broadcasted_iota(jnp.int32, (ATT_ROWS, LANES), 1)
                    qchunk = (row % tq) // CHUNK + (qi % ratio) * (tq // CHUNK)
                    slabs = [jnp.where((c * LANES + lane) // CHUNK <= qchunk, sl, NEG_INF)
                             for c, sl in enumerate(slabs)]
                mt = slabs[0]
                for sl in slabs[1:]:
                    mt = jnp.maximum(mt, sl)
                m_prev = m_scr[hh, rows, :]
                m_new = jnp.maximum(m_prev, jnp.max(mt, axis=-1, keepdims=True))
                alpha = jnp.exp2(m_prev - m_new)
                e = jnp.concatenate([jnp.exp2(sl - m_new).astype(BF16) for sl in slabs], axis=1)
                pv = _dot(e, vx_scr[hh])
                acc_scr[hh, rows, :] = jnp.concatenate([alpha, alpha], axis=1) * acc_scr[hh, rows, :] + pv
                m_scr[hh, rows, :] = m_new

    @pl.when(ki < qi // ratio)
    def _():
        step(False)

    @pl.when(ki == qi // ratio)
    def _():
        step(True)
        for hh, hs in enumerate(heads):
            o = acc_scr[hh, :, 0:LANES] / acc_scr[hh, :, LANES:2 * LANES]
            o = o[0:tq] - lam_ref[...] * o[tq:2 * tq]
            o_ref[:, hs] = _subln(o, g_ref[...]).astype(o_ref.dtype)


def _attn_prompt(q, k, v, lam_row, subln_g, batch, seq, tq, tk):
    d = q.shape[1]
    assert tk % tq == 0 and seq % tk == 0 and tq % CHUNK == 0
    qi_l, ki_l = [], []
    for a in range(seq // tq):
        for b in range(a * tq // tk + 1):
            qi_l.append(a)
            ki_l.append(b)
    qi_tab = jnp.asarray(qi_l, jnp.int32)
    ki_tab = jnp.asarray(ki_l, jnp.int32)
    q3, k3, v3 = (a.reshape(batch, seq, d) for a in (q, k, v))
    hw = ATT_HEADS * LANES
    qspec = pl.BlockSpec((None, tq, hw), lambda b, h, p, qt, kt: (b, qt[p], h))
    kspec = pl.BlockSpec((None, tk, hw), lambda b, h, p, qt, kt: (b, kt[p], h))
    vec = pl.BlockSpec((1, LANES), lambda b, h, p, qt, kt: (0, 0))
    out = pl.pallas_call(
        functools.partial(_attn_prompt_kernel, tq=tq, tk=tk),
        grid_spec=pltpu.PrefetchScalarGridSpec(
            num_scalar_prefetch=2,
            grid=(batch, DA_HEADS // ATT_HEADS, len(qi_l)),
            in_specs=[vec, qspec, kspec, kspec, vec],
            out_specs=qspec,
            scratch_shapes=[pltpu.VMEM((ATT_HEADS, 2 * tq, LANES), BF16),
                            pltpu.VMEM((ATT_HEADS, tk, 2 * LANES), BF16),
                            pltpu.VMEM((ATT_HEADS, 2 * tq, LANES), F32),
                            pltpu.VMEM((ATT_HEADS, 2 * tq, 2 * LANES), F32)]),
        out_shape=jax.ShapeDtypeStruct((batch, seq, d), BF16),
        compiler_params=_cparams("parallel", "parallel", "arbitrary"),
        name="attn_prompt",
    )(qi_tab, ki_tab, lam_row, q3, k3, v3, subln_g)
    return out.reshape(batch * seq, d)


def _attn_sample_kernel(lam_ref, q_ref, kn_ref, vn_ref, ck_ref, cv_ref, g_ref, o_ref, *, past, sq):
    for h in range(DA_HEADS):
        hs = slice(h * LANES, (h + 1) * LANES)
        q = q_ref[:, hs]
        lane = lax.broadcasted_iota(jnp.int32, q.shape, 1)
        zero = jnp.zeros_like(q)
        qs = jnp.concatenate([jnp.where(lane < DA_HEAD_DIM, q, zero),
                              jnp.where(lane >= DA_HEAD_DIM, q, zero)], axis=0)
        kc = ck_ref[pl.ds(h, past, stride=DA_HEADS), :].astype(BF16)
        vc = cv_ref[pl.ds(h, past, stride=DA_HEADS), :].astype(BF16)
        s_c = _dot_nt(qs, kc)
        s_n = _dot_nt(qs, kn_ref[:, hs])

        def masked(s, key0):
            row = lax.broadcasted_iota(jnp.int32, s.shape, 0)
            col = lax.broadcasted_iota(jnp.int32, s.shape, 1)
            qchunk = (past + row % sq) // CHUNK
            return jnp.where((key0 + col) // CHUNK <= qchunk, s, NEG_INF)

        s_c = masked(s_c, 0)
        s_n = masked(s_n, past)
        m = jnp.maximum(jnp.max(s_c, axis=-1, keepdims=True), jnp.max(s_n, axis=-1, keepdims=True))
        e_c = jnp.exp2(s_c - m)
        e_n = jnp.exp2(s_n - m)
        l = jnp.sum(e_c, axis=-1, keepdims=True) + jnp.sum(e_n, axis=-1, keepdims=True)
        o = (_dot(e_c.astype(BF16), vc) + _dot(e_n.astype(BF16), vn_ref[:, hs])) / l
        o = o[0:sq] - lam_ref[...] * o[sq:2 * sq]
        o_ref[:, hs] = _subln(o, g_ref[...]).astype(o_ref.dtype)


def _attn_sample(q, kn, vn, cache_k, cache_v, lam_row, subln_g, batch, sq):
    d = q.shape[1]
    past = cache_k.shape[-3]
    ck = cache_k.reshape(batch, past * DA_HEADS, LANES)
    cv = cache_v.reshape(batch, past * DA_HEADS, LANES)
    q3, k3, v3 = (a.reshape(batch, sq, d) for a in (q, kn, vn))
    new = pl.BlockSpec((None, sq, d), lambda b: (b, 0, 0))
    old = pl.BlockSpec((None, past * DA_HEADS, LANES), lambda b: (b, 0, 0))
    vec = pl.BlockSpec((1, LANES), lambda b: (0, 0))
    out = pl.pallas_call(
        functools.partial(_attn_sample_kernel, past=past, sq=sq),
        grid=(batch,),
        in_specs=[vec, new, new, new, old, old, vec],
        out_specs=new,
        out_shape=jax.ShapeDtypeStruct((batch, sq, d), BF16),
        compiler_params=_cparams("parallel"),
        name="attn_sample",
    )(lam_row, q3, k3, v3, ck, cv, subln_g)
    return out.reshape(batch * sq, d)


RW_L = 64
RW_G = 2
RW_GW = RW_G * RW_HEAD
RW_GROUPS = 16 // RW_G
RW_CPS = 2


def _mm3(a, b, dot=_dot):
    ah, al = _split2(a)
    bh, bl = _split2(b)
    return dot(ah, bh) + dot(ah, bl) + dot(al, bh)


def _terms(x, n):
    return (x.astype(BF16),) if n == 1 else _split2(x)


def _mmt(at, bt, dot=_dot):
    acc = dot(at[0], bt[0])
    if len(bt) > 1:
        acc = acc + dot(at[0], bt[1])
    if len(at) > 1:
        acc = acc + dot(at[1], bt[0])
    return acc


RW_PREC = dict(a_all=(1, 1), neumann=(1, 1), av=(1, 1), pu=(1, 1), qy=(1, 1), mh=(1, 1), state=(1, 1))


def _softplus(z):
    return jnp.maximum(z, 0.0) + jnp.log(1.0 + jnp.exp(-jnp.abs(z)))


def _rwkv_kernel(rw_ref, sh0_ref, mu_ref, lw_ref, w0_ref, a0_ref, kk_ref, ka_ref, rk_ref, lng_ref, lnb_ref,
                 h0_ref, gm_ref, tri_ref, ob_ref, hout_ref, h_scr, last_scr, *, n_valid, cps):
    L = RW_L
    rows = cps * L
    c = pl.program_id(1)
    d = ob_ref.shape[-1]

    @pl.when(c == 0)
    def _():
        h_scr[...] = h0_ref[...]
        last_scr[...] = sh0_ref[...]

    rw = rw_ref[...]
    row_w = lax.broadcasted_iota(jnp.int32, rw.shape, 0)
    prev = jnp.where(row_w == 0, last_scr[...], pltpu.roll(rw, 1, 0))
    last_scr[...] = rw[rows - 1:rows, :]
    xm = rw + (prev - rw) * mu_ref[...]
    r = xm[:, 0:d]
    k = xm[:, d:2 * d]
    v = xm[:, 2 * d:3 * d]
    lo = xm[:, 3 * d:3 * d + LORA_PAD]
    lane_l = lax.broadcasted_iota(jnp.int32, lo.shape, 1)
    z = jnp.where(lane_l < W_LORA, jnp.tanh(lo),
                  jnp.where(lane_l < W_LORA + A_LORA, lo, jax.nn.sigmoid(lo)))
    lora = _dot(z.astype(BF16), lw_ref[...])
    w = -_softplus(-(w0_ref[...] + lora[:, 0:d])) - 0.5
    ld = -jnp.exp(w)
    a = jax.nn.sigmoid(a0_ref[...] + lora[:, d:2 * d])
    g = lora[:, 2 * d:3 * d]
    kkr = k * kk_ref[...]
    kmod = k * (1.0 + (a - 1.0) * ka_ref[...])
    rkk = r * kmod * rk_ref[...]
    gm = gm_ref[...]
    if n_valid < rows:
        valid = lax.broadcasted_iota(jnp.int32, ld.shape, 0) < n_valid
        ld = jnp.where(valid, ld, 0.0)
        kkr = jnp.where(valid, kkr, 0.0)
        kmod = jnp.where(valid, kmod, 0.0)

    tri = tri_ref[...]
    l1, l2, l3 = _split3(ld)
    cin = _dot(tri, l1) + _dot(tri, l2) + _dot(tri, l3)
    cex = cin - ld
    c_last = [cin[(ch + 1) * L - 1:(ch + 1) * L, :] for ch in range(cps)]
    e_in = jnp.exp(cin)
    e_ex = jnp.exp(cex)
    e_neg = jnp.exp(-cin)
    e_tail = jnp.exp(jnp.concatenate([jnp.broadcast_to(cl, (L, d)) for cl in c_last], axis=0) - cin)
    g_last = [jnp.exp(cl) for cl in c_last]

    GW = RW_GW
    row = lax.broadcasted_iota(jnp.int32, (L, GW), 0)
    lane = lax.broadcasted_iota(jnp.int32, (L, GW), 1)
    head_of_lane = lane // RW_HEAD
    tri_s = row > (lane % L)
    tri_i = row >= (lane % L)
    eye_pair = jnp.where(row == (lane % L), 1.0, 0.0).astype(F32)
    r2 = lax.broadcasted_iota(jnp.int32, (GW, GW), 0)
    c2 = lax.broadcasted_iota(jnp.int32, (GW, GW), 1)
    bd_mask = (r2 // RW_HEAD) == (c2 // RW_HEAD)
    diag_mask = r2 == c2

    def bd(x):
        z0 = jnp.zeros_like(x)
        return jnp.concatenate([jnp.where(head_of_lane == j, x, z0) for j in range(RW_G)], axis=0)

    gmw = min(GW, GMW)
    gm_g = gm[0:gmw, 0:gmw]

    def gmean(x):
        return jnp.concatenate([_group_mean(x[:, j * gmw:(j + 1) * gmw], gm_g) for j in range(GW // gmw)], axis=1)

    def bdt(t):
        return tuple(bd(x) for x in t)

    def cat(ts, axis):
        return tuple(jnp.concatenate(xs, axis=axis) for xs in zip(*ts))

    prec = RW_PREC
    GL = RW_G * L
    items = [(slice(ch * L, (ch + 1) * L), slice(p * GW, (p + 1) * GW))
             for ch in range(cps) for p in range(RW_GROUPS)]
    pairs = range(len(items))

    v_p = [v[rs, sl] for rs, sl in items]
    kk_p = []
    for rs, sl in items:
        kk_raw = kkr[rs, sl]
        ss = gmean(kk_raw * kk_raw) * RW_HEAD
        kk_p.append(kk_raw / jnp.maximum(jnp.sqrt(ss), 1e-12))
    b_p = [kk_p[i] * a[rs, sl] for i, (rs, sl) in enumerate(items)]
    abar = [-kk_p[i] * e_ex[rs, sl] for i, (rs, sl) in enumerate(items)]
    rbar = [r[rs, sl] * e_in[rs, sl] for rs, sl in items]
    bbar = [b_p[i] * e_neg[rs, sl] for i, (rs, sl) in enumerate(items)]
    kbar = [kmod[rs, sl] * e_neg[rs, sl] for rs, sl in items]
    btil = [b_p[i] * e_tail[rs, sl] for i, (rs, sl) in enumerate(items)]
    ktil = [kmod[rs, sl] * e_tail[rs, sl] for rs, sl in items]

    na, nb = prec["a_all"]
    abar_t = [_terms(abar[p], max(na, prec["pu"][1])) for p in pairs]
    v_t = [_terms(v_p[p], max(prec["av"][1], prec["qy"][1], prec["mh"][1])) for p in pairs]
    a_all = []
    for p in pairs:
        lhs = cat([abar_t[p][:na], _terms(rbar[p], na)], 0)
        rhs = cat([bdt(_terms(bbar[p], nb)), bdt(_terms(kbar[p], nb))], 0)
        a_all.append(_mmt(lhs, rhs, _dot_nt))
    n_ab = [jnp.where(tri_s, a_all[p][0:L, 0:GL], 0.0) for p in pairs]
    a_ak = [jnp.where(tri_s, a_all[p][0:L, GL:2 * GL], 0.0) for p in pairs]
    a_rb = [jnp.where(tri_i, a_all[p][L:2 * L, 0:GL], 0.0) for p in pairs]
    a_rk = [jnp.where(tri_i, a_all[p][L:2 * L, GL:2 * GL], 0.0) for p in pairs]

    na, nb = prec["neumann"]
    t_inv = [eye_pair + n_ab[p] for p in pairs]
    pw_t = [_terms(n_ab[p], max(na, nb)) for p in pairs]
    pw = [_mmt(pw_t[p][:na], bdt(pw_t[p][:nb])) for p in pairs]
    for _ in range(int(math.log2(L)) - 2):
        pw_t = [_terms(pw[p], max(na, nb)) for p in pairs]
        st = [_mmt(cat([_terms(t_inv[p], na), pw_t[p][:na]], 0), bdt(pw_t[p][:nb])) for p in pairs]
        t_inv = [t_inv[p] + st[p][0:L] for p in pairs]
        pw = [st[p][L:2 * L] for p in pairs]
    t_inv = [t_inv[p] + _mmt(_terms(t_inv[p], na), bdt(_terms(pw[p], nb))) for p in pairs]

    na, nb = prec["av"]
    vv = [_mmt(cat([_terms(a_ak[p], na), _terms(a_rk[p], na)], 0), bdt(v_t[p][:nb])) for p in pairs]
    av = [x[0:L] for x in vv]
    arkv = [x[L:2 * L] for x in vv]
    na, nb = prec["pu"]
    pu = [_mmt(_terms(t_inv[p], na), cat([bdt(abar_t[p][:nb]), bdt(_terms(av[p], nb))], 1)) for p in pairs]
    p_m = [x[:, 0:GW] for x in pu]
    u0 = [x[:, GW:2 * GW] for x in pu]
    na, nb = prec["qy"]
    nmh = prec["mh"][1]
    p_t = [_terms(p_m[p], max(nb, nmh)) for p in pairs]
    u_t = [_terms(u0[p], max(nb, nmh)) for p in pairs]
    qy = [_mmt(_terms(a_rb[p], na), cat([bdt(p_t[p][:nb]), bdt(u_t[p][:nb])], 1)) for p in pairs]
    q_m = [rbar[p] + qy[p][:, 0:GW] for p in pairs]
    y0 = [qy[p][:, GW:2 * GW] + arkv[p] for p in pairs]

    na, nb = prec["mh"]
    btil_t = [_terms(btil[p], na) for p in pairs]
    m_full = [_mmt(btil_t[p], p_t[p][:nb], _dot_tn) for p in pairs]
    h0_full = [_mmt(cat([btil_t[p], _terms(ktil[p], na)], 0), cat([u_t[p][:nb], v_t[p][:nb]], 0), _dot_tn)
               for p in pairs]
    m_bd = [jnp.where(bd_mask, m_full[i], 0.0) + jnp.where(diag_mask, g_last[i // RW_GROUPS][:, sl], 0.0)
            for i, (rs, sl) in enumerate(items)]
    h0_bd = [jnp.where(bd_mask, h0_full[p], 0.0) for p in pairs]

    na, nb = prec["state"]
    h_cur = [h_scr[p] for p in range(RW_GROUPS)]
    y = []
    for ch in range(cps):
        idx = [ch * RW_GROUPS + p for p in range(RW_GROUPS)]
        st = [_mmt(cat([_terms(q_m[i], na), _terms(m_bd[i], na)], 0), _terms(h_cur[p], nb))
              for p, i in enumerate(idx)]
        y.extend(st[p][0:L] + y0[i] for p, i in enumerate(idx))
        h_cur = [st[p][L:L + GW] + h0_bd[i] for p, i in enumerate(idx)]
    for p in range(RW_GROUPS):
        h_scr[p] = h_cur[p]

    for i, (rs, sl) in enumerate(items):
        mu_y = gmean(y[i])
        dy = y[i] - mu_y
        var = gmean(dy * dy)
        yn = dy * lax.rsqrt(var + RW_GN_EPS) * lng_ref[:, sl] + lnb_ref[:, sl]
        bonus = gmean(rkk[rs, sl]) * RW_HEAD * v_p[i]
        ob_ref[rs, sl] = ((yn + bonus) * g[rs, sl]).astype(ob_ref.dtype)

    @pl.when(c == pl.num_programs(1) - 1)
    def _():
        hout_ref[...] = h_scr[...]


def _rwkv(rw3, shift0, h0_bd, prm, gm, n_valid, cps):
    batch, seq, wcols = rw3.shape
    d = prm["w0"].shape[1]
    rows = cps * RW_L
    nc = seq // rows
    tri = jnp.kron(jnp.eye(cps, dtype=F32), jnp.tril(jnp.ones((RW_L, RW_L), F32))).astype(BF16)
    const2 = lambda b, c: (0, 0)
    vec = pl.BlockSpec((1, d), const2)
    st = pl.BlockSpec((None, RW_GROUPS, RW_GW, RW_GW), lambda b, c: (b, 0, 0, 0))
    return pl.pallas_call(
        functools.partial(_rwkv_kernel, n_valid=n_valid, cps=cps),
        grid=(batch, nc),
        in_specs=[pl.BlockSpec((None, rows, wcols), lambda b, c: (b, c, 0)),
                  pl.BlockSpec((None, 1, wcols), lambda b, c: (b, 0, 0)),
                  pl.BlockSpec((1, wcols), const2),
                  pl.BlockSpec(prm["lw"].shape, const2),
                  vec, vec, vec, vec, vec, vec, vec, st,
                  pl.BlockSpec((GMW, GMW), const2),
                  pl.BlockSpec((rows, rows), const2)],
        out_specs=[pl.BlockSpec((None, rows, d), lambda b, c: (b, c, 0)), st],
        out_shape=[jax.ShapeDtypeStruct((batch, seq, d), BF16),
                   jax.ShapeDtypeStruct((batch, RW_GROUPS, RW_GW, RW_GW), F32)],
        scratch_shapes=[pltpu.VMEM((RW_GROUPS, RW_GW, RW_GW), F32), pltpu.VMEM((1, wcols), F32)],
        compiler_params=_cparams("parallel", "arbitrary"),
        name="rwkv7",
    )(rw3, shift0, prm["mu"], prm["lw"], prm["w0"], prm["a0"], prm["k_k"], prm["k_a"], prm["r_k"],
      prm["ln_g"], prm["ln_b"], h0_bd, gm, tri)


def _state_to_bd(s):
    b = s.shape[0]
    ht = jnp.swapaxes(s, -1, -2).reshape(b, RW_GROUPS, RW_G, RW_HEAD, RW_HEAD)
    z = jnp.zeros_like(ht[:, :, 0])
    rows = [jnp.concatenate([ht[:, :, j] if i == j else z for i in range(RW_G)], axis=-1) for j in range(RW_G)]
    return jnp.concatenate(rows, axis=-2)


def _bd_to_state(hbd):
    b = hbd.shape[0]
    blocks = [hbd[:, :, j * RW_HEAD:(j + 1) * RW_HEAD, j * RW_HEAD:(j + 1) * RW_HEAD] for j in range(RW_G)]
    ht = jnp.stack(blocks, axis=2).reshape(b, RW_G * RW_GROUPS, RW_HEAD, RW_HEAD)
    return jnp.swapaxes(ht, -1, -2)


def _route(logits):
    lane = lax.broadcasted_iota(jnp.int32, logits.shape, 1).astype(F32)
    big = float(LANES)
    lg = jnp.where(lane < N_GROUPS, logits, NEG_INF)
    mg = jnp.max(lg, axis=-1, keepdims=True)
    sg = jnp.sum(jnp.exp(lg - mg), axis=-1, keepdims=True)
    p_top = 1.0 / sg
    g_idx = jnp.min(jnp.where(lg == mg, lane, big), axis=-1, keepdims=True)
    e0 = N_GROUPS + EXPERTS_PER_GROUP * g_idx
    emask = jnp.where(lane >= e0, jnp.where(lane < e0 + EXPERTS_PER_GROUP, 1.0, 0.0), 0.0) > 0.5
    le = jnp.where(emask, logits, NEG_INF)
    me = jnp.max(le, axis=-1, keepdims=True)
    ee = jnp.exp(le - me)
    pe = ee / jnp.sum(ee, axis=-1, keepdims=True)
    pe = jnp.where(emask, pe, -1.0)
    v1 = jnp.max(pe, axis=-1, keepdims=True)
    i1 = jnp.min(jnp.where(pe == v1, lane, big), axis=-1, keepdims=True)
    pe2 = jnp.where(lane == i1, -1.0, pe)
    v2 = jnp.max(pe2, axis=-1, keepdims=True)
    i2 = jnp.min(jnp.where(pe2 == v2, lane, big), axis=-1, keepdims=True)
    den = v1 + v2
    ew = jnp.where(lane == i1, v1 / den, 0.0) + jnp.where(lane == i2, v2 / den, 0.0)
    return p_top * ew + jnp.where(lane == 0.0, g_idx, 0.0)


def _merge_kernel(oa_ref, ob_ref, gates_ref, x_ref, gt1_ref, sc2_ref, sh2_ref, g2_ref,
                  woa_ref, wob_ref, wout_ref, wr_ref, br_ref, x1_ref, hx_ref):
    d = x_ref.shape[1]
    a_out = _dot(oa_ref[...], woa_ref[...])
    b_out = _dot(ob_ref[...], wob_ref[...])
    merged = gates_ref[:, 0:d].astype(F32) * a_out + gates_ref[:, d:2 * d].astype(F32) * b_out
    x1 = x_ref[...] + gt1_ref[...] * _dot(merged.astype(BF16), wout_ref[...])
    x1_ref[...] = x1
    ms = jnp.mean(x1 * x1, axis=-1, keepdims=True)
    h2 = x1 * lax.rsqrt(ms + EPS) * g2_ref[...]
    h2 = h2 * (1.0 + sc2_ref[...]) + sh2_ref[...]
    logits = _mm3(h2, wr_ref[...]) + br_ref[...]
    hx_ref[:, 0:d] = h2
    hx_ref[:, d:d + LANES] = _route(logits)


def _merge(oa, ob, gates, x2d, gt1, sc2, sh2, g2, w_oa, w_ob, w_out, wr, br, tm, seq):
    t, d = x2d.shape
    row = lambda i: (i, 0)
    const = lambda i: (0, 0)
    full = lambda w: pl.BlockSpec(w.shape, const)
    return pl.pallas_call(
        _merge_kernel,
        grid=(t // tm,),
        in_specs=[pl.BlockSpec((tm, d), row), pl.BlockSpec((tm, d), row), pl.BlockSpec((tm, 2 * d), row),
                  pl.BlockSpec((tm, d), row), _mod_spec(gt1, tm, seq), _mod_spec(sc2, tm, seq),
                  _mod_spec(sh2, tm, seq), pl.BlockSpec((1, d), const),
                  full(w_oa), full(w_ob), full(w_out), full(wr), full(br)],
        out_specs=[pl.BlockSpec((tm, d), row), pl.BlockSpec((tm, d + LANES), row)],
        out_shape=[jax.ShapeDtypeStruct((t, d), F32), jax.ShapeDtypeStruct((t, d + LANES), F32)],
        compiler_params=_cparams("parallel"),
        name="merge_router",
    )(oa, ob, gates, x2d, gt1, sc2, sh2, g2, w_oa, w_ob, w_out, wr, br)


def _expert_ffn(xb, comb, lane_e, w13, w2):
    au = _dot(xb, w13)
    f = au.shape[1] // 2
    a = au[:, 0:f]
    u = au[:, f:2 * f]
    lane = lax.broadcasted_iota(jnp.int32, comb.shape, 1)
    cw = jnp.sum(jnp.where(lane == lane_e, comb, 0.0), axis=-1, keepdims=True)
    act = a * jax.nn.sigmoid(a) * u * cw
    return _dot(act.astype(BF16), w2)


def _moe_dense_kernel(hx_ref, w13_ref, w2_ref, x1_ref, gt2_ref, y_ref, xb_scr, acc_scr):
    e = pl.program_id(1)
    d = y_ref.shape[1]

    @pl.when(e == 0)
    def _():
        xb_scr[...] = hx_ref[:, 0:d].astype(BF16)
        acc_scr[...] = jnp.zeros(acc_scr.shape, F32)

    acc_scr[...] += _expert_ffn(xb_scr[...], hx_ref[:, d:d + LANES], e + N_GROUPS, w13_ref[...], w2_ref[...])

    @pl.when(e == pl.num_programs(1) - 1)
    def _():
        y_ref[...] = x1_ref[...] + gt2_ref[...] * acc_scr[...]


def _moe_dense(hx, w13, w2, x1, gt2, tm, seq):
    t, d = x1.shape
    ne, _, f2 = w13.shape
    row = lambda i, e: (i, 0)
    return pl.pallas_call(
        _moe_dense_kernel,
        grid=(t // tm, ne),
        in_specs=[pl.BlockSpec((tm, d + LANES), row),
                  pl.BlockSpec((None, d, f2), lambda i, e: (e, 0, 0)),
                  pl.BlockSpec((None, f2 // 2, d), lambda i, e: (e, 0, 0)),
                  pl.BlockSpec((tm, d), row), _mod_spec(gt2, tm, seq)],
        out_specs=pl.BlockSpec((tm, d), row),
        out_shape=jax.ShapeDtypeStruct((t, d), F32),
        scratch_shapes=[pltpu.VMEM((tm, d), BF16), pltpu.VMEM((tm, d), F32)],
        compiler_params=_cparams("parallel", "arbitrary"),
        name="moe_dense",
    )(hx, w13, w2, x1, gt2)


MOE_TM = 1024
MOE_TS = 1024


def _plan_kernel(rt_ref, tri_ref, rank_ref, cnt_ref, carry_scr):
    @pl.when(pl.program_id(0) == 0)
    def _():
        carry_scr[...] = jnp.zeros(carry_scr.shape, F32)

    rt = rt_ref[...]
    lane = lax.broadcasted_iota(jnp.int32, rt.shape, 1).astype(F32)
    onehot = jnp.where(lane == rt[:, 0:1], 1.0, 0.0)
    before = _dot(tri_ref[...], onehot.astype(BF16)) + carry_scr[...]
    rank_ref[...] = jnp.sum(onehot * before, axis=-1, keepdims=True).astype(jnp.int32)
    carry_scr[...] += jnp.sum(onehot, axis=0, keepdims=True)
    cnt_ref[...] = carry_scr[...]


def _moe_plan(hx, d):
    t = hx.shape[0]
    tp = 512
    tri = jnp.tril(jnp.ones((tp, tp), F32), -1).astype(BF16)
    return pl.pallas_call(
        _plan_kernel,
        grid=(t // tp,),
        in_specs=[pl.BlockSpec((tp, LANES), lambda i: (i, d // LANES)),
                  pl.BlockSpec((tp, tp), lambda i: (0, 0))],
        out_specs=[pl.BlockSpec((tp, 1), lambda i: (i, 0)), pl.BlockSpec((1, LANES), lambda i: (0, 0))],
        out_shape=[jax.ShapeDtypeStruct((t, 1), jnp.int32), jax.ShapeDtypeStruct((1, LANES), F32)],
        scratch_shapes=[pltpu.VMEM((1, LANES), F32)],
        compiler_params=_cparams("arbitrary"),
        name="moe_plan",
    )(hx, tri)


def _row_copy(src_hbm, src_row, dst, dst_row, sem):
    return pltpu.make_async_copy(src_hbm.at[pl.ds(src_row, 1), :], dst.at[pl.ds(dst_row, 1), :], sem)


def _dispatch_kernel(pos_ref, hx_ref, xs_in_hbm, xs_hbm, sem):
    del xs_in_hbm

    def issue(t, carry):
        _row_copy(hx_ref, t, xs_hbm, pos_ref[t], sem).start()
        return carry

    def drain(t, carry):
        _row_copy(hx_ref, t, xs_hbm, pos_ref[t], sem).wait()
        return carry

    lax.fori_loop(0, MOE_TS, issue, 0, unroll=8)
    lax.fori_loop(0, MOE_TS, drain, 0, unroll=8)


def _moe_dispatch(pos, hx, n_rows):
    t, w = hx.shape
    xs0 = jnp.zeros((n_rows, w), F32)
    return pl.pallas_call(
        _dispatch_kernel,
        grid=(t // MOE_TS,),
        in_specs=[pl.BlockSpec((MOE_TS,), lambda i: (i,), memory_space=pltpu.SMEM),
                  pl.BlockSpec((MOE_TS, w), lambda i: (i, 0)), pl.BlockSpec(memory_space=pl.ANY)],
        out_specs=pl.BlockSpec(memory_space=pl.ANY),
        out_shape=jax.ShapeDtypeStruct((n_rows, w), F32),
        scratch_shapes=[pltpu.SemaphoreType.DMA(())],
        input_output_aliases={2: 0},
        compiler_params=_cparams("arbitrary"),
        name="moe_dispatch",
    )(pos, hx, xs0)


def _moe_bucket_kernel(tg_ref, tv_ref, xs_ref, w13_ref, w2_ref, ys_ref, xb_scr):
    i = pl.program_id(0)
    e = pl.program_id(1)
    d = ys_ref.shape[1]

    @pl.when(e == 0)
    def _():
        xb_scr[...] = xs_ref[:, 0:d].astype(BF16)
        ys_ref[...] = jnp.zeros(ys_ref.shape, F32)

    @pl.when(tv_ref[i] > 0)
    def _():
        lane_e = N_GROUPS + tg_ref[i] * EXPERTS_PER_GROUP + e
        ys_ref[...] += _expert_ffn(xb_scr[...], xs_ref[:, d:d + LANES], lane_e, w13_ref[...], w2_ref[...])


def _moe_buckets(tile_group, tile_valid, xs, w13, w2, d):
    n_rows, w = xs.shape
    _, _, f2 = w13.shape
    row = lambda i, e, tg, tv: (i, 0)
    wsel = lambda i, e, tg, tv: (tg[i] * EXPERTS_PER_GROUP + e, 0, 0)
    return pl.pallas_call(
        _moe_bucket_kernel,
        grid_spec=pltpu.PrefetchScalarGridSpec(
            num_scalar_prefetch=2,
            grid=(n_rows // MOE_TM, EXPERTS_PER_GROUP),
            in_specs=[pl.BlockSpec((MOE_TM, w), row),
                      pl.BlockSpec((None, d, f2), wsel), pl.BlockSpec((None, f2 // 2, d), wsel)],
            out_specs=pl.BlockSpec((MOE_TM, d), row),
            scratch_shapes=[pltpu.VMEM((MOE_TM, d), BF16)]),
        out_shape=jax.ShapeDtypeStruct((n_rows, d), F32),
        compiler_params=_cparams("parallel", "arbitrary"),
        name="moe_buckets",
    )(tile_group, tile_valid, xs, w13, w2)


def _collect_kernel(pos_ref, ys_hbm, x1_ref, gt2_ref, y_ref, buf, sem):
    tu = buf.shape[0]

    def issue(t, carry):
        _row_copy(ys_hbm, pos_ref[t], buf, t, sem).start()
        return carry

    def drain(t, carry):
        _row_copy(ys_hbm, pos_ref[t], buf, t, sem).wait()
        return carry

    lax.fori_loop(0, tu, issue, 0, unroll=8)
    lax.fori_loop(0, tu, drain, 0, unroll=8)
    y_ref[...] = x1_ref[...] + gt2_ref[...] * buf[...]


def _moe_collect(pos, ys, x1, gt2, seq):
    t, d = x1.shape
    tu = MOE_TS
    return pl.pallas_call(
        _collect_kernel,
        grid=(t // tu,),
        in_specs=[pl.BlockSpec((tu,), lambda i: (i,), memory_space=pltpu.SMEM),
                  pl.BlockSpec(memory_space=pl.ANY),
                  pl.BlockSpec((tu, d), lambda i: (i, 0)), _mod_spec(gt2, tu, seq)],
        out_specs=pl.BlockSpec((tu, d), lambda i: (i, 0)),
        out_shape=jax.ShapeDtypeStruct((t, d), F32),
        scratch_shapes=[pltpu.VMEM((tu, d), F32), pltpu.SemaphoreType.DMA(())],
        compiler_params=_cparams("arbitrary"),
        name="moe_collect",
    )(pos, ys, x1, gt2)


def _moe_sorted(hx, w13, w2, x1, gt2, seq):
    t, d = x1.shape
    rank, cnt = _moe_plan(hx, d)
    group = hx[:, d].astype(jnp.int32)
    counts = cnt[0, 0:N_GROUPS].astype(jnp.int32)
    padded = (counts + MOE_TM - 1) // MOE_TM * MOE_TM
    ends = jnp.cumsum(padded)
    starts = ends - padded
    pos = jnp.take(starts, group) + rank[:, 0]
    n_tiles = t // MOE_TM + N_GROUPS
    tile_start = jnp.arange(n_tiles, dtype=jnp.int32) * MOE_TM
    tile_group = jnp.minimum(jnp.sum((tile_start[:, None] >= ends[None, :]).astype(jnp.int32), axis=1),
                             N_GROUPS - 1)
    tile_valid = (tile_start < ends[-1]).astype(jnp.int32)
    xs = _moe_dispatch(pos, hx, n_tiles * MOE_TM)
    ys = _moe_buckets(tile_group, tile_valid, xs, w13, w2, d)
    return _moe_collect(pos, ys, x1, gt2, seq)


def _rope_tables(pos0, seq, reps):
    pos = (pos0 + jnp.arange(seq)).astype(F32)
    inv = ROPE_THETA ** (-jnp.arange(0, DA_HEAD_DIM, 2, dtype=F32) / DA_HEAD_DIM)
    ang = pos[:, None] * inv[None, :]
    cos, sin = jnp.cos(ang), jnp.sin(ang)
    cos_t = jnp.tile(jnp.concatenate([cos, cos], axis=-1), (reps, GMW // DA_HEAD_DIM))
    sin_t = jnp.tile(jnp.concatenate([-sin, sin], axis=-1), (reps, GMW // DA_HEAD_DIM))
    return cos_t, sin_t


def _layer(x, mod, pos0, past, wts):
    bx, sx, d = x.shape
    t = bx * sx
    x2d = x.reshape(t, d)
    per_batch = sx >= 512
    tm = 512 if per_batch else t

    def modv(i):
        if per_batch:
            return mod[:, i:i + 1, :]
        return jnp.repeat(mod[:, i, :], sx, axis=0)[None]

    sh1, sc1, gt1, sh2, sc2, gt2 = (modv(i) for i in range(6))
    cos_t, sin_t = _rope_tables(pos0, sx, 1 if per_batch else bx)

    tm_big = 1024 if (per_batch and sx % 1024 == 0) else tm
    h = _prenorm(x2d, sc1, sh1, wts["norm1_g"], tm, sx)
    q, k32, kb, v32, vb = _qkv(h, wts["w_qkv"], wts["qg"], wts["kg"], cos_t, sin_t, wts["gm"], tm_big)
    rw = _proj(h, wts["w_rw"], tm_big, wts["w_rw"].shape[1] // 3, None, F32, "rw_proj")
    gates = _proj(h, wts["w_gate"], tm_big, d, "sigmoid", BF16, "gate_proj")

    if past is None:
        oa = _attn_prompt(q, kb, vb, wts["lam"], wts["subln_g"], bx, sx, min(1024, sx), min(1024, sx))
        shift0 = jnp.zeros((bx, 1, rw.shape[1]), F32)
        h0_bd = jnp.zeros((bx, RW_GROUPS, RW_GW, RW_GW), F32)
    else:
        ck, cv, s0, sh0 = past
        oa = _attn_sample(q, kb, vb, ck, cv, wts["lam"], wts["subln_g"], bx, sx)
        shift0 = jnp.pad(sh0, ((0, 0), (0, 0), (0, rw.shape[1] - sh0.shape[-1])))
        h0_bd = _state_to_bd(s0.astype(F32))

    rw3 = rw.reshape(bx, sx, rw.shape[1])
    if sx % RW_L == 0:
        cps = RW_CPS if sx % (RW_CPS * RW_L) == 0 else 1
        rw_in, n_valid = rw3, cps * RW_L
    else:
        assert sx < RW_L
        cps = 1
        rw_in, n_valid = jnp.pad(rw3, ((0, 0), (0, RW_L - sx), (0, 0))), sx
    ob, h_bd = _rwkv(rw_in, shift0, h0_bd, wts, wts["gm"], n_valid, cps)
    ob = ob[:, 0:sx, :].reshape(t, d)

    x1, hx = _merge(oa, ob, gates, x2d, gt1, sc2, sh2, wts["norm2_g"],
                    wts["w_oa"], wts["w_ob"], wts["w_out"], wts["wr"], wts["br"], tm, sx)
    if per_batch and sx % MOE_TS == 0:
        y = _moe_sorted(hx, wts["w13"], wts["w2"], x1, gt2, sx)
    else:
        y = _moe_dense(hx, wts["w13"], wts["w2"], x1, gt2, tm, sx)

    n_cols = 3 * d + W_LORA + A_LORA + G_LORA
    return (y.reshape(bx, sx, d),
            k32.reshape(1, bx, sx, DA_HEADS, 2 * DA_HEAD_DIM),
            v32.reshape(1, bx, sx, DA_HEADS, 2 * DA_HEAD_DIM),
            _bd_to_state(h_bd)[None],
            rw3[:, sx - 1:sx, 0:n_cols][None])


def kernel(x_prompt, x_sample, cache_k, cache_v, state_wkv, state_shift, c_prompt, c_sample, norm1_g, norm2_g, w_ada, b_ada, w_in, da_qn_g, da_kn_g, da_lambda, da_subln_g, w_oa, rw_mu, rw_w0, rw_w2, rw_a0, rw_a2, rw_g2, rw_k_k, rw_k_a, rw_r_k, rw_ln_g, rw_ln_b, w_ob, w_out, router_g, router_g_b, router_e, router_e_b, exp_w1, exp_w3, exp_w2):
    b, s, d = x_prompt.shape
    bs = x_sample.shape[0]
    past_len = cache_k.shape[2]
    assert w_in.shape[0] == 1, "single layer"

    c_all = jnp.concatenate([c_prompt, c_sample], axis=0)
    c_all = jnp.pad(c_all, ((0, (-c_all.shape[0]) % 8), (0, 0)))
    mod, lam_tile = _adaln(c_all, w_ada[0], b_ada[0], da_lambda[0])
    mod_p = mod[0:b].reshape(b, 6, d)
    mod_s = mod[b:b + bs].reshape(bs, 6, d)

    win = w_in[0]
    rw_cols = 3 * d + W_LORA + A_LORA + G_LORA
    rw_pad = 3 * d + LORA_PAD
    w_rw = jnp.pad(win[:, 3 * d:3 * d + rw_cols], ((0, 0), (0, rw_pad - rw_cols))).astype(BF16)
    lw = jnp.zeros((LORA_PAD, 3 * d), F32)
    lw = lw.at[0:W_LORA, 0:d].set(rw_w2[0])
    lw = lw.at[W_LORA:W_LORA + A_LORA, d:2 * d].set(rw_a2[0])
    lw = lw.at[W_LORA + A_LORA:W_LORA + A_LORA + G_LORA, 2 * d:3 * d].set(rw_g2[0])
    f = exp_w1.shape[-1]
    wr = jnp.pad(jnp.concatenate([router_g[0], router_e[0]], axis=1), ((0, 0), (0, LANES - N_GROUPS - N_EXPERTS)))
    br = jnp.pad(jnp.concatenate([router_g_b[0], router_e_b[0]]), (0, LANES - N_GROUPS - N_EXPERTS))[None]
    half = jnp.ones((DA_HEAD_DIM, DA_HEAD_DIM), F32) / DA_HEAD_DIM
    wts = dict(
        norm1_g=norm1_g[0][None], norm2_g=norm2_g[0][None],
        w_qkv=win[:, 0:3 * d].astype(BF16), w_rw=w_rw, w_gate=win[:, 3 * d + rw_cols:].astype(BF16),
        qg=jnp.tile(da_qn_g[0], GMW // DA_HEAD_DIM)[None], kg=jnp.tile(da_kn_g[0], GMW // DA_HEAD_DIM)[None],
        gm=jnp.kron(jnp.eye(GMW // DA_HEAD_DIM, dtype=F32), half).astype(BF16),
        lam=lam_tile[0:1], subln_g=da_subln_g[0][None],
        mu=jnp.pad(rw_mu[0], (0, rw_pad - rw_cols))[None], lw=lw.astype(BF16),
        w0=rw_w0[0][None], a0=rw_a0[0][None], k_k=rw_k_k[0][None], k_a=rw_k_a[0][None],
        r_k=rw_r_k[0].reshape(1, d), ln_g=rw_ln_g[0][None], ln_b=rw_ln_b[0][None],
        w_oa=w_oa[0].astype(BF16), w_ob=w_ob[0].astype(BF16), w_out=w_out[0].astype(BF16),
        wr=wr, br=br,
        w13=jnp.concatenate([exp_w1[0], exp_w3[0]], axis=-1).reshape(N_EXPERTS, d, 2 * f).astype(BF16),
        w2=exp_w2[0].reshape(N_EXPERTS, f, d).astype(BF16),
    )

    out_p = _layer(x_prompt, mod_p, 0, None, wts)
    out_s = _layer(x_sample, mod_s, past_len,
                   (cache_k, cache_v, state_wkv[0], state_shift[0]), wts)
    return (out_p[0], out_s[0], out_p[1], out_p[2], out_p[3], out_p[4],
            out_s[1], out_s[2], out_s[3], out_s[4])
```

```python
import functools
import math

import jax
import jax.numpy as jnp
from jax import lax
from jax.experimental import pallas as pl
from jax.experimental.pallas import tpu as pltpu

F32 = jnp.float32
BF16 = jnp.bfloat16

EPS = 1e-6
NEG_INF = -1e30
CHUNK = 64
DA_HEADS = 8
DA_HEAD_DIM = 64
ROPE_THETA = 10000.0
RW_HEAD = 64
RW_GN_EPS = 64e-5
W_LORA, A_LORA, G_LORA = 64, 64, 160
N_GROUPS, EXPERTS_PER_GROUP = 4, 8
N_EXPERTS = N_GROUPS * EXPERTS_PER_GROUP
LAM_INIT = 0.8 - 0.6 * math.exp(-0.3 * 0)

LANES = 128
LORA_PAD = 384
VMEM_LIMIT = 56 * 1024 * 1024


def _cparams(*sem):
    return pltpu.CompilerParams(dimension_semantics=sem, vmem_limit_bytes=VMEM_LIMIT)


def _dot(a, b):
    return jnp.dot(a, b, preferred_element_type=F32)


def _dot_nt(a, b):
    return lax.dot_general(a, b, (((1,), (1,)), ((), ())), preferred_element_type=F32)


def _dot_tn(a, b):
    return lax.dot_general(a, b, (((0,), (0,)), ((), ())), preferred_element_type=F32)


def _split2(a):
    hi = a.astype(BF16)
    lo = (a - hi.astype(F32)).astype(BF16)
    return hi, lo


def _split3(a):
    hi = a.astype(BF16)
    r = a - hi.astype(F32)
    mid = r.astype(BF16)
    lo = (r - mid.astype(F32)).astype(BF16)
    return hi, mid, lo


GMW = 256


def _group_mean(sq, gm):
    hi, lo = _split2(sq)
    return _dot(hi, gm) + _dot(lo, gm)


def _adaln_kernel(c_ref, w_ref, b_ref, l_ref, o_ref, lam_ref):
    c = c_ref[...]
    sc = (c * jax.nn.sigmoid(c)).astype(BF16)
    o_ref[...] = _dot(sc, w_ref[...].astype(BF16)) + b_ref[...]
    l = l_ref[...]
    s1 = jnp.sum(l[0:1] * l[1:2], axis=-1, keepdims=True)
    s2 = jnp.sum(l[2:3] * l[3:4], axis=-1, keepdims=True)
    lam = jnp.exp(s1) - jnp.exp(s2) + LAM_INIT
    lam_ref[...] = jnp.broadcast_to(lam, lam_ref.shape)


def _adaln(c_all, w_ada, b_ada, da_lambda):
    bp, d = c_all.shape
    n = w_ada.shape[1]
    tn = 1024
    return pl.pallas_call(
        _adaln_kernel,
        grid=(n // tn,),
        in_specs=[pl.BlockSpec((bp, d), lambda j: (0, 0)),
                  pl.BlockSpec((d, tn), lambda j: (0, j)),
                  pl.BlockSpec((1, tn), lambda j: (0, j)),
                  pl.BlockSpec(da_lambda.shape, lambda j: (0, 0))],
        out_specs=[pl.BlockSpec((bp, tn), lambda j: (0, j)),
                   pl.BlockSpec((8, LANES), lambda j: (0, 0))],
        out_shape=[jax.ShapeDtypeStruct((bp, n), F32), jax.ShapeDtypeStruct((8, LANES), F32)],
        compiler_params=_cparams("arbitrary"),
        name="adaln",
    )(c_all, w_ada, b_ada.reshape(1, n), da_lambda)


def _prenorm_kernel(x_ref, sc_ref, sh_ref, g_ref, o_ref):
    x = x_ref[...]
    ms = jnp.mean(x * x, axis=-1, keepdims=True)
    h = x * lax.rsqrt(ms + EPS) * g_ref[...]
    o_ref[...] = (h * (1.0 + sc_ref[...]) + sh_ref[...]).astype(o_ref.dtype)


def _mod_spec(mod, tm, seq):
    d = mod.shape[-1]
    if mod.shape[1] == 1:
        per = seq // tm
        return pl.BlockSpec((None, 1, d), lambda i, *_: (i // per, 0, 0))
    return pl.BlockSpec((None, tm, d), lambda i, *_: (0, i, 0))


def _prenorm(x2d, sc, sh, g, tm, seq):
    t, d = x2d.shape
    return pl.pallas_call(
        _prenorm_kernel,
        grid=(t // tm,),
        in_specs=[pl.BlockSpec((tm, d), lambda i: (i, 0)), _mod_spec(sc, tm, seq), _mod_spec(sh, tm, seq),
                  pl.BlockSpec((1, d), lambda i: (0, 0))],
        out_specs=pl.BlockSpec((tm, d), lambda i: (i, 0)),
        out_shape=jax.ShapeDtypeStruct((t, d), BF16),
        compiler_params=_cparams("parallel"),
        name="prenorm",
    )(x2d, sc, sh, g)


def _qkv_kernel(h_ref, w_ref, qg_ref, kg_ref, cos_ref, sin_ref, gm_ref,
                q_ref, k_ref, kb_ref, v_ref, vb_ref):
    j = pl.program_id(1)
    acc = _dot(h_ref[...], w_ref[...])
    n_slab = acc.shape[1] // LANES

    def norm_rope(gain_ref, scale, write):
        lane = lax.broadcasted_iota(jnp.int32, (acc.shape[0], LANES), 1)
        first_half = (lane % DA_HEAD_DIM) < (DA_HEAD_DIM // 2)
        gm = gm_ref[0:LANES, 0:LANES]
        for c in range(n_slab):
            a = acc[:, c * LANES:(c + 1) * LANES]
            ms = _group_mean(a * a, gm)
            y = a * lax.rsqrt(ms + EPS) * gain_ref[:, 0:LANES]
            rot = jnp.where(first_half, pltpu.roll(y, LANES - DA_HEAD_DIM // 2, 1),
                            pltpu.roll(y, DA_HEAD_DIM // 2, 1))
            write(c, (y * cos_ref[:, 0:LANES] + rot * sin_ref[:, 0:LANES]) * scale)

    @pl.when(j == 0)
    def _():
        def write(c, val):
            q_ref[:, c * LANES:(c + 1) * LANES] = val.astype(q_ref.dtype)
        norm_rope(qg_ref, DA_HEAD_DIM ** -0.5 * math.log2(math.e), write)

    @pl.when(j == 1)
    def _():
        def write(c, val):
            k_ref[:, c * LANES:(c + 1) * LANES] = val
            kb_ref[:, c * LANES:(c + 1) * LANES] = val.astype(BF16)
        norm_rope(kg_ref, 1.0, write)

    @pl.when(j == 2)
    def _():
        v_ref[...] = acc
        vb_ref[...] = acc.astype(BF16)


def _qkv(h, w_qkv, qg, kg, cos_t, sin_t, gm, tm):
    t, d = h.shape
    n = 1024
    per = cos_t.shape[0] // tm
    row = lambda i, j: (i, 0)
    return pl.pallas_call(
        _qkv_kernel,
        grid=(t // tm, 3),
        in_specs=[pl.BlockSpec((tm, d), row),
                  pl.BlockSpec((d, n), lambda i, j: (0, j)),
                  pl.BlockSpec((1, GMW), lambda i, j: (0, 0)),
                  pl.BlockSpec((1, GMW), lambda i, j: (0, 0)),
                  pl.BlockSpec((tm, GMW), lambda i, j: (i % per, 0)),
                  pl.BlockSpec((tm, GMW), lambda i, j: (i % per, 0)),
                  pl.BlockSpec((GMW, GMW), lambda i, j: (0, 0))],
        out_specs=[pl.BlockSpec((tm, n), row)] * 5,
        out_shape=[jax.ShapeDtypeStruct((t, n), BF16), jax.ShapeDtypeStruct((t, n), F32),
                   jax.ShapeDtypeStruct((t, n), BF16), jax.ShapeDtypeStruct((t, n), F32),
                   jax.ShapeDtypeStruct((t, n), BF16)],
        compiler_params=_cparams("parallel", "arbitrary"),
        name="qkv_proj",
    )(h, w_qkv, qg, kg, cos_t, sin_t, gm)


def _proj_kernel(h_ref, w_ref, o_ref, *, act):
    acc = _dot(h_ref[...], w_ref[...])
    if act == "sigmoid":
        acc = 0.5 * jnp.tanh(0.5 * acc) + 0.5
    o_ref[...] = acc.astype(o_ref.dtype)


def _proj(h, w, tm, tn, act, out_dtype, name):
    t, d = h.shape
    n = w.shape[1]
    return pl.pallas_call(
        functools.partial(_proj_kernel, act=act),
        grid=(t // tm, n // tn),
        in_specs=[pl.BlockSpec((tm, d), lambda i, j: (i, 0)), pl.BlockSpec((d, tn), lambda i, j: (0, j))],
        out_specs=pl.BlockSpec((tm, tn), lambda i, j: (i, j)),
        out_shape=jax.ShapeDtypeStruct((t, n), out_dtype),
        compiler_params=_cparams("parallel", "arbitrary"),
        name=name,
    )(h, w)


def _subln(o, g):
    ms = jnp.mean(o * o, axis=-1, keepdims=True)
    return o * lax.rsqrt(ms + EPS) * g * (1.0 - LAM_INIT)


ATT_ROWS = 256
ATT_HEADS = 2


def _attn_prompt_kernel(qi_tab, ki_tab, lam_ref, q_ref, k_ref, v_ref, g_ref, o_ref,
                        qs_scr, vx_scr, m_scr, acc_scr, *, tq, tk):
    p = pl.program_id(2)
    qi = qi_tab[p]
    ki = ki_tab[p]
    ratio = tk // tq
    n_slab = tk // LANES

    heads = [slice(hh * LANES, (hh + 1) * LANES) for hh in range(ATT_HEADS)]

    @pl.when(ki == 0)
    def _():
        for hh, hs in enumerate(heads):
            q = q_ref[:, hs]
            lane = lax.broadcasted_iota(jnp.int32, q.shape, 1)
            zero = jnp.zeros_like(q)
            qs_scr[hh, 0:tq, :] = jnp.where(lane < DA_HEAD_DIM, q, zero)
            qs_scr[hh, tq:2 * tq, :] = jnp.where(lane >= DA_HEAD_DIM, q, zero)
            vx_scr[hh, :, LANES:2 * LANES] = jnp.ones((tk, LANES), BF16)
        m_scr[...] = jnp.full(m_scr.shape, NEG_INF, F32)
        acc_scr[...] = jnp.zeros(acc_scr.shape, F32)

    for hh, hs in enumerate(heads):
        vx_scr[hh, :, 0:LANES] = v_ref[:, hs]

    def step(masked):
        for rb in range(2 * tq // ATT_ROWS):
            rows = slice(rb * ATT_ROWS, (rb + 1) * ATT_ROWS)
            nk = min(tk, (rb * ATT_ROWS) % tq + ATT_ROWS) if (masked and ratio == 1) else tk
            for hh, hs in enumerate(heads):
                s = _dot_nt(qs_scr[hh, rows, :], k_ref[0:nk, hs])
                slabs = [s[:, c * LANES:(c + 1) * LANES] for c in range(nk // LANES)]
                if masked:
                    row = lax.broadcasted_iota(jnp.int32, (ATT_ROWS, LANES), 0) + rb * ATT_ROWS
                    lane = lax.broadcasted_iota(jnp.int32, (ATT_ROWS, LANES), 1)
                    qchunk = (row % tq) // CHUNK + (qi % ratio) * (tq // CHUNK)
                    first = (nk - ATT_ROWS) // LANES if ratio == 1 else 0
                    slabs = [jnp.where((c * LANES + lane) // CHUNK <= qchunk, sl, NEG_INF) if c >= first else sl
                             for c, sl in enumerate(slabs)]
                mt = slabs[0]
                for sl in slabs[1:]:
                    mt = jnp.maximum(mt, sl)
                m_prev = m_scr[hh, rows, :]
                m_new = jnp.maximum(m_prev, jnp.max(mt, axis=-1, keepdims=True))
                alpha = jnp.exp2(m_prev - m_new)
                e = jnp.concatenate([jnp.exp2(sl - m_new).astype(BF16) for sl in slabs], axis=1)
                pv = _dot(e, vx_scr[hh, 0:nk, :])
                acc_scr[hh, rows, :] = jnp.concatenate([alpha, alpha], axis=1) * acc_scr[hh, rows, :] + pv
                m_scr[hh, rows, :] = m_new

    @pl.when(ki < qi // ratio)
    def _():
        step(False)

    @pl.when(ki == qi // ratio)
    def _():
        step(True)
        for hh, hs in enumerate(heads):
            o = acc_scr[hh, :, 0:LANES] / acc_scr[hh, :, LANES:2 * LANES]
            o = o[0:tq] - lam_ref[...] * o[tq:2 * tq]
            o_ref[:, hs] = _subln(o, g_ref[...]).astype(o_ref.dtype)


def _attn_prompt(q, k, v, lam_row, subln_g, batch, seq, tq, tk):
    d = q.shape[1]
    assert tk % tq == 0 and seq % tk == 0 and tq % CHUNK == 0
    qi_l, ki_l = [], []
    for a in range(seq // tq):
        for b in range(a * tq // tk + 1):
            qi_l.append(a)
            ki_l.append(b)
    qi_tab = jnp.asarray(qi_l, jnp.int32)
    ki_tab = jnp.asarray(ki_l, jnp.int32)
    q3, k3, v3 = (a.reshape(batch, seq, d) for a in (q, k, v))
    hw = ATT_HEADS * LANES
    qspec = pl.BlockSpec((None, tq, hw), lambda b, h, p, qt, kt: (b, qt[p], h))
    kspec = pl.BlockSpec((None, tk, hw), lambda b, h, p, qt, kt: (b, kt[p], h))
    vec = pl.BlockSpec((1, LANES), lambda b, h, p, qt, kt: (0, 0))
    out = pl.pallas_call(
        functools.partial(_attn_prompt_kernel, tq=tq, tk=tk),
        grid_spec=pltpu.PrefetchScalarGridSpec(
            num_scalar_prefetch=2,
            grid=(batch, DA_HEADS // ATT_HEADS, len(qi_l)),
            in_specs=[vec, qspec, kspec, kspec, vec],
            out_specs=qspec,
            scratch_shapes=[pltpu.VMEM((ATT_HEADS, 2 * tq, LANES), BF16),
                            pltpu.VMEM((ATT_HEADS, tk, 2 * LANES), BF16),
                            pltpu.VMEM((ATT_HEADS, 2 * tq, LANES), F32),
                            pltpu.VMEM((ATT_HEADS, 2 * tq, 2 * LANES), F32)]),
        out_shape=jax.ShapeDtypeStruct((batch, seq, d), BF16),
        compiler_params=_cparams("parallel", "parallel", "arbitrary"),
        name="attn_prompt",
    )(qi_tab, ki_tab, lam_row, q3, k3, v3, subln_g)
    return out.reshape(batch * seq, d)


def _attn_sample_kernel(lam_ref, q_ref, kn_ref, vn_ref, ck_ref, cv_ref, g_ref, o_ref, *, past, sq):
    for h in range(DA_HEADS):
        hs = slice(h * LANES, (h + 1) * LANES)
        q = q_ref[:, hs]
        lane = lax.broadcasted_iota(jnp.int32, q.shape, 1)
        zero = jnp.zeros_like(q)
        qs = jnp.concatenate([jnp.where(lane < DA_HEAD_DIM, q, zero),
                              jnp.where(lane >= DA_HEAD_DIM, q, zero)], axis=0)
        kc = ck_ref[pl.ds(h, past, stride=DA_HEADS), :].astype(BF16)
        vc = cv_ref[pl.ds(h, past, stride=DA_HEADS), :].astype(BF16)
        s_c = _dot_nt(qs, kc)
        s_n = _dot_nt(qs, kn_ref[:, hs])

        def masked(s, key0):
            row = lax.broadcasted_iota(jnp.int32, s.shape, 0)
            col = lax.broadcasted_iota(jnp.int32, s.shape, 1)
            qchunk = (past + row % sq) // CHUNK
            return jnp.where((key0 + col) // CHUNK <= qchunk, s, NEG_INF)

        s_c = masked(s_c, 0)
        s_n = masked(s_n, past)
        m = jnp.maximum(jnp.max(s_c, axis=-1, keepdims=True), jnp.max(s_n, axis=-1, keepdims=True))
        e_c = jnp.exp2(s_c - m)
        e_n = jnp.exp2(s_n - m)
        l = jnp.sum(e_c, axis=-1, keepdims=True) + jnp.sum(e_n, axis=-1, keepdims=True)
        o = (_dot(e_c.astype(BF16), vc) + _dot(e_n.astype(BF16), vn_ref[:, hs])) / l
        o = o[0:sq] - lam_ref[...] * o[sq:2 * sq]
        o_ref[:, hs] = _subln(o, g_ref[...]).astype(o_ref.dtype)


def _attn_sample(q, kn, vn, cache_k, cache_v, lam_row, subln_g, batch, sq):
    d = q.shape[1]
    past = cache_k.shape[-3]
    ck = cache_k.reshape(batch, past * DA_HEADS, LANES)
    cv = cache_v.reshape(batch, past * DA_HEADS, LANES)
    q3, k3, v3 = (a.reshape(batch, sq, d) for a in (q, kn, vn))
    new = pl.BlockSpec((None, sq, d), lambda b: (b, 0, 0))
    old = pl.BlockSpec((None, past * DA_HEADS, LANES), lambda b: (b, 0, 0))
    vec = pl.BlockSpec((1, LANES), lambda b: (0, 0))
    out = pl.pallas_call(
        functools.partial(_attn_sample_kernel, past=past, sq=sq),
        grid=(batch,),
        in_specs=[vec, new, new, new, old, old, vec],
        out_specs=new,
        out_shape=jax.ShapeDtypeStruct((batch, sq, d), BF16),
        compiler_params=_cparams("parallel"),
        name="attn_sample",
    )(lam_row, q3, k3, v3, ck, cv, subln_g)
    return out.reshape(batch * sq, d)


RW_L = 64
RW_G = 2
RW_GW = RW_G * RW_HEAD
RW_GROUPS = 16 // RW_G
RW_CPS = 2


def _mm3(a, b, dot=_dot):
    ah, al = _split2(a)
    bh, bl = _split2(b)
    return dot(ah, bh) + dot(ah, bl) + dot(al, bh)


def _terms(x, n):
    return (x.astype(BF16),) if n == 1 else _split2(x)


def _mmt(at, bt, dot=_dot):
    acc = dot(at[0], bt[0])
    if len(bt) > 1:
        acc = acc + dot(at[0], bt[1])
    if len(at) > 1:
        acc = acc + dot(at[1], bt[0])
    return acc


RW_PREC = dict(a_all=(1, 1), neumann=(1, 1), av=(1, 1), pu=(1, 1), qy=(1, 1), mh=(1, 1), state=(1, 1))


def _softplus(z):
    return jnp.maximum(z, 0.0) + jnp.log(1.0 + jnp.exp(-jnp.abs(z)))


def _rwkv_kernel(rw_ref, sh0_ref, mu_ref, lw_ref, w0_ref, a0_ref, kk_ref, ka_ref, rk_ref, lng_ref, lnb_ref,
                 h0_ref, gm_ref, tri_ref, ob_ref, hout_ref, h_scr, last_scr, *, n_valid, cps):
    L = RW_L
    rows = cps * L
    c = pl.program_id(1)
    d = ob_ref.shape[-1]

    @pl.when(c == 0)
    def _():
        h_scr[...] = h0_ref[...]
        last_scr[...] = sh0_ref[...]

    rw = rw_ref[...]
    row_w = lax.broadcasted_iota(jnp.int32, rw.shape, 0)
    prev = jnp.where(row_w == 0, last_scr[...], pltpu.roll(rw, 1, 0))
    last_scr[...] = rw[rows - 1:rows, :]
    xm = rw + (prev - rw) * mu_ref[...]
    r = xm[:, 0:d]
    k = xm[:, d:2 * d]
    v = xm[:, 2 * d:3 * d]
    lo = xm[:, 3 * d:3 * d + LORA_PAD]
    lane_l = lax.broadcasted_iota(jnp.int32, lo.shape, 1)
    z = jnp.where(lane_l < W_LORA, jnp.tanh(lo),
                  jnp.where(lane_l < W_LORA + A_LORA, lo, jax.nn.sigmoid(lo)))
    lora = _dot(z.astype(BF16), lw_ref[...])
    w = -_softplus(-(w0_ref[...] + lora[:, 0:d])) - 0.5
    ld = -jnp.exp(w)
    a = jax.nn.sigmoid(a0_ref[...] + lora[:, d:2 * d])
    g = lora[:, 2 * d:3 * d]
    kkr = k * kk_ref[...]
    kmod = k * (1.0 + (a - 1.0) * ka_ref[...])
    rkk = r * kmod * rk_ref[...]
    gm = gm_ref[...]
    if n_valid < rows:
        valid = lax.broadcasted_iota(jnp.int32, ld.shape, 0) < n_valid
        ld = jnp.where(valid, ld, 0.0)
        kkr = jnp.where(valid, kkr, 0.0)
        kmod = jnp.where(valid, kmod, 0.0)

    tri = tri_ref[...]
    l1, l2, l3 = _split3(ld)
    cin = _dot(tri, l1) + _dot(tri, l2) + _dot(tri, l3)
    cex = cin - ld
    c_last = [cin[(ch + 1) * L - 1:(ch + 1) * L, :] for ch in range(cps)]
    e_in = jnp.exp(cin)
    e_ex = jnp.exp(cex)
    e_neg = jnp.exp(-cin)
    e_tail = jnp.exp(jnp.concatenate([jnp.broadcast_to(cl, (L, d)) for cl in c_last], axis=0) - cin)
    g_last = [jnp.exp(cl) for cl in c_last]

    GW = RW_GW
    row = lax.broadcasted_iota(jnp.int32, (L, GW), 0)
    lane = lax.broadcasted_iota(jnp.int32, (L, GW), 1)
    head_of_lane = lane // RW_HEAD
    tri_s = row > (lane % L)
    tri_i = row >= (lane % L)
    eye_pair = jnp.where(row == (lane % L), 1.0, 0.0).astype(F32)
    r2 = lax.broadcasted_iota(jnp.int32, (GW, GW), 0)
    c2 = lax.broadcasted_iota(jnp.int32, (GW, GW), 1)
    bd_mask = (r2 // RW_HEAD) == (c2 // RW_HEAD)
    diag_mask = r2 == c2

    def bd(x):
        z0 = jnp.zeros_like(x)
        return jnp.concatenate([jnp.where(head_of_lane == j, x, z0) for j in range(RW_G)], axis=0)

    gmw = min(GW, GMW)
    gm_g = gm[0:gmw, 0:gmw]

    def gmean(x):
        return jnp.concatenate([_group_mean(x[:, j * gmw:(j + 1) * gmw], gm_g) for j in range(GW // gmw)], axis=1)

    def bdt(t):
        return tuple(bd(x) for x in t)

    def cat(ts, axis):
        return tuple(jnp.concatenate(xs, axis=axis) for xs in zip(*ts))

    prec = RW_PREC
    GL = RW_G * L
    items = [(slice(ch * L, (ch + 1) * L), slice(p * GW, (p + 1) * GW))
             for ch in range(cps) for p in range(RW_GROUPS)]
    pairs = range(len(items))

    v_p = [v[rs, sl] for rs, sl in items]
    kk_p = []
    for rs, sl in items:
        kk_raw = kkr[rs, sl]
        ss = gmean(kk_raw * kk_raw) * RW_HEAD
        kk_p.append(kk_raw / jnp.maximum(jnp.sqrt(ss), 1e-12))
    b_p = [kk_p[i] * a[rs, sl] for i, (rs, sl) in enumerate(items)]
    abar = [-kk_p[i] * e_ex[rs, sl] for i, (rs, sl) in enumerate(items)]
    rbar = [r[rs, sl] * e_in[rs, sl] for rs, sl in items]
    bbar = [b_p[i] * e_neg[rs, sl] for i, (rs, sl) in enumerate(items)]
    kbar = [kmod[rs, sl] * e_neg[rs, sl] for rs, sl in items]
    btil = [b_p[i] * e_tail[rs, sl] for i, (rs, sl) in enumerate(items)]
    ktil = [kmod[rs, sl] * e_tail[rs, sl] for rs, sl in items]

    na, nb = prec["a_all"]
    abar_t = [_terms(abar[p], max(na, prec["pu"][1])) for p in pairs]
    v_t = [_terms(v_p[p], max(prec["av"][1], prec["qy"][1], prec["mh"][1])) for p in pairs]
    a_all = []
    for p in pairs:
        lhs = cat([abar_t[p][:na], _terms(rbar[p], na)], 0)
        rhs = cat([bdt(_terms(bbar[p], nb)), bdt(_terms(kbar[p], nb))], 0)
        a_all.append(_mmt(lhs, rhs, _dot_nt))
    n_ab = [jnp.where(tri_s, a_all[p][0:L, 0:GL], 0.0) for p in pairs]
    a_ak = [jnp.where(tri_s, a_all[p][0:L, GL:2 * GL], 0.0) for p in pairs]
    a_rb = [jnp.where(tri_i, a_all[p][L:2 * L, 0:GL], 0.0) for p in pairs]
    a_rk = [jnp.where(tri_i, a_all[p][L:2 * L, GL:2 * GL], 0.0) for p in pairs]

    na, nb = prec["neumann"]
    t_inv = [eye_pair + n_ab[p] for p in pairs]
    pw_t = [_terms(n_ab[p], max(na, nb)) for p in pairs]
    pw = [_mmt(pw_t[p][:na], bdt(pw_t[p][:nb])) for p in pairs]
    for _ in range(int(math.log2(L)) - 2):
        pw_t = [_terms(pw[p], max(na, nb)) for p in pairs]
        st = [_mmt(cat([_terms(t_inv[p], na), pw_t[p][:na]], 0), bdt(pw_t[p][:nb])) for p in pairs]
        t_inv = [t_inv[p] + st[p][0:L] for p in pairs]
        pw = [st[p][L:2 * L] for p in pairs]
    t_inv = [t_inv[p] + _mmt(_terms(t_inv[p], na), bdt(_terms(pw[p], nb))) for p in pairs]

    na, nb = prec["av"]
    vv = [_mmt(cat([_terms(a_ak[p], na), _terms(a_rk[p], na)], 0), bdt(v_t[p][:nb])) for p in pairs]
    av = [x[0:L] for x in vv]
    arkv = [x[L:2 * L] for x in vv]
    na, nb = prec["pu"]
    pu = [_mmt(_terms(t_inv[p], na), cat([bdt(abar_t[p][:nb]), bdt(_terms(av[p], nb))], 1)) for p in pairs]
    p_m = [x[:, 0:GW] for x in pu]
    u0 = [x[:, GW:2 * GW] for x in pu]
    na, nb = prec["qy"]
    nmh = prec["mh"][1]
    p_t = [_terms(p_m[p], max(nb, nmh)) for p in pairs]
    u_t = [_terms(u0[p], max(nb, nmh)) for p in pairs]
    qy = [_mmt(_terms(a_rb[p], na), cat([bdt(p_t[p][:nb]), bdt(u_t[p][:nb])], 1)) for p in pairs]
    q_m = [rbar[p] + qy[p][:, 0:GW] for p in pairs]
    y0 = [qy[p][:, GW:2 * GW] + arkv[p] for p in pairs]

    na, nb = prec["mh"]
    btil_t = [_terms(btil[p], na) for p in pairs]
    m_full = [_mmt(btil_t[p], p_t[p][:nb], _dot_tn) for p in pairs]
    h0_full = [_mmt(cat([btil_t[p], _terms(ktil[p], na)], 0), cat([u_t[p][:nb], v_t[p][:nb]], 0), _dot_tn)
               for p in pairs]
    m_bd = [jnp.where(bd_mask, m_full[i], 0.0) + jnp.where(diag_mask, g_last[i // RW_GROUPS][:, sl], 0.0)
            for i, (rs, sl) in enumerate(items)]
    h0_bd = [jnp.where(bd_mask, h0_full[p], 0.0) for p in pairs]

    na, nb = prec["state"]
    h_cur = [h_scr[p] for p in range(RW_GROUPS)]
    y = []
    for ch in range(cps):
        idx = [ch * RW_GROUPS + p for p in range(RW_GROUPS)]
        st = [_mmt(cat([_terms(q_m[i], na), _terms(m_bd[i], na)], 0), _terms(h_cur[p], nb))
              for p, i in enumerate(idx)]
        y.extend(st[p][0:L] + y0[i] for p, i in enumerate(idx))
        h_cur = [st[p][L:L + GW] + h0_bd[i] for p, i in enumerate(idx)]
    for p in range(RW_GROUPS):
        h_scr[p] = h_cur[p]

    for i, (rs, sl) in enumerate(items):
        mu_y = gmean(y[i])
        dy = y[i] - mu_y
        var = gmean(dy * dy)
        yn = dy * lax.rsqrt(var + RW_GN_EPS) * lng_ref[:, sl] + lnb_ref[:, sl]
        bonus = gmean(rkk[rs, sl]) * RW_HEAD * v_p[i]
        ob_ref[rs, sl] = ((yn + bonus) * g[rs, sl]).astype(ob_ref.dtype)

    @pl.when(c == pl.num_programs(1) - 1)
    def _():
        hout_ref[...] = h_scr[...]


def _rwkv(rw3, shift0, h0_bd, prm, gm, n_valid, cps):
    batch, seq, wcols = rw3.shape
    d = prm["w0"].shape[1]
    rows = cps * RW_L
    nc = seq // rows
    tri = jnp.kron(jnp.eye(cps, dtype=F32), jnp.tril(jnp.ones((RW_L, RW_L), F32))).astype(BF16)
    const2 = lambda b, c: (0, 0)
    vec = pl.BlockSpec((1, d), const2)
    st = pl.BlockSpec((None, RW_GROUPS, RW_GW, RW_GW), lambda b, c: (b, 0, 0, 0))
    return pl.pallas_call(
        functools.partial(_rwkv_kernel, n_valid=n_valid, cps=cps),
        grid=(batch, nc),
        in_specs=[pl.BlockSpec((None, rows, wcols), lambda b, c: (b, c, 0)),
                  pl.BlockSpec((None, 1, wcols), lambda b, c: (b, 0, 0)),
                  pl.BlockSpec((1, wcols), const2),
                  pl.BlockSpec(prm["lw"].shape, const2),
                  vec, vec, vec, vec, vec, vec, vec, st,
                  pl.BlockSpec((GMW, GMW), const2),
                  pl.BlockSpec((rows, rows), const2)],
        out_specs=[pl.BlockSpec((None, rows, d), lambda b, c: (b, c, 0)), st],
        out_shape=[jax.ShapeDtypeStruct((batch, seq, d), BF16),
                   jax.ShapeDtypeStruct((batch, RW_GROUPS, RW_GW, RW_GW), F32)],
        scratch_shapes=[pltpu.VMEM((RW_GROUPS, RW_GW, RW_GW), F32), pltpu.VMEM((1, wcols), F32)],
        compiler_params=_cparams("parallel", "arbitrary"),
        name="rwkv7",
    )(rw3, shift0, prm["mu"], prm["lw"], prm["w0"], prm["a0"], prm["k_k"], prm["k_a"], prm["r_k"],
      prm["ln_g"], prm["ln_b"], h0_bd, gm, tri)


def _state_to_bd(s):
    b = s.shape[0]
    ht = jnp.swapaxes(s, -1, -2).reshape(b, RW_GROUPS, RW_G, RW_HEAD, RW_HEAD)
    z = jnp.zeros_like(ht[:, :, 0])
    rows = [jnp.concatenate([ht[:, :, j] if i == j else z for i in range(RW_G)], axis=-1) for j in range(RW_G)]
    return jnp.concatenate(rows, axis=-2)


def _bd_to_state(hbd):
    b = hbd.shape[0]
    blocks = [hbd[:, :, j * RW_HEAD:(j + 1) * RW_HEAD, j * RW_HEAD:(j + 1) * RW_HEAD] for j in range(RW_G)]
    ht = jnp.stack(blocks, axis=2).reshape(b, RW_G * RW_GROUPS, RW_HEAD, RW_HEAD)
    return jnp.swapaxes(ht, -1, -2)


def _route(logits):
    lane = lax.broadcasted_iota(jnp.int32, logits.shape, 1).astype(F32)
    big = float(LANES)
    lg = jnp.where(lane < N_GROUPS, logits, NEG_INF)
    mg = jnp.max(lg, axis=-1, keepdims=True)
    sg = jnp.sum(jnp.exp(lg - mg), axis=-1, keepdims=True)
    p_top = 1.0 / sg
    g_idx = jnp.min(jnp.where(lg == mg, lane, big), axis=-1, keepdims=True)
    e0 = N_GROUPS + EXPERTS_PER_GROUP * g_idx
    emask = jnp.where(lane >= e0, jnp.where(lane < e0 + EXPERTS_PER_GROUP, 1.0, 0.0), 0.0) > 0.5
    le = jnp.where(emask, logits, NEG_INF)
    me = jnp.max(le, axis=-1, keepdims=True)
    ee = jnp.exp(le - me)
    pe = ee / jnp.sum(ee, axis=-1, keepdims=True)
    pe = jnp.where(emask, pe, -1.0)
    v1 = jnp.max(pe, axis=-1, keepdims=True)
    i1 = jnp.min(jnp.where(pe == v1, lane, big), axis=-1, keepdims=True)
    pe2 = jnp.where(lane == i1, -1.0, pe)
    v2 = jnp.max(pe2, axis=-1, keepdims=True)
    i2 = jnp.min(jnp.where(pe2 == v2, lane, big), axis=-1, keepdims=True)
    den = v1 + v2
    ew = jnp.where(lane == i1, v1 / den, 0.0) + jnp.where(lane == i2, v2 / den, 0.0)
    return p_top * ew + jnp.where(lane == 0.0, g_idx, 0.0)


def _merge_kernel(oa_ref, ob_ref, gates_ref, x_ref, gt1_ref, sc2_ref, sh2_ref, g2_ref,
                  woa_ref, wob_ref, wout_ref, wr_ref, br_ref, x1_ref, hx_ref):
    d = x_ref.shape[1]
    a_out = _dot(oa_ref[...], woa_ref[...])
    b_out = _dot(ob_ref[...], wob_ref[...])
    merged = gates_ref[:, 0:d].astype(F32) * a_out + gates_ref[:, d:2 * d].astype(F32) * b_out
    x1 = x_ref[...] + gt1_ref[...] * _dot(merged.astype(BF16), wout_ref[...])
    x1_ref[...] = x1
    ms = jnp.mean(x1 * x1, axis=-1, keepdims=True)
    h2 = x1 * lax.rsqrt(ms + EPS) * g2_ref[...]
    h2 = h2 * (1.0 + sc2_ref[...]) + sh2_ref[...]
    logits = _mm3(h2, wr_ref[...]) + br_ref[...]
    hx_ref[:, 0:d] = h2
    hx_ref[:, d:d + LANES] = _route(logits)


def _merge(oa, ob, gates, x2d, gt1, sc2, sh2, g2, w_oa, w_ob, w_out, wr, br, tm, seq):
    t, d = x2d.shape
    row = lambda i: (i, 0)
    const = lambda i: (0, 0)
    full = lambda w: pl.BlockSpec(w.shape, const)
    return pl.pallas_call(
        _merge_kernel,
        grid=(t // tm,),
        in_specs=[pl.BlockSpec((tm, d), row), pl.BlockSpec((tm, d), row), pl.BlockSpec((tm, 2 * d), row),
                  pl.BlockSpec((tm, d), row), _mod_spec(gt1, tm, seq), _mod_spec(sc2, tm, seq),
                  _mod_spec(sh2, tm, seq), pl.BlockSpec((1, d), const),
                  full(w_oa), full(w_ob), full(w_out), full(wr), full(br)],
        out_specs=[pl.BlockSpec((tm, d), row), pl.BlockSpec((tm, d + LANES), row)],
        out_shape=[jax.ShapeDtypeStruct((t, d), F32), jax.ShapeDtypeStruct((t, d + LANES), F32)],
        compiler_params=_cparams("parallel"),
        name="merge_router",
    )(oa, ob, gates, x2d, gt1, sc2, sh2, g2, w_oa, w_ob, w_out, wr, br)


def _expert_ffn(xb, comb, lane_e, w13, w2):
    au = _dot(xb, w13)
    f = au.shape[1] // 2
    a = au[:, 0:f]
    u = au[:, f:2 * f]
    lane = lax.broadcasted_iota(jnp.int32, comb.shape, 1)
    cw = jnp.sum(jnp.where(lane == lane_e, comb, 0.0), axis=-1, keepdims=True)
    act = a * jax.nn.sigmoid(a) * u * cw
    return _dot(act.astype(BF16), w2)


def _moe_dense_kernel(hx_ref, w13_ref, w2_ref, x1_ref, gt2_ref, y_ref, xb_scr, acc_scr):
    e = pl.program_id(1)
    d = y_ref.shape[1]

    @pl.when(e == 0)
    def _():
        xb_scr[...] = hx_ref[:, 0:d].astype(BF16)
        acc_scr[...] = jnp.zeros(acc_scr.shape, F32)

    acc_scr[...] += _expert_ffn(xb_scr[...], hx_ref[:, d:d + LANES], e + N_GROUPS, w13_ref[...], w2_ref[...])

    @pl.when(e == pl.num_programs(1) - 1)
    def _():
        y_ref[...] = x1_ref[...] + gt2_ref[...] * acc_scr[...]


def _moe_dense(hx, w13, w2, x1, gt2, tm, seq):
    t, d = x1.shape
    ne, _, f2 = w13.shape
    row = lambda i, e: (i, 0)
    return pl.pallas_call(
        _moe_dense_kernel,
        grid=(t // tm, ne),
        in_specs=[pl.BlockSpec((tm, d + LANES), row),
                  pl.BlockSpec((None, d, f2), lambda i, e: (e, 0, 0)),
                  pl.BlockSpec((None, f2 // 2, d), lambda i, e: (e, 0, 0)),
                  pl.BlockSpec((tm, d), row), _mod_spec(gt2, tm, seq)],
        out_specs=pl.BlockSpec((tm, d), row),
        out_shape=jax.ShapeDtypeStruct((t, d), F32),
        scratch_shapes=[pltpu.VMEM((tm, d), BF16), pltpu.VMEM((tm, d), F32)],
        compiler_params=_cparams("parallel", "arbitrary"),
        name="moe_dense",
    )(hx, w13, w2, x1, gt2)


MOE_TM = 1024
MOE_TS = 1024


def _plan_kernel(rt_ref, tri_ref, rank_ref, cnt_ref, carry_scr):
    @pl.when(pl.program_id(0) == 0)
    def _():
        carry_scr[...] = jnp.zeros(carry_scr.shape, F32)

    rt = rt_ref[...]
    lane = lax.broadcasted_iota(jnp.int32, rt.shape, 1).astype(F32)
    onehot = jnp.where(lane == rt[:, 0:1], 1.0, 0.0)
    before = _dot(tri_ref[...], onehot.astype(BF16)) + carry_scr[...]
    rank_ref[...] = jnp.sum(onehot * before, axis=-1, keepdims=True).astype(jnp.int32)
    carry_scr[...] += jnp.sum(onehot, axis=0, keepdims=True)
    cnt_ref[...] = carry_scr[...]


def _moe_plan(hx, d):
    t = hx.shape[0]
    tp = 512
    tri = jnp.tril(jnp.ones((tp, tp), F32), -1).astype(BF16)
    return pl.pallas_call(
        _plan_kernel,
        grid=(t // tp,),
        in_specs=[pl.BlockSpec((tp, LANES), lambda i: (i, d // LANES)),
                  pl.BlockSpec((tp, tp), lambda i: (0, 0))],
        out_specs=[pl.BlockSpec((tp, 1), lambda i: (i, 0)), pl.BlockSpec((1, LANES), lambda i: (0, 0))],
        out_shape=[jax.ShapeDtypeStruct((t, 1), jnp.int32), jax.ShapeDtypeStruct((1, LANES), F32)],
        scratch_shapes=[pltpu.VMEM((1, LANES), F32)],
        compiler_params=_cparams("arbitrary"),
        name="moe_plan",
    )(hx, tri)


def _row_copy(src_hbm, src_row, dst, dst_row, sem):
    return pltpu.make_async_copy(src_hbm.at[pl.ds(src_row, 1), :], dst.at[pl.ds(dst_row, 1), :], sem)


def _dispatch_kernel(pos_ref, hx_ref, xs_in_hbm, xs_hbm, sem):
    del xs_in_hbm

    def issue(t, carry):
        _row_copy(hx_ref, t, xs_hbm, pos_ref[t], sem).start()
        return carry

    def drain(t, carry):
        _row_copy(hx_ref, t, xs_hbm, pos_ref[t], sem).wait()
        return carry

    lax.fori_loop(0, MOE_TS, issue, 0, unroll=8)
    lax.fori_loop(0, MOE_TS, drain, 0, unroll=8)


def _moe_dispatch(pos, hx, n_rows):
    t, w = hx.shape
    xs0 = jnp.zeros((n_rows, w), F32)
    return pl.pallas_call(
        _dispatch_kernel,
        grid=(t // MOE_TS,),
        in_specs=[pl.BlockSpec((MOE_TS,), lambda i: (i,), memory_space=pltpu.SMEM),
                  pl.BlockSpec((MOE_TS, w), lambda i: (i, 0)), pl.BlockSpec(memory_space=pl.ANY)],
        out_specs=pl.BlockSpec(memory_space=pl.ANY),
        out_shape=jax.ShapeDtypeStruct((n_rows, w), F32),
        scratch_shapes=[pltpu.SemaphoreType.DMA(())],
        input_output_aliases={2: 0},
        compiler_params=_cparams("arbitrary"),
        name="moe_dispatch",
    )(pos, hx, xs0)


def _moe_bucket_kernel(tg_ref, tv_ref, xs_ref, w13_ref, w2_ref, ys_ref, xb_scr):
    i = pl.program_id(0)
    e = pl.program_id(1)
    d = ys_ref.shape[1]

    @pl.when(e == 0)
    def _():
        xb_scr[...] = xs_ref[:, 0:d].astype(BF16)
        ys_ref[...] = jnp.zeros(ys_ref.shape, F32)

    @pl.when(tv_ref[i] > 0)
    def _():
        lane_e = N_GROUPS + tg_ref[i] * EXPERTS_PER_GROUP + e
        ys_ref[...] += _expert_ffn(xb_scr[...], xs_ref[:, d:d + LANES], lane_e, w13_ref[...], w2_ref[...])


def _moe_buckets(tile_group, tile_valid, xs, w13, w2, d):
    n_rows, w = xs.shape
    _, _, f2 = w13.shape
    row = lambda i, e, tg, tv: (i, 0)
    wsel = lambda i, e, tg, tv: (tg[i] * EXPERTS_PER_GROUP + e, 0, 0)
    return pl.pallas_call(
        _moe_bucket_kernel,
        grid_spec=pltpu.PrefetchScalarGridSpec(
            num_scalar_prefetch=2,
            grid=(n_rows // MOE_TM, EXPERTS_PER_GROUP),
            in_specs=[pl.BlockSpec((MOE_TM, w), row),
                      pl.BlockSpec((None, d, f2), wsel), pl.BlockSpec((None, f2 // 2, d), wsel)],
            out_specs=pl.BlockSpec((MOE_TM, d), row),
            scratch_shapes=[pltpu.VMEM((MOE_TM, d), BF16)]),
        out_shape=jax.ShapeDtypeStruct((n_rows, d), F32),
        compiler_params=_cparams("parallel", "arbitrary"),
        name="moe_buckets",
    )(tile_group, tile_valid, xs, w13, w2)


def _collect_kernel(pos_ref, ys_hbm, x1_ref, gt2_ref, y_ref, buf, sem):
    tu = buf.shape[0]

    def issue(t, carry):
        _row_copy(ys_hbm, pos_ref[t], buf, t, sem).start()
        return carry

    def drain(t, carry):
        _row_copy(ys_hbm, pos_ref[t], buf, t, sem).wait()
        return carry

    lax.fori_loop(0, tu, issue, 0, unroll=8)
    lax.fori_loop(0, tu, drain, 0, unroll=8)
    y_ref[...] = x1_ref[...] + gt2_ref[...] * buf[...]


def _moe_collect(pos, ys, x1, gt2, seq):
    t, d = x1.shape
    tu = MOE_TS
    return pl.pallas_call(
        _collect_kernel,
        grid=(t // tu,),
        in_specs=[pl.BlockSpec((tu,), lambda i: (i,), memory_space=pltpu.SMEM),
                  pl.BlockSpec(memory_space=pl.ANY),
                  pl.BlockSpec((tu, d), lambda i: (i, 0)), _mod_spec(gt2, tu, seq)],
        out_specs=pl.BlockSpec((tu, d), lambda i: (i, 0)),
        out_shape=jax.ShapeDtypeStruct((t, d), F32),
        scratch_shapes=[pltpu.VMEM((tu, d), F32), pltpu.SemaphoreType.DMA(())],
        compiler_params=_cparams("arbitrary"),
        name="moe_collect",
    )(pos, ys, x1, gt2)


def _moe_sorted(hx, w13, w2, x1, gt2, seq):
    t, d = x1.shape
    rank, cnt = _moe_plan(hx, d)
    group = hx[:, d].astype(jnp.int32)
    counts = cnt[0, 0:N_GROUPS].astype(jnp.int32)
    padded = (counts + MOE_TM - 1) // MOE_TM * MOE_TM
    ends = jnp.cumsum(padded)
    starts = ends - padded
    pos = jnp.take(starts, group) + rank[:, 0]
    n_tiles = t // MOE_TM + N_GROUPS
    tile_start = jnp.arange(n_tiles, dtype=jnp.int32) * MOE_TM
    tile_group = jnp.minimum(jnp.sum((tile_start[:, None] >= ends[None, :]).astype(jnp.int32), axis=1),
                             N_GROUPS - 1)
    tile_valid = (tile_start < ends[-1]).astype(jnp.int32)
    xs = _moe_dispatch(pos, hx, n_tiles * MOE_TM)
    ys = _moe_buckets(tile_group, tile_valid, xs, w13, w2, d)
    return _moe_collect(pos, ys, x1, gt2, seq)


def _rope_tables(pos0, seq, reps):
    pos = (pos0 + jnp.arange(seq)).astype(F32)
    inv = ROPE_THETA ** (-jnp.arange(0, DA_HEAD_DIM, 2, dtype=F32) / DA_HEAD_DIM)
    ang = pos[:, None] * inv[None, :]
    cos, sin = jnp.cos(ang), jnp.sin(ang)
    cos_t = jnp.tile(jnp.concatenate([cos, cos], axis=-1), (reps, GMW // DA_HEAD_DIM))
    sin_t = jnp.tile(jnp.concatenate([-sin, sin], axis=-1), (reps, GMW // DA_HEAD_DIM))
    return cos_t, sin_t


def _layer(x, mod, pos0, past, wts):
    bx, sx, d = x.shape
    t = bx * sx
    x2d = x.reshape(t, d)
    per_batch = sx >= 512
    tm = 512 if per_batch else t

    def modv(i):
        if per_batch:
            return mod[:, i:i + 1, :]
        return jnp.repeat(mod[:, i, :], sx, axis=0)[None]

    sh1, sc1, gt1, sh2, sc2, gt2 = (modv(i) for i in range(6))
    cos_t, sin_t = _rope_tables(pos0, sx, 1 if per_batch else bx)

    tm_big = 1024 if (per_batch and sx % 1024 == 0) else tm
    h = _prenorm(x2d, sc1, sh1, wts["norm1_g"], tm, sx)
    q, k32, kb, v32, vb = _qkv(h, wts["w_qkv"], wts["qg"], wts["kg"], cos_t, sin_t, wts["gm"], tm_big)
    rw = _proj(h, wts["w_rw"], tm_big, wts["w_rw"].shape[1] // 3, None, F32, "rw_proj")
    gates = _proj(h, wts["w_gate"], tm_big, d, "sigmoid", BF16, "gate_proj")

    if past is None:
        oa = _attn_prompt(q, kb, vb, wts["lam"], wts["subln_g"], bx, sx, min(1024, sx), min(1024, sx))
        shift0 = jnp.zeros((bx, 1, rw.shape[1]), F32)
        h0_bd = jnp.zeros((bx, RW_GROUPS, RW_GW, RW_GW), F32)
    else:
        ck, cv, s0, sh0 = past
        oa = _attn_sample(q, kb, vb, ck, cv, wts["lam"], wts["subln_g"], bx, sx)
        shift0 = jnp.pad(sh0, ((0, 0), (0, 0), (0, rw.shape[1] - sh0.shape[-1])))
        h0_bd = _state_to_bd(s0.astype(F32))

    rw3 = rw.reshape(bx, sx, rw.shape[1])
    if sx % RW_L == 0:
        cps = RW_CPS if sx % (RW_CPS * RW_L) == 0 else 1
        rw_in, n_valid = rw3, cps * RW_L
    else:
        assert sx < RW_L
        cps = 1
        rw_in, n_valid = jnp.pad(rw3, ((0, 0), (0, RW_L - sx), (0, 0))), sx
    ob, h_bd = _rwkv(rw_in, shift0, h0_bd, wts, wts["gm"], n_valid, cps)
    ob = ob[:, 0:sx, :].reshape(t, d)

    x1, hx = _merge(oa, ob, gates, x2d, gt1, sc2, sh2, wts["norm2_g"],
                    wts["w_oa"], wts["w_ob"], wts["w_out"], wts["wr"], wts["br"], tm, sx)
    if per_batch and sx % MOE_TS == 0:
        y = _moe_sorted(hx, wts["w13"], wts["w2"], x1, gt2, sx)
    else:
        y = _moe_dense(hx, wts["w13"], wts["w2"], x1, gt2, tm, sx)

    n_cols = 3 * d + W_LORA + A_LORA + G_LORA
    return (y.reshape(bx, sx, d),
            k32.reshape(1, bx, sx, DA_HEADS, 2 * DA_HEAD_DIM),
            v32.reshape(1, bx, sx, DA_HEADS, 2 * DA_HEAD_DIM),
            _bd_to_state(h_bd)[None],
            rw3[:, sx - 1:sx, 0:n_cols][None])


def kernel(x_prompt, x_sample, cache_k, cache_v, state_wkv, state_shift, c_prompt, c_sample, norm1_g, norm2_g, w_ada, b_ada, w_in, da_qn_g, da_kn_g, da_lambda, da_subln_g, w_oa, rw_mu, rw_w0, rw_w2, rw_a0, rw_a2, rw_g2, rw_k_k, rw_k_a, rw_r_k, rw_ln_g, rw_ln_b, w_ob, w_out, router_g, router_g_b, router_e, router_e_b, exp_w1, exp_w3, exp_w2):
    b, s, d = x_prompt.shape
    bs = x_sample.shape[0]
    past_len = cache_k.shape[2]
    assert w_in.shape[0] == 1, "single layer"

    c_all = jnp.concatenate([c_prompt, c_sample], axis=0)
    c_all = jnp.pad(c_all, ((0, (-c_all.shape[0]) % 8), (0, 0)))
    mod, lam_tile = _adaln(c_all, w_ada[0], b_ada[0], da_lambda[0])
    mod_p = mod[0:b].reshape(b, 6, d)
    mod_s = mod[b:b + bs].reshape(bs, 6, d)

    win = w_in[0]
    rw_cols = 3 * d + W_LORA + A_LORA + G_LORA
    rw_pad = 3 * d + LORA_PAD
    w_rw = jnp.pad(win[:, 3 * d:3 * d + rw_cols], ((0, 0), (0, rw_pad - rw_cols))).astype(BF16)
    lw = jnp.zeros((LORA_PAD, 3 * d), F32)
    lw = lw.at[0:W_LORA, 0:d].set(rw_w2[0])
    lw = lw.at[W_LORA:W_LORA + A_LORA, d:2 * d].set(rw_a2[0])
    lw = lw.at[W_LORA + A_LORA:W_LORA + A_LORA + G_LORA, 2 * d:3 * d].set(rw_g2[0])
    f = exp_w1.shape[-1]
    wr = jnp.pad(jnp.concatenate([router_g[0], router_e[0]], axis=1), ((0, 0), (0, LANES - N_GROUPS - N_EXPERTS)))
    br = jnp.pad(jnp.concatenate([router_g_b[0], router_e_b[0]]), (0, LANES - N_GROUPS - N_EXPERTS))[None]
    half = jnp.ones((DA_HEAD_DIM, DA_HEAD_DIM), F32) / DA_HEAD_DIM
    wts = dict(
        norm1_g=norm1_g[0][None], norm2_g=norm2_g[0][None],
        w_qkv=win[:, 0:3 * d].astype(BF16), w_rw=w_rw, w_gate=win[:, 3 * d + rw_cols:].astype(BF16),
        qg=jnp.tile(da_qn_g[0], GMW // DA_HEAD_DIM)[None], kg=jnp.tile(da_kn_g[0], GMW // DA_HEAD_DIM)[None],
        gm=jnp.kron(jnp.eye(GMW // DA_HEAD_DIM, dtype=F32), half).astype(BF16),
        lam=lam_tile[0:1], subln_g=da_subln_g[0][None],
        mu=jnp.pad(rw_mu[0], (0, rw_pad - rw_cols))[None], lw=lw.astype(BF16),
        w0=rw_w0[0][None], a0=rw_a0[0][None], k_k=rw_k_k[0][None], k_a=rw_k_a[0][None],
        r_k=rw_r_k[0].reshape(1, d), ln_g=rw_ln_g[0][None], ln_b=rw_ln_b[0][None],
        w_oa=w_oa[0].astype(BF16), w_ob=w_ob[0].astype(BF16), w_out=w_out[0].astype(BF16),
        wr=wr, br=br,
        w13=jnp.concatenate([exp_w1[0], exp_w3[0]], axis=-1).reshape(N_EXPERTS, d, 2 * f).astype(BF16),
        w2=exp_w2[0].reshape(N_EXPERTS, f, d).astype(BF16),
    )

    out_p = _layer(x_prompt, mod_p, 0, None, wts)
    out_s = _layer(x_sample, mod_s, past_len,
                   (cache_k, cache_v, state_wkv[0], state_shift[0]), wts)
    return (out_p[0], out_s[0], out_p[1], out_p[2], out_p[3], out_p[4],
            out_s[1], out_s[2], out_s[3], out_s[4])
```

```python
import functools
import math

import jax
import jax.numpy as jnp
from jax import lax
from jax.experimental import pallas as pl
from jax.experimental.pallas import tpu as pltpu

F32 = jnp.float32
BF16 = jnp.bfloat16

EPS = 1e-6
NEG_INF = -1e30
CHUNK = 64
DA_HEADS = 8
DA_HEAD_DIM = 64
ROPE_THETA = 10000.0
RW_HEAD = 64
RW_GN_EPS = 64e-5
W_LORA, A_LORA, G_LORA = 64, 64, 160
N_GROUPS, EXPERTS_PER_GROUP = 4, 8
N_EXPERTS = N_GROUPS * EXPERTS_PER_GROUP
LAM_INIT = 0.8 - 0.6 * math.exp(-0.3 * 0)

LANES = 128
LORA_PAD = 384
VMEM_LIMIT = 56 * 1024 * 1024


def _cparams(*sem):
    return pltpu.CompilerParams(dimension_semantics=sem, vmem_limit_bytes=VMEM_LIMIT)


def _dot(a, b):
    return jnp.dot(a, b, preferred_element_type=F32)


def _dot_nt(a, b):
    return lax.dot_general(a, b, (((1,), (1,)), ((), ())), preferred_element_type=F32)


def _dot_tn(a, b):
    return lax.dot_general(a, b, (((0,), (0,)), ((), ())), preferred_element_type=F32)


def _split2(a):
    hi = a.astype(BF16)
    lo = (a - hi.astype(F32)).astype(BF16)
    return hi, lo


def _split3(a):
    hi = a.astype(BF16)
    r = a - hi.astype(F32)
    mid = r.astype(BF16)
    lo = (r - mid.astype(F32)).astype(BF16)
    return hi, mid, lo


GMW = 256


def _group_mean(sq, gm):
    hi, lo = _split2(sq)
    return _dot(hi, gm) + _dot(lo, gm)


def _adaln_kernel(c_ref, w_ref, b_ref, l_ref, o_ref, lam_ref):
    c = c_ref[...]
    sc = (c * jax.nn.sigmoid(c)).astype(BF16)
    o_ref[...] = _dot(sc, w_ref[...].astype(BF16)) + b_ref[...]
    l = l_ref[...]
    s1 = jnp.sum(l[0:1] * l[1:2], axis=-1, keepdims=True)
    s2 = jnp.sum(l[2:3] * l[3:4], axis=-1, keepdims=True)
    lam = jnp.exp(s1) - jnp.exp(s2) + LAM_INIT
    lam_ref[...] = jnp.broadcast_to(lam, lam_ref.shape)


def _adaln(c_all, w_ada, b_ada, da_lambda):
    bp, d = c_all.shape
    n = w_ada.shape[1]
    tn = 1024
    return pl.pallas_call(
        _adaln_kernel,
        grid=(n // tn,),
        in_specs=[pl.BlockSpec((bp, d), lambda j: (0, 0)),
                  pl.BlockSpec((d, tn), lambda j: (0, j)),
                  pl.BlockSpec((1, tn), lambda j: (0, j)),
                  pl.BlockSpec(da_lambda.shape, lambda j: (0, 0))],
        out_specs=[pl.BlockSpec((bp, tn), lambda j: (0, j)),
                   pl.BlockSpec((8, LANES), lambda j: (0, 0))],
        out_shape=[jax.ShapeDtypeStruct((bp, n), F32), jax.ShapeDtypeStruct((8, LANES), F32)],
        compiler_params=_cparams("arbitrary"),
        name="adaln",
    )(c_all, w_ada, b_ada.reshape(1, n), da_lambda)


def _prenorm_kernel(x_ref, sc_ref, sh_ref, g_ref, o_ref):
    x = x_ref[...]
    ms = jnp.mean(x * x, axis=-1, keepdims=True)
    h = x * lax.rsqrt(ms + EPS) * g_ref[...]
    o_ref[...] = (h * (1.0 + sc_ref[...]) + sh_ref[...]).astype(o_ref.dtype)


def _mod_spec(mod, tm, seq):
    d = mod.shape[-1]
    if mod.shape[1] == 1:
        per = seq // tm
        return pl.BlockSpec((None, 1, d), lambda i, *_: (i // per, 0, 0))
    return pl.BlockSpec((None, tm, d), lambda i, *_: (0, i, 0))


def _prenorm(x2d, sc, sh, g, tm, seq):
    t, d = x2d.shape
    return pl.pallas_call(
        _prenorm_kernel,
        grid=(t // tm,),
        in_specs=[pl.BlockSpec((tm, d), lambda i: (i, 0)), _mod_spec(sc, tm, seq), _mod_spec(sh, tm, seq),
                  pl.BlockSpec((1, d), lambda i: (0, 0))],
        out_specs=pl.BlockSpec((tm, d), lambda i: (i, 0)),
        out_shape=jax.ShapeDtypeStruct((t, d), BF16),
        compiler_params=_cparams("parallel"),
        name="prenorm",
    )(x2d, sc, sh, g)


def _qkv_kernel(h_ref, w_ref, qg_ref, kg_ref, cos_ref, sin_ref, gm_ref,
                q_ref, k_ref, kb_ref, v_ref, vb_ref):
    j = pl.program_id(1)
    acc = _dot(h_ref[...], w_ref[...])
    n_slab = acc.shape[1] // LANES

    def norm_rope(gain_ref, scale, write):
        lane = lax.broadcasted_iota(jnp.int32, (acc.shape[0], LANES), 1)
        first_half = (lane % DA_HEAD_DIM) < (DA_HEAD_DIM // 2)
        gm = gm_ref[0:LANES, 0:LANES]
        for c in range(n_slab):
            a = acc[:, c * LANES:(c + 1) * LANES]
            ms = _group_mean(a * a, gm)
            y = a * lax.rsqrt(ms + EPS) * gain_ref[:, 0:LANES]
            rot = jnp.where(first_half, pltpu.roll(y, LANES - DA_HEAD_DIM // 2, 1),
                            pltpu.roll(y, DA_HEAD_DIM // 2, 1))
            write(c, (y * cos_ref[:, 0:LANES] + rot * sin_ref[:, 0:LANES]) * scale)

    @pl.when(j == 0)
    def _():
        def write(c, val):
            q_ref[:, c * LANES:(c + 1) * LANES] = val.astype(q_ref.dtype)
        norm_rope(qg_ref, DA_HEAD_DIM ** -0.5 * math.log2(math.e), write)

    @pl.when(j == 1)
    def _():
        def write(c, val):
            k_ref[:, c * LANES:(c + 1) * LANES] = val
            kb_ref[:, c * LANES:(c + 1) * LANES] = val.astype(BF16)
        norm_rope(kg_ref, 1.0, write)

    @pl.when(j == 2)
    def _():
        v_ref[...] = acc
        vb_ref[...] = acc.astype(BF16)


def _qkv(h, w_qkv, qg, kg, cos_t, sin_t, gm, tm):
    t, d = h.shape
    n = 1024
    per = cos_t.shape[0] // tm
    row = lambda i, j: (i, 0)
    return pl.pallas_call(
        _qkv_kernel,
        grid=(t // tm, 3),
        in_specs=[pl.BlockSpec((tm, d), row),
                  pl.BlockSpec((d, n), lambda i, j: (0, j)),
                  pl.BlockSpec((1, GMW), lambda i, j: (0, 0)),
                  pl.BlockSpec((1, GMW), lambda i, j: (0, 0)),
                  pl.BlockSpec((tm, GMW), lambda i, j: (i % per, 0)),
                  pl.BlockSpec((tm, GMW), lambda i, j: (i % per, 0)),
                  pl.BlockSpec((GMW, GMW), lambda i, j: (0, 0))],
        out_specs=[pl.BlockSpec((tm, n), row)] * 5,
        out_shape=[jax.ShapeDtypeStruct((t, n), BF16), jax.ShapeDtypeStruct((t, n), F32),
                   jax.ShapeDtypeStruct((t, n), BF16), jax.ShapeDtypeStruct((t, n), F32),
                   jax.ShapeDtypeStruct((t, n), BF16)],
        compiler_params=_cparams("parallel", "arbitrary"),
        name="qkv_proj",
    )(h, w_qkv, qg, kg, cos_t, sin_t, gm)


def _proj_kernel(h_ref, w_ref, o_ref, *, act):
    acc = _dot(h_ref[...], w_ref[...])
    if act == "sigmoid":
        acc = 0.5 * jnp.tanh(0.5 * acc) + 0.5
    o_ref[...] = acc.astype(o_ref.dtype)


def _proj(h, w, tm, tn, act, out_dtype, name):
    t, d = h.shape
    n = w.shape[1]
    return pl.pallas_call(
        functools.partial(_proj_kernel, act=act),
        grid=(t // tm, n // tn),
        in_specs=[pl.BlockSpec((tm, d), lambda i, j: (i, 0)), pl.BlockSpec((d, tn), lambda i, j: (0, j))],
        out_specs=pl.BlockSpec((tm, tn), lambda i, j: (i, j)),
        out_shape=jax.ShapeDtypeStruct((t, n), out_dtype),
        compiler_params=_cparams("parallel", "arbitrary"),
        name=name,
    )(h, w)


def _subln(o, g):
    ms = jnp.mean(o * o, axis=-1, keepdims=True)
    return o * lax.rsqrt(ms + EPS) * g * (1.0 - LAM_INIT)


ATT_ROWS = 256
ATT_HEADS = 2


def _attn_prompt_kernel(qi_tab, ki_tab, lam_ref, q_ref, k_ref, v_ref, g_ref, o_ref,
                        qs_scr, vx_scr, m_scr, acc_scr, *, tq, tk):
    p = pl.program_id(2)
    qi = qi_tab[p]
    ki = ki_tab[p]
    ratio = tk // tq
    n_slab = tk // LANES

    heads = [slice(hh * LANES, (hh + 1) * LANES) for hh in range(ATT_HEADS)]

    @pl.when(ki == 0)
    def _():
        for hh, hs in enumerate(heads):
            q = q_ref[:, hs]
            lane = lax.broadcasted_iota(jnp.int32, q.shape, 1)
            zero = jnp.zeros_like(q)
            qs_scr[hh, 0:tq, :] = jnp.where(lane < DA_HEAD_DIM, q, zero)
            qs_scr[hh, tq:2 * tq, :] = jnp.where(lane >= DA_HEAD_DIM, q, zero)
            vx_scr[hh, :, LANES:2 * LANES] = jnp.ones((tk, LANES), BF16)
        m_scr[...] = jnp.full(m_scr.shape, NEG_INF, F32)
        acc_scr[...] = jnp.zeros(acc_scr.shape, F32)

    for hh, hs in enumerate(heads):
        vx_scr[hh, :, 0:LANES] = v_ref[:, hs]

    def step(masked):
        for rb in range(2 * tq // ATT_ROWS):
            rows = slice(rb * ATT_ROWS, (rb + 1) * ATT_ROWS)
            nk = min(tk, (rb * ATT_ROWS) % tq + ATT_ROWS) if (masked and ratio == 1) else tk
            for hh, hs in enumerate(heads):
                s = _dot_nt(qs_scr[hh, rows, :], k_ref[0:nk, hs])
                slabs = [s[:, c * LANES:(c + 1) * LANES] for c in range(nk // LANES)]
                if masked:
                    row = lax.broadcasted_iota(jnp.int32, (ATT_ROWS, LANES), 0) + rb * ATT_ROWS
                    lane = lax.broadcasted_iota(jnp.int32, (ATT_ROWS, LANES), 1)
                    qchunk = (row % tq) // CHUNK + (qi % ratio) * (tq // CHUNK)
                    first = (nk - ATT_ROWS) // LANES if ratio == 1 else 0
                    slabs = [jnp.where((c * LANES + lane) // CHUNK <= qchunk, sl, NEG_INF) if c >= first else sl
                             for c, sl in enumerate(slabs)]
                mt = slabs[0]
                for sl in slabs[1:]:
                    mt = jnp.maximum(mt, sl)
                m_prev = m_scr[hh, rows, :]
                m_new = jnp.maximum(m_prev, jnp.max(mt, axis=-1, keepdims=True))
                alpha = jnp.exp2(m_prev - m_new)
                e = jnp.concatenate([jnp.exp2(sl - m_new).astype(BF16) for sl in slabs], axis=1)
                pv = _dot(e, vx_scr[hh, 0:nk, :])
                acc_scr[hh, rows, :] = jnp.concatenate([alpha, alpha], axis=1) * acc_scr[hh, rows, :] + pv
                m_scr[hh, rows, :] = m_new

    @pl.when(ki < qi // ratio)
    def _():
        step(False)

    @pl.when(ki == qi // ratio)
    def _():
        step(True)
        for hh, hs in enumerate(heads):
            o = acc_scr[hh, :, 0:LANES] / acc_scr[hh, :, LANES:2 * LANES]
            o = o[0:tq] - lam_ref[...] * o[tq:2 * tq]
            o_ref[:, hs] = _subln(o, g_ref[...]).astype(o_ref.dtype)


def _attn_prompt(q, k, v, lam_row, subln_g, batch, seq, tq, tk):
    d = q.shape[1]
    assert tk % tq == 0 and seq % tk == 0 and tq % CHUNK == 0
    qi_l, ki_l = [], []
    for a in range(seq // tq):
        for b in range(a * tq // tk + 1):
            qi_l.append(a)
            ki_l.append(b)
    qi_tab = jnp.asarray(qi_l, jnp.int32)
    ki_tab = jnp.asarray(ki_l, jnp.int32)
    q3, k3, v3 = (a.reshape(batch, seq, d) for a in (q, k, v))
    hw = ATT_HEADS * LANES
    qspec = pl.BlockSpec((None, tq, hw), lambda b, h, p, qt, kt: (b, qt[p], h))
    kspec = pl.BlockSpec((None, tk, hw), lambda b, h, p, qt, kt: (b, kt[p], h))
    vec = pl.BlockSpec((1, LANES), lambda b, h, p, qt, kt: (0, 0))
    out = pl.pallas_call(
        functools.partial(_attn_prompt_kernel, tq=tq, tk=tk),
        grid_spec=pltpu.PrefetchScalarGridSpec(
            num_scalar_prefetch=2,
            grid=(batch, DA_HEADS // ATT_HEADS, len(qi_l)),
            in_specs=[vec, qspec, kspec, kspec, vec],
            out_specs=qspec,
            scratch_shapes=[pltpu.VMEM((ATT_HEADS, 2 * tq, LANES), BF16),
                            pltpu.VMEM((ATT_HEADS, tk, 2 * LANES), BF16),
                            pltpu.VMEM((ATT_HEADS, 2 * tq, LANES), F32),
                            pltpu.VMEM((ATT_HEADS, 2 * tq, 2 * LANES), F32)]),
        out_shape=jax.ShapeDtypeStruct((batch, seq, d), BF16),
        compiler_params=_cparams("parallel", "parallel", "arbitrary"),
        name="attn_prompt",
    )(qi_tab, ki_tab, lam_row, q3, k3, v3, subln_g)
    return out.reshape(batch * seq, d)


def _attn_sample_kernel(lam_ref, q_ref, kn_ref, vn_ref, ck_ref, cv_ref, g_ref, o_ref, *, past, sq):
    for h in range(DA_HEADS):
        hs = slice(h * LANES, (h + 1) * LANES)
        q = q_ref[:, hs]
        lane = lax.broadcasted_iota(jnp.int32, q.shape, 1)
        zero = jnp.zeros_like(q)
        qs = jnp.concatenate([jnp.where(lane < DA_HEAD_DIM, q, zero),
                              jnp.where(lane >= DA_HEAD_DIM, q, zero)], axis=0)
        kc = ck_ref[pl.ds(h, past, stride=DA_HEADS), :].astype(BF16)
        vc = cv_ref[pl.ds(h, past, stride=DA_HEADS), :].astype(BF16)
        s_c = _dot_nt(qs, kc)
        s_n = _dot_nt(qs, kn_ref[:, hs])

        def masked(s, key0):
            row = lax.broadcasted_iota(jnp.int32, s.shape, 0)
            col = lax.broadcasted_iota(jnp.int32, s.shape, 1)
            qchunk = (past + row % sq) // CHUNK
            return jnp.where((key0 + col) // CHUNK <= qchunk, s, NEG_INF)

        s_c = masked(s_c, 0)
        s_n = masked(s_n, past)
        m = jnp.maximum(jnp.max(s_c, axis=-1, keepdims=True), jnp.max(s_n, axis=-1, keepdims=True))
        e_c = jnp.exp2(s_c - m)
        e_n = jnp.exp2(s_n - m)
        l = jnp.sum(e_c, axis=-1, keepdims=True) + jnp.sum(e_n, axis=-1, keepdims=True)
        o = (_dot(e_c.astype(BF16), vc) + _dot(e_n.astype(BF16), vn_ref[:, hs])) / l
        o = o[0:sq] - lam_ref[...] * o[sq:2 * sq]
        o_ref[:, hs] = _subln(o, g_ref[...]).astype(o_ref.dtype)


def _attn_sample(q, kn, vn, cache_k, cache_v, lam_row, subln_g, batch, sq):
    d = q.shape[1]
    past = cache_k.shape[-3]
    ck = cache_k.reshape(batch, past * DA_HEADS, LANES)
    cv = cache_v.reshape(batch, past * DA_HEADS, LANES)
    q3, k3, v3 = (a.reshape(batch, sq, d) for a in (q, kn, vn))
    new = pl.BlockSpec((None, sq, d), lambda b: (b, 0, 0))
    old = pl.BlockSpec((None, past * DA_HEADS, LANES), lambda b: (b, 0, 0))
    vec = pl.BlockSpec((1, LANES), lambda b: (0, 0))
    out = pl.pallas_call(
        functools.partial(_attn_sample_kernel, past=past, sq=sq),
        grid=(batch,),
        in_specs=[vec, new, new, new, old, old, vec],
        out_specs=new,
        out_shape=jax.ShapeDtypeStruct((batch, sq, d), BF16),
        compiler_params=_cparams("parallel"),
        name="attn_sample",
    )(lam_row, q3, k3, v3, ck, cv, subln_g)
    return out.reshape(batch * sq, d)


RW_L = 64
RW_G = 2
RW_GW = RW_G * RW_HEAD
RW_GROUPS = 16 // RW_G
RW_CPS = 4


def _mm3(a, b, dot=_dot):
    ah, al = _split2(a)
    bh, bl = _split2(b)
    return dot(ah, bh) + dot(ah, bl) + dot(al, bh)


def _terms(x, n):
    return (x.astype(BF16),) if n == 1 else _split2(x)


def _mmt(at, bt, dot=_dot):
    acc = dot(at[0], bt[0])
    if len(bt) > 1:
        acc = acc + dot(at[0], bt[1])
    if len(at) > 1:
        acc = acc + dot(at[1], bt[0])
    return acc


RW_PREC = dict(a_all=(1, 1), neumann=(1, 1), av=(1, 1), pu=(1, 1), qy=(1, 1), mh=(1, 1), state=(1, 1))


def _softplus(z):
    return jnp.maximum(z, 0.0) + jnp.log(1.0 + jnp.exp(-jnp.abs(z)))


def _rwkv_kernel(rw_ref, sh0_ref, mu_ref, lw_ref, w0_ref, a0_ref, kk_ref, ka_ref, rk_ref, lng_ref, lnb_ref,
                 h0_ref, gm_ref, tri_ref, ob_ref, hout_ref, h_scr, last_scr, *, n_valid, cps):
    L = RW_L
    rows = cps * L
    c = pl.program_id(1)
    d = ob_ref.shape[-1]

    @pl.when(c == 0)
    def _():
        h_scr[...] = h0_ref[...]
        last_scr[...] = sh0_ref[...]

    rw = rw_ref[...]
    row_w = lax.broadcasted_iota(jnp.int32, rw.shape, 0)
    prev = jnp.where(row_w == 0, last_scr[...], pltpu.roll(rw, 1, 0))
    last_scr[...] = rw[rows - 1:rows, :]
    xm = rw + (prev - rw) * mu_ref[...]
    r = xm[:, 0:d]
    k = xm[:, d:2 * d]
    v = xm[:, 2 * d:3 * d]
    lo = xm[:, 3 * d:3 * d + LORA_PAD]
    lane_l = lax.broadcasted_iota(jnp.int32, lo.shape, 1)
    z = jnp.where(lane_l < W_LORA, jnp.tanh(lo),
                  jnp.where(lane_l < W_LORA + A_LORA, lo, jax.nn.sigmoid(lo)))
    lora = _dot(z.astype(BF16), lw_ref[...])
    w = -_softplus(-(w0_ref[...] + lora[:, 0:d])) - 0.5
    ld = -jnp.exp(w)
    a = jax.nn.sigmoid(a0_ref[...] + lora[:, d:2 * d])
    g = lora[:, 2 * d:3 * d]
    kkr = k * kk_ref[...]
    kmod = k * (1.0 + (a - 1.0) * ka_ref[...])
    rkk = r * kmod * rk_ref[...]
    gm = gm_ref[...]
    if n_valid < rows:
        valid = lax.broadcasted_iota(jnp.int32, ld.shape, 0) < n_valid
        ld = jnp.where(valid, ld, 0.0)
        kkr = jnp.where(valid, kkr, 0.0)
        kmod = jnp.where(valid, kmod, 0.0)

    tri = tri_ref[...]
    l1, l2, l3 = _split3(ld)
    cin = _dot(tri, l1) + _dot(tri, l2) + _dot(tri, l3)
    cex = cin - ld
    c_last = [cin[(ch + 1) * L - 1:(ch + 1) * L, :] for ch in range(cps)]
    e_in = jnp.exp(cin)
    e_ex = jnp.exp(cex)
    e_neg = jnp.exp(-cin)
    e_tail = jnp.exp(jnp.concatenate([jnp.broadcast_to(cl, (L, d)) for cl in c_last], axis=0) - cin)
    g_last = [jnp.exp(cl) for cl in c_last]

    GW = RW_GW
    row = lax.broadcasted_iota(jnp.int32, (L, GW), 0)
    lane = lax.broadcasted_iota(jnp.int32, (L, GW), 1)
    head_of_lane = lane // RW_HEAD
    tri_s = row > (lane % L)
    tri_i = row >= (lane % L)
    eye_pair = jnp.where(row == (lane % L), 1.0, 0.0).astype(F32)
    r2 = lax.broadcasted_iota(jnp.int32, (GW, GW), 0)
    c2 = lax.broadcasted_iota(jnp.int32, (GW, GW), 1)
    bd_mask = (r2 // RW_HEAD) == (c2 // RW_HEAD)
    diag_mask = r2 == c2

    def bd(x):
        z0 = jnp.zeros_like(x)
        return jnp.concatenate([jnp.where(head_of_lane == j, x, z0) for j in range(RW_G)], axis=0)

    def bdt(t):
        return tuple(bd(x) for x in t)

    def cat(ts, axis):
        return tuple(jnp.concatenate(xs, axis=axis) for xs in zip(*ts))

    prec = RW_PREC
    GL = RW_G * L
    items = [(slice(ch * L, (ch + 1) * L), slice(p * GW, (p + 1) * GW))
             for ch in range(cps) for p in range(RW_GROUPS)]
    pairs = range(len(items))

    v_p = [v[rs, sl] for rs, sl in items]
    kk_n = []
    for q in range(d // GMW):
        kk_raw = kkr[:, q * GMW:(q + 1) * GMW]
        ss = _group_mean(kk_raw * kk_raw, gm) * RW_HEAD
        kk_n.append(kk_raw / jnp.maximum(jnp.sqrt(ss), 1e-12))
    kk_n = jnp.concatenate(kk_n, axis=1)
    kk_p = [kk_n[rs, sl] for rs, sl in items]
    b_p = [kk_p[i] * a[rs, sl] for i, (rs, sl) in enumerate(items)]
    abar = [-kk_p[i] * e_ex[rs, sl] for i, (rs, sl) in enumerate(items)]
    rbar = [r[rs, sl] * e_in[rs, sl] for rs, sl in items]
    bbar = [b_p[i] * e_neg[rs, sl] for i, (rs, sl) in enumerate(items)]
    kbar = [kmod[rs, sl] * e_neg[rs, sl] for rs, sl in items]
    btil = [b_p[i] * e_tail[rs, sl] for i, (rs, sl) in enumerate(items)]
    ktil = [kmod[rs, sl] * e_tail[rs, sl] for rs, sl in items]

    na, nb = prec["a_all"]
    abar_t = [_terms(abar[p], max(na, prec["pu"][1])) for p in pairs]
    v_t = [_terms(v_p[p], max(prec["av"][1], prec["qy"][1], prec["mh"][1])) for p in pairs]
    a_all = []
    for p in pairs:
        lhs = cat([abar_t[p][:na], _terms(rbar[p], na)], 0)
        rhs = cat([bdt(_terms(bbar[p], nb)), bdt(_terms(kbar[p], nb))], 0)
        a_all.append(_mmt(lhs, rhs, _dot_nt))
    n_ab = [jnp.where(tri_s, a_all[p][0:L, 0:GL], 0.0) for p in pairs]
    a_ak = [jnp.where(tri_s, a_all[p][0:L, GL:2 * GL], 0.0) for p in pairs]
    a_rb = [jnp.where(tri_i, a_all[p][L:2 * L, 0:GL], 0.0) for p in pairs]
    a_rk = [jnp.where(tri_i, a_all[p][L:2 * L, GL:2 * GL], 0.0) for p in pairs]

    na, nb = prec["neumann"]
    t_inv = [eye_pair + n_ab[p] for p in pairs]
    pw_t = [_terms(n_ab[p], max(na, nb)) for p in pairs]
    pw = [_mmt(pw_t[p][:na], bdt(pw_t[p][:nb])) for p in pairs]
    for _ in range(int(math.log2(L)) - 2):
        pw_t = [_terms(pw[p], max(na, nb)) for p in pairs]
        st = [_mmt(cat([_terms(t_inv[p], na), pw_t[p][:na]], 0), bdt(pw_t[p][:nb])) for p in pairs]
        t_inv = [t_inv[p] + st[p][0:L] for p in pairs]
        pw = [st[p][L:2 * L] for p in pairs]
    t_inv = [t_inv[p] + _mmt(_terms(t_inv[p], na), bdt(_terms(pw[p], nb))) for p in pairs]

    na, nb = prec["av"]
    vv = [_mmt(cat([_terms(a_ak[p], na), _terms(a_rk[p], na)], 0), bdt(v_t[p][:nb])) for p in pairs]
    av = [x[0:L] for x in vv]
    arkv = [x[L:2 * L] for x in vv]
    na, nb = prec["pu"]
    pu = [_mmt(_terms(t_inv[p], na), cat([bdt(abar_t[p][:nb]), bdt(_terms(av[p], nb))], 1)) for p in pairs]
    p_m = [x[:, 0:GW] for x in pu]
    u0 = [x[:, GW:2 * GW] for x in pu]
    na, nb = prec["qy"]
    nmh = prec["mh"][1]
    p_t = [_terms(p_m[p], max(nb, nmh)) for p in pairs]
    u_t = [_terms(u0[p], max(nb, nmh)) for p in pairs]
    qy = [_mmt(_terms(a_rb[p], na), cat([bdt(p_t[p][:nb]), bdt(u_t[p][:nb])], 1)) for p in pairs]
    q_m = [rbar[p] + qy[p][:, 0:GW] for p in pairs]
    y0 = [qy[p][:, GW:2 * GW] + arkv[p] for p in pairs]

    na, nb = prec["mh"]
    btil_t = [_terms(btil[p], na) for p in pairs]
    m_full = [_mmt(btil_t[p], p_t[p][:nb], _dot_tn) for p in pairs]
    h0_full = [_mmt(cat([btil_t[p], _terms(ktil[p], na)], 0), cat([u_t[p][:nb], v_t[p][:nb]], 0), _dot_tn)
               for p in pairs]
    m_bd = [jnp.where(bd_mask, m_full[i], 0.0) + jnp.where(diag_mask, g_last[i // RW_GROUPS][:, sl], 0.0)
            for i, (rs, sl) in enumerate(items)]
    h0_bd = [jnp.where(bd_mask, h0_full[p], 0.0) for p in pairs]

    na, nb = prec["state"]
    h_cur = [h_scr[p] for p in range(RW_GROUPS)]
    y = []
    for ch in range(cps):
        idx = [ch * RW_GROUPS + p for p in range(RW_GROUPS)]
        st = [_mmt(cat([_terms(q_m[i], na), _terms(m_bd[i], na)], 0), _terms(h_cur[p], nb))
              for p, i in enumerate(idx)]
        y.extend(st[p][0:L] + y0[i] for p, i in enumerate(idx))
        h_cur = [st[p][L:L + GW] + h0_bd[i] for p, i in enumerate(idx)]
    for p in range(RW_GROUPS):
        h_scr[p] = h_cur[p]

    per = GMW // GW
    for ch in range(cps):
        rs = slice(ch * L, (ch + 1) * L)
        for q in range(RW_GROUPS // per):
            sl = slice(q * GMW, (q + 1) * GMW)
            yq = jnp.concatenate([y[ch * RW_GROUPS + q * per + j] for j in range(per)], axis=1)
            mu_y = _group_mean(yq, gm)
            dy = yq - mu_y
            var = _dot((dy * dy).astype(BF16), gm)
            yn = dy * lax.rsqrt(var + RW_GN_EPS) * lng_ref[:, sl] + lnb_ref[:, sl]
            bonus = _dot(rkk[rs, sl].astype(BF16), gm) * RW_HEAD * v[rs, sl]
            ob_ref[rs, sl] = ((yn + bonus) * g[rs, sl]).astype(ob_ref.dtype)

    @pl.when(c == pl.num_programs(1) - 1)
    def _():
        hout_ref[...] = h_scr[...]


def _rwkv(rw3, shift0, h0_bd, prm, gm, n_valid, cps):
    batch, seq, wcols = rw3.shape
    d = prm["w0"].shape[1]
    rows = cps * RW_L
    nc = seq // rows
    tri = jnp.kron(jnp.eye(cps, dtype=F32), jnp.tril(jnp.ones((RW_L, RW_L), F32))).astype(BF16)
    const2 = lambda b, c: (0, 0)
    vec = pl.BlockSpec((1, d), const2)
    st = pl.BlockSpec((None, RW_GROUPS, RW_GW, RW_GW), lambda b, c: (b, 0, 0, 0))
    return pl.pallas_call(
        functools.partial(_rwkv_kernel, n_valid=n_valid, cps=cps),
        grid=(batch, nc),
        in_specs=[pl.BlockSpec((None, rows, wcols), lambda b, c: (b, c, 0)),
                  pl.BlockSpec((None, 1, wcols), lambda b, c: (b, 0, 0)),
                  pl.BlockSpec((1, wcols), const2),
                  pl.BlockSpec(prm["lw"].shape, const2),
                  vec, vec, vec, vec, vec, vec, vec, st,
                  pl.BlockSpec((GMW, GMW), const2),
                  pl.BlockSpec((rows, rows), const2)],
        out_specs=[pl.BlockSpec((None, rows, d), lambda b, c: (b, c, 0)), st],
        out_shape=[jax.ShapeDtypeStruct((batch, seq, d), BF16),
                   jax.ShapeDtypeStruct((batch, RW_GROUPS, RW_GW, RW_GW), F32)],
        scratch_shapes=[pltpu.VMEM((RW_GROUPS, RW_GW, RW_GW), F32), pltpu.VMEM((1, wcols), F32)],
        compiler_params=_cparams("parallel", "arbitrary"),
        name="rwkv7",
    )(rw3, shift0, prm["mu"], prm["lw"], prm["w0"], prm["a0"], prm["k_k"], prm["k_a"], prm["r_k"],
      prm["ln_g"], prm["ln_b"], h0_bd, gm, tri)


def _state_to_bd(s):
    b = s.shape[0]
    ht = jnp.swapaxes(s, -1, -2).reshape(b, RW_GROUPS, RW_G, RW_HEAD, RW_HEAD)
    z = jnp.zeros_like(ht[:, :, 0])
    rows = [jnp.concatenate([ht[:, :, j] if i == j else z for i in range(RW_G)], axis=-1) for j in range(RW_G)]
    return jnp.concatenate(rows, axis=-2)


def _bd_to_state(hbd):
    b = hbd.shape[0]
    blocks = [hbd[:, :, j * RW_HEAD:(j + 1) * RW_HEAD, j * RW_HEAD:(j + 1) * RW_HEAD] for j in range(RW_G)]
    ht = jnp.stack(blocks, axis=2).reshape(b, RW_G * RW_GROUPS, RW_HEAD, RW_HEAD)
    return jnp.swapaxes(ht, -1, -2)


def _route(logits):
    lane = lax.broadcasted_iota(jnp.int32, logits.shape, 1).astype(F32)
    big = float(LANES)
    lg = jnp.where(lane < N_GROUPS, logits, NEG_INF)
    mg = jnp.max(lg, axis=-1, keepdims=True)
    sg = jnp.sum(jnp.exp(lg - mg), axis=-1, keepdims=True)
    p_top = 1.0 / sg
    g_idx = jnp.min(jnp.where(lg == mg, lane, big), axis=-1, keepdims=True)
    e0 = N_GROUPS + EXPERTS_PER_GROUP * g_idx
    emask = jnp.where(lane >= e0, jnp.where(lane < e0 + EXPERTS_PER_GROUP, 1.0, 0.0), 0.0) > 0.5
    le = jnp.where(emask, logits, NEG_INF)
    me = jnp.max(le, axis=-1, keepdims=True)
    ee = jnp.exp(le - me)
    pe = ee / jnp.sum(ee, axis=-1, keepdims=True)
    pe = jnp.where(emask, pe, -1.0)
    v1 = jnp.max(pe, axis=-1, keepdims=True)
    i1 = jnp.min(jnp.where(pe == v1, lane, big), axis=-1, keepdims=True)
    pe2 = jnp.where(lane == i1, -1.0, pe)
    v2 = jnp.max(pe2, axis=-1, keepdims=True)
    i2 = jnp.min(jnp.where(pe2 == v2, lane, big), axis=-1, keepdims=True)
    den = v1 + v2
    ew = jnp.where(lane == i1, v1 / den, 0.0) + jnp.where(lane == i2, v2 / den, 0.0)
    return p_top * ew + jnp.where(lane == 0.0, g_idx, 0.0)


def _merge_kernel(oa_ref, ob_ref, gates_ref, x_ref, gt1_ref, sc2_ref, sh2_ref, g2_ref,
                  woa_ref, wob_ref, wout_ref, wr_ref, br_ref, x1_ref, hx_ref):
    d = x_ref.shape[1]
    a_out = _dot(oa_ref[...], woa_ref[...])
    b_out = _dot(ob_ref[...], wob_ref[...])
    merged = gates_ref[:, 0:d].astype(F32) * a_out + gates_ref[:, d:2 * d].astype(F32) * b_out
    x1 = x_ref[...] + gt1_ref[...] * _dot(merged.astype(BF16), wout_ref[...])
    x1_ref[...] = x1
    ms = jnp.mean(x1 * x1, axis=-1, keepdims=True)
    h2 = x1 * lax.rsqrt(ms + EPS) * g2_ref[...]
    h2 = h2 * (1.0 + sc2_ref[...]) + sh2_ref[...]
    logits = _mm3(h2, wr_ref[...]) + br_ref[...]
    hx_ref[:, 0:d] = h2
    hx_ref[:, d:d + LANES] = _route(logits)


def _merge(oa, ob, gates, x2d, gt1, sc2, sh2, g2, w_oa, w_ob, w_out, wr, br, tm, seq):
    t, d = x2d.shape
    row = lambda i: (i, 0)
    const = lambda i: (0, 0)
    full = lambda w: pl.BlockSpec(w.shape, const)
    return pl.pallas_call(
        _merge_kernel,
        grid=(t // tm,),
        in_specs=[pl.BlockSpec((tm, d), row), pl.BlockSpec((tm, d), row), pl.BlockSpec((tm, 2 * d), row),
                  pl.BlockSpec((tm, d), row), _mod_spec(gt1, tm, seq), _mod_spec(sc2, tm, seq),
                  _mod_spec(sh2, tm, seq), pl.BlockSpec((1, d), const),
                  full(w_oa), full(w_ob), full(w_out), full(wr), full(br)],
        out_specs=[pl.BlockSpec((tm, d), row), pl.BlockSpec((tm, d + LANES), row)],
        out_shape=[jax.ShapeDtypeStruct((t, d), F32), jax.ShapeDtypeStruct((t, d + LANES), F32)],
        compiler_params=_cparams("parallel"),
        name="merge_router",
    )(oa, ob, gates, x2d, gt1, sc2, sh2, g2, w_oa, w_ob, w_out, wr, br)


def _expert_ffn(xb, comb, lane_e, w13, w2):
    au = _dot(xb, w13)
    f = au.shape[1] // 2
    a = au[:, 0:f]
    u = au[:, f:2 * f]
    lane = lax.broadcasted_iota(jnp.int32, comb.shape, 1)
    cw = jnp.sum(jnp.where(lane == lane_e, comb, 0.0), axis=-1, keepdims=True)
    act = a * jax.nn.sigmoid(a) * u * cw
    return _dot(act.astype(BF16), w2)


def _moe_dense_kernel(hx_ref, w13_ref, w2_ref, x1_ref, gt2_ref, y_ref, xb_scr, acc_scr):
    e = pl.program_id(1)
    d = y_ref.shape[1]

    @pl.when(e == 0)
    def _():
        xb_scr[...] = hx_ref[:, 0:d].astype(BF16)
        acc_scr[...] = jnp.zeros(acc_scr.shape, F32)

    acc_scr[...] += _expert_ffn(xb_scr[...], hx_ref[:, d:d + LANES], e + N_GROUPS, w13_ref[...], w2_ref[...])

    @pl.when(e == pl.num_programs(1) - 1)
    def _():
        y_ref[...] = x1_ref[...] + gt2_ref[...] * acc_scr[...]


def _moe_dense(hx, w13, w2, x1, gt2, tm, seq):
    t, d = x1.shape
    ne, _, f2 = w13.shape
    row = lambda i, e: (i, 0)
    return pl.pallas_call(
        _moe_dense_kernel,
        grid=(t // tm, ne),
        in_specs=[pl.BlockSpec((tm, d + LANES), row),
                  pl.BlockSpec((None, d, f2), lambda i, e: (e, 0, 0)),
                  pl.BlockSpec((None, f2 // 2, d), lambda i, e: (e, 0, 0)),
                  pl.BlockSpec((tm, d), row), _mod_spec(gt2, tm, seq)],
        out_specs=pl.BlockSpec((tm, d), row),
        out_shape=jax.ShapeDtypeStruct((t, d), F32),
        scratch_shapes=[pltpu.VMEM((tm, d), BF16), pltpu.VMEM((tm, d), F32)],
        compiler_params=_cparams("parallel", "arbitrary"),
        name="moe_dense",
    )(hx, w13, w2, x1, gt2)


MOE_TM = 1024
MOE_TS = 1024


def _plan_kernel(rt_ref, tri_ref, rank_ref, cnt_ref, carry_scr):
    @pl.when(pl.program_id(0) == 0)
    def _():
        carry_scr[...] = jnp.zeros(carry_scr.shape, F32)

    rt = rt_ref[...]
    lane = lax.broadcasted_iota(jnp.int32, rt.shape, 1).astype(F32)
    onehot = jnp.where(lane == rt[:, 0:1], 1.0, 0.0)
    before = _dot(tri_ref[...], onehot.astype(BF16)) + carry_scr[...]
    rank_ref[...] = jnp.sum(onehot * before, axis=-1, keepdims=True).astype(jnp.int32)
    carry_scr[...] += jnp.sum(onehot, axis=0, keepdims=True)
    cnt_ref[...] = carry_scr[...]


def _moe_plan(hx, d):
    t = hx.shape[0]
    tp = 512
    tri = jnp.tril(jnp.ones((tp, tp), F32), -1).astype(BF16)
    return pl.pallas_call(
        _plan_kernel,
        grid=(t // tp,),
        in_specs=[pl.BlockSpec((tp, LANES), lambda i: (i, d // LANES)),
                  pl.BlockSpec((tp, tp), lambda i: (0, 0))],
        out_specs=[pl.BlockSpec((tp, 1), lambda i: (i, 0)), pl.BlockSpec((1, LANES), lambda i: (0, 0))],
        out_shape=[jax.ShapeDtypeStruct((t, 1), jnp.int32), jax.ShapeDtypeStruct((1, LANES), F32)],
        scratch_shapes=[pltpu.VMEM((1, LANES), F32)],
        compiler_params=_cparams("arbitrary"),
        name="moe_plan",
    )(hx, tri)


def _row_copy(src_hbm, src_row, dst, dst_row, sem):
    return pltpu.make_async_copy(src_hbm.at[pl.ds(src_row, 1), :], dst.at[pl.ds(dst_row, 1), :], sem)


def _dispatch_kernel(pos_ref, hx_ref, xs_in_hbm, xs_hbm, sem):
    del xs_in_hbm

    def issue(t, carry):
        _row_copy(hx_ref, t, xs_hbm, pos_ref[t], sem).start()
        return carry

    def drain(t, carry):
        _row_copy(hx_ref, t, xs_hbm, pos_ref[t], sem).wait()
        return carry

    lax.fori_loop(0, MOE_TS, issue, 0, unroll=8)
    lax.fori_loop(0, MOE_TS, drain, 0, unroll=8)


def _moe_dispatch(pos, hx, n_rows):
    t, w = hx.shape
    xs0 = jnp.zeros((n_rows, w), F32)
    return pl.pallas_call(
        _dispatch_kernel,
        grid=(t // MOE_TS,),
        in_specs=[pl.BlockSpec((MOE_TS,), lambda i: (i,), memory_space=pltpu.SMEM),
                  pl.BlockSpec((MOE_TS, w), lambda i: (i, 0)), pl.BlockSpec(memory_space=pl.ANY)],
        out_specs=pl.BlockSpec(memory_space=pl.ANY),
        out_shape=jax.ShapeDtypeStruct((n_rows, w), F32),
        scratch_shapes=[pltpu.SemaphoreType.DMA(())],
        input_output_aliases={2: 0},
        compiler_params=_cparams("arbitrary"),
        name="moe_dispatch",
    )(pos, hx, xs0)


def _moe_bucket_kernel(tg_ref, tv_ref, xs_ref, w13_ref, w2_ref, ys_ref, xb_scr):
    i = pl.program_id(0)
    e = pl.program_id(1)
    d = ys_ref.shape[1]

    @pl.when(e == 0)
    def _():
        xb_scr[...] = xs_ref[:, 0:d].astype(BF16)
        ys_ref[...] = jnp.zeros(ys_ref.shape, F32)

    @pl.when(tv_ref[i] > 0)
    def _():
        lane_e = N_GROUPS + tg_ref[i] * EXPERTS_PER_GROUP + e
        ys_ref[...] += _expert_ffn(xb_scr[...], xs_ref[:, d:d + LANES], lane_e, w13_ref[...], w2_ref[...])


def _moe_buckets(tile_group, tile_valid, xs, w13, w2, d):
    n_rows, w = xs.shape
    _, _, f2 = w13.shape
    row = lambda i, e, tg, tv: (i, 0)
    wsel = lambda i, e, tg, tv: (tg[i] * EXPERTS_PER_GROUP + e, 0, 0)
    return pl.pallas_call(
        _moe_bucket_kernel,
        grid_spec=pltpu.PrefetchScalarGridSpec(
            num_scalar_prefetch=2,
            grid=(n_rows // MOE_TM, EXPERTS_PER_GROUP),
            in_specs=[pl.BlockSpec((MOE_TM, w), row),
                      pl.BlockSpec((None, d, f2), wsel), pl.BlockSpec((None, f2 // 2, d), wsel)],
            out_specs=pl.BlockSpec((MOE_TM, d), row),
            scratch_shapes=[pltpu.VMEM((MOE_TM, d), BF16)]),
        out_shape=jax.ShapeDtypeStruct((n_rows, d), F32),
        compiler_params=_cparams("parallel", "arbitrary"),
        name="moe_buckets",
    )(tile_group, tile_valid, xs, w13, w2)


def _collect_kernel(pos_ref, ys_hbm, x1_ref, gt2_ref, y_ref, buf, sem):
    tu = buf.shape[0]

    def issue(t, carry):
        _row_copy(ys_hbm, pos_ref[t], buf, t, sem).start()
        return carry

    def drain(t, carry):
        _row_copy(ys_hbm, pos_ref[t], buf, t, sem).wait()
        return carry

    lax.fori_loop(0, tu, issue, 0, unroll=8)
    lax.fori_loop(0, tu, drain, 0, unroll=8)
    y_ref[...] = x1_ref[...] + gt2_ref[...] * buf[...]


def _moe_collect(pos, ys, x1, gt2, seq):
    t, d = x1.shape
    tu = MOE_TS
    return pl.pallas_call(
        _collect_kernel,
        grid=(t // tu,),
        in_specs=[pl.BlockSpec((tu,), lambda i: (i,), memory_space=pltpu.SMEM),
                  pl.BlockSpec(memory_space=pl.ANY),
                  pl.BlockSpec((tu, d), lambda i: (i, 0)), _mod_spec(gt2, tu, seq)],
        out_specs=pl.BlockSpec((tu, d), lambda i: (i, 0)),
        out_shape=jax.ShapeDtypeStruct((t, d), F32),
        scratch_shapes=[pltpu.VMEM((tu, d), F32), pltpu.SemaphoreType.DMA(())],
        compiler_params=_cparams("arbitrary"),
        name="moe_collect",
    )(pos, ys, x1, gt2)


def _moe_sorted(hx, w13, w2, x1, gt2, seq):
    t, d = x1.shape
    rank, cnt = _moe_plan(hx, d)
    group = hx[:, d].astype(jnp.int32)
    counts = cnt[0, 0:N_GROUPS].astype(jnp.int32)
    padded = (counts + MOE_TM - 1) // MOE_TM * MOE_TM
    ends = jnp.cumsum(padded)
    starts = ends - padded
    pos = jnp.take(starts, group) + rank[:, 0]
    n_tiles = t // MOE_TM + N_GROUPS
    tile_start = jnp.arange(n_tiles, dtype=jnp.int32) * MOE_TM
    tile_group = jnp.minimum(jnp.sum((tile_start[:, None] >= ends[None, :]).astype(jnp.int32), axis=1),
                             N_GROUPS - 1)
    tile_valid = (tile_start < ends[-1]).astype(jnp.int32)
    xs = _moe_dispatch(pos, hx, n_tiles * MOE_TM)
    ys = _moe_buckets(tile_group, tile_valid, xs, w13, w2, d)
    return _moe_collect(pos, ys, x1, gt2, seq)


def _rope_tables(pos0, seq, reps):
    pos = (pos0 + jnp.arange(seq)).astype(F32)
    inv = ROPE_THETA ** (-jnp.arange(0, DA_HEAD_DIM, 2, dtype=F32) / DA_HEAD_DIM)
    ang = pos[:, None] * inv[None, :]
    cos, sin = jnp.cos(ang), jnp.sin(ang)
    cos_t = jnp.tile(jnp.concatenate([cos, cos], axis=-1), (reps, GMW // DA_HEAD_DIM))
    sin_t = jnp.tile(jnp.concatenate([-sin, sin], axis=-1), (reps, GMW // DA_HEAD_DIM))
    return cos_t, sin_t


def _layer(x, mod, pos0, past, wts):
    bx, sx, d = x.shape
    t = bx * sx
    x2d = x.reshape(t, d)
    per_batch = sx >= 512
    tm = 512 if per_batch else t

    def modv(i):
        if per_batch:
            return mod[:, i:i + 1, :]
        return jnp.repeat(mod[:, i, :], sx, axis=0)[None]

    sh1, sc1, gt1, sh2, sc2, gt2 = (modv(i) for i in range(6))
    cos_t, sin_t = _rope_tables(pos0, sx, 1 if per_batch else bx)

    tm_big = 1024 if (per_batch and sx % 1024 == 0) else tm
    h = _prenorm(x2d, sc1, sh1, wts["norm1_g"], tm, sx)
    q, k32, kb, v32, vb = _qkv(h, wts["w_qkv"], wts["qg"], wts["kg"], cos_t, sin_t, wts["gm"], tm_big)
    rw = _proj(h, wts["w_rw"], tm_big, wts["w_rw"].shape[1] // 3, None, F32, "rw_proj")
    gates = _proj(h, wts["w_gate"], tm_big, d, "sigmoid", BF16, "gate_proj")

    if past is None:
        oa = _attn_prompt(q, kb, vb, wts["lam"], wts["subln_g"], bx, sx, min(1024, sx), min(1024, sx))
        shift0 = jnp.zeros((bx, 1, rw.shape[1]), F32)
        h0_bd = jnp.zeros((bx, RW_GROUPS, RW_GW, RW_GW), F32)
    else:
        ck, cv, s0, sh0 = past
        oa = _attn_sample(q, kb, vb, ck, cv, wts["lam"], wts["subln_g"], bx, sx)
        shift0 = jnp.pad(sh0, ((0, 0), (0, 0), (0, rw.shape[1] - sh0.shape[-1])))
        h0_bd = _state_to_bd(s0.astype(F32))

    rw3 = rw.reshape(bx, sx, rw.shape[1])
    if sx % RW_L == 0:
        cps = RW_CPS if sx % (RW_CPS * RW_L) == 0 else 1
        rw_in, n_valid = rw3, cps * RW_L
    else:
        assert sx < RW_L
        cps = 1
        rw_in, n_valid = jnp.pad(rw3, ((0, 0), (0, RW_L - sx), (0, 0))), sx
    ob, h_bd = _rwkv(rw_in, shift0, h0_bd, wts, wts["gm"], n_valid, cps)
    ob = ob[:, 0:sx, :].reshape(t, d)

    x1, hx = _merge(oa, ob, gates, x2d, gt1, sc2, sh2, wts["norm2_g"],
                    wts["w_oa"], wts["w_ob"], wts["w_out"], wts["wr"], wts["br"], tm, sx)
    if per_batch and sx % MOE_TS == 0:
        y = _moe_sorted(hx, wts["w13"], wts["w2"], x1, gt2, sx)
    else:
        y = _moe_dense(hx, wts["w13"], wts["w2"], x1, gt2, tm, sx)

    n_cols = 3 * d + W_LORA + A_LORA + G_LORA
    return (y.reshape(bx, sx, d),
            k32.reshape(1, bx, sx, DA_HEADS, 2 * DA_HEAD_DIM),
            v32.reshape(1, bx, sx, DA_HEADS, 2 * DA_HEAD_DIM),
            _bd_to_state(h_bd)[None],
            rw3[:, sx - 1:sx, 0:n_cols][None])


def kernel(x_prompt, x_sample, cache_k, cache_v, state_wkv, state_shift, c_prompt, c_sample, norm1_g, norm2_g, w_ada, b_ada, w_in, da_qn_g, da_kn_g, da_lambda, da_subln_g, w_oa, rw_mu, rw_w0, rw_w2, rw_a0, rw_a2, rw_g2, rw_k_k, rw_k_a, rw_r_k, rw_ln_g, rw_ln_b, w_ob, w_out, router_g, router_g_b, router_e, router_e_b, exp_w1, exp_w3, exp_w2):
    b, s, d = x_prompt.shape
    bs = x_sample.shape[0]
    past_len = cache_k.shape[2]
    assert w_in.shape[0] == 1, "single layer"

    c_all = jnp.concatenate([c_prompt, c_sample], axis=0)
    c_all = jnp.pad(c_all, ((0, (-c_all.shape[0]) % 8), (0, 0)))
    mod, lam_tile = _adaln(c_all, w_ada[0], b_ada[0], da_lambda[0])
    mod_p = mod[0:b].reshape(b, 6, d)
    mod_s = mod[b:b + bs].reshape(bs, 6, d)

    win = w_in[0]
    rw_cols = 3 * d + W_LORA + A_LORA + G_LORA
    rw_pad = 3 * d + LORA_PAD
    w_rw = jnp.pad(win[:, 3 * d:3 * d + rw_cols], ((0, 0), (0, rw_pad - rw_cols))).astype(BF16)
    lw = jnp.zeros((LORA_PAD, 3 * d), F32)
    lw = lw.at[0:W_LORA, 0:d].set(rw_w2[0])
    lw = lw.at[W_LORA:W_LORA + A_LORA, d:2 * d].set(rw_a2[0])
    lw = lw.at[W_LORA + A_LORA:W_LORA + A_LORA + G_LORA, 2 * d:3 * d].set(rw_g2[0])
    f = exp_w1.shape[-1]
    wr = jnp.pad(jnp.concatenate([router_g[0], router_e[0]], axis=1), ((0, 0), (0, LANES - N_GROUPS - N_EXPERTS)))
    br = jnp.pad(jnp.concatenate([router_g_b[0], router_e_b[0]]), (0, LANES - N_GROUPS - N_EXPERTS))[None]
    half = jnp.ones((DA_HEAD_DIM, DA_HEAD_DIM), F32) / DA_HEAD_DIM
    wts = dict(
        norm1_g=norm1_g[0][None], norm2_g=norm2_g[0][None],
        w_qkv=win[:, 0:3 * d].astype(BF16), w_rw=w_rw, w_gate=win[:, 3 * d + rw_cols:].astype(BF16),
        qg=jnp.tile(da_qn_g[0], GMW // DA_HEAD_DIM)[None], kg=jnp.tile(da_kn_g[0], GMW // DA_HEAD_DIM)[None],
        gm=jnp.kron(jnp.eye(GMW // DA_HEAD_DIM, dtype=F32), half).astype(BF16),
        lam=lam_tile[0:1], subln_g=da_subln_g[0][None],
        mu=jnp.pad(rw_mu[0], (0, rw_pad - rw_cols))[None], lw=lw.astype(BF16),
        w0=rw_w0[0][None], a0=rw_a0[0][None], k_k=rw_k_k[0][None], k_a=rw_k_a[0][None],
        r_k=rw_r_k[0].reshape(1, d), ln_g=rw_ln_g[0][None], ln_b=rw_ln_b[0][None],
        w_oa=w_oa[0].astype(BF16), w_ob=w_ob[0].astype(BF16), w_out=w_out[0].astype(BF16),
        wr=wr, br=br,
        w13=jnp.concatenate([exp_w1[0], exp_w3[0]], axis=-1).reshape(N_EXPERTS, d, 2 * f).astype(BF16),
        w2=exp_w2[0].reshape(N_EXPERTS, f, d).astype(BF16),
    )

    out_p = _layer(x_prompt, mod_p, 0, None, wts)
    out_s = _layer(x_sample, mod_s, past_len,
                   (cache_k, cache_v, state_wkv[0], state_shift[0]), wts)
    return (out_p[0], out_s[0], out_p[1], out_p[2], out_p[3], out_p[4],
            out_s[1], out_s[2], out_s[3], out_s[4])
```

```python
import functools
import math

import jax
import jax.numpy as jnp
from jax import lax
from jax.experimental import pallas as pl
from jax.experimental.pallas import tpu as pltpu

F32 = jnp.float32
BF16 = jnp.bfloat16

EPS = 1e-6
NEG_INF = -1e30
CHUNK = 64
DA_HEADS = 8
DA_HEAD_DIM = 64
ROPE_THETA = 10000.0
RW_HEAD = 64
RW_GN_EPS = 64e-5
W_LORA, A_LORA, G_LORA = 64, 64, 160
N_GROUPS, EXPERTS_PER_GROUP = 4, 8
N_EXPERTS = N_GROUPS * EXPERTS_PER_GROUP
LAM_INIT = 0.8 - 0.6 * math.exp(-0.3 * 0)

LANES = 128
LORA_PAD = 384
VMEM_LIMIT = 56 * 1024 * 1024


def _cparams(*sem):
    return pltpu.CompilerParams(dimension_semantics=sem, vmem_limit_bytes=VMEM_LIMIT)


def _dot(a, b):
    return jnp.dot(a, b, preferred_element_type=F32)


def _dot_nt(a, b):
    return lax.dot_general(a, b, (((1,), (1,)), ((), ())), preferred_element_type=F32)


def _dot_tn(a, b):
    return lax.dot_general(a, b, (((0,), (0,)), ((), ())), preferred_element_type=F32)


def _split2(a):
    hi = a.astype(BF16)
    lo = (a - hi.astype(F32)).astype(BF16)
    return hi, lo


def _split3(a):
    hi = a.astype(BF16)
    r = a - hi.astype(F32)
    mid = r.astype(BF16)
    lo = (r - mid.astype(F32)).astype(BF16)
    return hi, mid, lo


GMW = 256


def _group_mean(sq, gm):
    hi, lo = _split2(sq)
    return _dot(hi, gm) + _dot(lo, gm)


def _adaln_kernel(c_ref, w_ref, b_ref, l_ref, o_ref, lam_ref):
    c = c_ref[...]
    sc = (c * jax.nn.sigmoid(c)).astype(BF16)
    o_ref[...] = _dot(sc, w_ref[...].astype(BF16)) + b_ref[...]
    l = l_ref[...]
    s1 = jnp.sum(l[0:1] * l[1:2], axis=-1, keepdims=True)
    s2 = jnp.sum(l[2:3] * l[3:4], axis=-1, keepdims=True)
    lam = jnp.exp(s1) - jnp.exp(s2) + LAM_INIT
    lam_ref[...] = jnp.broadcast_to(lam, lam_ref.shape)


def _adaln(c_all, w_ada, b_ada, da_lambda):
    bp, d = c_all.shape
    n = w_ada.shape[1]
    tn = 1024
    return pl.pallas_call(
        _adaln_kernel,
        grid=(n // tn,),
        in_specs=[pl.BlockSpec((bp, d), lambda j: (0, 0)),
                  pl.BlockSpec((d, tn), lambda j: (0, j)),
                  pl.BlockSpec((1, tn), lambda j: (0, j)),
                  pl.BlockSpec(da_lambda.shape, lambda j: (0, 0))],
        out_specs=[pl.BlockSpec((bp, tn), lambda j: (0, j)),
                   pl.BlockSpec((8, LANES), lambda j: (0, 0))],
        out_shape=[jax.ShapeDtypeStruct((bp, n), F32), jax.ShapeDtypeStruct((8, LANES), F32)],
        compiler_params=_cparams("arbitrary"),
        name="adaln",
    )(c_all, w_ada, b_ada.reshape(1, n), da_lambda)


def _mod_spec(mod, tm, seq):
    d = mod.shape[-1]
    if mod.shape[1] == 1:
        per = seq // tm
        return pl.BlockSpec((None, 1, d), lambda i, *_: (i // per, 0, 0))
    return pl.BlockSpec((None, tm, d), lambda i, *_: (0, i, 0))


def _qkv_kernel(x_ref, sc_ref, sh_ref, g_ref, w_ref, qg_ref, kg_ref, cos_ref, sin_ref, gm_ref,
                h_ref, q_ref, k_ref, kb_ref, v_ref, vb_ref):
    j = pl.program_id(1)

    @pl.when(j == 0)
    def _():
        x = x_ref[...]
        ms = jnp.mean(x * x, axis=-1, keepdims=True)
        hn = x * lax.rsqrt(ms + EPS) * g_ref[...]
        h_ref[...] = (hn * (1.0 + sc_ref[...]) + sh_ref[...]).astype(h_ref.dtype)

    acc = _dot(h_ref[...], w_ref[...])
    def norm_rope(gain_ref, scale, write):
        lane = lax.broadcasted_iota(jnp.int32, (acc.shape[0], LANES), 1)
        first_half = (lane % DA_HEAD_DIM) < (DA_HEAD_DIM // 2)
        for c2 in range(acc.shape[1] // GMW):
            a2 = acc[:, c2 * GMW:(c2 + 1) * GMW]
            ms2 = _dot((a2 * a2).astype(BF16), gm_ref[...])
            y2 = a2 * lax.rsqrt(ms2 + EPS) * gain_ref[...]
            for hf in range(GMW // LANES):
                y = y2[:, hf * LANES:(hf + 1) * LANES]
                rot = jnp.where(first_half, pltpu.roll(y, LANES - DA_HEAD_DIM // 2, 1),
                                pltpu.roll(y, DA_HEAD_DIM // 2, 1))
                write(c2 * (GMW // LANES) + hf, (y * cos_ref[:, 0:LANES] + rot * sin_ref[:, 0:LANES]) * scale)

    @pl.when(j == 0)
    def _():
        def write(c, val):
            q_ref[:, c * LANES:(c + 1) * LANES] = val.astype(q_ref.dtype)
        norm_rope(qg_ref, DA_HEAD_DIM ** -0.5 * math.log2(math.e), write)

    @pl.when(j == 1)
    def _():
        def write(c, val):
            k_ref[:, c * LANES:(c + 1) * LANES] = val
            kb_ref[:, c * LANES:(c + 1) * LANES] = val.astype(BF16)
        norm_rope(kg_ref, 1.0, write)

    @pl.when(j == 2)
    def _():
        v_ref[...] = acc
        vb_ref[...] = acc.astype(BF16)


def _qkv(x2d, sc, sh, g, w_qkv, qg, kg, cos_t, sin_t, gm, tm, seq):
    t, d = x2d.shape
    n = 1024
    per = cos_t.shape[0] // tm
    row = lambda i, j: (i, 0)
    return pl.pallas_call(
        _qkv_kernel,
        grid=(t // tm, 3),
        in_specs=[pl.BlockSpec((tm, d), row), _mod_spec(sc, tm, seq), _mod_spec(sh, tm, seq),
                  pl.BlockSpec((1, d), lambda i, j: (0, 0)),
                  pl.BlockSpec((d, n), lambda i, j: (0, j)),
                  pl.BlockSpec((1, GMW), lambda i, j: (0, 0)),
                  pl.BlockSpec((1, GMW), lambda i, j: (0, 0)),
                  pl.BlockSpec((tm, LANES), lambda i, j: (i % per, 0)),
                  pl.BlockSpec((tm, LANES), lambda i, j: (i % per, 0)),
                  pl.BlockSpec((GMW, GMW), lambda i, j: (0, 0))],
        out_specs=[pl.BlockSpec((tm, d), row)] + [pl.BlockSpec((tm, n), row)] * 5,
        out_shape=[jax.ShapeDtypeStruct((t, d), BF16),
                   jax.ShapeDtypeStruct((t, n), BF16), jax.ShapeDtypeStruct((t, n), F32),
                   jax.ShapeDtypeStruct((t, n), BF16), jax.ShapeDtypeStruct((t, n), F32),
                   jax.ShapeDtypeStruct((t, n), BF16)],
        compiler_params=_cparams("parallel", "arbitrary"),
        name="qkv_proj",
    )(x2d, sc, sh, g, w_qkv, qg, kg, cos_t, sin_t, gm)


def _proj_kernel(h_ref, w_ref, o_ref, *, act):
    acc = _dot(h_ref[...], w_ref[...])
    if act == "sigmoid":
        acc = 0.5 * jnp.tanh(0.5 * acc) + 0.5
    o_ref[...] = acc.astype(o_ref.dtype)


def _proj(h, w, tm, tn, act, out_dtype, name):
    t, d = h.shape
    n = w.shape[1]
    return pl.pallas_call(
        functools.partial(_proj_kernel, act=act),
        grid=(t // tm, n // tn),
        in_specs=[pl.BlockSpec((tm, d), lambda i, j: (i, 0)), pl.BlockSpec((d, tn), lambda i, j: (0, j))],
        out_specs=pl.BlockSpec((tm, tn), lambda i, j: (i, j)),
        out_shape=jax.ShapeDtypeStruct((t, n), out_dtype),
        compiler_params=_cparams("parallel", "arbitrary"),
        name=name,
    )(h, w)


def _subln(o, g):
    ms = jnp.mean(o * o, axis=-1, keepdims=True)
    return o * lax.rsqrt(ms + EPS) * g * (1.0 - LAM_INIT)


ATT_ROWS = 256
ATT_HEADS = 2


def _attn_prompt_kernel(qi_tab, ki_tab, lam_ref, q_ref, k_ref, v_ref, g_ref, o_ref,
                        qs_scr, vx_scr, m_scr, acc_scr, *, tq, tk):
    p = pl.program_id(2)
    qi = qi_tab[p]
    ki = ki_tab[p]
    ratio = tk // tq
    n_slab = tk // LANES

    heads = [slice(hh * LANES, (hh + 1) * LANES) for hh in range(ATT_HEADS)]

    @pl.when(ki == 0)
    def _():
        for hh, hs in enumerate(heads):
            q = q_ref[:, hs]
            lane = lax.broadcasted_iota(jnp.int32, q.shape, 1)
            zero = jnp.zeros_like(q)
            qs_scr[hh, 0:tq, :] = jnp.where(lane < DA_HEAD_DIM, q, zero)
            qs_scr[hh, tq:2 * tq, :] = jnp.where(lane >= DA_HEAD_DIM, q, zero)
            vx_scr[hh, :, LANES:2 * LANES] = jnp.ones((tk, LANES), BF16)
        m_scr[...] = jnp.full(m_scr.shape, NEG_INF, F32)
        acc_scr[...] = jnp.zeros(acc_scr.shape, F32)

    for hh, hs in enumerate(heads):
        vx_scr[hh, :, 0:LANES] = v_ref[:, hs]

    def step(masked):
        for rb in range(2 * tq // ATT_ROWS):
            rows = slice(rb * ATT_ROWS, (rb + 1) * ATT_ROWS)
            nk = min(tk, (rb * ATT_ROWS) % tq + ATT_ROWS) if (masked and ratio == 1) else tk
            for hh, hs in enumerate(heads):
                s = _dot_nt(qs_scr[hh, rows, :], k_ref[0:nk, hs])
                slabs = [s[:, c * LANES:(c + 1) * LANES] for c in range(nk // LANES)]
                if masked:
                    row = lax.broadcasted_iota(jnp.int32, (ATT_ROWS, LANES), 0) + rb * ATT_ROWS
                    lane = lax.broadcasted_iota(jnp.int32, (ATT_ROWS, LANES), 1)
                    qchunk = (row % tq) // CHUNK + (qi % ratio) * (tq // CHUNK)
                    first = (nk - ATT_ROWS) // LANES if ratio == 1 else 0
                    slabs = [jnp.where((c * LANES + lane) // CHUNK <= qchunk, sl, NEG_INF) if c >= first else sl
                             for c, sl in enumerate(slabs)]
                mt = slabs[0]
                for sl in slabs[1:]:
                    mt = jnp.maximum(mt, sl)
                m_prev = m_scr[hh, rows, :]
                m_new = jnp.maximum(m_prev, jnp.max(mt, axis=-1, keepdims=True))
                alpha = jnp.exp2(m_prev - m_new)
                e = jnp.concatenate([jnp.exp2(sl - m_new).astype(BF16) for sl in slabs], axis=1)
                pv = _dot(e, vx_scr[hh, 0:nk, :])
                acc_scr[hh, rows, :] = jnp.concatenate([alpha, alpha], axis=1) * acc_scr[hh, rows, :] + pv
                m_scr[hh, rows, :] = m_new

    @pl.when(ki < qi // ratio)
    def _():
        step(False)

    @pl.when(ki == qi // ratio)
    def _():
        step(True)
        for hh, hs in enumerate(heads):
            o = acc_scr[hh, :, 0:LANES] / acc_scr[hh, :, LANES:2 * LANES]
            o = o[0:tq] - lam_ref[...] * o[tq:2 * tq]
            o_ref[:, hs] = _subln(o, g_ref[...]).astype(o_ref.dtype)


def _attn_prompt(q, k, v, lam_row, subln_g, batch, seq, tq, tk):
    d = q.shape[1]
    assert tk % tq == 0 and seq % tk == 0 and tq % CHUNK == 0
    qi_l, ki_l = [], []
    for a in range(seq // tq):
        for b in range(a * tq // tk + 1):
            qi_l.append(a)
            ki_l.append(b)
    qi_tab = jnp.asarray(qi_l, jnp.int32)
    ki_tab = jnp.asarray(ki_l, jnp.int32)
    q3, k3, v3 = (a.reshape(batch, seq, d) for a in (q, k, v))
    hw = ATT_HEADS * LANES
    qspec = pl.BlockSpec((None, tq, hw), lambda b, h, p, qt, kt: (b, qt[p], h))
    kspec = pl.BlockSpec((None, tk, hw), lambda b, h, p, qt, kt: (b, kt[p], h))
    vec = pl.BlockSpec((1, LANES), lambda b, h, p, qt, kt: (0, 0))
    out = pl.pallas_call(
        functools.partial(_attn_prompt_kernel, tq=tq, tk=tk),
        grid_spec=pltpu.PrefetchScalarGridSpec(
            num_scalar_prefetch=2,
            grid=(batch, DA_HEADS // ATT_HEADS, len(qi_l)),
            in_specs=[vec, qspec, kspec, kspec, vec],
            out_specs=qspec,
            scratch_shapes=[pltpu.VMEM((ATT_HEADS, 2 * tq, LANES), BF16),
                            pltpu.VMEM((ATT_HEADS, tk, 2 * LANES), BF16),
                            pltpu.VMEM((ATT_HEADS, 2 * tq, LANES), F32),
                            pltpu.VMEM((ATT_HEADS, 2 * tq, 2 * LANES), F32)]),
        out_shape=jax.ShapeDtypeStruct((batch, seq, d), BF16),
        compiler_params=_cparams("parallel", "parallel", "arbitrary"),
        name="attn_prompt",
    )(qi_tab, ki_tab, lam_row, q3, k3, v3, subln_g)
    return out.reshape(batch * seq, d)


def _attn_sample_kernel(lam_ref, q_ref, kn_ref, vn_ref, ck_ref, cv_ref, g_ref, o_ref, *, past, sq):
    for h in range(DA_HEADS):
        hs = slice(h * LANES, (h + 1) * LANES)
        q = q_ref[:, hs]
        lane = lax.broadcasted_iota(jnp.int32, q.shape, 1)
        zero = jnp.zeros_like(q)
        qs = jnp.concatenate([jnp.where(lane < DA_HEAD_DIM, q, zero),
                              jnp.where(lane >= DA_HEAD_DIM, q, zero)], axis=0)
        kc = ck_ref[pl.ds(h, past, stride=DA_HEADS), :].astype(BF16)
        vc = cv_ref[pl.ds(h, past, stride=DA_HEADS), :].astype(BF16)
        s_c = _dot_nt(qs, kc)
        s_n = _dot_nt(qs, kn_ref[:, hs])

        def masked(s, key0):
            row = lax.broadcasted_iota(jnp.int32, s.shape, 0)
            col = lax.broadcasted_iota(jnp.int32, s.shape, 1)
            qchunk = (past + row % sq) // CHUNK
            return jnp.where((key0 + col) // CHUNK <= qchunk, s, NEG_INF)

        s_c = masked(s_c, 0)
        s_n = masked(s_n, past)
        m = jnp.maximum(jnp.max(s_c, axis=-1, keepdims=True), jnp.max(s_n, axis=-1, keepdims=True))
        e_c = jnp.exp2(s_c - m)
        e_n = jnp.exp2(s_n - m)
        l = jnp.sum(e_c, axis=-1, keepdims=True) + jnp.sum(e_n, axis=-1, keepdims=True)
        o = (_dot(e_c.astype(BF16), vc) + _dot(e_n.astype(BF16), vn_ref[:, hs])) / l
        o = o[0:sq] - lam_ref[...] * o[sq:2 * sq]
        o_ref[:, hs] = _subln(o, g_ref[...]).astype(o_ref.dtype)


def _attn_sample(q, kn, vn, cache_k, cache_v, lam_row, subln_g, batch, sq):
    d = q.shape[1]
    past = cache_k.shape[-3]
    ck = cache_k.reshape(batch, past * DA_HEADS, LANES)
    cv = cache_v.reshape(batch, past * DA_HEADS, LANES)
    q3, k3, v3 = (a.reshape(batch, sq, d) for a in (q, kn, vn))
    new = pl.BlockSpec((None, sq, d), lambda b: (b, 0, 0))
    old = pl.BlockSpec((None, past * DA_HEADS, LANES), lambda b: (b, 0, 0))
    vec = pl.BlockSpec((1, LANES), lambda b: (0, 0))
    out = pl.pallas_call(
        functools.partial(_attn_sample_kernel, past=past, sq=sq),
        grid=(batch,),
        in_specs=[vec, new, new, new, old, old, vec],
        out_specs=new,
        out_shape=jax.ShapeDtypeStruct((batch, sq, d), BF16),
        compiler_params=_cparams("parallel"),
        name="attn_sample",
    )(lam_row, q3, k3, v3, ck, cv, subln_g)
    return out.reshape(batch * sq, d)


RW_L = 64
RW_G = 2
RW_GW = RW_G * RW_HEAD
RW_GROUPS = 16 // RW_G
RW_CPS = 4


def _mm3(a, b, dot=_dot):
    ah, al = _split2(a)
    bh, bl = _split2(b)
    return dot(ah, bh) + dot(ah, bl) + dot(al, bh)


def _terms(x, n):
    return (x.astype(BF16),) if n == 1 else _split2(x)


def _mmt(at, bt, dot=_dot):
    acc = dot(at[0], bt[0])
    if len(bt) > 1:
        acc = acc + dot(at[0], bt[1])
    if len(at) > 1:
        acc = acc + dot(at[1], bt[0])
    return acc


RW_PREC = dict(a_all=(1, 1), neumann=(1, 1), av=(1, 1), pu=(1, 1), qy=(1, 1), mh=(1, 1), state=(1, 1))


def _softplus(z):
    return jnp.maximum(z, 0.0) + jnp.log(1.0 + jnp.exp(-jnp.abs(z)))


def _rwkv_kernel(rw_ref, sh0_ref, mu_ref, lw_ref, w0_ref, a0_ref, kk_ref, ka_ref, rk_ref, lng_ref, lnb_ref,
                 h0_ref, gm_ref, tri_ref, ob_ref, hout_ref, h_scr, last_scr, *, n_valid, cps):
    L = RW_L
    rows = cps * L
    c = pl.program_id(1)
    d = ob_ref.shape[-1]

    @pl.when(c == 0)
    def _():
        h_scr[...] = h0_ref[...]
        last_scr[...] = sh0_ref[...]

    rw = rw_ref[...]
    row_w = lax.broadcasted_iota(jnp.int32, rw.shape, 0)
    prev = jnp.where(row_w == 0, last_scr[...], pltpu.roll(rw, 1, 0))
    last_scr[...] = rw[rows - 1:rows, :]
    xm = rw + (prev - rw) * mu_ref[...]
    r = xm[:, 0:d]
    k = xm[:, d:2 * d]
    v = xm[:, 2 * d:3 * d]
    lo = xm[:, 3 * d:3 * d + LORA_PAD]
    lane_l = lax.broadcasted_iota(jnp.int32, lo.shape, 1)
    z = jnp.where(lane_l < W_LORA, jnp.tanh(lo),
                  jnp.where(lane_l < W_LORA + A_LORA, lo, jax.nn.sigmoid(lo)))
    lora = _dot(z.astype(BF16), lw_ref[...])
    w = -_softplus(-(w0_ref[...] + lora[:, 0:d])) - 0.5
    ld = -jnp.exp(w)
    a = jax.nn.sigmoid(a0_ref[...] + lora[:, d:2 * d])
    g = lora[:, 2 * d:3 * d]
    kkr = k * kk_ref[...]
    kmod = k * (1.0 + (a - 1.0) * ka_ref[...])
    rkk = r * kmod * rk_ref[...]
    gm = gm_ref[...]
    if n_valid < rows:
        valid = lax.broadcasted_iota(jnp.int32, ld.shape, 0) < n_valid
        ld = jnp.where(valid, ld, 0.0)
        kkr = jnp.where(valid, kkr, 0.0)
        kmod = jnp.where(valid, kmod, 0.0)

    tri = tri_ref[...]
    l1, l2, l3 = _split3(ld)
    cin = _dot(tri, l1) + _dot(tri, l2) + _dot(tri, l3)
    cex = cin - ld
    c_last = [cin[(ch + 1) * L - 1:(ch + 1) * L, :] for ch in range(cps)]
    e_in = jnp.exp(cin)
    e_ex = jnp.exp(cex)
    e_neg = jnp.exp(-cin)
    e_tail = jnp.exp(jnp.concatenate([jnp.broadcast_to(cl, (L, d)) for cl in c_last], axis=0) - cin)
    g_last = [jnp.exp(cl) for cl in c_last]

    GW = RW_GW
    row = lax.broadcasted_iota(jnp.int32, (L, GW), 0)
    lane = lax.broadcasted_iota(jnp.int32, (L, GW), 1)
    head_of_lane = lane // RW_HEAD
    tri_s = row > (lane % L)
    tri_i = row >= (lane % L)
    eye_pair = jnp.where(row == (lane % L), 1.0, 0.0).astype(F32)
    r2 = lax.broadcasted_iota(jnp.int32, (GW, GW), 0)
    c2 = lax.broadcasted_iota(jnp.int32, (GW, GW), 1)
    bd_mask = (r2 // RW_HEAD) == (c2 // RW_HEAD)
    diag_mask = r2 == c2

    def bd(x):
        z0 = jnp.zeros_like(x)
        return jnp.concatenate([jnp.where(head_of_lane == j, x, z0) for j in range(RW_G)], axis=0)

    def bdt(t):
        return tuple(bd(x) for x in t)

    def cat(ts, axis):
        return tuple(jnp.concatenate(xs, axis=axis) for xs in zip(*ts))

    prec = RW_PREC
    GL = RW_G * L
    items = [(slice(ch * L, (ch + 1) * L), slice(p * GW, (p + 1) * GW))
             for ch in range(cps) for p in range(RW_GROUPS)]
    pairs = range(len(items))

    v_p = [v[rs, sl] for rs, sl in items]
    kk_n = []
    for q in range(d // GMW):
        kk_raw = kkr[:, q * GMW:(q + 1) * GMW]
        ss = _group_mean(kk_raw * kk_raw, gm) * RW_HEAD
        kk_n.append(kk_raw / jnp.maximum(jnp.sqrt(ss), 1e-12))
    kk_n = jnp.concatenate(kk_n, axis=1)
    kk_p = [kk_n[rs, sl] for rs, sl in items]
    b_p = [kk_p[i] * a[rs, sl] for i, (rs, sl) in enumerate(items)]
    abar = [-kk_p[i] * e_ex[rs, sl] for i, (rs, sl) in enumerate(items)]
    rbar = [r[rs, sl] * e_in[rs, sl] for rs, sl in items]
    bbar = [b_p[i] * e_neg[rs, sl] for i, (rs, sl) in enumerate(items)]
    kbar = [kmod[rs, sl] * e_neg[rs, sl] for rs, sl in items]
    btil = [b_p[i] * e_tail[rs, sl] for i, (rs, sl) in enumerate(items)]
    ktil = [kmod[rs, sl] * e_tail[rs, sl] for rs, sl in items]

    na, nb = prec["a_all"]
    abar_t = [_terms(abar[p], max(na, prec["pu"][1])) for p in pairs]
    v_t = [_terms(v_p[p], max(prec["av"][1], prec["qy"][1], prec["mh"][1])) for p in pairs]
    a_all = []
    for p in pairs:
        lhs = cat([abar_t[p][:na], _terms(rbar[p], na)], 0)
        rhs = cat([bdt(_terms(bbar[p], nb)), bdt(_terms(kbar[p], nb))], 0)
        a_all.append(_mmt(lhs, rhs, _dot_nt))
    n_ab = [jnp.where(tri_s, a_all[p][0:L, 0:GL], 0.0) for p in pairs]
    a_ak = [jnp.where(tri_s, a_all[p][0:L, GL:2 * GL], 0.0) for p in pairs]
    a_rb = [jnp.where(tri_i, a_all[p][L:2 * L, 0:GL], 0.0) for p in pairs]
    a_rk = [jnp.where(tri_i, a_all[p][L:2 * L, GL:2 * GL], 0.0) for p in pairs]

    na, nb = prec["neumann"]
    t_inv = [eye_pair + n_ab[p] for p in pairs]
    pw_t = [_terms(n_ab[p], max(na, nb)) for p in pairs]
    pw = [_mmt(pw_t[p][:na], bdt(pw_t[p][:nb])) for p in pairs]
    for _ in range(int(math.log2(L)) - 2):
        pw_t = [_terms(pw[p], max(na, nb)) for p in pairs]
        st = [_mmt(cat([_terms(t_inv[p], na), pw_t[p][:na]], 0), bdt(pw_t[p][:nb])) for p in pairs]
        t_inv = [t_inv[p] + st[p][0:L] for p in pairs]
        pw = [st[p][L:2 * L] for p in pairs]
    t_inv = [t_inv[p] + _mmt(_terms(t_inv[p], na), bdt(_terms(pw[p], nb))) for p in pairs]

    na, nb = prec["av"]
    vv = [_mmt(cat([_terms(a_ak[p], na), _terms(a_rk[p], na)], 0), bdt(v_t[p][:nb])) for p in pairs]
    av = [x[0:L] for x in vv]
    arkv = [x[L:2 * L] for x in vv]
    na, nb = prec["pu"]
    pu = [_mmt(_terms(t_inv[p], na), cat([bdt(abar_t[p][:nb]), bdt(_terms(av[p], nb))], 1)) for p in pairs]
    p_m = [x[:, 0:GW] for x in pu]
    u0 = [x[:, GW:2 * GW] for x in pu]
    na, nb = prec["qy"]
    nmh = prec["mh"][1]
    p_t = [_terms(p_m[p], max(nb, nmh)) for p in pairs]
    u_t = [_terms(u0[p], max(nb, nmh)) for p in pairs]
    qy = [_mmt(_terms(a_rb[p], na), cat([bdt(p_t[p][:nb]), bdt(u_t[p][:nb])], 1)) for p in pairs]
    q_m = [rbar[p] + qy[p][:, 0:GW] for p in pairs]
    y0 = [qy[p][:, GW:2 * GW] + arkv[p] for p in pairs]

    na, nb = prec["mh"]
    btil_t = [_terms(btil[p], na) for p in pairs]
    m_full = [_mmt(btil_t[p], p_t[p][:nb], _dot_tn) for p in pairs]
    h0_full = [_mmt(cat([btil_t[p], _terms(ktil[p], na)], 0), cat([u_t[p][:nb], v_t[p][:nb]], 0), _dot_tn)
               for p in pairs]
    m_bd = [jnp.where(bd_mask, m_full[i], 0.0) + jnp.where(diag_mask, g_last[i // RW_GROUPS][:, sl], 0.0)
            for i, (rs, sl) in enumerate(items)]
    h0_bd = [jnp.where(bd_mask, h0_full[p], 0.0) for p in pairs]

    na, nb = prec["state"]
    h_cur = [h_scr[p] for p in range(RW_GROUPS)]
    y = []
    for ch in range(cps):
        idx = [ch * RW_GROUPS + p for p in range(RW_GROUPS)]
        st = [_mmt(cat([_terms(q_m[i], na), _terms(m_bd[i], na)], 0), _terms(h_cur[p], nb))
              for p, i in enumerate(idx)]
        y.extend(st[p][0:L] + y0[i] for p, i in enumerate(idx))
        h_cur = [st[p][L:L + GW] + h0_bd[i] for p, i in enumerate(idx)]
    for p in range(RW_GROUPS):
        h_scr[p] = h_cur[p]

    per = GMW // GW
    for ch in range(cps):
        rs = slice(ch * L, (ch + 1) * L)
        for q in range(RW_GROUPS // per):
            sl = slice(q * GMW, (q + 1) * GMW)
            yq = jnp.concatenate([y[ch * RW_GROUPS + q * per + j] for j in range(per)], axis=1)
            mu_y = _group_mean(yq, gm)
            dy = yq - mu_y
            var = _dot((dy * dy).astype(BF16), gm)
            yn = dy * lax.rsqrt(var + RW_GN_EPS) * lng_ref[:, sl] + lnb_ref[:, sl]
            bonus = _dot(rkk[rs, sl].astype(BF16), gm) * RW_HEAD * v[rs, sl]
            ob_ref[rs, sl] = ((yn + bonus) * g[rs, sl]).astype(ob_ref.dtype)

    @pl.when(c == pl.num_programs(1) - 1)
    def _():
        hout_ref[...] = h_scr[...]


def _rwkv(rw3, shift0, h0_bd, prm, gm, n_valid, cps):
    batch, seq, wcols = rw3.shape
    d = prm["w0"].shape[1]
    rows = cps * RW_L
    nc = seq // rows
    tri = jnp.kron(jnp.eye(cps, dtype=F32), jnp.tril(jnp.ones((RW_L, RW_L), F32))).astype(BF16)
    const2 = lambda b, c: (0, 0)
    vec = pl.BlockSpec((1, d), const2)
    st = pl.BlockSpec((None, RW_GROUPS, RW_GW, RW_GW), lambda b, c: (b, 0, 0, 0))
    return pl.pallas_call(
        functools.partial(_rwkv_kernel, n_valid=n_valid, cps=cps),
        grid=(batch, nc),
        in_specs=[pl.BlockSpec((None, rows, wcols), lambda b, c: (b, c, 0)),
                  pl.BlockSpec((None, 1, wcols), lambda b, c: (b, 0, 0)),
                  pl.BlockSpec((1, wcols), const2),
                  pl.BlockSpec(prm["lw"].shape, const2),
                  vec, vec, vec, vec, vec, vec, vec, st,
                  pl.BlockSpec((GMW, GMW), const2),
                  pl.BlockSpec((rows, rows), const2)],
        out_specs=[pl.BlockSpec((None, rows, d), lambda b, c: (b, c, 0)), st],
        out_shape=[jax.ShapeDtypeStruct((batch, seq, d), BF16),
                   jax.ShapeDtypeStruct((batch, RW_GROUPS, RW_GW, RW_GW), F32)],
        scratch_shapes=[pltpu.VMEM((RW_GROUPS, RW_GW, RW_GW), F32), pltpu.VMEM((1, wcols), F32)],
        compiler_params=_cparams("parallel", "arbitrary"),
        name="rwkv7",
    )(rw3, shift0, prm["mu"], prm["lw"], prm["w0"], prm["a0"], prm["k_k"], prm["k_a"], prm["r_k"],
      prm["ln_g"], prm["ln_b"], h0_bd, gm, tri)


def _state_to_bd(s):
    b = s.shape[0]
    ht = jnp.swapaxes(s, -1, -2).reshape(b, RW_GROUPS, RW_G, RW_HEAD, RW_HEAD)
    z = jnp.zeros_like(ht[:, :, 0])
    rows = [jnp.concatenate([ht[:, :, j] if i == j else z for i in range(RW_G)], axis=-1) for j in range(RW_G)]
    return jnp.concatenate(rows, axis=-2)


def _bd_to_state(hbd):
    b = hbd.shape[0]
    blocks = [hbd[:, :, j * RW_HEAD:(j + 1) * RW_HEAD, j * RW_HEAD:(j + 1) * RW_HEAD] for j in range(RW_G)]
    ht = jnp.stack(blocks, axis=2).reshape(b, RW_G * RW_GROUPS, RW_HEAD, RW_HEAD)
    return jnp.swapaxes(ht, -1, -2)


def _route(logits):
    lane = lax.broadcasted_iota(jnp.int32, logits.shape, 1).astype(F32)
    big = float(LANES)
    lg = jnp.where(lane < N_GROUPS, logits, NEG_INF)
    mg = jnp.max(lg, axis=-1, keepdims=True)
    sg = jnp.sum(jnp.exp(lg - mg), axis=-1, keepdims=True)
    p_top = 1.0 / sg
    g_idx = jnp.min(jnp.where(lg == mg, lane, big), axis=-1, keepdims=True)
    e0 = N_GROUPS + EXPERTS_PER_GROUP * g_idx
    emask = jnp.where(lane >= e0, jnp.where(lane < e0 + EXPERTS_PER_GROUP, 1.0, 0.0), 0.0) > 0.5
    le = jnp.where(emask, logits, NEG_INF)
    me = jnp.max(le, axis=-1, keepdims=True)
    ee = jnp.exp(le - me)
    pe = ee / jnp.sum(ee, axis=-1, keepdims=True)
    pe = jnp.where(emask, pe, -1.0)
    v1 = jnp.max(pe, axis=-1, keepdims=True)
    i1 = jnp.min(jnp.where(pe == v1, lane, big), axis=-1, keepdims=True)
    pe2 = jnp.where(lane == i1, -1.0, pe)
    v2 = jnp.max(pe2, axis=-1, keepdims=True)
    i2 = jnp.min(jnp.where(pe2 == v2, lane, big), axis=-1, keepdims=True)
    den = v1 + v2
    ew = jnp.where(lane == i1, v1 / den, 0.0) + jnp.where(lane == i2, v2 / den, 0.0)
    return p_top * ew + jnp.where(lane == 0.0, g_idx, 0.0)


def _merge_kernel(oa_ref, ob_ref, gates_ref, x_ref, gt1_ref, sc2_ref, sh2_ref, g2_ref,
                  woa_ref, wob_ref, wout_ref, wr_ref, br_ref, x1_ref, hx_ref):
    d = x_ref.shape[1]
    a_out = _dot(oa_ref[...], woa_ref[...])
    b_out = _dot(ob_ref[...], wob_ref[...])
    merged = gates_ref[:, 0:d].astype(F32) * a_out + gates_ref[:, d:2 * d].astype(F32) * b_out
    x1 = x_ref[...] + gt1_ref[...] * _dot(merged.astype(BF16), wout_ref[...])
    x1_ref[...] = x1
    ms = jnp.mean(x1 * x1, axis=-1, keepdims=True)
    h2 = x1 * lax.rsqrt(ms + EPS) * g2_ref[...]
    h2 = h2 * (1.0 + sc2_ref[...]) + sh2_ref[...]
    hh, hl = _split2(h2)
    wh, wl = _split2(wr_ref[...])
    part = _dot(hh, jnp.concatenate([wh, wl], axis=1))
    logits = part[:, 0:LANES] + part[:, LANES:2 * LANES] + _dot(hl, wh) + br_ref[...]
    hx_ref[:, 0:d] = h2
    hx_ref[:, d:d + LANES] = _route(logits)


def _merge(oa, ob, gates, x2d, gt1, sc2, sh2, g2, w_oa, w_ob, w_out, wr, br, tm, seq):
    t, d = x2d.shape
    row = lambda i: (i, 0)
    const = lambda i: (0, 0)
    full = lambda w: pl.BlockSpec(w.shape, const)
    return pl.pallas_call(
        _merge_kernel,
        grid=(t // tm,),
        in_specs=[pl.BlockSpec((tm, d), row), pl.BlockSpec((tm, d), row), pl.BlockSpec((tm, 2 * d), row),
                  pl.BlockSpec((tm, d), row), _mod_spec(gt1, tm, seq), _mod_spec(sc2, tm, seq),
                  _mod_spec(sh2, tm, seq), pl.BlockSpec((1, d), const),
                  full(w_oa), full(w_ob), full(w_out), full(wr), full(br)],
        out_specs=[pl.BlockSpec((tm, d), row), pl.BlockSpec((tm, d + LANES), row)],
        out_shape=[jax.ShapeDtypeStruct((t, d), F32), jax.ShapeDtypeStruct((t, d + LANES), F32)],
        compiler_params=_cparams("parallel"),
        name="merge_router",
    )(oa, ob, gates, x2d, gt1, sc2, sh2, g2, w_oa, w_ob, w_out, wr, br)


def _expert_ffn(xb, comb, lane_e, w13, w2):
    au = _dot(xb, w13)
    f = au.shape[1] // 2
    a = au[:, 0:f]
    u = au[:, f:2 * f]
    lane = lax.broadcasted_iota(jnp.int32, comb.shape, 1)
    cw = jnp.sum(jnp.where(lane == lane_e, comb, 0.0), axis=-1, keepdims=True)
    act = a * jax.nn.sigmoid(a) * u * cw
    return _dot(act.astype(BF16), w2)


def _moe_dense_kernel(hx_ref, w13_ref, w2_ref, x1_ref, gt2_ref, y_ref, xb_scr, acc_scr):
    e = pl.program_id(1)
    d = y_ref.shape[1]

    @pl.when(e == 0)
    def _():
        xb_scr[...] = hx_ref[:, 0:d].astype(BF16)
        acc_scr[...] = jnp.zeros(acc_scr.shape, F32)

    acc_scr[...] += _expert_ffn(xb_scr[...], hx_ref[:, d:d + LANES], e + N_GROUPS, w13_ref[...], w2_ref[...])

    @pl.when(e == pl.num_programs(1) - 1)
    def _():
        y_ref[...] = x1_ref[...] + gt2_ref[...] * acc_scr[...]


def _moe_dense(hx, w13, w2, x1, gt2, tm, seq):
    t, d = x1.shape
    ne, _, f2 = w13.shape
    row = lambda i, e: (i, 0)
    return pl.pallas_call(
        _moe_dense_kernel,
        grid=(t // tm, ne),
        in_specs=[pl.BlockSpec((tm, d + LANES), row),
                  pl.BlockSpec((None, d, f2), lambda i, e: (e, 0, 0)),
                  pl.BlockSpec((None, f2 // 2, d), lambda i, e: (e, 0, 0)),
                  pl.BlockSpec((tm, d), row), _mod_spec(gt2, tm, seq)],
        out_specs=pl.BlockSpec((tm, d), row),
        out_shape=jax.ShapeDtypeStruct((t, d), F32),
        scratch_shapes=[pltpu.VMEM((tm, d), BF16), pltpu.VMEM((tm, d), F32)],
        compiler_params=_cparams("parallel", "arbitrary"),
        name="moe_dense",
    )(hx, w13, w2, x1, gt2)


MOE_TM = 1024
MOE_TS = 1024


def _plan_kernel(rt_ref, tri_ref, rank_ref, cnt_ref, carry_scr):
    @pl.when(pl.program_id(0) == 0)
    def _():
        carry_scr[...] = jnp.zeros(carry_scr.shape, F32)

    rt = rt_ref[...]
    lane = lax.broadcasted_iota(jnp.int32, rt.shape, 1).astype(F32)
    onehot = jnp.where(lane == rt[:, 0:1], 1.0, 0.0)
    before = _dot(tri_ref[...], onehot.astype(BF16)) + carry_scr[...]
    rank_ref[...] = jnp.sum(onehot * before, axis=-1, keepdims=True).astype(jnp.int32)
    carry_scr[...] += jnp.sum(onehot, axis=0, keepdims=True)
    cnt_ref[...] = carry_scr[...]


def _moe_plan(hx, d):
    t = hx.shape[0]
    tp = 512
    tri = jnp.tril(jnp.ones((tp, tp), F32), -1).astype(BF16)
    return pl.pallas_call(
        _plan_kernel,
        grid=(t // tp,),
        in_specs=[pl.BlockSpec((tp, LANES), lambda i: (i, d // LANES)),
                  pl.BlockSpec((tp, tp), lambda i: (0, 0))],
        out_specs=[pl.BlockSpec((tp, 1), lambda i: (i, 0)), pl.BlockSpec((1, LANES), lambda i: (0, 0))],
        out_shape=[jax.ShapeDtypeStruct((t, 1), jnp.int32), jax.ShapeDtypeStruct((1, LANES), F32)],
        scratch_shapes=[pltpu.VMEM((1, LANES), F32)],
        compiler_params=_cparams("arbitrary"),
        name="moe_plan",
    )(hx, tri)


def _row_copy(src_hbm, src_row, dst, dst_row, sem):
    return pltpu.make_async_copy(src_hbm.at[pl.ds(src_row, 1), :], dst.at[pl.ds(dst_row, 1), :], sem)


def _dispatch_kernel(pos_ref, hx_ref, xs_in_hbm, xs_hbm, sem):
    del xs_in_hbm

    def issue(t, carry):
        _row_copy(hx_ref, t, xs_hbm, pos_ref[t], sem).start()
        return carry

    def drain(t, carry):
        _row_copy(hx_ref, t, xs_hbm, pos_ref[t], sem).wait()
        return carry

    lax.fori_loop(0, MOE_TS, issue, 0, unroll=8)
    lax.fori_loop(0, MOE_TS, drain, 0, unroll=8)


def _moe_dispatch(pos, hx, n_rows):
    t, w = hx.shape
    xs0 = jnp.zeros((n_rows, w), F32)
    return pl.pallas_call(
        _dispatch_kernel,
        grid=(t // MOE_TS,),
        in_specs=[pl.BlockSpec((MOE_TS,), lambda i: (i,), memory_space=pltpu.SMEM),
                  pl.BlockSpec((MOE_TS, w), lambda i: (i, 0)), pl.BlockSpec(memory_space=pl.ANY)],
        out_specs=pl.BlockSpec(memory_space=pl.ANY),
        out_shape=jax.ShapeDtypeStruct((n_rows, w), F32),
        scratch_shapes=[pltpu.SemaphoreType.DMA(())],
        input_output_aliases={2: 0},
        compiler_params=_cparams("arbitrary"),
        name="moe_dispatch",
    )(pos, hx, xs0)


def _moe_bucket_kernel(tg_ref, tv_ref, xs_ref, w13_ref, w2_ref, ys_ref, xb_scr):
    i = pl.program_id(0)
    e = pl.program_id(1)
    d = ys_ref.shape[1]

    @pl.when(e == 0)
    def _():
        xb_scr[...] = xs_ref[:, 0:d].astype(BF16)
        ys_ref[...] = jnp.zeros(ys_ref.shape, F32)

    @pl.when(tv_ref[i] > 0)
    def _():
        lane_e = N_GROUPS + tg_ref[i] * EXPERTS_PER_GROUP + e
        ys_ref[...] += _expert_ffn(xb_scr[...], xs_ref[:, d:d + LANES], lane_e, w13_ref[...], w2_ref[...])


def _moe_buckets(tile_group, tile_valid, xs, w13, w2, d):
    n_rows, w = xs.shape
    _, _, f2 = w13.shape
    row = lambda i, e, tg, tv: (i, 0)
    wsel = lambda i, e, tg, tv: (tg[i] * EXPERTS_PER_GROUP + e, 0, 0)
    return pl.pallas_call(
        _moe_bucket_kernel,
        grid_spec=pltpu.PrefetchScalarGridSpec(
            num_scalar_prefetch=2,
            grid=(n_rows // MOE_TM, EXPERTS_PER_GROUP),
            in_specs=[pl.BlockSpec((MOE_TM, w), row),
                      pl.BlockSpec((None, d, f2), wsel), pl.BlockSpec((None, f2 // 2, d), wsel)],
            out_specs=pl.BlockSpec((MOE_TM, d), row),
            scratch_shapes=[pltpu.VMEM((MOE_TM, d), BF16)]),
        out_shape=jax.ShapeDtypeStruct((n_rows, d), F32),
        compiler_params=_cparams("parallel", "arbitrary"),
        name="moe_buckets",
    )(tile_group, tile_valid, xs, w13, w2)


def _collect_kernel(pos_ref, ys_hbm, x1_ref, gt2_ref, y_ref, buf, sem):
    tu = buf.shape[0]

    def issue(t, carry):
        _row_copy(ys_hbm, pos_ref[t], buf, t, sem).start()
        return carry

    def drain(t, carry):
        _row_copy(ys_hbm, pos_ref[t], buf, t, sem).wait()
        return carry

    lax.fori_loop(0, tu, issue, 0, unroll=8)
    lax.fori_loop(0, tu, drain, 0, unroll=8)
    y_ref[...] = x1_ref[...] + gt2_ref[...] * buf[...]


def _moe_collect(pos, ys, x1, gt2, seq):
    t, d = x1.shape
    tu = MOE_TS
    return pl.pallas_call(
        _collect_kernel,
        grid=(t // tu,),
        in_specs=[pl.BlockSpec((tu,), lambda i: (i,), memory_space=pltpu.SMEM),
                  pl.BlockSpec(memory_space=pl.ANY),
                  pl.BlockSpec((tu, d), lambda i: (i, 0)), _mod_spec(gt2, tu, seq)],
        out_specs=pl.BlockSpec((tu, d), lambda i: (i, 0)),
        out_shape=jax.ShapeDtypeStruct((t, d), F32),
        scratch_shapes=[pltpu.VMEM((tu, d), F32), pltpu.SemaphoreType.DMA(())],
        compiler_params=_cparams("arbitrary"),
        name="moe_collect",
    )(pos, ys, x1, gt2)


def _moe_sorted(hx, w13, w2, x1, gt2, seq):
    t, d = x1.shape
    rank, cnt = _moe_plan(hx, d)
    group = hx[:, d].astype(jnp.int32)
    counts = cnt[0, 0:N_GROUPS].astype(jnp.int32)
    padded = (counts + MOE_TM - 1) // MOE_TM * MOE_TM
    ends = jnp.cumsum(padded)
    starts = ends - padded
    pos = jnp.take(starts, group) + rank[:, 0]
    n_tiles = t // MOE_TM + N_GROUPS
    tile_start = jnp.arange(n_tiles, dtype=jnp.int32) * MOE_TM
    tile_group = jnp.minimum(jnp.sum((tile_start[:, None] >= ends[None, :]).astype(jnp.int32), axis=1),
                             N_GROUPS - 1)
    tile_valid = (tile_start < ends[-1]).astype(jnp.int32)
    xs = _moe_dispatch(pos, hx, n_tiles * MOE_TM)
    ys = _moe_buckets(tile_group, tile_valid, xs, w13, w2, d)
    return _moe_collect(pos, ys, x1, gt2, seq)


def _rope_tables(pos0, seq, reps):
    pos = (pos0 + jnp.arange(seq)).astype(F32)
    inv = ROPE_THETA ** (-jnp.arange(0, DA_HEAD_DIM, 2, dtype=F32) / DA_HEAD_DIM)
    ang = pos[:, None] * inv[None, :]
    cos, sin = jnp.cos(ang), jnp.sin(ang)
    cos_t = jnp.tile(jnp.concatenate([cos, cos], axis=-1), (reps, LANES // DA_HEAD_DIM))
    sin_t = jnp.tile(jnp.concatenate([-sin, sin], axis=-1), (reps, LANES // DA_HEAD_DIM))
    return cos_t, sin_t


def _layer(x, mod, pos0, past, wts):
    bx, sx, d = x.shape
    t = bx * sx
    x2d = x.reshape(t, d)
    per_batch = sx >= 512
    tm = 512 if per_batch else t

    def modv(i):
        if per_batch:
            return mod[:, i:i + 1, :]
        return jnp.repeat(mod[:, i, :], sx, axis=0)[None]

    sh1, sc1, gt1, sh2, sc2, gt2 = (modv(i) for i in range(6))
    cos_t, sin_t = _rope_tables(pos0, sx, 1 if per_batch else bx)

    tm_big = 1024 if (per_batch and sx % 1024 == 0) else tm
    h, q, k32, kb, v32, vb = _qkv(x2d, sc1, sh1, wts["norm1_g"], wts["w_qkv"], wts["qg"], wts["kg"],
                                  cos_t, sin_t, wts["gm"], tm_big, sx)
    rw = _proj(h, wts["w_rw"], tm_big, wts["w_rw"].shape[1] // 3, None, F32, "rw_proj")
    gates = _proj(h, wts["w_gate"], tm_big, d, "sigmoid", BF16, "gate_proj")

    if past is None:
        oa = _attn_prompt(q, kb, vb, wts["lam"], wts["subln_g"], bx, sx, min(1024, sx), min(1024, sx))
        shift0 = jnp.zeros((bx, 1, rw.shape[1]), F32)
        h0_bd = jnp.zeros((bx, RW_GROUPS, RW_GW, RW_GW), F32)
    else:
        ck, cv, s0, sh0 = past
        oa = _attn_sample(q, kb, vb, ck, cv, wts["lam"], wts["subln_g"], bx, sx)
        shift0 = jnp.pad(sh0, ((0, 0), (0, 0), (0, rw.shape[1] - sh0.shape[-1])))
        h0_bd = _state_to_bd(s0.astype(F32))

    rw3 = rw.reshape(bx, sx, rw.shape[1])
    if sx % RW_L == 0:
        cps = RW_CPS if sx % (RW_CPS * RW_L) == 0 else 1
        rw_in, n_valid = rw3, cps * RW_L
    else:
        assert sx < RW_L
        cps = 1
        rw_in, n_valid = jnp.pad(rw3, ((0, 0), (0, RW_L - sx), (0, 0))), sx
    ob, h_bd = _rwkv(rw_in, shift0, h0_bd, wts, wts["gm"], n_valid, cps)
    ob = ob[:, 0:sx, :].reshape(t, d)

    x1, hx = _merge(oa, ob, gates, x2d, gt1, sc2, sh2, wts["norm2_g"],
                    wts["w_oa"], wts["w_ob"], wts["w_out"], wts["wr"], wts["br"], tm, sx)
    if per_batch and sx % MOE_TS == 0:
        y = _moe_sorted(hx, wts["w13"], wts["w2"], x1, gt2, sx)
    else:
        y = _moe_dense(hx, wts["w13"], wts["w2"], x1, gt2, tm, sx)

    n_cols = 3 * d + W_LORA + A_LORA + G_LORA
    return (y.reshape(bx, sx, d),
            k32.reshape(1, bx, sx, DA_HEADS, 2 * DA_HEAD_DIM),
            v32.reshape(1, bx, sx, DA_HEADS, 2 * DA_HEAD_DIM),
            _bd_to_state(h_bd)[None],
            rw3[:, sx - 1:sx, 0:n_cols][None])


def kernel(x_prompt, x_sample, cache_k, cache_v, state_wkv, state_shift, c_prompt, c_sample, norm1_g, norm2_g, w_ada, b_ada, w_in, da_qn_g, da_kn_g, da_lambda, da_subln_g, w_oa, rw_mu, rw_w0, rw_w2, rw_a0, rw_a2, rw_g2, rw_k_k, rw_k_a, rw_r_k, rw_ln_g, rw_ln_b, w_ob, w_out, router_g, router_g_b, router_e, router_e_b, exp_w1, exp_w3, exp_w2):
    b, s, d = x_prompt.shape
    bs = x_sample.shape[0]
    past_len = cache_k.shape[2]
    assert w_in.shape[0] == 1, "single layer"

    c_all = jnp.concatenate([c_prompt, c_sample], axis=0)
    c_all = jnp.pad(c_all, ((0, (-c_all.shape[0]) % 8), (0, 0)))
    mod, lam_tile = _adaln(c_all, w_ada[0], b_ada[0], da_lambda[0])
    mod_p = mod[0:b].reshape(b, 6, d)
    mod_s = mod[b:b + bs].reshape(bs, 6, d)

    win = w_in[0]
    rw_cols = 3 * d + W_LORA + A_LORA + G_LORA
    rw_pad = 3 * d + LORA_PAD
    w_rw = jnp.pad(win[:, 3 * d:3 * d + rw_cols], ((0, 0), (0, rw_pad - rw_cols))).astype(BF16)
    lw = jnp.zeros((LORA_PAD, 3 * d), F32)
    lw = lw.at[0:W_LORA, 0:d].set(rw_w2[0])
    lw = lw.at[W_LORA:W_LORA + A_LORA, d:2 * d].set(rw_a2[0])
    lw = lw.at[W_LORA + A_LORA:W_LORA + A_LORA + G_LORA, 2 * d:3 * d].set(rw_g2[0])
    f = exp_w1.shape[-1]
    wr = jnp.pad(jnp.concatenate([router_g[0], router_e[0]], axis=1), ((0, 0), (0, LANES - N_GROUPS - N_EXPERTS)))
    br = jnp.pad(jnp.concatenate([router_g_b[0], router_e_b[0]]), (0, LANES - N_GROUPS - N_EXPERTS))[None]
    half = jnp.ones((DA_HEAD_DIM, DA_HEAD_DIM), F32) / DA_HEAD_DIM
    wts = dict(
        norm1_g=norm1_g[0][None], norm2_g=norm2_g[0][None],
        w_qkv=win[:, 0:3 * d].astype(BF16), w_rw=w_rw, w_gate=win[:, 3 * d + rw_cols:].astype(BF16),
        qg=jnp.tile(da_qn_g[0], GMW // DA_HEAD_DIM)[None], kg=jnp.tile(da_kn_g[0], GMW // DA_HEAD_DIM)[None],
        gm=jnp.kron(jnp.eye(GMW // DA_HEAD_DIM, dtype=F32), half).astype(BF16),
        lam=lam_tile[0:1], subln_g=da_subln_g[0][None],
        mu=jnp.pad(rw_mu[0], (0, rw_pad - rw_cols))[None], lw=lw.astype(BF16),
        w0=rw_w0[0][None], a0=rw_a0[0][None], k_k=rw_k_k[0][None], k_a=rw_k_a[0][None],
        r_k=rw_r_k[0].reshape(1, d), ln_g=rw_ln_g[0][None], ln_b=rw_ln_b[0][None],
        w_oa=w_oa[0].astype(BF16), w_ob=w_ob[0].astype(BF16), w_out=w_out[0].astype(BF16),
        wr=wr, br=br,
        w13=jnp.concatenate([exp_w1[0], exp_w3[0]], axis=-1).reshape(N_EXPERTS, d, 2 * f).astype(BF16),
        w2=exp_w2[0].reshape(N_EXPERTS, f, d).astype(BF16),
    )

    out_p = _layer(x_prompt, mod_p, 0, None, wts)
    out_s = _layer(x_sample, mod_s, past_len,
                   (cache_k, cache_v, state_wkv[0], state_shift[0]), wts)
    return (out_p[0], out_s[0], out_p[1], out_p[2], out_p[3], out_p[4],
            out_s[1], out_s[2], out_s[3], out_s[4])
```

```python
import functools
import math

import jax
import jax.numpy as jnp
from jax import lax
from jax.experimental import pallas as pl
from jax.experimental.pallas import tpu as pltpu

F32 = jnp.float32
BF16 = jnp.bfloat16

EPS = 1e-6
NEG_INF = -1e30
CHUNK = 64
DA_HEADS = 8
DA_HEAD_DIM = 64
ROPE_THETA = 10000.0
RW_HEAD = 64
RW_GN_EPS = 64e-5
W_LORA, A_LORA, G_LORA = 64, 64, 160
N_GROUPS, EXPERTS_PER_GROUP = 4, 8
N_EXPERTS = N_GROUPS * EXPERTS_PER_GROUP
LAM_INIT = 0.8 - 0.6 * math.exp(-0.3 * 0)

LANES = 128
LORA_PAD = 384
VMEM_LIMIT = 56 * 1024 * 1024


def _cparams(*sem):
    return pltpu.CompilerParams(dimension_semantics=sem, vmem_limit_bytes=VMEM_LIMIT)


def _dot(a, b):
    return jnp.dot(a, b, preferred_element_type=F32)


def _dot_nt(a, b):
    return lax.dot_general(a, b, (((1,), (1,)), ((), ())), preferred_element_type=F32)


def _dot_tn(a, b):
    return lax.dot_general(a, b, (((0,), (0,)), ((), ())), preferred_element_type=F32)


def _split2(a):
    hi = a.astype(BF16)
    lo = (a - hi.astype(F32)).astype(BF16)
    return hi, lo


def _split3(a):
    hi = a.astype(BF16)
    r = a - hi.astype(F32)
    mid = r.astype(BF16)
    lo = (r - mid.astype(F32)).astype(BF16)
    return hi, mid, lo


GMW = 256


def _group_mean(sq, gm):
    hi, lo = _split2(sq)
    return _dot(hi, gm) + _dot(lo, gm)


def _adaln_kernel(c_ref, w_ref, b_ref, l_ref, o_ref, lam_ref):
    c = c_ref[...]
    sc = (c * jax.nn.sigmoid(c)).astype(BF16)
    o_ref[...] = _dot(sc, w_ref[...].astype(BF16)) + b_ref[...]
    l = l_ref[...]
    s1 = jnp.sum(l[0:1] * l[1:2], axis=-1, keepdims=True)
    s2 = jnp.sum(l[2:3] * l[3:4], axis=-1, keepdims=True)
    lam = jnp.exp(s1) - jnp.exp(s2) + LAM_INIT
    lam_ref[...] = jnp.broadcast_to(lam, lam_ref.shape)


def _adaln(c_all, w_ada, b_ada, da_lambda):
    bp, d = c_all.shape
    n = w_ada.shape[1]
    tn = 1024
    return pl.pallas_call(
        _adaln_kernel,
        grid=(n // tn,),
        in_specs=[pl.BlockSpec((bp, d), lambda j: (0, 0)),
                  pl.BlockSpec((d, tn), lambda j: (0, j)),
                  pl.BlockSpec((1, tn), lambda j: (0, j)),
                  pl.BlockSpec(da_lambda.shape, lambda j: (0, 0))],
        out_specs=[pl.BlockSpec((bp, tn), lambda j: (0, j)),
                   pl.BlockSpec((8, LANES), lambda j: (0, 0))],
        out_shape=[jax.ShapeDtypeStruct((bp, n), F32), jax.ShapeDtypeStruct((8, LANES), F32)],
        compiler_params=_cparams("arbitrary"),
        name="adaln",
    )(c_all, w_ada, b_ada.reshape(1, n), da_lambda)


def _mod_spec(mod, tm, seq):
    d = mod.shape[-1]
    if mod.shape[1] == 1:
        per = seq // tm
        return pl.BlockSpec((None, 1, d), lambda i, *_: (i // per, 0, 0))
    return pl.BlockSpec((None, tm, d), lambda i, *_: (0, i, 0))


def _qkv_kernel(x_ref, sc_ref, sh_ref, g_ref, w_ref, qg_ref, kg_ref, cos_ref, sin_ref, gm_ref,
                h_ref, q_ref, k_ref, kb_ref, v_ref, vb_ref):
    j = pl.program_id(1)

    @pl.when(j == 0)
    def _():
        x = x_ref[...]
        ms = jnp.mean(x * x, axis=-1, keepdims=True)
        hn = x * lax.rsqrt(ms + EPS) * g_ref[...]
        h_ref[...] = (hn * (1.0 + sc_ref[...]) + sh_ref[...]).astype(h_ref.dtype)

    acc = _dot(h_ref[...], w_ref[...])
    def norm_rope(gain_ref, scale, write):
        lane = lax.broadcasted_iota(jnp.int32, (acc.shape[0], LANES), 1)
        first_half = (lane % DA_HEAD_DIM) < (DA_HEAD_DIM // 2)
        for c2 in range(acc.shape[1] // GMW):
            a2 = acc[:, c2 * GMW:(c2 + 1) * GMW]
            ms2 = _dot((a2 * a2).astype(BF16), gm_ref[...])
            y2 = a2 * lax.rsqrt(ms2 + EPS) * gain_ref[...]
            for hf in range(GMW // LANES):
                y = y2[:, hf * LANES:(hf + 1) * LANES]
                rot = jnp.where(first_half, pltpu.roll(y, LANES - DA_HEAD_DIM // 2, 1),
                                pltpu.roll(y, DA_HEAD_DIM // 2, 1))
                write(c2 * (GMW // LANES) + hf, (y * cos_ref[:, 0:LANES] + rot * sin_ref[:, 0:LANES]) * scale)

    @pl.when(j == 0)
    def _():
        def write(c, val):
            q_ref[:, c * LANES:(c + 1) * LANES] = val.astype(q_ref.dtype)
        norm_rope(qg_ref, DA_HEAD_DIM ** -0.5 * math.log2(math.e), write)

    @pl.when(j == 1)
    def _():
        def write(c, val):
            k_ref[:, c * LANES:(c + 1) * LANES] = val
            kb_ref[:, c * LANES:(c + 1) * LANES] = val.astype(BF16)
        norm_rope(kg_ref, 1.0, write)

    @pl.when(j == 2)
    def _():
        v_ref[...] = acc
        vb_ref[...] = acc.astype(BF16)


def _qkv(x2d, sc, sh, g, w_qkv, qg, kg, cos_t, sin_t, gm, tm, seq):
    t, d = x2d.shape
    n = 1024
    per = cos_t.shape[0] // tm
    row = lambda i, j: (i, 0)
    return pl.pallas_call(
        _qkv_kernel,
        grid=(t // tm, 3),
        in_specs=[pl.BlockSpec((tm, d), row), _mod_spec(sc, tm, seq), _mod_spec(sh, tm, seq),
                  pl.BlockSpec((1, d), lambda i, j: (0, 0)),
                  pl.BlockSpec((d, n), lambda i, j: (0, j)),
                  pl.BlockSpec((1, GMW), lambda i, j: (0, 0)),
                  pl.BlockSpec((1, GMW), lambda i, j: (0, 0)),
                  pl.BlockSpec((tm, LANES), lambda i, j: (i % per, 0)),
                  pl.BlockSpec((tm, LANES), lambda i, j: (i % per, 0)),
                  pl.BlockSpec((GMW, GMW), lambda i, j: (0, 0))],
        out_specs=[pl.BlockSpec((tm, d), row)] + [pl.BlockSpec((tm, n), row)] * 5,
        out_shape=[jax.ShapeDtypeStruct((t, d), BF16),
                   jax.ShapeDtypeStruct((t, n), BF16), jax.ShapeDtypeStruct((t, n), F32),
                   jax.ShapeDtypeStruct((t, n), BF16), jax.ShapeDtypeStruct((t, n), F32),
                   jax.ShapeDtypeStruct((t, n), BF16)],
        compiler_params=_cparams("parallel", "arbitrary"),
        name="qkv_proj",
    )(x2d, sc, sh, g, w_qkv, qg, kg, cos_t, sin_t, gm)


def _proj_kernel(h_ref, w_ref, o_ref, *, act):
    acc = _dot(h_ref[...], w_ref[...])
    if act == "sigmoid":
        acc = 0.5 * jnp.tanh(0.5 * acc) + 0.5
    o_ref[...] = acc.astype(o_ref.dtype)


def _proj(h, w, tm, tn, act, out_dtype, name):
    t, d = h.shape
    n = w.shape[1]
    return pl.pallas_call(
        functools.partial(_proj_kernel, act=act),
        grid=(t // tm, n // tn),
        in_specs=[pl.BlockSpec((tm, d), lambda i, j: (i, 0)), pl.BlockSpec((d, tn), lambda i, j: (0, j))],
        out_specs=pl.BlockSpec((tm, tn), lambda i, j: (i, j)),
        out_shape=jax.ShapeDtypeStruct((t, n), out_dtype),
        compiler_params=_cparams("parallel", "arbitrary"),
        name=name,
    )(h, w)


def _subln(o, g):
    ms = jnp.mean(o * o, axis=-1, keepdims=True)
    return o * lax.rsqrt(ms + EPS) * g * (1.0 - LAM_INIT)


ATT_ROWS = 256
ATT_HEADS = 4


def _attn_prompt_kernel(qi_tab, ki_tab, lam_ref, q_ref, k_ref, v_ref, g_ref, o_ref,
                        qs_scr, vx_scr, m_scr, acc_scr, *, tq, tk):
    p = pl.program_id(2)
    qi = qi_tab[p]
    ki = ki_tab[p]
    ratio = tk // tq
    n_slab = tk // LANES

    heads = [slice(hh * LANES, (hh + 1) * LANES) for hh in range(ATT_HEADS)]

    @pl.when(ki == 0)
    def _():
        for hh, hs in enumerate(heads):
            q = q_ref[:, hs]
            lane = lax.broadcasted_iota(jnp.int32, q.shape, 1)
            zero = jnp.zeros_like(q)
            qs_scr[hh, 0:tq, :] = jnp.where(lane < DA_HEAD_DIM, q, zero)
            qs_scr[hh, tq:2 * tq, :] = jnp.where(lane >= DA_HEAD_DIM, q, zero)
            vx_scr[hh, :, LANES:2 * LANES] = jnp.ones((tk, LANES), BF16)
        m_scr[...] = jnp.full(m_scr.shape, NEG_INF, F32)
        acc_scr[...] = jnp.zeros(acc_scr.shape, F32)

    for hh, hs in enumerate(heads):
        vx_scr[hh, :, 0:LANES] = v_ref[:, hs]

    def step(masked):
        for rb in range(2 * tq // ATT_ROWS):
            rows = slice(rb * ATT_ROWS, (rb + 1) * ATT_ROWS)
            nk = min(tk, (rb * ATT_ROWS) % tq + ATT_ROWS) if (masked and ratio == 1) else tk
            for hh, hs in enumerate(heads):
                s = _dot_nt(qs_scr[hh, rows, :], k_ref[0:nk, hs])
                slabs = [s[:, c * LANES:(c + 1) * LANES] for c in range(nk // LANES)]
                if masked:
                    row = lax.broadcasted_iota(jnp.int32, (ATT_ROWS, LANES), 0) + rb * ATT_ROWS
                    lane = lax.broadcasted_iota(jnp.int32, (ATT_ROWS, LANES), 1)
                    qchunk = (row % tq) // CHUNK + (qi % ratio) * (tq // CHUNK)
                    first = (nk - ATT_ROWS) // LANES if ratio == 1 else 0
                    slabs = [jnp.where((c * LANES + lane) // CHUNK <= qchunk, sl, NEG_INF) if c >= first else sl
                             for c, sl in enumerate(slabs)]
                mt = slabs[0]
                for sl in slabs[1:]:
                    mt = jnp.maximum(mt, sl)
                m_prev = m_scr[hh, rows, :]
                m_new = jnp.maximum(m_prev, jnp.max(mt, axis=-1, keepdims=True))
                alpha = jnp.exp2(m_prev - m_new)
                e = jnp.concatenate([jnp.exp2(sl - m_new).astype(BF16) for sl in slabs], axis=1)
                pv = _dot(e, vx_scr[hh, 0:nk, :])
                acc_scr[hh, rows, :] = jnp.concatenate([alpha, alpha], axis=1) * acc_scr[hh, rows, :] + pv
                m_scr[hh, rows, :] = m_new

    @pl.when(ki < qi // ratio)
    def _():
        step(False)

    @pl.when(ki == qi // ratio)
    def _():
        step(True)
        for hh, hs in enumerate(heads):
            o = acc_scr[hh, :, 0:LANES] / acc_scr[hh, :, LANES:2 * LANES]
            o = o[0:tq] - lam_ref[...] * o[tq:2 * tq]
            o_ref[:, hs] = _subln(o, g_ref[...]).astype(o_ref.dtype)


def _attn_prompt(q, k, v, lam_row, subln_g, batch, seq, tq, tk):
    d = q.shape[1]
    assert tk % tq == 0 and seq % tk == 0 and tq % CHUNK == 0
    qi_l, ki_l = [], []
    for a in range(seq // tq):
        for b in range(a * tq // tk + 1):
            qi_l.append(a)
            ki_l.append(b)
    qi_tab = jnp.asarray(qi_l, jnp.int32)
    ki_tab = jnp.asarray(ki_l, jnp.int32)
    q3, k3, v3 = (a.reshape(batch, seq, d) for a in (q, k, v))
    hw = ATT_HEADS * LANES
    qspec = pl.BlockSpec((None, tq, hw), lambda b, h, p, qt, kt: (b, qt[p], h))
    kspec = pl.BlockSpec((None, tk, hw), lambda b, h, p, qt, kt: (b, kt[p], h))
    vec = pl.BlockSpec((1, LANES), lambda b, h, p, qt, kt: (0, 0))
    out = pl.pallas_call(
        functools.partial(_attn_prompt_kernel, tq=tq, tk=tk),
        grid_spec=pltpu.PrefetchScalarGridSpec(
            num_scalar_prefetch=2,
            grid=(batch, DA_HEADS // ATT_HEADS, len(qi_l)),
            in_specs=[vec, qspec, kspec, kspec, vec],
            out_specs=qspec,
            scratch_shapes=[pltpu.VMEM((ATT_HEADS, 2 * tq, LANES), BF16),
                            pltpu.VMEM((ATT_HEADS, tk, 2 * LANES), BF16),
                            pltpu.VMEM((ATT_HEADS, 2 * tq, LANES), F32),
                            pltpu.VMEM((ATT_HEADS, 2 * tq, 2 * LANES), F32)]),
        out_shape=jax.ShapeDtypeStruct((batch, seq, d), BF16),
        compiler_params=_cparams("parallel", "parallel", "arbitrary"),
        name="attn_prompt",
    )(qi_tab, ki_tab, lam_row, q3, k3, v3, subln_g)
    return out.reshape(batch * seq, d)


def _attn_sample_kernel(lam_ref, q_ref, kn_ref, vn_ref, ck_ref, cv_ref, g_ref, o_ref, *, past, sq):
    for h in range(DA_HEADS):
        hs = slice(h * LANES, (h + 1) * LANES)
        q = q_ref[:, hs]
        lane = lax.broadcasted_iota(jnp.int32, q.shape, 1)
        zero = jnp.zeros_like(q)
        qs = jnp.concatenate([jnp.where(lane < DA_HEAD_DIM, q, zero),
                              jnp.where(lane >= DA_HEAD_DIM, q, zero)], axis=0)
        kc = ck_ref[pl.ds(h, past, stride=DA_HEADS), :].astype(BF16)
        vc = cv_ref[pl.ds(h, past, stride=DA_HEADS), :].astype(BF16)
        s_c = _dot_nt(qs, kc)
        s_n = _dot_nt(qs, kn_ref[:, hs])

        def masked(s, key0):
            row = lax.broadcasted_iota(jnp.int32, s.shape, 0)
            col = lax.broadcasted_iota(jnp.int32, s.shape, 1)
            qchunk = (past + row % sq) // CHUNK
            return jnp.where((key0 + col) // CHUNK <= qchunk, s, NEG_INF)

        s_c = masked(s_c, 0)
        s_n = masked(s_n, past)
        m = jnp.maximum(jnp.max(s_c, axis=-1, keepdims=True), jnp.max(s_n, axis=-1, keepdims=True))
        e_c = jnp.exp2(s_c - m)
        e_n = jnp.exp2(s_n - m)
        l = jnp.sum(e_c, axis=-1, keepdims=True) + jnp.sum(e_n, axis=-1, keepdims=True)
        o = (_dot(e_c.astype(BF16), vc) + _dot(e_n.astype(BF16), vn_ref[:, hs])) / l
        o = o[0:sq] - lam_ref[...] * o[sq:2 * sq]
        o_ref[:, hs] = _subln(o, g_ref[...]).astype(o_ref.dtype)


def _attn_sample(q, kn, vn, cache_k, cache_v, lam_row, subln_g, batch, sq):
    d = q.shape[1]
    past = cache_k.shape[-3]
    ck = cache_k.reshape(batch, past * DA_HEADS, LANES)
    cv = cache_v.reshape(batch, past * DA_HEADS, LANES)
    q3, k3, v3 = (a.reshape(batch, sq, d) for a in (q, kn, vn))
    new = pl.BlockSpec((None, sq, d), lambda b: (b, 0, 0))
    old = pl.BlockSpec((None, past * DA_HEADS, LANES), lambda b: (b, 0, 0))
    vec = pl.BlockSpec((1, LANES), lambda b: (0, 0))
    out = pl.pallas_call(
        functools.partial(_attn_sample_kernel, past=past, sq=sq),
        grid=(batch,),
        in_specs=[vec, new, new, new, old, old, vec],
        out_specs=new,
        out_shape=jax.ShapeDtypeStruct((batch, sq, d), BF16),
        compiler_params=_cparams("parallel"),
        name="attn_sample",
    )(lam_row, q3, k3, v3, ck, cv, subln_g)
    return out.reshape(batch * sq, d)


RW_L = 64
RW_G = 2
RW_GW = RW_G * RW_HEAD
RW_GROUPS = 16 // RW_G
RW_CPS = 4


def _mm3(a, b, dot=_dot):
    ah, al = _split2(a)
    bh, bl = _split2(b)
    return dot(ah, bh) + dot(ah, bl) + dot(al, bh)


def _terms(x, n):
    return (x.astype(BF16),) if n == 1 else _split2(x)


def _mmt(at, bt, dot=_dot):
    acc = dot(at[0], bt[0])
    if len(bt) > 1:
        acc = acc + dot(at[0], bt[1])
    if len(at) > 1:
        acc = acc + dot(at[1], bt[0])
    return acc


RW_PREC = dict(a_all=(1, 1), neumann=(1, 1), av=(1, 1), pu=(1, 1), qy=(1, 1), mh=(1, 1), state=(1, 1))


def _softplus(z):
    return jnp.maximum(z, 0.0) + jnp.log(1.0 + jnp.exp(-jnp.abs(z)))


def _rwkv_kernel(rw_ref, sh0_ref, mu_ref, lw_ref, w0_ref, a0_ref, kk_ref, ka_ref, rk_ref, lng_ref, lnb_ref,
                 h0_ref, gm_ref, tri_ref, ob_ref, hout_ref, h_scr, last_scr, *, n_valid, cps):
    L = RW_L
    rows = cps * L
    c = pl.program_id(1)
    d = ob_ref.shape[-1]

    @pl.when(c == 0)
    def _():
        h_scr[...] = h0_ref[...]
        last_scr[...] = sh0_ref[...]

    rw = rw_ref[...]
    row_w = lax.broadcasted_iota(jnp.int32, rw.shape, 0)
    prev = jnp.where(row_w == 0, last_scr[...], pltpu.roll(rw, 1, 0))
    last_scr[...] = rw[rows - 1:rows, :]
    xm = rw + (prev - rw) * mu_ref[...]
    r = xm[:, 0:d]
    k = xm[:, d:2 * d]
    v = xm[:, 2 * d:3 * d]
    lo = xm[:, 3 * d:3 * d + LORA_PAD]
    lane_l = lax.broadcasted_iota(jnp.int32, lo.shape, 1)
    z = jnp.where(lane_l < W_LORA, jnp.tanh(lo),
                  jnp.where(lane_l < W_LORA + A_LORA, lo, jax.nn.sigmoid(lo)))
    lora = _dot(z.astype(BF16), lw_ref[...])
    w = -_softplus(-(w0_ref[...] + lora[:, 0:d])) - 0.5
    ld = -jnp.exp(w)
    a = jax.nn.sigmoid(a0_ref[...] + lora[:, d:2 * d])
    g = lora[:, 2 * d:3 * d]
    kkr = k * kk_ref[...]
    kmod = k * (1.0 + (a - 1.0) * ka_ref[...])
    rkk = r * kmod * rk_ref[...]
    gm = gm_ref[...]
    if n_valid < rows:
        valid = lax.broadcasted_iota(jnp.int32, ld.shape, 0) < n_valid
        ld = jnp.where(valid, ld, 0.0)
        kkr = jnp.where(valid, kkr, 0.0)
        kmod = jnp.where(valid, kmod, 0.0)

    tri = tri_ref[...]
    l1, l2, l3 = _split3(ld)
    cin = _dot(tri, l1) + _dot(tri, l2) + _dot(tri, l3)
    cex = cin - ld
    c_last = [cin[(ch + 1) * L - 1:(ch + 1) * L, :] for ch in range(cps)]
    e_in = jnp.exp(cin)
    e_ex = jnp.exp(cex)
    e_neg = jnp.exp(-cin)
    e_tail = jnp.exp(jnp.concatenate([jnp.broadcast_to(cl, (L, d)) for cl in c_last], axis=0) - cin)
    g_last = [jnp.exp(cl) for cl in c_last]

    GW = RW_GW
    row = lax.broadcasted_iota(jnp.int32, (L, GW), 0)
    lane = lax.broadcasted_iota(jnp.int32, (L, GW), 1)
    head_of_lane = lane // RW_HEAD
    tri_s = row > (lane % L)
    tri_i = row >= (lane % L)
    eye_pair = jnp.where(row == (lane % L), 1.0, 0.0).astype(F32)
    r2 = lax.broadcasted_iota(jnp.int32, (GW, GW), 0)
    c2 = lax.broadcasted_iota(jnp.int32, (GW, GW), 1)
    bd_mask = (r2 // RW_HEAD) == (c2 // RW_HEAD)
    diag_mask = r2 == c2

    def bd(x):
        z0 = jnp.zeros_like(x)
        return jnp.concatenate([jnp.where(head_of_lane == j, x, z0) for j in range(RW_G)], axis=0)

    def bdt(t):
        return tuple(bd(x) for x in t)

    def cat(ts, axis):
        return tuple(jnp.concatenate(xs, axis=axis) for xs in zip(*ts))

    prec = RW_PREC
    GL = RW_G * L
    items = [(slice(ch * L, (ch + 1) * L), slice(p * GW, (p + 1) * GW))
             for ch in range(cps) for p in range(RW_GROUPS)]
    pairs = range(len(items))

    v_p = [v[rs, sl] for rs, sl in items]
    kk_n = []
    for q in range(d // GMW):
        kk_raw = kkr[:, q * GMW:(q + 1) * GMW]
        ss = _group_mean(kk_raw * kk_raw, gm) * RW_HEAD
        kk_n.append(kk_raw / jnp.maximum(jnp.sqrt(ss), 1e-12))
    kk_n = jnp.concatenate(kk_n, axis=1)
    kk_p = [kk_n[rs, sl] for rs, sl in items]
    b_p = [kk_p[i] * a[rs, sl] for i, (rs, sl) in enumerate(items)]
    abar = [-kk_p[i] * e_ex[rs, sl] for i, (rs, sl) in enumerate(items)]
    rbar = [r[rs, sl] * e_in[rs, sl] for rs, sl in items]
    bbar = [b_p[i] * e_neg[rs, sl] for i, (rs, sl) in enumerate(items)]
    kbar = [kmod[rs, sl] * e_neg[rs, sl] for rs, sl in items]
    btil = [b_p[i] * e_tail[rs, sl] for i, (rs, sl) in enumerate(items)]
    ktil = [kmod[rs, sl] * e_tail[rs, sl] for rs, sl in items]

    na, nb = prec["a_all"]
    abar_t = [_terms(abar[p], max(na, prec["pu"][1])) for p in pairs]
    v_t = [_terms(v_p[p], max(prec["av"][1], prec["qy"][1], prec["mh"][1])) for p in pairs]
    a_all = []
    for p in pairs:
        lhs = cat([abar_t[p][:na], _terms(rbar[p], na)], 0)
        rhs = cat([bdt(_terms(bbar[p], nb)), bdt(_terms(kbar[p], nb))], 0)
        a_all.append(_mmt(lhs, rhs, _dot_nt))
    n_ab = [jnp.where(tri_s, a_all[p][0:L, 0:GL], 0.0) for p in pairs]
    a_ak = [jnp.where(tri_s, a_all[p][0:L, GL:2 * GL], 0.0) for p in pairs]
    a_rb = [jnp.where(tri_i, a_all[p][L:2 * L, 0:GL], 0.0) for p in pairs]
    a_rk = [jnp.where(tri_i, a_all[p][L:2 * L, GL:2 * GL], 0.0) for p in pairs]

    na, nb = prec["neumann"]
    t_inv = [eye_pair + n_ab[p] for p in pairs]
    pw_t = [_terms(n_ab[p], max(na, nb)) for p in pairs]
    pw = [_mmt(pw_t[p][:na], bdt(pw_t[p][:nb])) for p in pairs]
    for _ in range(int(math.log2(L)) - 2):
        pw_t = [_terms(pw[p], max(na, nb)) for p in pairs]
        st = [_mmt(cat([_terms(t_inv[p], na), pw_t[p][:na]], 0), bdt(pw_t[p][:nb])) for p in pairs]
        t_inv = [t_inv[p] + st[p][0:L] for p in pairs]
        pw = [st[p][L:2 * L] for p in pairs]
    t_inv = [t_inv[p] + _mmt(_terms(t_inv[p], na), bdt(_terms(pw[p], nb))) for p in pairs]

    na, nb = prec["av"]
    vv = [_mmt(cat([_terms(a_ak[p], na), _terms(a_rk[p], na)], 0), bdt(v_t[p][:nb])) for p in pairs]
    av = [x[0:L] for x in vv]
    arkv = [x[L:2 * L] for x in vv]
    na, nb = prec["pu"]
    pu = [_mmt(_terms(t_inv[p], na), cat([bdt(abar_t[p][:nb]), bdt(_terms(av[p], nb))], 1)) for p in pairs]
    p_m = [x[:, 0:GW] for x in pu]
    u0 = [x[:, GW:2 * GW] for x in pu]
    na, nb = prec["qy"]
    nmh = prec["mh"][1]
    p_t = [_terms(p_m[p], max(nb, nmh)) for p in pairs]
    u_t = [_terms(u0[p], max(nb, nmh)) for p in pairs]
    qy = [_mmt(_terms(a_rb[p], na), cat([bdt(p_t[p][:nb]), bdt(u_t[p][:nb])], 1)) for p in pairs]
    q_m = [rbar[p] + qy[p][:, 0:GW] for p in pairs]
    y0 = [qy[p][:, GW:2 * GW] + arkv[p] for p in pairs]

    na, nb = prec["mh"]
    btil_t = [_terms(btil[p], na) for p in pairs]
    m_full = [_mmt(btil_t[p], p_t[p][:nb], _dot_tn) for p in pairs]
    h0_full = [_mmt(cat([btil_t[p], _terms(ktil[p], na)], 0), cat([u_t[p][:nb], v_t[p][:nb]], 0), _dot_tn)
               for p in pairs]
    m_bd = [jnp.where(bd_mask, m_full[i], 0.0) + jnp.where(diag_mask, g_last[i // RW_GROUPS][:, sl], 0.0)
            for i, (rs, sl) in enumerate(items)]
    h0_bd = [jnp.where(bd_mask, h0_full[p], 0.0) for p in pairs]

    na, nb = prec["state"]
    h_cur = [h_scr[p] for p in range(RW_GROUPS)]
    y = []
    for ch in range(cps):
        idx = [ch * RW_GROUPS + p for p in range(RW_GROUPS)]
        st = [_mmt(cat([_terms(q_m[i], na), _terms(m_bd[i], na)], 0), _terms(h_cur[p], nb))
              for p, i in enumerate(idx)]
        y.extend(st[p][0:L] + y0[i] for p, i in enumerate(idx))
        h_cur = [st[p][L:L + GW] + h0_bd[i] for p, i in enumerate(idx)]
    for p in range(RW_GROUPS):
        h_scr[p] = h_cur[p]

    per = GMW // GW
    for ch in range(cps):
        rs = slice(ch * L, (ch + 1) * L)
        for q in range(RW_GROUPS // per):
            sl = slice(q * GMW, (q + 1) * GMW)
            yq = jnp.concatenate([y[ch * RW_GROUPS + q * per + j] for j in range(per)], axis=1)
            mu_y = _group_mean(yq, gm)
            dy = yq - mu_y
            var = _dot((dy * dy).astype(BF16), gm)
            yn = dy * lax.rsqrt(var + RW_GN_EPS) * lng_ref[:, sl] + lnb_ref[:, sl]
            bonus = _dot(rkk[rs, sl].astype(BF16), gm) * RW_HEAD * v[rs, sl]
            ob_ref[rs, sl] = ((yn + bonus) * g[rs, sl]).astype(ob_ref.dtype)

    @pl.when(c == pl.num_programs(1) - 1)
    def _():
        hout_ref[...] = h_scr[...]


def _rwkv(rw3, shift0, h0_bd, prm, gm, n_valid, cps):
    batch, seq, wcols = rw3.shape
    d = prm["w0"].shape[1]
    rows = cps * RW_L
    nc = seq // rows
    tri = jnp.kron(jnp.eye(cps, dtype=F32), jnp.tril(jnp.ones((RW_L, RW_L), F32))).astype(BF16)
    const2 = lambda b, c: (0, 0)
    vec = pl.BlockSpec((1, d), const2)
    st = pl.BlockSpec((None, RW_GROUPS, RW_GW, RW_GW), lambda b, c: (b, 0, 0, 0))
    return pl.pallas_call(
        functools.partial(_rwkv_kernel, n_valid=n_valid, cps=cps),
        grid=(batch, nc),
        in_specs=[pl.BlockSpec((None, rows, wcols), lambda b, c: (b, c, 0)),
                  pl.BlockSpec((None, 1, wcols), lambda b, c: (b, 0, 0)),
                  pl.BlockSpec((1, wcols), const2),
                  pl.BlockSpec(prm["lw"].shape, const2),
                  vec, vec, vec, vec, vec, vec, vec, st,
                  pl.BlockSpec((GMW, GMW), const2),
                  pl.BlockSpec((rows, rows), const2)],
        out_specs=[pl.BlockSpec((None, rows, d), lambda b, c: (b, c, 0)), st],
        out_shape=[jax.ShapeDtypeStruct((batch, seq, d), BF16),
                   jax.ShapeDtypeStruct((batch, RW_GROUPS, RW_GW, RW_GW), F32)],
        scratch_shapes=[pltpu.VMEM((RW_GROUPS, RW_GW, RW_GW), F32), pltpu.VMEM((1, wcols), F32)],
        compiler_params=_cparams("parallel", "arbitrary"),
        name="rwkv7",
    )(rw3, shift0, prm["mu"], prm["lw"], prm["w0"], prm["a0"], prm["k_k"], prm["k_a"], prm["r_k"],
      prm["ln_g"], prm["ln_b"], h0_bd, gm, tri)


def _state_to_bd(s):
    b = s.shape[0]
    ht = jnp.swapaxes(s, -1, -2).reshape(b, RW_GROUPS, RW_G, RW_HEAD, RW_HEAD)
    z = jnp.zeros_like(ht[:, :, 0])
    rows = [jnp.concatenate([ht[:, :, j] if i == j else z for i in range(RW_G)], axis=-1) for j in range(RW_G)]
    return jnp.concatenate(rows, axis=-2)


def _bd_to_state(hbd):
    b = hbd.shape[0]
    blocks = [hbd[:, :, j * RW_HEAD:(j + 1) * RW_HEAD, j * RW_HEAD:(j + 1) * RW_HEAD] for j in range(RW_G)]
    ht = jnp.stack(blocks, axis=2).reshape(b, RW_G * RW_GROUPS, RW_HEAD, RW_HEAD)
    return jnp.swapaxes(ht, -1, -2)


def _route(logits):
    lane = lax.broadcasted_iota(jnp.int32, logits.shape, 1).astype(F32)
    big = float(LANES)
    lg = jnp.where(lane < N_GROUPS, logits, NEG_INF)
    mg = jnp.max(lg, axis=-1, keepdims=True)
    sg = jnp.sum(jnp.exp(lg - mg), axis=-1, keepdims=True)
    p_top = 1.0 / sg
    g_idx = jnp.min(jnp.where(lg == mg, lane, big), axis=-1, keepdims=True)
    e0 = N_GROUPS + EXPERTS_PER_GROUP * g_idx
    emask = jnp.where(lane >= e0, jnp.where(lane < e0 + EXPERTS_PER_GROUP, 1.0, 0.0), 0.0) > 0.5
    le = jnp.where(emask, logits, NEG_INF)
    me = jnp.max(le, axis=-1, keepdims=True)
    ee = jnp.exp(le - me)
    pe = ee / jnp.sum(ee, axis=-1, keepdims=True)
    pe = jnp.where(emask, pe, -1.0)
    v1 = jnp.max(pe, axis=-1, keepdims=True)
    i1 = jnp.min(jnp.where(pe == v1, lane, big), axis=-1, keepdims=True)
    pe2 = jnp.where(lane == i1, -1.0, pe)
    v2 = jnp.max(pe2, axis=-1, keepdims=True)
    i2 = jnp.min(jnp.where(pe2 == v2, lane, big), axis=-1, keepdims=True)
    den = v1 + v2
    ew = jnp.where(lane == i1, v1 / den, 0.0) + jnp.where(lane == i2, v2 / den, 0.0)
    return p_top * ew + jnp.where(lane == 0.0, g_idx, 0.0)


def _merge_kernel(oa_ref, ob_ref, gates_ref, x_ref, gt1_ref, sc2_ref, sh2_ref, g2_ref,
                  woa_ref, wob_ref, wout_ref, wr_ref, br_ref, x1_ref, hx_ref):
    d = x_ref.shape[1]
    a_out = _dot(oa_ref[...], woa_ref[...])
    b_out = _dot(ob_ref[...], wob_ref[...])
    merged = gates_ref[:, 0:d].astype(F32) * a_out + gates_ref[:, d:2 * d].astype(F32) * b_out
    x1 = x_ref[...] + gt1_ref[...] * _dot(merged.astype(BF16), wout_ref[...])
    x1_ref[...] = x1
    ms = jnp.mean(x1 * x1, axis=-1, keepdims=True)
    h2 = x1 * lax.rsqrt(ms + EPS) * g2_ref[...]
    h2 = h2 * (1.0 + sc2_ref[...]) + sh2_ref[...]
    hh, hl = _split2(h2)
    wh, wl = _split2(wr_ref[...])
    part = _dot(hh, jnp.concatenate([wh, wl], axis=1))
    logits = part[:, 0:LANES] + part[:, LANES:2 * LANES] + _dot(hl, wh) + br_ref[...]
    hx_ref[:, 0:d] = h2
    hx_ref[:, d:d + LANES] = _route(logits)


def _merge(oa, ob, gates, x2d, gt1, sc2, sh2, g2, w_oa, w_ob, w_out, wr, br, tm, seq):
    t, d = x2d.shape
    row = lambda i: (i, 0)
    const = lambda i: (0, 0)
    full = lambda w: pl.BlockSpec(w.shape, const)
    return pl.pallas_call(
        _merge_kernel,
        grid=(t // tm,),
        in_specs=[pl.BlockSpec((tm, d), row), pl.BlockSpec((tm, d), row), pl.BlockSpec((tm, 2 * d), row),
                  pl.BlockSpec((tm, d), row), _mod_spec(gt1, tm, seq), _mod_spec(sc2, tm, seq),
                  _mod_spec(sh2, tm, seq), pl.BlockSpec((1, d), const),
                  full(w_oa), full(w_ob), full(w_out), full(wr), full(br)],
        out_specs=[pl.BlockSpec((tm, d), row), pl.BlockSpec((tm, d + LANES), row)],
        out_shape=[jax.ShapeDtypeStruct((t, d), F32), jax.ShapeDtypeStruct((t, d + LANES), F32)],
        compiler_params=_cparams("parallel"),
        name="merge_router",
    )(oa, ob, gates, x2d, gt1, sc2, sh2, g2, w_oa, w_ob, w_out, wr, br)


def _expert_ffn(xb, comb, lane_e, w13, w2):
    au = _dot(xb, w13)
    f = au.shape[1] // 2
    a = au[:, 0:f]
    u = au[:, f:2 * f]
    lane = lax.broadcasted_iota(jnp.int32, comb.shape, 1)
    cw = jnp.sum(jnp.where(lane == lane_e, comb, 0.0), axis=-1, keepdims=True)
    act = a * jax.nn.sigmoid(a) * u * cw
    return _dot(act.astype(BF16), w2)


def _moe_dense_kernel(hx_ref, w13_ref, w2_ref, x1_ref, gt2_ref, y_ref, xb_scr, acc_scr):
    e = pl.program_id(1)
    d = y_ref.shape[1]

    @pl.when(e == 0)
    def _():
        xb_scr[...] = hx_ref[:, 0:d].astype(BF16)
        acc_scr[...] = jnp.zeros(acc_scr.shape, F32)

    acc_scr[...] += _expert_ffn(xb_scr[...], hx_ref[:, d:d + LANES], e + N_GROUPS, w13_ref[...], w2_ref[...])

    @pl.when(e == pl.num_programs(1) - 1)
    def _():
        y_ref[...] = x1_ref[...] + gt2_ref[...] * acc_scr[...]


def _moe_dense(hx, w13, w2, x1, gt2, tm, seq):
    t, d = x1.shape
    ne, _, f2 = w13.shape
    row = lambda i, e: (i, 0)
    return pl.pallas_call(
        _moe_dense_kernel,
        grid=(t // tm, ne),
        in_specs=[pl.BlockSpec((tm, d + LANES), row),
                  pl.BlockSpec((None, d, f2), lambda i, e: (e, 0, 0)),
                  pl.BlockSpec((None, f2 // 2, d), lambda i, e: (e, 0, 0)),
                  pl.BlockSpec((tm, d), row), _mod_spec(gt2, tm, seq)],
        out_specs=pl.BlockSpec((tm, d), row),
        out_shape=jax.ShapeDtypeStruct((t, d), F32),
        scratch_shapes=[pltpu.VMEM((tm, d), BF16), pltpu.VMEM((tm, d), F32)],
        compiler_params=_cparams("parallel", "arbitrary"),
        name="moe_dense",
    )(hx, w13, w2, x1, gt2)


MOE_TM = 1024
MOE_TS = 1024


def _plan_kernel(rt_ref, tri_ref, rank_ref, cnt_ref, carry_scr):
    @pl.when(pl.program_id(0) == 0)
    def _():
        carry_scr[...] = jnp.zeros(carry_scr.shape, F32)

    rt = rt_ref[...]
    lane = lax.broadcasted_iota(jnp.int32, rt.shape, 1).astype(F32)
    onehot = jnp.where(lane == rt[:, 0:1], 1.0, 0.0)
    before = _dot(tri_ref[...], onehot.astype(BF16)) + carry_scr[...]
    rank_ref[...] = jnp.sum(onehot * before, axis=-1, keepdims=True).astype(jnp.int32)
    carry_scr[...] += jnp.sum(onehot, axis=0, keepdims=True)
    cnt_ref[...] = carry_scr[...]


def _moe_plan(hx, d):
    t = hx.shape[0]
    tp = 512
    tri = jnp.tril(jnp.ones((tp, tp), F32), -1).astype(BF16)
    return pl.pallas_call(
        _plan_kernel,
        grid=(t // tp,),
        in_specs=[pl.BlockSpec((tp, LANES), lambda i: (i, d // LANES)),
                  pl.BlockSpec((tp, tp), lambda i: (0, 0))],
        out_specs=[pl.BlockSpec((tp, 1), lambda i: (i, 0)), pl.BlockSpec((1, LANES), lambda i: (0, 0))],
        out_shape=[jax.ShapeDtypeStruct((t, 1), jnp.int32), jax.ShapeDtypeStruct((1, LANES), F32)],
        scratch_shapes=[pltpu.VMEM((1, LANES), F32)],
        compiler_params=_cparams("arbitrary"),
        name="moe_plan",
    )(hx, tri)


def _row_copy(src_hbm, src_row, dst, dst_row, sem):
    return pltpu.make_async_copy(src_hbm.at[pl.ds(src_row, 1), :], dst.at[pl.ds(dst_row, 1), :], sem)


def _dispatch_kernel(pos_ref, hx_ref, xs_in_hbm, xs_hbm, sem):
    del xs_in_hbm

    def issue(t, carry):
        _row_copy(hx_ref, t, xs_hbm, pos_ref[t], sem).start()
        return carry

    def drain(t, carry):
        _row_copy(hx_ref, t, xs_hbm, pos_ref[t], sem).wait()
        return carry

    lax.fori_loop(0, MOE_TS, issue, 0, unroll=8)
    lax.fori_loop(0, MOE_TS, drain, 0, unroll=8)


def _moe_dispatch(pos, hx, n_rows):
    t, w = hx.shape
    xs0 = jnp.zeros((n_rows, w), F32)
    return pl.pallas_call(
        _dispatch_kernel,
        grid=(t // MOE_TS,),
        in_specs=[pl.BlockSpec((MOE_TS,), lambda i: (i,), memory_space=pltpu.SMEM),
                  pl.BlockSpec((MOE_TS, w), lambda i: (i, 0)), pl.BlockSpec(memory_space=pl.ANY)],
        out_specs=pl.BlockSpec(memory_space=pl.ANY),
        out_shape=jax.ShapeDtypeStruct((n_rows, w), F32),
        scratch_shapes=[pltpu.SemaphoreType.DMA(())],
        input_output_aliases={2: 0},
        compiler_params=_cparams("arbitrary"),
        name="moe_dispatch",
    )(pos, hx, xs0)


def _moe_bucket_kernel(tg_ref, tv_ref, xs_ref, w13_ref, w2_ref, ys_ref, xb_scr):
    i = pl.program_id(0)
    e = pl.program_id(1)
    d = ys_ref.shape[1]

    @pl.when(e == 0)
    def _():
        xb_scr[...] = xs_ref[:, 0:d].astype(BF16)
        ys_ref[...] = jnp.zeros(ys_ref.shape, F32)

    @pl.when(tv_ref[i] > 0)
    def _():
        lane_e = N_GROUPS + tg_ref[i] * EXPERTS_PER_GROUP + e
        ys_ref[...] += _expert_ffn(xb_scr[...], xs_ref[:, d:d + LANES], lane_e, w13_ref[...], w2_ref[...])


def _moe_buckets(tile_group, tile_valid, xs, w13, w2, d):
    n_rows, w = xs.shape
    _, _, f2 = w13.shape
    row = lambda i, e, tg, tv: (i, 0)
    wsel = lambda i, e, tg, tv: (tg[i] * EXPERTS_PER_GROUP + e, 0, 0)
    return pl.pallas_call(
        _moe_bucket_kernel,
        grid_spec=pltpu.PrefetchScalarGridSpec(
            num_scalar_prefetch=2,
            grid=(n_rows // MOE_TM, EXPERTS_PER_GROUP),
            in_specs=[pl.BlockSpec((MOE_TM, w), row),
                      pl.BlockSpec((None, d, f2), wsel), pl.BlockSpec((None, f2 // 2, d), wsel)],
            out_specs=pl.BlockSpec((MOE_TM, d), row),
            scratch_shapes=[pltpu.VMEM((MOE_TM, d), BF16)]),
        out_shape=jax.ShapeDtypeStruct((n_rows, d), F32),
        compiler_params=_cparams("parallel", "arbitrary"),
        name="moe_buckets",
    )(tile_group, tile_valid, xs, w13, w2)


def _collect_kernel(pos_ref, ys_hbm, x1_ref, gt2_ref, y_ref, buf, sem):
    tu = buf.shape[0]

    def issue(t, carry):
        _row_copy(ys_hbm, pos_ref[t], buf, t, sem).start()
        return carry

    def drain(t, carry):
        _row_copy(ys_hbm, pos_ref[t], buf, t, sem).wait()
        return carry

    lax.fori_loop(0, tu, issue, 0, unroll=8)
    lax.fori_loop(0, tu, drain, 0, unroll=8)
    y_ref[...] = x1_ref[...] + gt2_ref[...] * buf[...]


def _moe_collect(pos, ys, x1, gt2, seq):
    t, d = x1.shape
    tu = MOE_TS
    return pl.pallas_call(
        _collect_kernel,
        grid=(t // tu,),
        in_specs=[pl.BlockSpec((tu,), lambda i: (i,), memory_space=pltpu.SMEM),
                  pl.BlockSpec(memory_space=pl.ANY),
                  pl.BlockSpec((tu, d), lambda i: (i, 0)), _mod_spec(gt2, tu, seq)],
        out_specs=pl.BlockSpec((tu, d), lambda i: (i, 0)),
        out_shape=jax.ShapeDtypeStruct((t, d), F32),
        scratch_shapes=[pltpu.VMEM((tu, d), F32), pltpu.SemaphoreType.DMA(())],
        compiler_params=_cparams("arbitrary"),
        name="moe_collect",
    )(pos, ys, x1, gt2)


def _moe_sorted(hx, w13, w2, x1, gt2, seq):
    t, d = x1.shape
    rank, cnt = _moe_plan(hx, d)
    group = hx[:, d].astype(jnp.int32)
    counts = cnt[0, 0:N_GROUPS].astype(jnp.int32)
    padded = (counts + MOE_TM - 1) // MOE_TM * MOE_TM
    ends = jnp.cumsum(padded)
    starts = ends - padded
    pos = jnp.take(starts, group) + rank[:, 0]
    n_tiles = t // MOE_TM + N_GROUPS
    tile_start = jnp.arange(n_tiles, dtype=jnp.int32) * MOE_TM
    tile_group = jnp.minimum(jnp.sum((tile_start[:, None] >= ends[None, :]).astype(jnp.int32), axis=1),
                             N_GROUPS - 1)
    tile_valid = (tile_start < ends[-1]).astype(jnp.int32)
    xs = _moe_dispatch(pos, hx, n_tiles * MOE_TM)
    ys = _moe_buckets(tile_group, tile_valid, xs, w13, w2, d)
    return _moe_collect(pos, ys, x1, gt2, seq)


def _rope_tables(pos0, seq, reps):
    pos = (pos0 + jnp.arange(seq)).astype(F32)
    inv = ROPE_THETA ** (-jnp.arange(0, DA_HEAD_DIM, 2, dtype=F32) / DA_HEAD_DIM)
    ang = pos[:, None] * inv[None, :]
    cos, sin = jnp.cos(ang), jnp.sin(ang)
    cos_t = jnp.tile(jnp.concatenate([cos, cos], axis=-1), (reps, LANES // DA_HEAD_DIM))
    sin_t = jnp.tile(jnp.concatenate([-sin, sin], axis=-1), (reps, LANES // DA_HEAD_DIM))
    return cos_t, sin_t


def _layer(x, mod, pos0, past, wts):
    bx, sx, d = x.shape
    t = bx * sx
    x2d = x.reshape(t, d)
    per_batch = sx >= 512
    tm = 512 if per_batch else t

    def modv(i):
        if per_batch:
            return mod[:, i:i + 1, :]
        return jnp.repeat(mod[:, i, :], sx, axis=0)[None]

    sh1, sc1, gt1, sh2, sc2, gt2 = (modv(i) for i in range(6))
    cos_t, sin_t = _rope_tables(pos0, sx, 1 if per_batch else bx)

    tm_big = 1024 if (per_batch and sx % 1024 == 0) else tm
    h, q, k32, kb, v32, vb = _qkv(x2d, sc1, sh1, wts["norm1_g"], wts["w_qkv"], wts["qg"], wts["kg"],
                                  cos_t, sin_t, wts["gm"], tm_big, sx)
    rw = _proj(h, wts["w_rw"], tm_big, wts["w_rw"].shape[1] // 3, None, F32, "rw_proj")
    gates = _proj(h, wts["w_gate"], tm_big, d, "sigmoid", BF16, "gate_proj")

    if past is None:
        oa = _attn_prompt(q, kb, vb, wts["lam"], wts["subln_g"], bx, sx, min(1024, sx), min(1024, sx))
        shift0 = jnp.zeros((bx, 1, rw.shape[1]), F32)
        h0_bd = jnp.zeros((bx, RW_GROUPS, RW_GW, RW_GW), F32)
    else:
        ck, cv, s0, sh0 = past
        oa = _attn_sample(q, kb, vb, ck, cv, wts["lam"], wts["subln_g"], bx, sx)
        shift0 = jnp.pad(sh0, ((0, 0), (0, 0), (0, rw.shape[1] - sh0.shape[-1])))
        h0_bd = _state_to_bd(s0.astype(F32))

    rw3 = rw.reshape(bx, sx, rw.shape[1])
    if sx % RW_L == 0:
        cps = RW_CPS if sx % (RW_CPS * RW_L) == 0 else 1
        rw_in, n_valid = rw3, cps * RW_L
    else:
        assert sx < RW_L
        cps = 1
        rw_in, n_valid = jnp.pad(rw3, ((0, 0), (0, RW_L - sx), (0, 0))), sx
    ob, h_bd = _rwkv(rw_in, shift0, h0_bd, wts, wts["gm"], n_valid, cps)
    ob = ob[:, 0:sx, :].reshape(t, d)

    x1, hx = _merge(oa, ob, gates, x2d, gt1, sc2, sh2, wts["norm2_g"],
                    wts["w_oa"], wts["w_ob"], wts["w_out"], wts["wr"], wts["br"], tm, sx)
    if per_batch and sx % MOE_TS == 0:
        y = _moe_sorted(hx, wts["w13"], wts["w2"], x1, gt2, sx)
    else:
        y = _moe_dense(hx, wts["w13"], wts["w2"], x1, gt2, tm, sx)

    n_cols = 3 * d + W_LORA + A_LORA + G_LORA
    return (y.reshape(bx, sx, d),
            k32.reshape(1, bx, sx, DA_HEADS, 2 * DA_HEAD_DIM),
            v32.reshape(1, bx, sx, DA_HEADS, 2 * DA_HEAD_DIM),
            _bd_to_state(h_bd)[None],
            rw3[:, sx - 1:sx, 0:n_cols][None])


def kernel(x_prompt, x_sample, cache_k, cache_v, state_wkv, state_shift, c_prompt, c_sample, norm1_g, norm2_g, w_ada, b_ada, w_in, da_qn_g, da_kn_g, da_lambda, da_subln_g, w_oa, rw_mu, rw_w0, rw_w2, rw_a0, rw_a2, rw_g2, rw_k_k, rw_k_a, rw_r_k, rw_ln_g, rw_ln_b, w_ob, w_out, router_g, router_g_b, router_e, router_e_b, exp_w1, exp_w3, exp_w2):
    b, s, d = x_prompt.shape
    bs = x_sample.shape[0]
    past_len = cache_k.shape[2]
    assert w_in.shape[0] == 1, "single layer"

    c_all = jnp.concatenate([c_prompt, c_sample], axis=0)
    c_all = jnp.pad(c_all, ((0, (-c_all.shape[0]) % 8), (0, 0)))
    mod, lam_tile = _adaln(c_all, w_ada[0], b_ada[0], da_lambda[0])
    mod_p = mod[0:b].reshape(b, 6, d)
    mod_s = mod[b:b + bs].reshape(bs, 6, d)

    win = w_in[0]
    rw_cols = 3 * d + W_LORA + A_LORA + G_LORA
    rw_pad = 3 * d + LORA_PAD
    w_rw = jnp.pad(win[:, 3 * d:3 * d + rw_cols], ((0, 0), (0, rw_pad - rw_cols))).astype(BF16)
    lw = jnp.zeros((LORA_PAD, 3 * d), F32)
    lw = lw.at[0:W_LORA, 0:d].set(rw_w2[0])
    lw = lw.at[W_LORA:W_LORA + A_LORA, d:2 * d].set(rw_a2[0])
    lw = lw.at[W_LORA + A_LORA:W_LORA + A_LORA + G_LORA, 2 * d:3 * d].set(rw_g2[0])
    f = exp_w1.shape[-1]
    wr = jnp.pad(jnp.concatenate([router_g[0], router_e[0]], axis=1), ((0, 0), (0, LANES - N_GROUPS - N_EXPERTS)))
    br = jnp.pad(jnp.concatenate([router_g_b[0], router_e_b[0]]), (0, LANES - N_GROUPS - N_EXPERTS))[None]
    half = jnp.ones((DA_HEAD_DIM, DA_HEAD_DIM), F32) / DA_HEAD_DIM
    wts = dict(
        norm1_g=norm1_g[0][None], norm2_g=norm2_g[0][None],
        w_qkv=win[:, 0:3 * d].astype(BF16), w_rw=w_rw, w_gate=win[:, 3 * d + rw_cols:].astype(BF16),
        qg=jnp.tile(da_qn_g[0], GMW // DA_HEAD_DIM)[None], kg=jnp.tile(da_kn_g[0], GMW // DA_HEAD_DIM)[None],
        gm=jnp.kron(jnp.eye(GMW // DA_HEAD_DIM, dtype=F32), half).astype(BF16),
        lam=lam_tile[0:1], subln_g=da_subln_g[0][None],
        mu=jnp.pad(rw_mu[0], (0, rw_pad - rw_cols))[None], lw=lw.astype(BF16),
        w0=rw_w0[0][None], a0=rw_a0[0][None], k_k=rw_k_k[0][None], k_a=rw_k_a[0][None],
        r_k=rw_r_k[0].reshape(1, d), ln_g=rw_ln_g[0][None], ln_b=rw_ln_b[0][None],
        w_oa=w_oa[0].astype(BF16), w_ob=w_ob[0].astype(BF16), w_out=w_out[0].astype(BF16),
        wr=wr, br=br,
        w13=jnp.concatenate([exp_w1[0], exp_w3[0]], axis=-1).reshape(N_EXPERTS, d, 2 * f).astype(BF16),
        w2=exp_w2[0].reshape(N_EXPERTS, f, d).astype(BF16),
    )

    out_p = _layer(x_prompt, mod_p, 0, None, wts)
    out_s = _layer(x_sample, mod_s, past_len,
                   (cache_k, cache_v, state_wkv[0], state_shift[0]), wts)
    return (out_p[0], out_s[0], out_p[1], out_p[2], out_p[3], out_p[4],
            out_s[1], out_s[2], out_s[3], out_s[4])
```

```python
import functools
import math

import jax
import jax.numpy as jnp
from jax import lax
from jax.experimental import pallas as pl
from jax.experimental.pallas import tpu as pltpu

F32 = jnp.float32
BF16 = jnp.bfloat16

EPS = 1e-6
NEG_INF = -1e30
CHUNK = 64
DA_HEADS = 8
DA_HEAD_DIM = 64
ROPE_THETA = 10000.0
RW_HEAD = 64
RW_GN_EPS = 64e-5
W_LORA, A_LORA, G_LORA = 64, 64, 160
N_GROUPS, EXPERTS_PER_GROUP = 4, 8
N_EXPERTS = N_GROUPS * EXPERTS_PER_GROUP
LAM_INIT = 0.8 - 0.6 * math.exp(-0.3 * 0)

LANES = 128
LORA_PAD = 384
VMEM_LIMIT = 56 * 1024 * 1024


def _cparams(*sem):
    return pltpu.CompilerParams(dimension_semantics=sem, vmem_limit_bytes=VMEM_LIMIT)


def _dot(a, b):
    return jnp.dot(a, b, preferred_element_type=F32)


def _dot_nt(a, b):
    return lax.dot_general(a, b, (((1,), (1,)), ((), ())), preferred_element_type=F32)


def _dot_tn(a, b):
    return lax.dot_general(a, b, (((0,), (0,)), ((), ())), preferred_element_type=F32)


def _split2(a):
    hi = a.astype(BF16)
    lo = (a - hi.astype(F32)).astype(BF16)
    return hi, lo


def _split3(a):
    hi = a.astype(BF16)
    r = a - hi.astype(F32)
    mid = r.astype(BF16)
    lo = (r - mid.astype(F32)).astype(BF16)
    return hi, mid, lo


GMW = 256


def _group_mean(sq, gm):
    hi, lo = _split2(sq)
    return _dot(hi, gm) + _dot(lo, gm)


def _adaln_kernel(c_ref, w_ref, b_ref, l_ref, o_ref, lam_ref):
    c = c_ref[...]
    sc = (c * jax.nn.sigmoid(c)).astype(BF16)
    o_ref[...] = _dot(sc, w_ref[...].astype(BF16)) + b_ref[...]
    l = l_ref[...]
    s1 = jnp.sum(l[0:1] * l[1:2], axis=-1, keepdims=True)
    s2 = jnp.sum(l[2:3] * l[3:4], axis=-1, keepdims=True)
    lam = jnp.exp(s1) - jnp.exp(s2) + LAM_INIT
    lam_ref[...] = jnp.broadcast_to(lam, lam_ref.shape)


def _adaln(c_all, w_ada, b_ada, da_lambda):
    bp, d = c_all.shape
    n = w_ada.shape[1]
    tn = 1024
    return pl.pallas_call(
        _adaln_kernel,
        grid=(n // tn,),
        in_specs=[pl.BlockSpec((bp, d), lambda j: (0, 0)),
                  pl.BlockSpec((d, tn), lambda j: (0, j)),
                  pl.BlockSpec((1, tn), lambda j: (0, j)),
                  pl.BlockSpec(da_lambda.shape, lambda j: (0, 0))],
        out_specs=[pl.BlockSpec((bp, tn), lambda j: (0, j)),
                   pl.BlockSpec((8, LANES), lambda j: (0, 0))],
        out_shape=[jax.ShapeDtypeStruct((bp, n), F32), jax.ShapeDtypeStruct((8, LANES), F32)],
        compiler_params=_cparams("arbitrary"),
        name="adaln",
    )(c_all, w_ada, b_ada.reshape(1, n), da_lambda)


def _mod_spec(mod, tm, seq):
    d = mod.shape[-1]
    if mod.shape[1] == 1:
        per = seq // tm
        return pl.BlockSpec((None, 1, d), lambda i, *_: (i // per, 0, 0))
    return pl.BlockSpec((None, tm, d), lambda i, *_: (0, i, 0))


def _qkv_kernel(x_ref, sc_ref, sh_ref, g_ref, w_ref, qg_ref, kg_ref, cos_ref, sin_ref, gm_ref,
                h_ref, q_ref, k_ref, kb_ref, v_ref, vb_ref):
    j = pl.program_id(1)

    @pl.when(j == 0)
    def _():
        x = x_ref[...]
        ms = jnp.mean(x * x, axis=-1, keepdims=True)
        hn = x * lax.rsqrt(ms + EPS) * g_ref[...]
        h_ref[...] = (hn * (1.0 + sc_ref[...]) + sh_ref[...]).astype(h_ref.dtype)

    acc = _dot(h_ref[...], w_ref[...])
    def norm_rope(gain_ref, scale, write):
        lane = lax.broadcasted_iota(jnp.int32, (acc.shape[0], LANES), 1)
        first_half = (lane % DA_HEAD_DIM) < (DA_HEAD_DIM // 2)
        for c2 in range(acc.shape[1] // GMW):
            a2 = acc[:, c2 * GMW:(c2 + 1) * GMW]
            ms2 = _dot((a2 * a2).astype(BF16), gm_ref[...])
            y2 = a2 * lax.rsqrt(ms2 + EPS) * gain_ref[...]
            for hf in range(GMW // LANES):
                y = y2[:, hf * LANES:(hf + 1) * LANES]
                rot = jnp.where(first_half, pltpu.roll(y, LANES - DA_HEAD_DIM // 2, 1),
                                pltpu.roll(y, DA_HEAD_DIM // 2, 1))
                write(c2 * (GMW // LANES) + hf, (y * cos_ref[:, 0:LANES] + rot * sin_ref[:, 0:LANES]) * scale)

    @pl.when(j == 0)
    def _():
        def write(c, val):
            q_ref[:, c * LANES:(c + 1) * LANES] = val.astype(q_ref.dtype)
        norm_rope(qg_ref, DA_HEAD_DIM ** -0.5 * math.log2(math.e), write)

    @pl.when(j == 1)
    def _():
        def write(c, val):
            k_ref[:, c * LANES:(c + 1) * LANES] = val
            kb_ref[:, c * LANES:(c + 1) * LANES] = val.astype(BF16)
        norm_rope(kg_ref, 1.0, write)

    @pl.when(j == 2)
    def _():
        v_ref[...] = acc
        vb_ref[...] = acc.astype(BF16)


def _qkv(x2d, sc, sh, g, w_qkv, qg, kg, cos_t, sin_t, gm, tm, seq):
    t, d = x2d.shape
    n = 1024
    per = cos_t.shape[0] // tm
    row = lambda i, j: (i, 0)
    return pl.pallas_call(
        _qkv_kernel,
        grid=(t // tm, 3),
        in_specs=[pl.BlockSpec((tm, d), row), _mod_spec(sc, tm, seq), _mod_spec(sh, tm, seq),
                  pl.BlockSpec((1, d), lambda i, j: (0, 0)),
                  pl.BlockSpec((d, n), lambda i, j: (0, j)),
                  pl.BlockSpec((1, GMW), lambda i, j: (0, 0)),
                  pl.BlockSpec((1, GMW), lambda i, j: (0, 0)),
                  pl.BlockSpec((tm, LANES), lambda i, j: (i % per, 0)),
                  pl.BlockSpec((tm, LANES), lambda i, j: (i % per, 0)),
                  pl.BlockSpec((GMW, GMW), lambda i, j: (0, 0))],
        out_specs=[pl.BlockSpec((tm, d), row)] + [pl.BlockSpec((tm, n), row)] * 5,
        out_shape=[jax.ShapeDtypeStruct((t, d), BF16),
                   jax.ShapeDtypeStruct((t, n), BF16), jax.ShapeDtypeStruct((t, n), F32),
                   jax.ShapeDtypeStruct((t, n), BF16), jax.ShapeDtypeStruct((t, n), F32),
                   jax.ShapeDtypeStruct((t, n), BF16)],
        compiler_params=_cparams("parallel", "arbitrary"),
        name="qkv_proj",
    )(x2d, sc, sh, g, w_qkv, qg, kg, cos_t, sin_t, gm)


def _proj_kernel(h_ref, w_ref, o_ref, *, act):
    acc = _dot(h_ref[...], w_ref[...])
    if act == "sigmoid":
        acc = 0.5 * jnp.tanh(0.5 * acc) + 0.5
    o_ref[...] = acc.astype(o_ref.dtype)


def _proj(h, w, tm, tn, act, out_dtype, name):
    t, d = h.shape
    n = w.shape[1]
    return pl.pallas_call(
        functools.partial(_proj_kernel, act=act),
        grid=(t // tm, n // tn),
        in_specs=[pl.BlockSpec((tm, d), lambda i, j: (i, 0)), pl.BlockSpec((d, tn), lambda i, j: (0, j))],
        out_specs=pl.BlockSpec((tm, tn), lambda i, j: (i, j)),
        out_shape=jax.ShapeDtypeStruct((t, n), out_dtype),
        compiler_params=_cparams("parallel", "arbitrary"),
        name=name,
    )(h, w)


def _subln(o, g):
    ms = jnp.mean(o * o, axis=-1, keepdims=True)
    return o * lax.rsqrt(ms + EPS) * g * (1.0 - LAM_INIT)


ATT_ROWS = 256
ATT_HEADS = 4


def _attn_prompt_kernel(qi_tab, ki_tab, lam_ref, q_ref, k_ref, v_ref, g_ref, o_ref,
                        qs_scr, vx_scr, m_scr, acc_scr, *, tq, tk):
    p = pl.program_id(2)
    qi = qi_tab[p]
    ki = ki_tab[p]
    ratio = tk // tq
    n_slab = tk // LANES

    heads = [slice(hh * LANES, (hh + 1) * LANES) for hh in range(ATT_HEADS)]

    @pl.when(ki == 0)
    def _():
        for hh, hs in enumerate(heads):
            q = q_ref[:, hs]
            lane = lax.broadcasted_iota(jnp.int32, q.shape, 1)
            zero = jnp.zeros_like(q)
            qs_scr[hh, 0:tq, :] = jnp.where(lane < DA_HEAD_DIM, q, zero)
            qs_scr[hh, tq:2 * tq, :] = jnp.where(lane >= DA_HEAD_DIM, q, zero)
            vx_scr[hh, :, LANES:2 * LANES] = jnp.ones((tk, LANES), BF16)
        m_scr[...] = jnp.full(m_scr.shape, NEG_INF, F32)
        acc_scr[...] = jnp.zeros(acc_scr.shape, F32)

    for hh, hs in enumerate(heads):
        vx_scr[hh, :, 0:LANES] = v_ref[:, hs]

    def step(masked):
        for rb in range(2 * tq // ATT_ROWS):
            rows = slice(rb * ATT_ROWS, (rb + 1) * ATT_ROWS)
            nk = min(tk, (rb * ATT_ROWS) % tq + ATT_ROWS) if (masked and ratio == 1) else tk
            for hh, hs in enumerate(heads):
                s = _dot_nt(qs_scr[hh, rows, :], k_ref[0:nk, hs])
                slabs = [s[:, c * LANES:(c + 1) * LANES] for c in range(nk // LANES)]
                if masked:
                    row = lax.broadcasted_iota(jnp.int32, (ATT_ROWS, LANES), 0) + rb * ATT_ROWS
                    lane = lax.broadcasted_iota(jnp.int32, (ATT_ROWS, LANES), 1)
                    qchunk = (row % tq) // CHUNK + (qi % ratio) * (tq // CHUNK)
                    first = (nk - ATT_ROWS) // LANES if ratio == 1 else 0
                    slabs = [jnp.where((c * LANES + lane) // CHUNK <= qchunk, sl, NEG_INF) if c >= first else sl
                             for c, sl in enumerate(slabs)]
                mt = slabs[0]
                for sl in slabs[1:]:
                    mt = jnp.maximum(mt, sl)
                m_prev = m_scr[hh, rows, :]
                m_new = jnp.maximum(m_prev, jnp.max(mt, axis=-1, keepdims=True))
                alpha = jnp.exp2(m_prev - m_new)
                e = jnp.concatenate([jnp.exp2(sl - m_new).astype(BF16) for sl in slabs], axis=1)
                pv = _dot(e, vx_scr[hh, 0:nk, :])
                acc_scr[hh, rows, :] = jnp.concatenate([alpha, alpha], axis=1) * acc_scr[hh, rows, :] + pv
                m_scr[hh, rows, :] = m_new

    @pl.when(ki < qi // ratio)
    def _():
        step(False)

    @pl.when(ki == qi // ratio)
    def _():
        step(True)
        for hh, hs in enumerate(heads):
            o = acc_scr[hh, :, 0:LANES] / acc_scr[hh, :, LANES:2 * LANES]
            o = o[0:tq] - lam_ref[...] * o[tq:2 * tq]
            o_ref[:, hs] = _subln(o, g_ref[...]).astype(o_ref.dtype)


def _attn_prompt(q, k, v, lam_row, subln_g, batch, seq, tq, tk):
    d = q.shape[1]
    assert tk % tq == 0 and seq % tk == 0 and tq % CHUNK == 0
    qi_l, ki_l = [], []
    for a in range(seq // tq):
        for b in range(a * tq // tk + 1):
            qi_l.append(a)
            ki_l.append(b)
    qi_tab = jnp.asarray(qi_l, jnp.int32)
    ki_tab = jnp.asarray(ki_l, jnp.int32)
    q3, k3, v3 = (a.reshape(batch, seq, d) for a in (q, k, v))
    hw = ATT_HEADS * LANES
    qspec = pl.BlockSpec((None, tq, hw), lambda b, h, p, qt, kt: (b, qt[p], h))
    kspec = pl.BlockSpec((None, tk, hw), lambda b, h, p, qt, kt: (b, kt[p], h))
    vec = pl.BlockSpec((1, LANES), lambda b, h, p, qt, kt: (0, 0))
    out = pl.pallas_call(
        functools.partial(_attn_prompt_kernel, tq=tq, tk=tk),
        grid_spec=pltpu.PrefetchScalarGridSpec(
            num_scalar_prefetch=2,
            grid=(batch, DA_HEADS // ATT_HEADS, len(qi_l)),
            in_specs=[vec, qspec, kspec, kspec, vec],
            out_specs=qspec,
            scratch_shapes=[pltpu.VMEM((ATT_HEADS, 2 * tq, LANES), BF16),
                            pltpu.VMEM((ATT_HEADS, tk, 2 * LANES), BF16),
                            pltpu.VMEM((ATT_HEADS, 2 * tq, LANES), F32),
                            pltpu.VMEM((ATT_HEADS, 2 * tq, 2 * LANES), F32)]),
        out_shape=jax.ShapeDtypeStruct((batch, seq, d), BF16),
        compiler_params=_cparams("parallel", "parallel", "arbitrary"),
        name="attn_prompt",
    )(qi_tab, ki_tab, lam_row, q3, k3, v3, subln_g)
    return out.reshape(batch * seq, d)


def _attn_sample_kernel(lam_ref, q_ref, kn_ref, vn_ref, ck_ref, cv_ref, g_ref, o_ref, *, past, sq):
    for h in range(DA_HEADS):
        hs = slice(h * LANES, (h + 1) * LANES)
        q = q_ref[:, hs]
        lane = lax.broadcasted_iota(jnp.int32, q.shape, 1)
        zero = jnp.zeros_like(q)
        qs = jnp.concatenate([jnp.where(lane < DA_HEAD_DIM, q, zero),
                              jnp.where(lane >= DA_HEAD_DIM, q, zero)], axis=0)
        kc = ck_ref[pl.ds(h, past, stride=DA_HEADS), :].astype(BF16)
        vc = cv_ref[pl.ds(h, past, stride=DA_HEADS), :].astype(BF16)
        s_c = _dot_nt(qs, kc)
        s_n = _dot_nt(qs, kn_ref[:, hs])

        def masked(s, key0):
            row = lax.broadcasted_iota(jnp.int32, s.shape, 0)
            col = lax.broadcasted_iota(jnp.int32, s.shape, 1)
            qchunk = (past + row % sq) // CHUNK
            return jnp.where((key0 + col) // CHUNK <= qchunk, s, NEG_INF)

        s_c = masked(s_c, 0)
        s_n = masked(s_n, past)
        m = jnp.maximum(jnp.max(s_c, axis=-1, keepdims=True), jnp.max(s_n, axis=-1, keepdims=True))
        e_c = jnp.exp2(s_c - m)
        e_n = jnp.exp2(s_n - m)
        l = jnp.sum(e_c, axis=-1, keepdims=True) + jnp.sum(e_n, axis=-1, keepdims=True)
        o = (_dot(e_c.astype(BF16), vc) + _dot(e_n.astype(BF16), vn_ref[:, hs])) / l
        o = o[0:sq] - lam_ref[...] * o[sq:2 * sq]
        o_ref[:, hs] = _subln(o, g_ref[...]).astype(o_ref.dtype)


def _attn_sample(q, kn, vn, cache_k, cache_v, lam_row, subln_g, batch, sq):
    d = q.shape[1]
    past = cache_k.shape[-3]
    ck = cache_k.reshape(batch, past * DA_HEADS, LANES)
    cv = cache_v.reshape(batch, past * DA_HEADS, LANES)
    q3, k3, v3 = (a.reshape(batch, sq, d) for a in (q, kn, vn))
    new = pl.BlockSpec((None, sq, d), lambda b: (b, 0, 0))
    old = pl.BlockSpec((None, past * DA_HEADS, LANES), lambda b: (b, 0, 0))
    vec = pl.BlockSpec((1, LANES), lambda b: (0, 0))
    out = pl.pallas_call(
        functools.partial(_attn_sample_kernel, past=past, sq=sq),
        grid=(batch,),
        in_specs=[vec, new, new, new, old, old, vec],
        out_specs=new,
        out_shape=jax.ShapeDtypeStruct((batch, sq, d), BF16),
        compiler_params=_cparams("parallel"),
        name="attn_sample",
    )(lam_row, q3, k3, v3, ck, cv, subln_g)
    return out.reshape(batch * sq, d)


RW_L = 64
RW_G = 2
RW_GW = RW_G * RW_HEAD
RW_GROUPS = 16 // RW_G
RW_CPS = 4


def _mm3(a, b, dot=_dot):
    ah, al = _split2(a)
    bh, bl = _split2(b)
    return dot(ah, bh) + dot(ah, bl) + dot(al, bh)


def _terms(x, n):
    return (x.astype(BF16),) if n == 1 else _split2(x)


def _mmt(at, bt, dot=_dot):
    acc = dot(at[0], bt[0])
    if len(bt) > 1:
        acc = acc + dot(at[0], bt[1])
    if len(at) > 1:
        acc = acc + dot(at[1], bt[0])
    return acc


RW_PREC = dict(a_all=(1, 1), neumann=(1, 1), av=(1, 1), pu=(1, 1), qy=(1, 1), mh=(1, 1), state=(1, 1))


def _softplus(z):
    return jnp.maximum(z, 0.0) + jnp.log(1.0 + jnp.exp(-jnp.abs(z)))


def _rwkv_kernel(rw_ref, sh0_ref, mu_ref, lw_ref, w0_ref, a0_ref, kk_ref, ka_ref, rk_ref, lng_ref, lnb_ref,
                 h0_ref, gm_ref, tri_ref, ob_ref, hout_ref, h_scr, last_scr, *, n_valid, cps):
    L = RW_L
    rows = cps * L
    c = pl.program_id(1)
    d = ob_ref.shape[-1]

    @pl.when(c == 0)
    def _():
        h_scr[...] = h0_ref[...]
        last_scr[...] = sh0_ref[...]

    rw = rw_ref[...]
    row_w = lax.broadcasted_iota(jnp.int32, rw.shape, 0)
    prev = jnp.where(row_w == 0, last_scr[...], pltpu.roll(rw, 1, 0))
    last_scr[...] = rw[rows - 1:rows, :]
    xm = rw + (prev - rw) * mu_ref[...]
    r = xm[:, 0:d]
    k = xm[:, d:2 * d]
    v = xm[:, 2 * d:3 * d]
    lo = xm[:, 3 * d:3 * d + LORA_PAD]
    lane_l = lax.broadcasted_iota(jnp.int32, lo.shape, 1)
    z = jnp.where(lane_l < W_LORA, jnp.tanh(lo),
                  jnp.where(lane_l < W_LORA + A_LORA, lo, jax.nn.sigmoid(lo)))
    lora = _dot(z.astype(BF16), lw_ref[...])
    w = -_softplus(-(w0_ref[...] + lora[:, 0:d])) - 0.5
    ld = -jnp.exp(w)
    a = jax.nn.sigmoid(a0_ref[...] + lora[:, d:2 * d])
    g = lora[:, 2 * d:3 * d]
    kkr = k * kk_ref[...]
    kmod = k * (1.0 + (a - 1.0) * ka_ref[...])
    rkk = r * kmod * rk_ref[...]
    gm = gm_ref[...]
    if n_valid < rows:
        valid = lax.broadcasted_iota(jnp.int32, ld.shape, 0) < n_valid
        ld = jnp.where(valid, ld, 0.0)
        kkr = jnp.where(valid, kkr, 0.0)
        kmod = jnp.where(valid, kmod, 0.0)

    tri = tri_ref[...]
    l1, l2, l3 = _split3(ld)
    cin = _dot(tri, l1) + _dot(tri, l2) + _dot(tri, l3)
    cex = cin - ld
    c_last = [cin[(ch + 1) * L - 1:(ch + 1) * L, :] for ch in range(cps)]
    e_in = jnp.exp(cin)
    e_ex = jnp.exp(cex)
    e_neg = jnp.exp(-cin)
    e_tail = jnp.exp(jnp.concatenate([jnp.broadcast_to(cl, (L, d)) for cl in c_last], axis=0) - cin)
    g_last = [jnp.exp(cl) for cl in c_last]

    GW = RW_GW
    row = lax.broadcasted_iota(jnp.int32, (L, GW), 0)
    lane = lax.broadcasted_iota(jnp.int32, (L, GW), 1)
    head_of_lane = lane // RW_HEAD
    tri_s = row > (lane % L)
    tri_i = row >= (lane % L)
    eye_pair = jnp.where(row == (lane % L), 1.0, 0.0).astype(F32)
    r2 = lax.broadcasted_iota(jnp.int32, (GW, GW), 0)
    c2 = lax.broadcasted_iota(jnp.int32, (GW, GW), 1)
    bd_mask = (r2 // RW_HEAD) == (c2 // RW_HEAD)
    diag_mask = r2 == c2

    def bd(x):
        z0 = jnp.zeros_like(x)
        return jnp.concatenate([jnp.where(head_of_lane == j, x, z0) for j in range(RW_G)], axis=0)

    def bdt(t):
        return tuple(bd(x) for x in t)

    def cat(ts, axis):
        return tuple(jnp.concatenate(xs, axis=axis) for xs in zip(*ts))

    prec = RW_PREC
    GL = RW_G * L
    items = [(slice(ch * L, (ch + 1) * L), slice(p * GW, (p + 1) * GW))
             for ch in range(cps) for p in range(RW_GROUPS)]
    pairs = range(len(items))

    v_p = [v[rs, sl] for rs, sl in items]
    kk_n = []
    for q in range(d // GMW):
        kk_raw = kkr[:, q * GMW:(q + 1) * GMW]
        ss = _group_mean(kk_raw * kk_raw, gm) * RW_HEAD
        kk_n.append(kk_raw / jnp.maximum(jnp.sqrt(ss), 1e-12))
    kk_n = jnp.concatenate(kk_n, axis=1)
    kk_p = [kk_n[rs, sl] for rs, sl in items]
    b_p = [kk_p[i] * a[rs, sl] for i, (rs, sl) in enumerate(items)]
    abar = [-kk_p[i] * e_ex[rs, sl] for i, (rs, sl) in enumerate(items)]
    rbar = [r[rs, sl] * e_in[rs, sl] for rs, sl in items]
    bbar = [b_p[i] * e_neg[rs, sl] for i, (rs, sl) in enumerate(items)]
    kbar = [kmod[rs, sl] * e_neg[rs, sl] for rs, sl in items]
    btil = [b_p[i] * e_tail[rs, sl] for i, (rs, sl) in enumerate(items)]
    ktil = [kmod[rs, sl] * e_tail[rs, sl] for rs, sl in items]

    na, nb = prec["a_all"]
    abar_t = [_terms(abar[p], max(na, prec["pu"][1])) for p in pairs]
    v_t = [_terms(v_p[p], max(prec["av"][1], prec["qy"][1], prec["mh"][1])) for p in pairs]
    a_all = []
    for p in pairs:
        lhs = cat([abar_t[p][:na], _terms(rbar[p], na)], 0)
        rhs = cat([bdt(_terms(bbar[p], nb)), bdt(_terms(kbar[p], nb))], 0)
        a_all.append(_mmt(lhs, rhs, _dot_nt))
    n_ab = [jnp.where(tri_s, a_all[p][0:L, 0:GL], 0.0) for p in pairs]
    a_ak = [jnp.where(tri_s, a_all[p][0:L, GL:2 * GL], 0.0) for p in pairs]
    a_rb = [jnp.where(tri_i, a_all[p][L:2 * L, 0:GL], 0.0) for p in pairs]
    a_rk = [jnp.where(tri_i, a_all[p][L:2 * L, GL:2 * GL], 0.0) for p in pairs]

    na, nb = prec["neumann"]
    t_inv = [eye_pair + n_ab[p] for p in pairs]
    pw_t = [_terms(n_ab[p], max(na, nb)) for p in pairs]
    pw = [_mmt(pw_t[p][:na], bdt(pw_t[p][:nb])) for p in pairs]
    for _ in range(int(math.log2(L)) - 2):
        pw_t = [_terms(pw[p], max(na, nb)) for p in pairs]
        st = [_mmt(cat([_terms(t_inv[p], na), pw_t[p][:na]], 0), bdt(pw_t[p][:nb])) for p in pairs]
        t_inv = [t_inv[p] + st[p][0:L] for p in pairs]
        pw = [st[p][L:2 * L] for p in pairs]
    t_inv = [t_inv[p] + _mmt(_terms(t_inv[p], na), bdt(_terms(pw[p], nb))) for p in pairs]

    na, nb = prec["av"]
    vv = [_mmt(cat([_terms(a_ak[p], na), _terms(a_rk[p], na)], 0), bdt(v_t[p][:nb])) for p in pairs]
    av = [x[0:L] for x in vv]
    arkv = [x[L:2 * L] for x in vv]
    na, nb = prec["pu"]
    pu = [_mmt(_terms(t_inv[p], na), cat([bdt(abar_t[p][:nb]), bdt(_terms(av[p], nb))], 1)) for p in pairs]
    p_m = [x[:, 0:GW] for x in pu]
    u0 = [x[:, GW:2 * GW] for x in pu]
    na, nb = prec["qy"]
    nmh = prec["mh"][1]
    p_t = [_terms(p_m[p], max(nb, nmh)) for p in pairs]
    u_t = [_terms(u0[p], max(nb, nmh)) for p in pairs]
    qy = [_mmt(_terms(a_rb[p], na), cat([bdt(p_t[p][:nb]), bdt(u_t[p][:nb])], 1)) for p in pairs]
    q_m = [rbar[p] + qy[p][:, 0:GW] for p in pairs]
    y0 = [qy[p][:, GW:2 * GW] + arkv[p] for p in pairs]

    na, nb = prec["mh"]
    btil_t = [_terms(btil[p], na) for p in pairs]
    m_full = [_mmt(btil_t[p], p_t[p][:nb], _dot_tn) for p in pairs]
    h0_full = [_mmt(cat([btil_t[p], _terms(ktil[p], na)], 0), cat([u_t[p][:nb], v_t[p][:nb]], 0), _dot_tn)
               for p in pairs]
    m_bd = [jnp.where(bd_mask, m_full[i], 0.0) + jnp.where(diag_mask, g_last[i // RW_GROUPS][:, sl], 0.0)
            for i, (rs, sl) in enumerate(items)]
    h0_bd = [jnp.where(bd_mask, h0_full[p], 0.0) for p in pairs]

    na, nb = prec["state"]
    h_cur = [h_scr[p] for p in range(RW_GROUPS)]
    y = []
    for ch in range(cps):
        idx = [ch * RW_GROUPS + p for p in range(RW_GROUPS)]
        st = [_mmt(cat([_terms(q_m[i], na), _terms(m_bd[i], na)], 0), _terms(h_cur[p], nb))
              for p, i in enumerate(idx)]
        y.extend(st[p][0:L] + y0[i] for p, i in enumerate(idx))
        h_cur = [st[p][L:L + GW] + h0_bd[i] for p, i in enumerate(idx)]
    for p in range(RW_GROUPS):
        h_scr[p] = h_cur[p]

    per = GMW // GW
    for ch in range(cps):
        rs = slice(ch * L, (ch + 1) * L)
        for q in range(RW_GROUPS // per):
            sl = slice(q * GMW, (q + 1) * GMW)
            yq = jnp.concatenate([y[ch * RW_GROUPS + q * per + j] for j in range(per)], axis=1)
            mu_y = _group_mean(yq, gm)
            dy = yq - mu_y
            var = _dot((dy * dy).astype(BF16), gm)
            yn = dy * lax.rsqrt(var + RW_GN_EPS) * lng_ref[:, sl] + lnb_ref[:, sl]
            bonus = _dot(rkk[rs, sl].astype(BF16), gm) * RW_HEAD * v[rs, sl]
            ob_ref[rs, sl] = ((yn + bonus) * g[rs, sl]).astype(ob_ref.dtype)

    @pl.when(c == pl.num_programs(1) - 1)
    def _():
        hout_ref[...] = h_scr[...]


def _rwkv(rw3, shift0, h0_bd, prm, gm, n_valid, cps):
    batch, seq, wcols = rw3.shape
    d = prm["w0"].shape[1]
    rows = cps * RW_L
    nc = seq // rows
    tri = jnp.kron(jnp.eye(cps, dtype=F32), jnp.tril(jnp.ones((RW_L, RW_L), F32))).astype(BF16)
    const2 = lambda b, c: (0, 0)
    vec = pl.BlockSpec((1, d), const2)
    st = pl.BlockSpec((None, RW_GROUPS, RW_GW, RW_GW), lambda b, c: (b, 0, 0, 0))
    return pl.pallas_call(
        functools.partial(_rwkv_kernel, n_valid=n_valid, cps=cps),
        grid=(batch, nc),
        in_specs=[pl.BlockSpec((None, rows, wcols), lambda b, c: (b, c, 0)),
                  pl.BlockSpec((None, 1, wcols), lambda b, c: (b, 0, 0)),
                  pl.BlockSpec((1, wcols), const2),
                  pl.BlockSpec(prm["lw"].shape, const2),
                  vec, vec, vec, vec, vec, vec, vec, st,
                  pl.BlockSpec((GMW, GMW), const2),
                  pl.BlockSpec((rows, rows), const2)],
        out_specs=[pl.BlockSpec((None, rows, d), lambda b, c: (b, c, 0)), st],
        out_shape=[jax.ShapeDtypeStruct((batch, seq, d), BF16),
                   jax.ShapeDtypeStruct((batch, RW_GROUPS, RW_GW, RW_GW), F32)],
        scratch_shapes=[pltpu.VMEM((RW_GROUPS, RW_GW, RW_GW), F32), pltpu.VMEM((1, wcols), F32)],
        compiler_params=_cparams("parallel", "arbitrary"),
        name="rwkv7",
    )(rw3, shift0, prm["mu"], prm["lw"], prm["w0"], prm["a0"], prm["k_k"], prm["k_a"], prm["r_k"],
      prm["ln_g"], prm["ln_b"], h0_bd, gm, tri)


def _state_to_bd(s):
    b = s.shape[0]
    ht = jnp.swapaxes(s, -1, -2).reshape(b, RW_GROUPS, RW_G, RW_HEAD, RW_HEAD)
    z = jnp.zeros_like(ht[:, :, 0])
    rows = [jnp.concatenate([ht[:, :, j] if i == j else z for i in range(RW_G)], axis=-1) for j in range(RW_G)]
    return jnp.concatenate(rows, axis=-2)


def _bd_to_state(hbd):
    b = hbd.shape[0]
    blocks = [hbd[:, :, j * RW_HEAD:(j + 1) * RW_HEAD, j * RW_HEAD:(j + 1) * RW_HEAD] for j in range(RW_G)]
    ht = jnp.stack(blocks, axis=2).reshape(b, RW_G * RW_GROUPS, RW_HEAD, RW_HEAD)
    return jnp.swapaxes(ht, -1, -2)


def _route(logits):
    lane = lax.broadcasted_iota(jnp.int32, logits.shape, 1).astype(F32)
    big = float(LANES)
    lg = jnp.where(lane < N_GROUPS, logits, NEG_INF)
    mg = jnp.max(lg, axis=-1, keepdims=True)
    sg = jnp.sum(jnp.exp(lg - mg), axis=-1, keepdims=True)
    p_top = 1.0 / sg
    g_idx = jnp.min(jnp.where(lg == mg, lane, big), axis=-1, keepdims=True)
    e0 = N_GROUPS + EXPERTS_PER_GROUP * g_idx
    emask = jnp.where(lane >= e0, jnp.where(lane < e0 + EXPERTS_PER_GROUP, 1.0, 0.0), 0.0) > 0.5
    le = jnp.where(emask, logits, NEG_INF)
    me = jnp.max(le, axis=-1, keepdims=True)
    ee = jnp.exp(le - me)
    pe = ee / jnp.sum(ee, axis=-1, keepdims=True)
    pe = jnp.where(emask, pe, -1.0)
    v1 = jnp.max(pe, axis=-1, keepdims=True)
    i1 = jnp.min(jnp.where(pe == v1, lane, big), axis=-1, keepdims=True)
    pe2 = jnp.where(lane == i1, -1.0, pe)
    v2 = jnp.max(pe2, axis=-1, keepdims=True)
    i2 = jnp.min(jnp.where(pe2 == v2, lane, big), axis=-1, keepdims=True)
    den = v1 + v2
    ew = jnp.where(lane == i1, v1 / den, 0.0) + jnp.where(lane == i2, v2 / den, 0.0)
    return p_top * ew + jnp.where(lane == 0.0, g_idx, 0.0)


def _merge_kernel(oa_ref, ob_ref, gates_ref, x_ref, gt1_ref, sc2_ref, sh2_ref, g2_ref,
                  woa_ref, wob_ref, wout_ref, wr_ref, br_ref, x1_ref, hx_ref):
    d = x_ref.shape[1]
    a_out = _dot(oa_ref[...], woa_ref[...])
    b_out = _dot(ob_ref[...], wob_ref[...])
    merged = gates_ref[:, 0:d].astype(F32) * a_out + gates_ref[:, d:2 * d].astype(F32) * b_out
    x1 = x_ref[...] + gt1_ref[...] * _dot(merged.astype(BF16), wout_ref[...])
    x1_ref[...] = x1
    ms = jnp.mean(x1 * x1, axis=-1, keepdims=True)
    h2 = x1 * lax.rsqrt(ms + EPS) * g2_ref[...]
    h2 = h2 * (1.0 + sc2_ref[...]) + sh2_ref[...]
    hh, hl = _split2(h2)
    wh, wl = _split2(wr_ref[...])
    part = _dot(hh, jnp.concatenate([wh, wl], axis=1))
    logits = part[:, 0:LANES] + part[:, LANES:2 * LANES] + _dot(hl, wh) + br_ref[...]
    hx_ref[:, 0:d] = h2
    hx_ref[:, d:d + LANES] = _route(logits)


def _merge(oa, ob, gates, x2d, gt1, sc2, sh2, g2, w_oa, w_ob, w_out, wr, br, tm, seq):
    t, d = x2d.shape
    row = lambda i: (i, 0)
    const = lambda i: (0, 0)
    full = lambda w: pl.BlockSpec(w.shape, const)
    return pl.pallas_call(
        _merge_kernel,
        grid=(t // tm,),
        in_specs=[pl.BlockSpec((tm, d), row), pl.BlockSpec((tm, d), row), pl.BlockSpec((tm, 2 * d), row),
                  pl.BlockSpec((tm, d), row), _mod_spec(gt1, tm, seq), _mod_spec(sc2, tm, seq),
                  _mod_spec(sh2, tm, seq), pl.BlockSpec((1, d), const),
                  full(w_oa), full(w_ob), full(w_out), full(wr), full(br)],
        out_specs=[pl.BlockSpec((tm, d), row), pl.BlockSpec((tm, d + LANES), row)],
        out_shape=[jax.ShapeDtypeStruct((t, d), F32), jax.ShapeDtypeStruct((t, d + LANES), F32)],
        compiler_params=_cparams("parallel"),
        name="merge_router",
    )(oa, ob, gates, x2d, gt1, sc2, sh2, g2, w_oa, w_ob, w_out, wr, br)


def _expert_ffn(xb, comb, lane_e, w13, w2):
    au = _dot(xb, w13)
    f = au.shape[1] // 2
    a = au[:, 0:f]
    u = au[:, f:2 * f]
    lane = lax.broadcasted_iota(jnp.int32, comb.shape, 1)
    cw = jnp.sum(jnp.where(lane == lane_e, comb, 0.0), axis=-1, keepdims=True)
    act = a * jax.nn.sigmoid(a) * u * cw
    return _dot(act.astype(BF16), w2)


def _moe_dense_kernel(hx_ref, w13_ref, w2_ref, x1_ref, gt2_ref, y_ref, xb_scr, acc_scr):
    e = pl.program_id(1)
    d = y_ref.shape[1]

    @pl.when(e == 0)
    def _():
        xb_scr[...] = hx_ref[:, 0:d].astype(BF16)
        acc_scr[...] = jnp.zeros(acc_scr.shape, F32)

    acc_scr[...] += _expert_ffn(xb_scr[...], hx_ref[:, d:d + LANES], e + N_GROUPS, w13_ref[...], w2_ref[...])

    @pl.when(e == pl.num_programs(1) - 1)
    def _():
        y_ref[...] = x1_ref[...] + gt2_ref[...] * acc_scr[...]


def _moe_dense(hx, w13, w2, x1, gt2, tm, seq):
    t, d = x1.shape
    ne, _, f2 = w13.shape
    row = lambda i, e: (i, 0)
    return pl.pallas_call(
        _moe_dense_kernel,
        grid=(t // tm, ne),
        in_specs=[pl.BlockSpec((tm, d + LANES), row),
                  pl.BlockSpec((None, d, f2), lambda i, e: (e, 0, 0)),
                  pl.BlockSpec((None, f2 // 2, d), lambda i, e: (e, 0, 0)),
                  pl.BlockSpec((tm, d), row), _mod_spec(gt2, tm, seq)],
        out_specs=pl.BlockSpec((tm, d), row),
        out_shape=jax.ShapeDtypeStruct((t, d), F32),
        scratch_shapes=[pltpu.VMEM((tm, d), BF16), pltpu.VMEM((tm, d), F32)],
        compiler_params=_cparams("parallel", "arbitrary"),
        name="moe_dense",
    )(hx, w13, w2, x1, gt2)


MOE_TM = 1024
MOE_TS = 1024
MOE_UNROLL = 8


def _plan_kernel(rt_ref, tri_ref, rank_ref, cnt_ref, carry_scr):
    @pl.when(pl.program_id(0) == 0)
    def _():
        carry_scr[...] = jnp.zeros(carry_scr.shape, F32)

    rt = rt_ref[...]
    lane = lax.broadcasted_iota(jnp.int32, rt.shape, 1).astype(F32)
    onehot = jnp.where(lane == rt[:, 0:1], 1.0, 0.0)
    before = _dot(tri_ref[...], onehot.astype(BF16)) + carry_scr[...]
    rank_ref[...] = jnp.sum(onehot * before, axis=-1, keepdims=True).astype(jnp.int32)
    carry_scr[...] += jnp.sum(onehot, axis=0, keepdims=True)
    cnt_ref[...] = carry_scr[...]


def _moe_plan(hx, d):
    t = hx.shape[0]
    tp = 512
    tri = jnp.tril(jnp.ones((tp, tp), F32), -1).astype(BF16)
    return pl.pallas_call(
        _plan_kernel,
        grid=(t // tp,),
        in_specs=[pl.BlockSpec((tp, LANES), lambda i: (i, d // LANES)),
                  pl.BlockSpec((tp, tp), lambda i: (0, 0))],
        out_specs=[pl.BlockSpec((tp, 1), lambda i: (i, 0)), pl.BlockSpec((1, LANES), lambda i: (0, 0))],
        out_shape=[jax.ShapeDtypeStruct((t, 1), jnp.int32), jax.ShapeDtypeStruct((1, LANES), F32)],
        scratch_shapes=[pltpu.VMEM((1, LANES), F32)],
        compiler_params=_cparams("arbitrary"),
        name="moe_plan",
    )(hx, tri)


def _row_copy(src_hbm, src_row, dst, dst_row, sem):
    return pltpu.make_async_copy(src_hbm.at[pl.ds(src_row, 1), :], dst.at[pl.ds(dst_row, 1), :], sem)


def _dispatch_kernel(pos_ref, hx_ref, xs_in_hbm, xs_hbm, sem):
    del xs_in_hbm

    def issue(u, carry):
        for j in range(MOE_UNROLL):
            t = u * MOE_UNROLL + j
            _row_copy(hx_ref, t, xs_hbm, pos_ref[t], sem).start(priority=j % 2)
        return carry

    def drain(t, carry):
        _row_copy(hx_ref, t, xs_hbm, pos_ref[t], sem).wait()
        return carry

    lax.fori_loop(0, MOE_TS // MOE_UNROLL, issue, 0)
    lax.fori_loop(0, MOE_TS, drain, 0, unroll=8)


def _moe_dispatch(pos, hx, n_rows):
    t, w = hx.shape
    xs0 = jnp.zeros((n_rows, w), F32)
    return pl.pallas_call(
        _dispatch_kernel,
        grid=(t // MOE_TS,),
        in_specs=[pl.BlockSpec((MOE_TS,), lambda i: (i,), memory_space=pltpu.SMEM),
                  pl.BlockSpec((MOE_TS, w), lambda i: (i, 0)), pl.BlockSpec(memory_space=pl.ANY)],
        out_specs=pl.BlockSpec(memory_space=pl.ANY),
        out_shape=jax.ShapeDtypeStruct((n_rows, w), F32),
        scratch_shapes=[pltpu.SemaphoreType.DMA(())],
        input_output_aliases={2: 0},
        compiler_params=_cparams("arbitrary"),
        name="moe_dispatch",
    )(pos, hx, xs0)


def _moe_bucket_kernel(tg_ref, tv_ref, xs_ref, w13_ref, w2_ref, ys_ref, xb_scr):
    i = pl.program_id(0)
    e = pl.program_id(1)
    d = ys_ref.shape[1]

    @pl.when(e == 0)
    def _():
        xb_scr[...] = xs_ref[:, 0:d].astype(BF16)
        ys_ref[...] = jnp.zeros(ys_ref.shape, F32)

    @pl.when(tv_ref[i] > 0)
    def _():
        lane_e = N_GROUPS + tg_ref[i] * EXPERTS_PER_GROUP + e
        ys_ref[...] += _expert_ffn(xb_scr[...], xs_ref[:, d:d + LANES], lane_e, w13_ref[...], w2_ref[...])


def _moe_buckets(tile_group, tile_valid, xs, w13, w2, d):
    n_rows, w = xs.shape
    _, _, f2 = w13.shape
    row = lambda i, e, tg, tv: (i, 0)
    wsel = lambda i, e, tg, tv: (tg[i] * EXPERTS_PER_GROUP + e, 0, 0)
    return pl.pallas_call(
        _moe_bucket_kernel,
        grid_spec=pltpu.PrefetchScalarGridSpec(
            num_scalar_prefetch=2,
            grid=(n_rows // MOE_TM, EXPERTS_PER_GROUP),
            in_specs=[pl.BlockSpec((MOE_TM, w), row),
                      pl.BlockSpec((None, d, f2), wsel), pl.BlockSpec((None, f2 // 2, d), wsel)],
            out_specs=pl.BlockSpec((MOE_TM, d), row),
            scratch_shapes=[pltpu.VMEM((MOE_TM, d), BF16)]),
        out_shape=jax.ShapeDtypeStruct((n_rows, d), F32),
        compiler_params=_cparams("parallel", "arbitrary"),
        name="moe_buckets",
    )(tile_group, tile_valid, xs, w13, w2)


def _collect_kernel(pos_ref, ys_hbm, x1_ref, gt2_ref, y_ref, buf, sem):
    tu = buf.shape[0]

    def issue(u, carry):
        for j in range(MOE_UNROLL):
            t = u * MOE_UNROLL + j
            _row_copy(ys_hbm, pos_ref[t], buf, t, sem).start(priority=j % 2)
        return carry

    def drain(t, carry):
        _row_copy(ys_hbm, pos_ref[t], buf, t, sem).wait()
        return carry

    lax.fori_loop(0, tu // MOE_UNROLL, issue, 0)
    lax.fori_loop(0, tu, drain, 0, unroll=8)
    y_ref[...] = x1_ref[...] + gt2_ref[...] * buf[...]


def _moe_collect(pos, ys, x1, gt2, seq):
    t, d = x1.shape
    tu = MOE_TS
    return pl.pallas_call(
        _collect_kernel,
        grid=(t // tu,),
        in_specs=[pl.BlockSpec((tu,), lambda i: (i,), memory_space=pltpu.SMEM),
                  pl.BlockSpec(memory_space=pl.ANY),
                  pl.BlockSpec((tu, d), lambda i: (i, 0)), _mod_spec(gt2, tu, seq)],
        out_specs=pl.BlockSpec((tu, d), lambda i: (i, 0)),
        out_shape=jax.ShapeDtypeStruct((t, d), F32),
        scratch_shapes=[pltpu.VMEM((tu, d), F32), pltpu.SemaphoreType.DMA(())],
        compiler_params=_cparams("arbitrary"),
        name="moe_collect",
    )(pos, ys, x1, gt2)


def _moe_sorted(hx, w13, w2, x1, gt2, seq):
    t, d = x1.shape
    rank, cnt = _moe_plan(hx, d)
    group = hx[:, d].astype(jnp.int32)
    counts = cnt[0, 0:N_GROUPS].astype(jnp.int32)
    padded = (counts + MOE_TM - 1) // MOE_TM * MOE_TM
    ends = jnp.cumsum(padded)
    starts = ends - padded
    pos = jnp.take(starts, group) + rank[:, 0]
    n_tiles = t // MOE_TM + N_GROUPS
    tile_start = jnp.arange(n_tiles, dtype=jnp.int32) * MOE_TM
    tile_group = jnp.minimum(jnp.sum((tile_start[:, None] >= ends[None, :]).astype(jnp.int32), axis=1),
                             N_GROUPS - 1)
    tile_valid = (tile_start < ends[-1]).astype(jnp.int32)
    xs = _moe_dispatch(pos, hx, n_tiles * MOE_TM)
    ys = _moe_buckets(tile_group, tile_valid, xs, w13, w2, d)
    return _moe_collect(pos, ys, x1, gt2, seq)


def _rope_tables(pos0, seq, reps):
    pos = (pos0 + jnp.arange(seq)).astype(F32)
    inv = ROPE_THETA ** (-jnp.arange(0, DA_HEAD_DIM, 2, dtype=F32) / DA_HEAD_DIM)
    ang = pos[:, None] * inv[None, :]
    cos, sin = jnp.cos(ang), jnp.sin(ang)
    cos_t = jnp.tile(jnp.concatenate([cos, cos], axis=-1), (reps, LANES // DA_HEAD_DIM))
    sin_t = jnp.tile(jnp.concatenate([-sin, sin], axis=-1), (reps, LANES // DA_HEAD_DIM))
    return cos_t, sin_t


def _layer(x, mod, pos0, past, wts):
    bx, sx, d = x.shape
    t = bx * sx
    x2d = x.reshape(t, d)
    per_batch = sx >= 512
    tm = 512 if per_batch else t

    def modv(i):
        if per_batch:
            return mod[:, i:i + 1, :]
        return jnp.repeat(mod[:, i, :], sx, axis=0)[None]

    sh1, sc1, gt1, sh2, sc2, gt2 = (modv(i) for i in range(6))
    cos_t, sin_t = _rope_tables(pos0, sx, 1 if per_batch else bx)

    tm_big = 1024 if (per_batch and sx % 1024 == 0) else tm
    h, q, k32, kb, v32, vb = _qkv(x2d, sc1, sh1, wts["norm1_g"], wts["w_qkv"], wts["qg"], wts["kg"],
                                  cos_t, sin_t, wts["gm"], tm_big, sx)
    rw = _proj(h, wts["w_rw"], tm_big, wts["w_rw"].shape[1] // 3, None, F32, "rw_proj")
    gates = _proj(h, wts["w_gate"], tm_big, d, "sigmoid", BF16, "gate_proj")

    if past is None:
        oa = _attn_prompt(q, kb, vb, wts["lam"], wts["subln_g"], bx, sx, min(1024, sx), min(1024, sx))
        shift0 = jnp.zeros((bx, 1, rw.shape[1]), F32)
        h0_bd = jnp.zeros((bx, RW_GROUPS, RW_GW, RW_GW), F32)
    else:
        ck, cv, s0, sh0 = past
        oa = _attn_sample(q, kb, vb, ck, cv, wts["lam"], wts["subln_g"], bx, sx)
        shift0 = jnp.pad(sh0, ((0, 0), (0, 0), (0, rw.shape[1] - sh0.shape[-1])))
        h0_bd = _state_to_bd(s0.astype(F32))

    rw3 = rw.reshape(bx, sx, rw.shape[1])
    if sx % RW_L == 0:
        cps = RW_CPS if sx % (RW_CPS * RW_L) == 0 else 1
        rw_in, n_valid = rw3, cps * RW_L
    else:
        assert sx < RW_L
        cps = 1
        rw_in, n_valid = jnp.pad(rw3, ((0, 0), (0, RW_L - sx), (0, 0))), sx
    ob, h_bd = _rwkv(rw_in, shift0, h0_bd, wts, wts["gm"], n_valid, cps)
    ob = ob[:, 0:sx, :].reshape(t, d)

    x1, hx = _merge(oa, ob, gates, x2d, gt1, sc2, sh2, wts["norm2_g"],
                    wts["w_oa"], wts["w_ob"], wts["w_out"], wts["wr"], wts["br"], tm, sx)
    if per_batch and sx % MOE_TS == 0:
        y = _moe_sorted(hx, wts["w13"], wts["w2"], x1, gt2, sx)
    else:
        y = _moe_dense(hx, wts["w13"], wts["w2"], x1, gt2, tm, sx)

    n_cols = 3 * d + W_LORA + A_LORA + G_LORA
    return (y.reshape(bx, sx, d),
            k32.reshape(1, bx, sx, DA_HEADS, 2 * DA_HEAD_DIM),
            v32.reshape(1, bx, sx, DA_HEADS, 2 * DA_HEAD_DIM),
            _bd_to_state(h_bd)[None],
            rw3[:, sx - 1:sx, 0:n_cols][None])


def kernel(x_prompt, x_sample, cache_k, cache_v, state_wkv, state_shift, c_prompt, c_sample, norm1_g, norm2_g, w_ada, b_ada, w_in, da_qn_g, da_kn_g, da_lambda, da_subln_g, w_oa, rw_mu, rw_w0, rw_w2, rw_a0, rw_a2, rw_g2, rw_k_k, rw_k_a, rw_r_k, rw_ln_g, rw_ln_b, w_ob, w_out, router_g, router_g_b, router_e, router_e_b, exp_w1, exp_w3, exp_w2):
    b, s, d = x_prompt.shape
    bs = x_sample.shape[0]
    past_len = cache_k.shape[2]
    assert w_in.shape[0] == 1, "single layer"

    c_all = jnp.concatenate([c_prompt, c_sample], axis=0)
    c_all = jnp.pad(c_all, ((0, (-c_all.shape[0]) % 8), (0, 0)))
    mod, lam_tile = _adaln(c_all, w_ada[0], b_ada[0], da_lambda[0])
    mod_p = mod[0:b].reshape(b, 6, d)
    mod_s = mod[b:b + bs].reshape(bs, 6, d)

    win = w_in[0]
    rw_cols = 3 * d + W_LORA + A_LORA + G_LORA
    rw_pad = 3 * d + LORA_PAD
    w_rw = jnp.pad(win[:, 3 * d:3 * d + rw_cols], ((0, 0), (0, rw_pad - rw_cols))).astype(BF16)
    lw = jnp.zeros((LORA_PAD, 3 * d), F32)
    lw = lw.at[0:W_LORA, 0:d].set(rw_w2[0])
    lw = lw.at[W_LORA:W_LORA + A_LORA, d:2 * d].set(rw_a2[0])
    lw = lw.at[W_LORA + A_LORA:W_LORA + A_LORA + G_LORA, 2 * d:3 * d].set(rw_g2[0])
    f = exp_w1.shape[-1]
    wr = jnp.pad(jnp.concatenate([router_g[0], router_e[0]], axis=1), ((0, 0), (0, LANES - N_GROUPS - N_EXPERTS)))
    br = jnp.pad(jnp.concatenate([router_g_b[0], router_e_b[0]]), (0, LANES - N_GROUPS - N_EXPERTS))[None]
    half = jnp.ones((DA_HEAD_DIM, DA_HEAD_DIM), F32) / DA_HEAD_DIM
    wts = dict(
        norm1_g=norm1_g[0][None], norm2_g=norm2_g[0][None],
        w_qkv=win[:, 0:3 * d].astype(BF16), w_rw=w_rw, w_gate=win[:, 3 * d + rw_cols:].astype(BF16),
        qg=jnp.tile(da_qn_g[0], GMW // DA_HEAD_DIM)[None], kg=jnp.tile(da_kn_g[0], GMW // DA_HEAD_DIM)[None],
        gm=jnp.kron(jnp.eye(GMW // DA_HEAD_DIM, dtype=F32), half).astype(BF16),
        lam=lam_tile[0:1], subln_g=da_subln_g[0][None],
        mu=jnp.pad(rw_mu[0], (0, rw_pad - rw_cols))[None], lw=lw.astype(BF16),
        w0=rw_w0[0][None], a0=rw_a0[0][None], k_k=rw_k_k[0][None], k_a=rw_k_a[0][None],
        r_k=rw_r_k[0].reshape(1, d), ln_g=rw_ln_g[0][None], ln_b=rw_ln_b[0][None],
        w_oa=w_oa[0].astype(BF16), w_ob=w_ob[0].astype(BF16), w_out=w_out[0].astype(BF16),
        wr=wr, br=br,
        w13=jnp.concatenate([exp_w1[0], exp_w3[0]], axis=-1).reshape(N_EXPERTS, d, 2 * f).astype(BF16),
        w2=exp_w2[0].reshape(N_EXPERTS, f, d).astype(BF16),
    )

    out_p = _layer(x_prompt, mod_p, 0, None, wts)
    out_s = _layer(x_sample, mod_s, past_len,
                   (cache_k, cache_v, state_wkv[0], state_shift[0]), wts)
    return (out_p[0], out_s[0], out_p[1], out_p[2], out_p[3], out_p[4],
            out_s[1], out_s[2], out_s[3], out_s[4])
```
